```python
import math
import jax
import jax.numpy as jnp
from jax import lax
import numpy as np

D_MODEL = 1024
BATCH = 32
SEQ = 256
DEPTH = 2
DEC_BATCH = 4
DEC_SEQ = 2048
PAST_LEN = 256

GRID_W = 64
N_GROUPS = 4
GROUP_W = D_MODEL // N_GROUPS
HEAD_DIM = 64
N_HEADS = GROUP_W // HEAD_DIM
N_DIR = 2
ML_DK = HEAD_DIM
ML_DV = HEAD_DIM
ML_CHUNK = 128
NA_ROWS = 8
NA_COLS = 16
RG_BLOCKS = N_HEADS
RG_BLOCK = GROUP_W // RG_BLOCKS
RG_C = 8.0
RG_CONV = 4
DF_DQK = HEAD_DIM // 2
DF_DV = HEAD_DIM
N_EXPERTS = 16
EC_CAPACITY = 2
D_FF_EXPERT = D_MODEL
Q_BLOCK = 128
ROPE_BASE = 10000.0
EPS = 1e-6
PROJ_SIZES = (GROUP_W, GROUP_W, GROUP_W, GROUP_W, N_DIR * 2 * N_HEADS, GROUP_W, GROUP_W, GROUP_W, GROUP_W, GROUP_W, GROUP_W, GROUP_W, GROUP_W)
PROJ_WIDTH = sum(PROJ_SIZES)

kernel_name = 'hybrid_diffusion_trunk_step'


def rms_norm(x, g):
    xf = x.astype(jnp.float32)
    y = xf * lax.rsqrt(jnp.mean(xf * xf, axis=-1, keepdims=True) + EPS)
    return (y * g.astype(jnp.float32)).astype(x.dtype)


def lambda_init(l):
    return 0.8 - 0.6 * math.exp(-0.3 * l)


def axial_rope(x):
    n = x.shape[1]
    half = x.shape[-1] // 2
    nf = half // 2
    t = jnp.arange(n)
    row = (t // GRID_W).astype(jnp.float32)
    col = (t % GRID_W).astype(jnp.float32)
    inv = ROPE_BASE ** (-jnp.arange(nf, dtype=jnp.float32) / nf)

    def rot(xh, pos):
        ang = pos[:, None] * inv[None, :]
        cos = jnp.cos(ang)[None, :, None, None, :].astype(x.dtype)
        sin = jnp.sin(ang)[None, :, None, None, :].astype(x.dtype)
        x1, x2 = xh[..., :nf], xh[..., nf:]
        return jnp.concatenate([x1 * cos - x2 * sin, x1 * sin + x2 * cos], axis=-1)

    return jnp.concatenate([rot(x[..., :half], row), rot(x[..., half:], col)], axis=-1)


def dense_attend(q, k, v):
    b, n, h, d = q.shape
    nb = n // Q_BLOCK
    scale = d ** -0.5
    qb = q.reshape(b, nb, Q_BLOCK, h, d).swapaxes(0, 1)

    def one(qblk):
        s = jnp.einsum('bqhd,bkhd->bhqk', qblk, k).astype(jnp.float32) * scale
        p = jax.nn.softmax(s, axis=-1).astype(v.dtype)
        return jnp.einsum('bhqk,bkhd->bqhd', p, v)

    return lax.map(one, qb).swapaxes(0, 1).reshape(b, n, h, v.shape[-1])


def diff_attend(q, k, v, lam):
    b, n, h, _, d = q.shape
    nb = n // Q_BLOCK
    scale = d ** -0.5
    qb = q.reshape(b, nb, Q_BLOCK, h, 2, d).swapaxes(0, 1)

    def one(qblk):
        s = jnp.einsum('bqhcd,bkhcd->bhcqk', qblk, k).astype(jnp.float32) * scale
        p = jax.nn.softmax(s, axis=-1)
        a = (p[:, :, 0] - lam * p[:, :, 1]).astype(v.dtype)
        return jnp.einsum('bhqk,bkhv->bqhv', a, v)

    return lax.map(one, qb).swapaxes(0, 1).reshape(b, n, h, v.shape[-1])


def neighborhood_attend(q, k, v, k_ctx, v_ctx, rpb):
    b, n, h, d = q.shape
    s_len = k_ctx.shape[1]
    rows = n // GRID_W
    kr = min(NA_ROWS, rows)
    scale = d ** -0.5
    qg = q.reshape(b, rows, GRID_W, h, d)
    kg = k.reshape(b, rows, GRID_W, h, d)
    vg = v.reshape(b, rows, GRID_W, h, d)
    r = jnp.arange(rows)
    r0 = jnp.clip(r - kr // 2, 0, rows - kr)
    win_rows = r0[:, None] + jnp.arange(kr)[None, :]
    k_win = kg[:, win_rows]
    v_win = vg[:, win_rows]
    cq = jnp.arange(GRID_W)
    c0 = jnp.clip(cq - NA_COLS // 2, 0, GRID_W - NA_COLS)
    col_ok = (cq[None, :] >= c0[:, None]) & (cq[None, :] < c0[:, None] + NA_COLS)
    dr_idx = win_rows - r[:, None] + (NA_ROWS - 1)
    dc_idx = jnp.clip(cq[None, :] - cq[:, None], -(NA_COLS - 1), NA_COLS - 1) + (NA_COLS - 1)
    bias = rpb[:, dr_idx[:, None, :, None], dc_idx[None, :, None, :]]
    s_loc = jnp.einsum('brqhd,brjkhd->bhrqjk', qg, k_win).astype(jnp.float32) * scale + bias[None].astype(jnp.float32)
    s_loc = jnp.where(col_ok[:, None, :], s_loc, -jnp.inf)
    s_ctx = jnp.einsum('brqhd,bshd->bhrqs', qg, k_ctx).astype(jnp.float32) * scale
    s = jnp.concatenate([s_ctx, s_loc.reshape(b, h, rows, GRID_W, kr * GRID_W)], axis=-1)
    p = jax.nn.softmax(s, axis=-1).astype(v.dtype)
    o = jnp.einsum('bhrqs,bshd->brqhd', p[..., :s_len], v_ctx) + jnp.einsum('bhrqjk,brjkhd->brqhd', p[..., s_len:].reshape(b, h, rows, GRID_W, kr, GRID_W), v_win)
    return o.reshape(b, n, h, d)


def mlstm_scan(q, k, v, ig, flog, c0, n0, m0):
    b, n, h, dk = q.shape
    dv = v.shape[-1]
    L = ML_CHUNK
    nc = n // L
    f32 = jnp.float32

    def chunks(t):
        return jnp.moveaxis(t.astype(f32).reshape((b, nc, L) + t.shape[2:]), 1, 0).swapaxes(2, 3)

    xs = tuple(chunks(t) for t in (q, k, v, ig, flog))
    tri = jnp.tril(jnp.ones((L, L), dtype=bool))

    def step(carry, inp):
        cm, nm, m = carry
        qc, kc, vc, ic, fc = inp
        bcum = jnp.cumsum(fc, axis=-1)
        logd = jnp.where(tri, bcum[..., :, None] - bcum[..., None, :] + ic[..., None, :], -jnp.inf)
        inter = bcum + m[..., None]
        mt = jnp.maximum(inter, jnp.max(logd, axis=-1))
        s = jnp.einsum('bhtd,bhsd->bhts', qc, kc) * jnp.exp(logd - mt[..., None])
        e = jnp.exp(inter - mt)
        num = jnp.einsum('bhts,bhsv->bhtv', s, vc) + e[..., None] * jnp.einsum('bhtd,bhdv->bhtv', qc, cm)
        den = jnp.sum(s, axis=-1) + e * jnp.einsum('bhtd,bhd->bht', qc, nm)
        hout = num / jnp.maximum(jnp.abs(den), jnp.exp(-mt))[..., None]
        bl = bcum[..., -1]
        wlog = bl[..., None] - bcum + ic
        m_new = jnp.maximum(bl + m, jnp.max(wlog, axis=-1))
        w = jnp.exp(wlog - m_new[..., None])
        g = jnp.exp(bl + m - m_new)
        c_new = g[..., None, None] * cm + jnp.einsum('bhs,bhsd,bhsv->bhdv', w, kc, vc)
        n_new = g[..., None] * nm + jnp.einsum('bhs,bhsd->bhd', w, kc)
        return (c_new, n_new, m_new), hout

    (cf, nf, mf), hs = lax.scan(step, (c0.astype(f32), n0.astype(f32), m0.astype(f32)), xs)
    hout = jnp.moveaxis(hs.swapaxes(2, 3), 0, 1).reshape(b, n, h, dv)
    return hout, (cf, nf, mf)


def mlstm_bidir(q, k, v, ig, flog, c0, n0, m0):
    hf, (cf, nf, mf) = mlstm_scan(q, k, v, ig[:, :, 0], flog[:, :, 0], c0[:, 0], n0[:, 0], m0[:, 0])
    fl = lambda t: jnp.flip(t, axis=1)
    hb, (cb, nb, mb) = mlstm_scan(fl(q), fl(k), fl(v), fl(ig[:, :, 1]), fl(flog[:, :, 1]), c0[:, 1], n0[:, 1], m0[:, 1])
    return hf + fl(hb), (jnp.stack([cf, cb], axis=1), jnp.stack([nf, nb], axis=1), jnp.stack([mf, mb], axis=1))


def centred_dwconv(x, w, bias):
    n = x.shape[1]
    pad_l = RG_CONV // 2
    xp = jnp.pad(x, ((0, 0), (pad_l, RG_CONV - 1 - pad_l), (0, 0)))
    y = bias
    for j in range(RG_CONV):
        y = y + xp[:, j:j + n] * w[j]
    return y


def linear_scan(a, b, h0, reverse):
    def comb(l, r):
        return l[0] * r[0], r[0] * l[1] + r[1]
    acum, hs = lax.associative_scan(comb, (a, b), axis=1, reverse=reverse)
    return hs + acum * h0[:, None, :]


def rglru_dirs(xc, p, h0):
    b, n, _ = xc.shape
    f32 = jnp.float32
    xf = xc.astype(f32)
    xb = xf.reshape(b, n, RG_BLOCKS, RG_BLOCK)
    outs = []
    for d in range(N_DIR):
        r = jax.nn.sigmoid(jnp.einsum('bnki,kij->bnkj', xb, p['rg_wa'][d].astype(f32)).reshape(b, n, GROUP_W) + p['rg_ba'][d].astype(f32))
        i = jax.nn.sigmoid(jnp.einsum('bnki,kij->bnkj', xb, p['rg_wx'][d].astype(f32)).reshape(b, n, GROUP_W) + p['rg_bx'][d].astype(f32))
        log_a = -RG_C * r * jax.nn.softplus(-p['rg_lam'][d].astype(f32))
        bt = jnp.sqrt(-jnp.expm1(2.0 * log_a)) * i * xf
        outs.append(linear_scan(jnp.exp(log_a), bt, h0[:, d].astype(f32), reverse=(d == 1)))
    return outs[0], outs[1]


def diff_lambda(p, lam_init):
    f32 = jnp.float32
    return (jnp.exp(jnp.sum(p['df_lq1'].astype(f32) * p['df_lk1'].astype(f32))) - jnp.exp(jnp.sum(p['df_lq2'].astype(f32) * p['df_lk2'].astype(f32))) + lam_init)


def mixer_inputs(hn, p):
    b, n, _ = hn.shape
    idx = [int(i) for i in np.cumsum(PROJ_SIZES)[:-1]]
    (ml_q, ml_k, ml_v, ml_o, ml_g, na_q, na_k, na_v, rg_x, rg_g, df_q, df_k, df_v) = jnp.split(hn @ p['w_in'], idx, axis=-1)
    heads = lambda t: t.reshape(b, n, N_HEADS, HEAD_DIM)
    gates = (ml_g + p['ml_gate_b']).astype(jnp.float32).reshape(b, n, N_DIR, 2, N_HEADS)
    dfh = lambda t, g: rms_norm(t.reshape(b, n, N_HEADS, 2, DF_DQK), g)
    return dict(
        ml_q=heads(ml_q), ml_k=heads(ml_k) * (ML_DK ** -0.5), ml_v=heads(ml_v), ml_o=ml_o,
        ml_ig=gates[:, :, :, 0], ml_flog=jax.nn.log_sigmoid(gates[:, :, :, 1]),
        na_q=rms_norm(heads(na_q), p['na_qn_g']), na_k=rms_norm(heads(na_k), p['na_kn_g']), na_v=heads(na_v),
        rg_x=centred_dwconv(rg_x, p['rg_conv_w'], p['rg_conv_b']), rg_g=rg_g,
        df_q=dfh(df_q, p['df_qn_g']), df_k=dfh(df_k, p['df_kn_g']), df_v=heads(df_v))


def merge_heads(m, h_ml, y_na, rg_f, rg_b, o_df, p, lam_init):
    b, n = y_na.shape[:2]
    dt = y_na.dtype
    y_ml = rms_norm(h_ml.astype(dt), p['ml_norm_g'].reshape(N_HEADS, ML_DV)).reshape(b, n, GROUP_W) * jax.nn.sigmoid(m['ml_o'])
    y_rg = (rg_f + rg_b).astype(dt) * jax.nn.gelu(m['rg_g'])
    y_df = (rms_norm(o_df, p['df_subln_g']) * (1.0 - lam_init)).reshape(b, n, GROUP_W)
    y = jnp.concatenate([y_ml, y_na.reshape(b, n, GROUP_W), y_rg, y_df], axis=-1)
    return y @ p['w_out']


def mix_context(hn, p, lam_init):
    b, n, _ = hn.shape
    f32 = jnp.float32
    dt = hn.dtype
    m = mixer_inputs(hn, p)
    h_ml, (ml_c, ml_n, ml_m) = mlstm_bidir(m['ml_q'], m['ml_k'], m['ml_v'], m['ml_ig'], m['ml_flog'], jnp.zeros((b, N_DIR, N_HEADS, ML_DK, ML_DV), f32), jnp.zeros((b, N_DIR, N_HEADS, ML_DK), f32), jnp.zeros((b, N_DIR, N_HEADS), f32))
    y_na = dense_attend(m['na_q'], m['na_k'], m['na_v'])
    rg_f, rg_b = rglru_dirs(m['rg_x'], p, jnp.zeros((b, N_DIR, GROUP_W), f32))
    o_df = diff_attend(m['df_q'], m['df_k'], m['df_v'], diff_lambda(p, lam_init))
    y = merge_heads(m, h_ml, y_na, rg_f, rg_b, o_df, p, lam_init)
    rg_final = jnp.stack([rg_f[:, -1], rg_b[:, 0]], axis=1)
    return y, (m['na_k'], m['na_v'], m['df_k'], m['df_v'], ml_c.astype(dt), ml_n.astype(dt), ml_m.astype(dt), rg_final.astype(dt))


def mix_latent(hn, p, lam_init, cache):
    na_kc, na_vc, df_kc, df_vc, ml_c0, ml_n0, ml_m0, rg_h0 = cache
    m = mixer_inputs(hn, p)
    h_ml, _ = mlstm_bidir(m['ml_q'], m['ml_k'], m['ml_v'], m['ml_ig'], m['ml_flog'], ml_c0, ml_n0, ml_m0)
    y_na = neighborhood_attend(m['na_q'], m['na_k'], m['na_v'], na_kc, na_vc, p['na_rpb'])
    rg_f, rg_b = rglru_dirs(m['rg_x'], p, rg_h0)
    df_k = jnp.concatenate([df_kc, axial_rope(m['df_k'])], axis=1)
    df_v = jnp.concatenate([df_vc, m['df_v']], axis=1)
    o_df = diff_attend(axial_rope(m['df_q']), df_k, df_v, diff_lambda(p, lam_init))
    return merge_heads(m, h_ml, y_na, rg_f, rg_b, o_df, p, lam_init)


def ec_moe(h, router_w, w_gate, w_up, w_down):
    b, n, _ = h.shape
    cap = EC_CAPACITY * n // N_EXPERTS
    aff = jax.nn.softmax(jnp.einsum('bnd,de->bne', h, router_w).astype(jnp.float32), axis=-1)
    top_w, top_i = lax.top_k(jnp.swapaxes(aff, 1, 2), cap)
    xs = jax.vmap(lambda hb, ib: hb[ib])(h, top_i)
    g = jnp.einsum('becd,edf->becf', xs, w_gate)
    u = jnp.einsum('becd,edf->becf', xs, w_up)
    y = jnp.einsum('becf,efd->becd', jax.nn.silu(g) * u, w_down) * top_w[..., None].astype(h.dtype)
    return jnp.zeros_like(h).at[jnp.arange(b)[:, None, None], top_i].add(y)


def trunk_layer(x, cvec, p, mix):
    sh1, sc1, g1, sh2, sc2, g2 = jnp.split(jax.nn.silu(cvec) @ p['w_mod'] + p['b_mod'], 6, axis=-1)
    y, extra = mix(rms_norm(x, p['norm1_g']) * (1 + sc1) + sh1)
    x = x + g1 * y
    hn = rms_norm(x, p['norm2_g']) * (1 + sc2) + sh2
    x = x + g2 * ec_moe(hn, p['router_w'], p['moe_wg'], p['moe_wu'], p['moe_wd'])
    return x, extra


def setup_inputs(seed: int = 0) -> dict:
    key = jax.random.key(seed)
    ks = iter(jax.random.split(key, 64))
    f32 = jnp.float32
    D = D_MODEL
    H = N_HEADS

    def nrm(shape, scale=1.0):
        return scale * jax.random.normal(next(ks), shape, f32)

    gate_b = jnp.concatenate([nrm((DEPTH, N_DIR, 1, H), 0.1), jnp.linspace(3.0, 6.0, H, dtype=f32) + nrm((DEPTH, N_DIR, 1, H), 0.1)], axis=2).reshape(DEPTH, N_DIR * 2 * H)
    u = jax.random.uniform(next(ks), (DEPTH, N_DIR, GROUP_W), f32, 0.9, 0.999)
    s = u ** (1.0 / RG_C)
    rg_lam = jnp.log(s) - jnp.log1p(-s)
    return dict(
        x_prompt=nrm((BATCH, SEQ, D)),
        x_sample=nrm((DEC_BATCH, DEC_SEQ, D)),
        cache_na_k=nrm((DEC_BATCH, DEPTH, PAST_LEN, H, HEAD_DIM)),
        cache_na_v=nrm((DEC_BATCH, DEPTH, PAST_LEN, H, HEAD_DIM)),
        cache_df_k=nrm((DEC_BATCH, DEPTH, PAST_LEN, H, 2, DF_DQK)),
        cache_df_v=nrm((DEC_BATCH, DEPTH, PAST_LEN, H, DF_DV)),
        state_ml_c=nrm((DEC_BATCH, DEPTH, N_DIR, H, ML_DK, ML_DV), 0.1),
        state_ml_n=nrm((DEC_BATCH, DEPTH, N_DIR, H, ML_DK), 0.1),
        state_ml_m=nrm((DEC_BATCH, DEPTH, N_DIR, H), 0.5),
        state_rg_h=nrm((DEC_BATCH, DEPTH, N_DIR, GROUP_W), 0.5),
        c=nrm((DEC_BATCH, D)),
        c_ctx=nrm((D,)),
        norm1_g=1.0 + nrm((DEPTH, D), 0.05),
        norm2_g=1.0 + nrm((DEPTH, D), 0.05),
        w_mod=nrm((DEPTH, D, 6 * D), 0.5 * D ** -0.5),
        b_mod=nrm((DEPTH, 6 * D), 0.02),
        w_in=nrm((DEPTH, D, PROJ_WIDTH), D ** -0.5),
        ml_gate_b=gate_b,
        ml_norm_g=1.0 + nrm((DEPTH, GROUP_W), 0.05),
        na_qn_g=1.0 + nrm((DEPTH, HEAD_DIM), 0.05),
        na_kn_g=1.0 + nrm((DEPTH, HEAD_DIM), 0.05),
        na_rpb=nrm((DEPTH, H, 2 * NA_ROWS - 1, 2 * NA_COLS - 1), 0.1),
        rg_conv_w=nrm((DEPTH, RG_CONV, GROUP_W), RG_CONV ** -0.5),
        rg_conv_b=nrm((DEPTH, GROUP_W), 0.02),
        rg_wa=nrm((DEPTH, N_DIR, RG_BLOCKS, RG_BLOCK, RG_BLOCK), RG_BLOCK ** -0.5),
        rg_ba=nrm((DEPTH, N_DIR, GROUP_W), 0.02),
        rg_wx=nrm((DEPTH, N_DIR, RG_BLOCKS, RG_BLOCK, RG_BLOCK), RG_BLOCK ** -0.5),
        rg_bx=nrm((DEPTH, N_DIR, GROUP_W), 0.02),
        rg_lam=rg_lam,
        df_qn_g=1.0 + nrm((DEPTH, DF_DQK), 0.05),
        df_kn_g=1.0 + nrm((DEPTH, DF_DQK), 0.05),
        df_lq1=nrm((DEPTH, DF_DQK), 0.1),
        df_lk1=nrm((DEPTH, DF_DQK), 0.1),
        df_lq2=nrm((DEPTH, DF_DQK), 0.1),
        df_lk2=nrm((DEPTH, DF_DQK), 0.1),
        df_subln_g=1.0 + nrm((DEPTH, DF_DV), 0.05),
        w_out=nrm((DEPTH, D, D), D ** -0.5),
        router_w=nrm((DEPTH, D, N_EXPERTS), D ** -0.5),
        moe_wg=nrm((DEPTH, N_EXPERTS, D, D_FF_EXPERT), D ** -0.5),
        moe_wu=nrm((DEPTH, N_EXPERTS, D, D_FF_EXPERT), D ** -0.5),
        moe_wd=nrm((DEPTH, N_EXPERTS, D_FF_EXPERT, D), D_FF_EXPERT ** -0.5),
    )


def reference(x_prompt, x_sample, cache_na_k, cache_na_v, cache_df_k, cache_df_v, state_ml_c, state_ml_n, state_ml_m, state_rg_h, c, c_ctx, norm1_g, norm2_g, w_mod, b_mod, w_in, ml_gate_b, ml_norm_g, na_qn_g, na_kn_g, na_rpb, rg_conv_w, rg_conv_b, rg_wa, rg_ba, rg_wx, rg_bx, rg_lam, df_qn_g, df_kn_g, df_lq1, df_lk1, df_lq2, df_lk2, df_subln_g, w_out, router_w, moe_wg, moe_wu, moe_wd):
    def layer_params(l):
        return dict(norm1_g=norm1_g[l], norm2_g=norm2_g[l], w_mod=w_mod[l], b_mod=b_mod[l], w_in=w_in[l], ml_gate_b=ml_gate_b[l], ml_norm_g=ml_norm_g[l], na_qn_g=na_qn_g[l], na_kn_g=na_kn_g[l], na_rpb=na_rpb[l], rg_conv_w=rg_conv_w[l], rg_conv_b=rg_conv_b[l], rg_wa=rg_wa[l], rg_ba=rg_ba[l], rg_wx=rg_wx[l], rg_bx=rg_bx[l], rg_lam=rg_lam[l], df_qn_g=df_qn_g[l], df_kn_g=df_kn_g[l], df_lq1=df_lq1[l], df_lk1=df_lk1[l], df_lq2=df_lq2[l], df_lk2=df_lk2[l], df_subln_g=df_subln_g[l], w_out=w_out[l], router_w=router_w[l], moe_wg=moe_wg[l], moe_wu=moe_wu[l], moe_wd=moe_wd[l])

    x = x_prompt
    states = []
    for l in range(DEPTH):
        p = layer_params(l)
        li = lambda_init(l)
        x, st = trunk_layer(x, c_ctx[None, None, :], p, lambda h, p=p, li=li: mix_context(h, p, li))
        states.append(st)
    y_prompt = x
    new_na_k = jnp.stack([s[0] for s in states], axis=1)
    new_na_v = jnp.stack([s[1] for s in states], axis=1)
    new_df_k = jnp.stack([s[2] for s in states], axis=1)
    new_df_v = jnp.stack([s[3] for s in states], axis=1)
    new_ml_c = jnp.stack([s[4] for s in states], axis=1)
    new_ml_n = jnp.stack([s[5] for s in states], axis=1)
    new_ml_m = jnp.stack([s[6] for s in states], axis=1)
    new_rg_h = jnp.stack([s[7] for s in states], axis=1)

    x = x_sample
    for l in range(DEPTH):
        p = layer_params(l)
        li = lambda_init(l)
        cache_l = (cache_na_k[:, l], cache_na_v[:, l], cache_df_k[:, l], cache_df_v[:, l], state_ml_c[:, l], state_ml_n[:, l], state_ml_m[:, l], state_rg_h[:, l])
        x, _ = trunk_layer(x, c[:, None, :], p, lambda h, p=p, li=li, cl=cache_l: (mix_latent(h, p, li, cl), None))
    y_sample = x
    return (y_prompt, y_sample, new_na_k, new_na_v, new_df_k, new_df_v, new_ml_c, new_ml_n, new_ml_m, new_rg_h)
```

```python
import functools
import math

import numpy as np
import jax
import jax.numpy as jnp
from jax import lax
from jax.experimental import pallas as pl
from jax.experimental.pallas import tpu as pltpu

F32 = jnp.float32
BF16 = jnp.bfloat16

D = 1024
GW = 256
NH = 4
HD = 64
NE = 16
EPS = 1e-6
CHUNK = 128
GRID_W = 64
NA_ROWS = 8
NA_COLS = 16
RG_C = 8.0
ROPE_BASE = 10000.0
PROJ_PAD = 3200
VMEM_LIMIT_BYTES = 56 * 1024 * 1024
NEG_BIG = -1e30


def _bf(x):
    return x.astype(BF16)


def _dot(a, b):
    return jnp.dot(a, b, preferred_element_type=F32)


def _dot_nt(a, b):
    return lax.dot_general(a, b, (((1,), (1,)), ((), ())), preferred_element_type=F32)


def _dot_tn(a, b):
    return lax.dot_general(a, b, (((0,), (0,)), ((), ())), preferred_element_type=F32)


def _dot_f32(a, b):
    return jnp.dot(a, b, precision=lax.Precision.HIGHEST, preferred_element_type=F32)


def _params(n_axes):
    return pltpu.CompilerParams(dimension_semantics=("arbitrary",) * n_axes,
                                vmem_limit_bytes=VMEM_LIMIT_BYTES)


def _full(shape):
    return pl.BlockSpec(shape, lambda *_: (0,) * len(shape))


def _sds(shape, dtype=F32):
    return jax.ShapeDtypeStruct(shape, dtype)


def _softplus(x):
    return jnp.maximum(x, 0.0) + jnp.log1p(jnp.exp(-jnp.abs(x)))


def _log_sigmoid(x):
    return jnp.minimum(x, 0.0) - jnp.log1p(jnp.exp(-jnp.abs(x)))


def _seg_rms(x, nseg, g_row):
    seg = x.shape[-1] // nseg
    lane = lax.broadcasted_iota(jnp.int32, x.shape, 1)
    x2 = x * x
    tot = jnp.zeros_like(x)
    for s in range(nseg):
        m = (lane >= s * seg) & (lane < (s + 1) * seg)
        t = jnp.sum(jnp.where(m, x2, 0.0), axis=-1, keepdims=True)
        tot = jnp.where(m, t, tot)
    return x * lax.rsqrt(tot * (1.0 / seg) + EPS) * g_row


def _mod_row(pid, row_base, row_div):
    return row_base + pid // row_div


def _mod_kernel(cv_ref, w_ref, b_ref, o_ref):
    cv = cv_ref[...]
    s = cv * jax.nn.sigmoid(cv)
    o_ref[0] = _dot(_bf(s), _bf(w_ref[0])) + b_ref[0]


def _modulation(cv, w_mod, b_mod):
    nl = w_mod.shape[0]
    tn = 1536
    return pl.pallas_call(
        _mod_kernel,
        grid=(nl, 6 * D // tn),
        in_specs=[_full((8, D)),
                  pl.BlockSpec((1, D, tn), lambda l, j: (l, 0, j)),
                  pl.BlockSpec((1, 1, tn), lambda l, j: (l, 0, j))],
        out_specs=pl.BlockSpec((1, 8, tn), lambda l, j: (l, 0, j)),
        out_shape=_sds((nl, 8, 6 * D)),
        compiler_params=_params(2),
        name="modulation",
    )(cv, w_mod, b_mod.reshape(nl, 1, 6 * D))


def _rope(x, cos_t, sin_t):
    lane = lax.broadcasted_iota(jnp.int32, x.shape, 1)
    first = (lane & 15) < 8
    sw = jnp.where(first, pltpu.roll(x, GW - 8, axis=1), pltpu.roll(x, 8, axis=1))
    return x * cos_t + sw * sin_t


def _proj_kernel(*refs, row_base, row_div, use_rope):
    if use_rope:
        (x_ref, mod_ref, g1_ref, w_ref, gb_ref, qkg_ref, cos_ref, sin_ref,
         ml_ref, gate_ref, na_ref, rg_ref, df_ref) = refs
    else:
        (x_ref, mod_ref, g1_ref, w_ref, gb_ref, qkg_ref,
         ml_ref, gate_ref, na_ref, rg_ref, df_ref) = refs
    r = _mod_row(pl.program_id(0), row_base, row_div)
    x = x_ref[...]
    ms = jnp.mean(x * x, axis=-1, keepdims=True)
    y = x * lax.rsqrt(ms + EPS) * g1_ref[...]
    sh = mod_ref[pl.ds(r, 1), 0:D]
    sc = mod_ref[pl.ds(r, 1), D:2 * D]
    hn = _bf(y * (1.0 + sc) + sh)

    ml = _dot(hn, w_ref[:, 0:1024])
    ml_ref[:, 0:256] = ml[:, 0:256]
    ml_ref[:, 256:512] = ml[:, 256:512] * (HD ** -0.5)
    ml_ref[:, 512:1024] = ml[:, 512:1024]

    gz = _dot(hn, w_ref[:, 3072:3200]) + gb_ref[...]
    lane = lax.broadcasted_iota(jnp.int32, gz.shape, 1)
    gate_ref[...] = jnp.where(((lane >> 2) & 1) == 1, _log_sigmoid(gz), gz)

    nz = _dot(hn, w_ref[:, 1024:1792])
    na_ref[:, 0:256] = _seg_rms(nz[:, 0:256], NH, qkg_ref[0:1, :])
    na_ref[:, 256:512] = _seg_rms(nz[:, 256:512], NH, qkg_ref[1:2, :])
    na_ref[:, 512:768] = nz[:, 512:768]

    rg_ref[...] = _dot(hn, w_ref[:, 1792:2304])

    dz = _dot(hn, w_ref[:, 2304:3072])
    dq = _seg_rms(dz[:, 0:256], 2 * NH, qkg_ref[2:3, :])
    dk = _seg_rms(dz[:, 256:512], 2 * NH, qkg_ref[3:4, :])
    if use_rope:
        cos_t = cos_ref[...]
        sin_t = sin_ref[...]
        dq = _rope(dq, cos_t, sin_t)
        dk = _rope(dk, cos_t, sin_t)
    df_ref[:, 0:256] = dq
    df_ref[:, 256:512] = dk
    df_ref[:, 512:768] = dz[:, 512:768]


def _proj_in(x, mod, g1, w_r, gate_b, qkg, rope, *, row_base, row_div, tm=512):
    tp = x.shape[0]
    use_rope = rope is not None
    in_specs = [pl.BlockSpec((tm, D), lambda i: (i, 0)), _full((8, 6 * D)), _full((1, D)),
                _full((D, PROJ_PAD)), _full((1, 128)), _full((4, GW))]
    args = [x, mod, g1, w_r, gate_b, qkg]
    if use_rope:
        tiles = rope[0].shape[0] // tm
        in_specs += [pl.BlockSpec((tm, GW), lambda i: (i % tiles, 0))] * 2
        args += [rope[0], rope[1]]
    widths = (1024, 128, 768, 512, 768)
    return pl.pallas_call(
        functools.partial(_proj_kernel, row_base=row_base, row_div=row_div, use_rope=use_rope),
        grid=(tp // tm,),
        in_specs=in_specs,
        out_specs=[pl.BlockSpec((tm, n), lambda i: (i, 0)) for n in widths],
        out_shape=[_sds((tp, n)) for n in widths],
        compiler_params=_params(1),
        name="proj_in",
    )(*args)


def _mlstm_dir(d, q_ref, k_ref, v_ref, g_ref, h_ref, cn_s, m_s):
    lc = CHUNK
    g = g_ref[...]
    gt = g.T
    row = lax.broadcasted_iota(jnp.int32, (lc, lc), 0)
    col = lax.broadcasted_iota(jnp.int32, (lc, lc), 1)
    tri = (row >= col) if d == 0 else (row <= col)
    tri_t = (col >= row) if d == 0 else (col <= row)
    bc_col = _dot_f32(tri.astype(F32), g)
    bc_row = _dot_f32(gt, tri_t.astype(F32))
    lane = lax.broadcasted_iota(jnp.int32, (lc, HD), 1)
    ones_col = (lane == 0).astype(F32)
    last = lc - 1 if d == 0 else 0
    outs = []
    for h in range(NH):
        jf = d * 8 + 4 + h
        ji = d * 8 + h
        b_col = bc_col[:, jf:jf + 1]
        b_row = bc_row[jf:jf + 1, :]
        i_col = g[:, ji:ji + 1]
        i_row = gt[ji:ji + 1, :]
        sr = d * NH + h
        m = m_s[sr:sr + 1, 0:1]
        cn = cn_s[d, h]
        qh = _bf(q_ref[:, h * HD:(h + 1) * HD])
        kh = k_ref[:, h * HD:(h + 1) * HD]
        vh = v_ref[:, h * HD:(h + 1) * HD]
        v_aug = _bf(jnp.concatenate([vh, ones_col], axis=-1))

        logd = jnp.where(tri, b_col - b_row + i_row, -jnp.inf)
        inter = b_col + m
        mt = jnp.maximum(inter, jnp.max(logd, axis=-1, keepdims=True))
        s = _dot_nt(qh, _bf(kh)) * jnp.exp(logd - mt)
        e = jnp.exp(inter - mt)
        tot = _dot(_bf(s), v_aug) + e * _dot(qh, _bf(cn))
        num = tot[:, 0:HD]
        den = tot[:, HD:HD + 1]
        outs.append(num / jnp.maximum(jnp.abs(den), jnp.exp(-mt)))

        bl = b_col[last:last + 1, :]
        wl_row = bl - b_row + i_row
        m_new = jnp.maximum(bl + m, jnp.max(wl_row, axis=-1, keepdims=True))
        w_col = jnp.exp(bl - b_col + i_col - m_new)
        gdec = jnp.exp(bl + m - m_new)
        cn_s[d, h] = gdec * cn + _dot_tn(_bf(kh * w_col), v_aug)
        m_s[sr:sr + 1, :] = jnp.broadcast_to(m_new, (1, 128))
    h_ref[...] = jnp.concatenate(outs, axis=-1)


def _mlstm_kernel(qf, kf, vf, gf, qb, kb, vb, gb, cn0_ref, m0_ref,
                  hf_ref, hb_ref, cn_out, m_out, cn_s, m_s):
    j = pl.program_id(1)

    @pl.when(j == 0)
    def _():
        cn_s[...] = cn0_ref[0]
        m_s[...] = m0_ref[0]

    _mlstm_dir(0, qf, kf, vf, gf, hf_ref, cn_s, m_s)
    _mlstm_dir(1, qb, kb, vb, gb, hb_ref, cn_s, m_s)

    @pl.when(j == pl.num_programs(1) - 1)
    def _():
        cn_out[0] = cn_s[...]
        m_out[0] = m_s[...]


def _mlstm(mlz, gates, cn0, m0, *, nreq, nc):
    tp = mlz.shape[0]

    def fwd(col):
        return lambda r, j: (r * nc + j, col)

    def bwd(col):
        return lambda r, j: (r * nc + nc - 1 - j, col)

    in_specs = []
    for mk in (fwd, bwd):
        in_specs += [pl.BlockSpec((CHUNK, GW), mk(0)), pl.BlockSpec((CHUNK, GW), mk(1)),
                     pl.BlockSpec((CHUNK, GW), mk(2)), pl.BlockSpec((CHUNK, 128), mk(0))]
    in_specs += [pl.BlockSpec((1, 2, NH, HD, 128), lambda r, j: (r, 0, 0, 0, 0)),
                 pl.BlockSpec((1, 8, 128), lambda r, j: (r, 0, 0))]
    return pl.pallas_call(
        _mlstm_kernel,
        grid=(nreq, nc),
        in_specs=in_specs,
        out_specs=[pl.BlockSpec((CHUNK, GW), fwd(0)), pl.BlockSpec((CHUNK, GW), bwd(0)),
                   pl.BlockSpec((1, 2, NH, HD, 128), lambda r, j: (r, 0, 0, 0, 0)),
                   pl.BlockSpec((1, 8, 128), lambda r, j: (r, 0, 0))],
        out_shape=[_sds((tp, GW)), _sds((tp, GW)), _sds((nreq, 2, NH, HD, 128)), _sds((nreq, 8, 128))],
        scratch_shapes=[pltpu.VMEM((2, NH, HD, 128), F32), pltpu.VMEM((8, 128), F32)],
        compiler_params=_params(2),
        name="mlstm",
    )(mlz, mlz, mlz, gates, mlz, mlz, mlz, gates, cn0, m0)


def _diff_lambda(lam_ref, lam_init):
    lp = lam_ref[...]
    a = jnp.exp(jnp.sum(lp[0:1, :] * lp[1:2, :], axis=-1, keepdims=True))
    b = jnp.exp(jnp.sum(lp[2:3, :] * lp[3:4, :], axis=-1, keepdims=True))
    return a - b + lam_init


def _ctx_attn_kernel(nq, nk, nv, dq, dk, dv, lam_ref, ona_ref, odf_ref, *, lam_init):
    lam = _diff_lambda(lam_ref, lam_init)
    na_out, df_out = [], []
    for h in range(NH):
        hs = slice(h * HD, (h + 1) * HD)
        s = _dot_nt(_bf(nq[:, hs]), _bf(nk[:, hs])) * (HD ** -0.5)
        e = jnp.exp(s - jnp.max(s, axis=-1, keepdims=True))
        p = e / jnp.sum(e, axis=-1, keepdims=True)
        na_out.append(_dot(_bf(p), _bf(nv[:, hs])))
        ps = []
        for c in range(2):
            cs = slice(h * HD + c * 32, h * HD + (c + 1) * 32)
            s = _dot_nt(_bf(dq[:, cs]), _bf(dk[:, cs])) * (32 ** -0.5)
            e = jnp.exp(s - jnp.max(s, axis=-1, keepdims=True))
            ps.append(e / jnp.sum(e, axis=-1, keepdims=True))
        a = ps[0] - lam * ps[1]
        df_out.append(_dot(_bf(a), _bf(dv[:, hs])))
    ona_ref[...] = jnp.concatenate(na_out, axis=-1)
    odf_ref[...] = jnp.concatenate(df_out, axis=-1)


def _ctx_attn(na, df, lamp, lam_init, *, nreq, n):
    tp = na.shape[0]
    blk = lambda c: pl.BlockSpec((n, GW), lambda b: (b, c))
    return pl.pallas_call(
        functools.partial(_ctx_attn_kernel, lam_init=lam_init),
        grid=(nreq,),
        in_specs=[blk(0), blk(1), blk(2), blk(0), blk(1), blk(2), _full((4, 32))],
        out_specs=[blk(0), blk(0)],
        out_shape=[_sds((tp, GW)), _sds((tp, GW))],
        compiler_params=_params(1),
        name="ctx_attn",
    )(na, na, na, df, df, df, lamp)


def _rpb_kernel(rpb_ref, o_ref):
    h = pl.program_id(0)
    s = pl.program_id(1)
    shape = (GRID_W, NA_ROWS * GRID_W)
    lane = lax.broadcasted_iota(jnp.int32, shape, 1)
    cq = lax.broadcasted_iota(jnp.int32, shape, 0)
    ck = lane & (GRID_W - 1)
    dc = jnp.clip(ck - cq, -(NA_COLS - 1), NA_COLS - 1) + (NA_COLS - 1)
    c0 = jnp.clip(cq - NA_COLS // 2, 0, GRID_W - NA_COLS)
    ok = (ck >= c0) & (ck < c0 + NA_COLS)
    jrow = lax.broadcasted_iota(jnp.int32, (1, shape[1]), 1) >> 6
    ncol = 2 * NA_COLS - 1
    acc = jnp.zeros(shape, F32)
    for b in range(ncol):
        vrow = jnp.zeros((1, shape[1]), F32)
        for j in range(NA_ROWS):
            val = rpb_ref[(h * (2 * NA_ROWS - 1) + s + j) * ncol + b]
            vrow = jnp.where(jrow == j, val, vrow)
        acc = jnp.where(dc == b, vrow, acc)
    o_ref[0, 0] = jnp.where(ok, acc, NEG_BIG)


def _rpb_table(rpb):
    return pl.pallas_call(
        _rpb_kernel,
        grid=(NH, NA_ROWS),
        in_specs=[pl.BlockSpec(memory_space=pltpu.SMEM)],
        out_specs=pl.BlockSpec((1, 1, GRID_W, NA_ROWS * GRID_W), lambda h, s: (h, s, 0, 0)),
        out_shape=_sds((NH, NA_ROWS, GRID_W, NA_ROWS * GRID_W)),
        compiler_params=_params(2),
        name="rpb_table",
    )(rpb.reshape(-1))


def _na_lat_kernel(q_ref, k_ref, v_ref, kc_ref, vc_ref, tb_ref, o_ref, *, rows):
    r = pl.program_id(1)
    r0 = jnp.clip(r - NA_ROWS // 2, 0, rows - NA_ROWS)
    sidx = r0 - r + (NA_ROWS - 1)
    start = pl.multiple_of(r0 * GRID_W, GRID_W)
    win = NA_ROWS * GRID_W
    outs = []
    for h in range(NH):
        hs = slice(h * HD, (h + 1) * HD)
        q = _bf(q_ref[:, hs])
        s_c = _dot_nt(q, _bf(kc_ref[0, :, hs])) * (HD ** -0.5)
        s_l = _dot_nt(q, _bf(k_ref[pl.ds(start, win), hs])) * (HD ** -0.5) + tb_ref[h, sidx]
        m = jnp.maximum(jnp.max(s_c, axis=-1, keepdims=True), jnp.max(s_l, axis=-1, keepdims=True))
        e_c = jnp.exp(s_c - m)
        e_l = jnp.exp(s_l - m)
        den = jnp.sum(e_c, axis=-1, keepdims=True) + jnp.sum(e_l, axis=-1, keepdims=True)
        o = _dot(_bf(e_c), _bf(vc_ref[0, :, hs])) + _dot(_bf(e_l), _bf(v_ref[pl.ds(start, win), hs]))
        outs.append(o / den)
    o_ref[...] = jnp.concatenate(outs, axis=-1)


def _na_lat(na, kc, vc, tb, *, nreq, n):
    tp = na.shape[0]
    rows = n // GRID_W
    past = kc.shape[1]
    return pl.pallas_call(
        functools.partial(_na_lat_kernel, rows=rows),
        grid=(nreq, rows),
        in_specs=[pl.BlockSpec((GRID_W, GW), lambda b, r: (b * rows + r, 0)),
                  pl.BlockSpec((n, GW), lambda b, r: (b, 1)),
                  pl.BlockSpec((n, GW), lambda b, r: (b, 2)),
                  pl.BlockSpec((1, past, GW), lambda b, r: (b, 0, 0)),
                  pl.BlockSpec((1, past, GW), lambda b, r: (b, 0, 0)),
                  _full(tb.shape)],
        out_specs=pl.BlockSpec((GRID_W, GW), lambda b, r: (b * rows + r, 0)),
        out_shape=_sds((tp, GW)),
        compiler_params=_params(2),
        name="na_latent",
    )(na, na, na, kc, vc, tb)


def _df_lat_kernel(q_ref, k_ref, v_ref, kc_ref, vc_ref, lam_ref, o_ref, *, lam_init):
    lam = _diff_lambda(lam_ref, lam_init)
    outs = []
    for h in range(NH):
        hs = slice(h * HD, (h + 1) * HD)
        ps = []
        for c in range(2):
            cs = slice(h * HD + c * 32, h * HD + (c + 1) * 32)
            q = _bf(q_ref[:, cs])
            s_c = _dot_nt(q, _bf(kc_ref[0, :, cs])) * (32 ** -0.5)
            s_l = _dot_nt(q, _bf(k_ref[:, cs])) * (32 ** -0.5)
            m = jnp.maximum(jnp.max(s_c, axis=-1, keepdims=True), jnp.max(s_l, axis=-1, keepdims=True))
            e_c = jnp.exp(s_c - m)
            e_l = jnp.exp(s_l - m)
            den = jnp.sum(e_c, axis=-1, keepdims=True) + jnp.sum(e_l, axis=-1, keepdims=True)
            ps.append((e_c / den, e_l / den))
        a_c = ps[0][0] - lam * ps[1][0]
        a_l = ps[0][1] - lam * ps[1][1]
        outs.append(_dot(_bf(a_c), _bf(vc_ref[0, :, hs])) + _dot(_bf(a_l), _bf(v_ref[:, hs])))
    o_ref[...] = jnp.concatenate(outs, axis=-1)


def _df_lat(df, kc, vc, lamp, lam_init, *, nreq, n, tq=256):
    tp = df.shape[0]
    nq = n // tq
    past = kc.shape[1]
    return pl.pallas_call(
        functools.partial(_df_lat_kernel, lam_init=lam_init),
        grid=(nreq, nq),
        in_specs=[pl.BlockSpec((tq, GW), lambda b, j: (b * nq + j, 0)),
                  pl.BlockSpec((n, GW), lambda b, j: (b, 1)),
                  pl.BlockSpec((n, GW), lambda b, j: (b, 2)),
                  pl.BlockSpec((1, past, GW), lambda b, j: (b, 0, 0)),
                  pl.BlockSpec((1, past, GW), lambda b, j: (b, 0, 0)),
                  _full((4, 32))],
        out_specs=pl.BlockSpec((tq, GW), lambda b, j: (b * nq + j, 0)),
        out_shape=_sds((tp, GW)),
        compiler_params=_params(2),
        name="df_latent",
    )(df, df, df, kc, vc, lamp)


def _rg_kernel(x_ref, g_ref, cw_ref, cb_ref, wbd_ref, bias_ref, lam_ref, h0_ref,
               y_ref, fin_ref, a_s, b_s, *, n):
    x = x_ref[...]
    row = lax.broadcasted_iota(jnp.int32, x.shape, 0)
    xc = cb_ref[...] + jnp.where(row >= 2, pltpu.roll(x, 2, axis=0), 0.0) * cw_ref[0:1, :]
    xc = xc + jnp.where(row >= 1, pltpu.roll(x, 1, axis=0), 0.0) * cw_ref[1:2, :]
    xc = xc + x * cw_ref[2:3, :]
    xc = xc + jnp.where(row < n - 1, pltpu.roll(x, n - 1, axis=0), 0.0) * cw_ref[3:4, :]
    z = _dot(_bf(xc), wbd_ref[...]) + bias_ref[...]
    sub = row & 7
    for d in range(2):
        rgate = jax.nn.sigmoid(z[:, 512 * d:512 * d + GW])
        igate = jax.nn.sigmoid(z[:, 512 * d + GW:512 * d + 2 * GW])
        la = -RG_C * rgate * _softplus(-lam_ref[d:d + 1, :])
        a = jnp.exp(la)
        t = jnp.tanh(la)
        b = jnp.sqrt(-2.0 * t / (1.0 - t)) * igate * xc
        for dd in (1, 2, 4):
            if d == 0:
                keep = sub >= dd
                a_sh = jnp.where(keep, pltpu.roll(a, dd, axis=0), 1.0)
                b_sh = jnp.where(keep, pltpu.roll(b, dd, axis=0), 0.0)
            else:
                keep = sub < 8 - dd
                a_sh = jnp.where(keep, pltpu.roll(a, n - dd, axis=0), 1.0)
                b_sh = jnp.where(keep, pltpu.roll(b, n - dd, axis=0), 0.0)
            b = b + a * b_sh
            a = a * a_sh
        a_s[d] = a
        b_s[d] = b

    nt = n // 8

    def body(t, carry):
        hf, hb = carry
        sf = pl.multiple_of(t * 8, 8)
        sb = pl.multiple_of((nt - 1 - t) * 8, 8)
        tf = a_s[0, pl.ds(sf, 8), :] * hf + b_s[0, pl.ds(sf, 8), :]
        tb = a_s[1, pl.ds(sb, 8), :] * hb + b_s[1, pl.ds(sb, 8), :]
        b_s[0, pl.ds(sf, 8), :] = tf
        b_s[1, pl.ds(sb, 8), :] = tb
        return tf[7:8, :], tb[0:1, :]

    hf, hb = lax.fori_loop(0, nt, body, (h0_ref[0, 0:1, :], h0_ref[0, 1:2, :]))
    fin_ref[0, 0:1, :] = hf
    fin_ref[0, 1:2, :] = hb
    gg = g_ref[...]
    cdf = 0.5 * (1.0 + jnp.tanh(math.sqrt(2.0 / math.pi) * (gg + 0.044715 * (gg * gg * gg))))
    y_ref[...] = (b_s[0] + b_s[1]) * (gg * cdf)


def _rglru(rg, cw, cb, wbd, bias, lam, h0, *, nreq, n):
    tp = rg.shape[0]
    return pl.pallas_call(
        functools.partial(_rg_kernel, n=n),
        grid=(nreq,),
        in_specs=[pl.BlockSpec((n, GW), lambda b: (b, 0)), pl.BlockSpec((n, GW), lambda b: (b, 1)),
                  _full((4, GW)), _full((1, GW)), _full((GW, 4 * GW)), _full((1, 4 * GW)), _full((2, GW)),
                  pl.BlockSpec((1, 2, GW), lambda b: (b, 0, 0))],
        out_specs=[pl.BlockSpec((n, GW), lambda b: (b, 0)), pl.BlockSpec((1, 2, GW), lambda b: (b, 0, 0))],
        out_shape=[_sds((tp, GW)), _sds((nreq, 2, GW))],
        scratch_shapes=[pltpu.VMEM((2, n, GW), F32), pltpu.VMEM((2, n, GW), F32)],
        compiler_params=_params(1),
        name="rglru",
    )(rg, rg, cw, cb, wbd, bias, lam, h0)


def _merge_kernel(hf_ref, hb_ref, mlo_ref, yna_ref, yrg_ref, odf_ref, x_ref, mod_ref, mlg_ref, sub_ref,
                  wout_ref, n2_ref, rw_ref, x1_ref, hn2_ref, lg_ref, *, row_base, row_div, lam_init):
    r = _mod_row(pl.program_id(0), row_base, row_div)
    y_ml = _seg_rms(hf_ref[...] + hb_ref[...], NH, mlg_ref[...]) * jax.nn.sigmoid(mlo_ref[...])
    y_df = _seg_rms(odf_ref[...], NH, sub_ref[...]) * (1.0 - lam_init)
    y = jnp.concatenate([_bf(y_ml), _bf(yna_ref[...]), _bf(yrg_ref[...]), _bf(y_df)], axis=-1)
    o = _dot(y, wout_ref[...])
    x1 = x_ref[...] + mod_ref[pl.ds(r, 1), 2 * D:3 * D] * o
    x1_ref[...] = x1
    ms = jnp.mean(x1 * x1, axis=-1, keepdims=True)
    hn = x1 * lax.rsqrt(ms + EPS) * n2_ref[...]
    hn = _bf(hn * (1.0 + mod_ref[pl.ds(r, 1), 4 * D:5 * D]) + mod_ref[pl.ds(r, 1), 3 * D:4 * D])
    hn2_ref[...] = hn
    lg_ref[...] = _dot(hn, rw_ref[...])


def _merge(hf, hb, mlz, yna, yrg, odf, x, mod, mlg, sub, wout, n2, rw, lam_init, *, row_base, row_div, tm=512):
    tp = x.shape[0]
    g = lambda c: pl.BlockSpec((tm, GW), lambda i: (i, c))
    return pl.pallas_call(
        functools.partial(_merge_kernel, row_base=row_base, row_div=row_div, lam_init=lam_init),
        grid=(tp // tm,),
        in_specs=[g(0), g(0), g(3), g(0), g(0), g(0),
                  pl.BlockSpec((tm, D), lambda i: (i, 0)), _full((8, 6 * D)), _full((1, GW)), _full((1, GW)),
                  _full((D, D)), _full((1, D)), _full((D, 128))],
        out_specs=[pl.BlockSpec((tm, D), lambda i: (i, 0)), pl.BlockSpec((tm, D), lambda i: (i, 0)),
                   pl.BlockSpec((tm, 128), lambda i: (i, 0))],
        out_shape=[_sds((tp, D)), _sds((tp, D), BF16), _sds((tp, 128))],
        compiler_params=_params(1),
        name="merge",
    )(hf, hb, mlz, yna, yrg, odf, x, mod, mlg, sub, wout, n2, rw)


def _excl_cumsum_lanes(mask):
    blk = 256
    r = lax.broadcasted_iota(jnp.int32, (blk, blk), 0)
    c = lax.broadcasted_iota(jnp.int32, (blk, blk), 1)
    tri = (r < c).astype(F32).astype(BF16)
    off = jnp.zeros((mask.shape[0], 1), F32)
    outs = []
    for i in range(mask.shape[1] // blk):
        mb = mask[:, i * blk:(i + 1) * blk]
        outs.append(_dot(_bf(mb), tri) + off)
        off = off + jnp.sum(mb, axis=-1, keepdims=True)
    return jnp.concatenate(outs, axis=-1)


def _route_kernel(lg_ref, pos_ref, aff_ref, *, cap):
    lg = lg_ref[...].T[0:NE, :]
    ex = jnp.exp(lg - jnp.max(lg, axis=0, keepdims=True))
    aff = ex / jnp.sum(ex, axis=0, keepdims=True)
    aff_ref[...] = aff
    bits = pltpu.bitcast(aff, jnp.int32)
    thr = jnp.zeros((NE, 1), jnp.int32)
    for bit in range(30, -1, -1):
        cand = thr | (1 << bit)
        cnt = jnp.sum((bits >= cand).astype(jnp.int32), axis=-1, keepdims=True)
        thr = jnp.where(cnt >= cap, cand, thr)
    gt = bits > thr
    eq = bits == thr
    need = (cap - jnp.sum(gt.astype(jnp.int32), axis=-1, keepdims=True)).astype(F32)
    eq_rank = _excl_cumsum_lanes(eq.astype(F32))
    sel = gt | (eq & (eq_rank < need))
    slot = _excl_cumsum_lanes(sel.astype(F32))
    pos_ref[...] = jnp.where(sel, slot.astype(jnp.int32), -1)


def _route(lg, *, nreq, n, cap):
    tp = lg.shape[0]
    return pl.pallas_call(
        functools.partial(_route_kernel, cap=cap),
        grid=(nreq,),
        in_specs=[pl.BlockSpec((n, 128), lambda b: (b, 0))],
        out_specs=[pl.BlockSpec((NE, n), lambda b: (0, b)), pl.BlockSpec((NE, n), lambda b: (0, b))],
        out_shape=[_sds((NE, tp), jnp.int32), _sds((NE, tp))],
        compiler_params=_params(1),
        name="route",
    )(lg)


def _gather_kernel(pos_ref, aff_ref, h_ref, xs_ref, w_ref, *, eb, cap, n):
    eg = pl.program_id(1)
    h = h_ref[...]
    io = lax.broadcasted_iota(jnp.int32, (cap, n), 0)
    for k in range(eb):
        e = eg * eb + k
        sel = pos_ref[pl.ds(e, 1), :] == io
        xs_ref[k] = _dot(sel.astype(F32).astype(BF16), h).astype(BF16)
        w = jnp.sum(jnp.where(sel, aff_ref[pl.ds(e, 1), :], 0.0), axis=-1, keepdims=True)
        w_ref[k] = jnp.broadcast_to(w, (cap, 128))


def _gather(pos, aff, hn2, *, nreq, n, cap, eb):
    return pl.pallas_call(
        functools.partial(_gather_kernel, eb=eb, cap=cap, n=n),
        grid=(nreq, NE // eb),
        in_specs=[pl.BlockSpec((NE, n), lambda b, g: (0, b)), pl.BlockSpec((NE, n), lambda b, g: (0, b)),
                  pl.BlockSpec((n, D), lambda b, g: (b, 0))],
        out_specs=[pl.BlockSpec((eb, cap, D), lambda b, g: (g, b, 0)),
                   pl.BlockSpec((eb, cap, 128), lambda b, g: (g, b, 0))],
        out_shape=[_sds((NE, nreq * cap, D), BF16), _sds((NE, nreq * cap, 128))],
        compiler_params=_params(2),
        name="moe_gather",
    )(pos, aff, hn2)


def _expert_kernel(xc_ref, xl_ref, wc_ref, wl_ref, wg_ref, wu_ref, wd_ref, yc_ref, yl_ref, acc_ref, *, rows):
    f = pl.program_id(1)

    @pl.when(f == 0)
    def _():
        acc_ref[...] = jnp.zeros_like(acc_ref)

    wg = _bf(wg_ref[0, 0])
    wu = _bf(wu_ref[0, 0])
    wd = _bf(wd_ref[0, 0])
    tm = 512
    for part, x_ref in enumerate((xc_ref, xl_ref)):
        for ch in range(rows // tm):
            x = x_ref[0, ch * tm:(ch + 1) * tm, :]
            g = _dot(x, wg)
            u = _dot(x, wu)
            a = _bf(g * jax.nn.sigmoid(g) * u)
            lo = part * rows + ch * tm
            acc_ref[lo:lo + tm, :] += _dot(a, wd)

    @pl.when(f == pl.num_programs(1) - 1)
    def _():
        yc_ref[0] = _bf(acc_ref[0:rows, :] * wc_ref[0][:, 0:1])
        yl_ref[0] = _bf(acc_ref[rows:2 * rows, :] * wl_ref[0][:, 0:1])


def _experts(xs_c, xs_l, w_c, w_l, wg, wu, wd, layer, *, fb=512):
    rows = xs_c.shape[1]
    dff = wg.shape[-1]
    xspec = pl.BlockSpec((1, rows, D), lambda e, f: (e, 0, 0))
    wspec = pl.BlockSpec((1, rows, 128), lambda e, f: (e, 0, 0))
    return pl.pallas_call(
        functools.partial(_expert_kernel, rows=rows),
        grid=(NE, dff // fb),
        in_specs=[xspec, xspec, wspec, wspec,
                  pl.BlockSpec((1, 1, D, fb), lambda e, f: (layer, e, 0, f)),
                  pl.BlockSpec((1, 1, D, fb), lambda e, f: (layer, e, 0, f)),
                  pl.BlockSpec((1, 1, fb, D), lambda e, f: (layer, e, f, 0))],
        out_specs=[xspec, xspec],
        out_shape=[_sds((NE, rows, D), BF16), _sds((NE, rows, D), BF16)],
        scratch_shapes=[pltpu.VMEM((2 * rows, D), F32)],
        compiler_params=_params(2),
        name="moe_experts",
    )(xs_c, xs_l, w_c, w_l, wg, wu, wd)


def _scatter_kernel(pos_ref, y_ref, x1_ref, mod_ref, o_ref, acc_ref, *, eb, cap, n, row_base, row_mul):
    eg = pl.program_id(1)

    @pl.when(eg == 0)
    def _():
        acc_ref[...] = jnp.zeros_like(acc_ref)

    io = lax.broadcasted_iota(jnp.int32, (cap, n), 0)
    for k in range(eb):
        sel = pos_ref[pl.ds(eg * eb + k, 1), :] == io
        acc_ref[...] += _dot_tn(sel.astype(F32).astype(BF16), y_ref[k])

    @pl.when(eg == pl.num_programs(1) - 1)
    def _():
        r = row_base + row_mul * pl.program_id(0)
        o_ref[...] = x1_ref[...] + mod_ref[pl.ds(r, 1), 5 * D:6 * D] * acc_ref[...]


def _scatter(pos, y, x1, mod, *, nreq, n, cap, eb, row_base, row_mul):
    tp = x1.shape[0]
    return pl.pallas_call(
        functools.partial(_scatter_kernel, eb=eb, cap=cap, n=n, row_base=row_base, row_mul=row_mul),
        grid=(nreq, NE // eb),
        in_specs=[pl.BlockSpec((NE, n), lambda b, g: (0, b)),
                  pl.BlockSpec((eb, cap, D), lambda b, g: (g, b, 0)),
                  pl.BlockSpec((n, D), lambda b, g: (b, 0)), _full((8, 6 * D))],
        out_specs=pl.BlockSpec((n, D), lambda b, g: (b, 0)),
        out_shape=_sds((tp, D)),
        scratch_shapes=[pltpu.VMEM((n, D), F32)],
        compiler_params=_params(2),
        name="moe_scatter",
    )(pos, y, x1, mod)


def _rope_tables(n):
    nf = 8
    t = np.arange(n)
    rowp = (t // GRID_W).astype(np.float32)
    colp = (t % GRID_W).astype(np.float32)
    inv = (np.float32(ROPE_BASE) ** (-np.arange(nf, dtype=np.float32) / np.float32(nf))).astype(np.float32)
    lane = np.arange(GW)
    c32 = lane % 32
    pos = np.where((c32 < 16)[None, :], rowp[:, None], colp[:, None]).astype(np.float32)
    ang = (pos * inv[(c32 % 8)][None, :]).astype(np.float32).astype(np.float64)
    sign = np.where((lane % 16) < 8, -1.0, 1.0)[None, :]
    return jnp.asarray(np.cos(ang), F32), jnp.asarray(np.sin(ang) * sign, F32)


def _block_diag(w):
    nb, bi, bo = w.shape
    return (jnp.eye(nb, dtype=w.dtype)[:, None, :, None] * w[:, :, None, :]).reshape(nb * bi, nb * bo)


def _layer_params(l, w_in, ml_gate_b, na_qn_g, na_kn_g, df_qn_g, df_kn_g, rg_wa, rg_wx, rg_ba, rg_bx,
                  df_lq1, df_lk1, df_lq2, df_lk2, df_subln_g, w_out, router_w):
    wi = w_in[l]
    w_r = jnp.concatenate([wi[:, 0:1024], wi[:, 1040:3088], wi[:, 1024:1040],
                           jnp.zeros((D, PROJ_PAD - 3088), F32)], axis=1).astype(BF16)
    gate_b = jnp.pad(ml_gate_b[l], (0, 128 - 16)).reshape(1, 128)
    qkg = jnp.stack([jnp.tile(na_qn_g[l], NH), jnp.tile(na_kn_g[l], NH),
                     jnp.tile(df_qn_g[l], 2 * NH), jnp.tile(df_kn_g[l], 2 * NH)])
    wbd = jnp.concatenate([_block_diag(rg_wa[l, 0]), _block_diag(rg_wx[l, 0]),
                           _block_diag(rg_wa[l, 1]), _block_diag(rg_wx[l, 1])], axis=1).astype(BF16)
    rg_bias = jnp.concatenate([rg_ba[l, 0], rg_bx[l, 0], rg_ba[l, 1], rg_bx[l, 1]]).reshape(1, 4 * GW)
    lamp = jnp.stack([df_lq1[l], df_lk1[l], df_lq2[l], df_lk2[l]])
    sub = jnp.tile(df_subln_g[l], NH).reshape(1, GW)
    rw = jnp.pad(router_w[l], ((0, 0), (0, 128 - NE))).astype(BF16)
    return dict(w_r=w_r, gate_b=gate_b, qkg=qkg, wbd=wbd, rg_bias=rg_bias, lamp=lamp, sub=sub,
                wout=w_out[l].astype(BF16), rw=rw)


def kernel(x_prompt, x_sample, cache_na_k, cache_na_v, cache_df_k, cache_df_v, state_ml_c, state_ml_n, state_ml_m, state_rg_h, c, c_ctx, norm1_g, norm2_g, w_mod, b_mod, w_in, ml_gate_b, ml_norm_g, na_qn_g, na_kn_g, na_rpb, rg_conv_w, rg_conv_b, rg_wa, rg_ba, rg_wx, rg_bx, rg_lam, df_qn_g, df_kn_g, df_lq1, df_lk1, df_lq2, df_lk2, df_subln_g, w_out, router_w, moe_wg, moe_wu, moe_wd):
    nb, seq, _ = x_prompt.shape
    db, dseq, _ = x_sample.shape
    depth = w_in.shape[0]
    past = cache_na_k.shape[2]
    tm = 512
    cap_c = 2 * seq // NE
    cap_l = 2 * dseq // NE

    cv = jnp.concatenate([c_ctx[None, :], c, jnp.zeros((8 - 1 - db, D), F32)], axis=0)
    mod_all = _modulation(cv, w_mod, b_mod)
    rope = _rope_tables(dseq)

    xc = x_prompt.reshape(nb * seq, D)
    xl = x_sample.reshape(db * dseq, D)
    ctx_out = []
    for l in range(depth):
        lam_init = 0.8 - 0.6 * math.exp(-0.3 * l)
        p = _layer_params(l, w_in, ml_gate_b, na_qn_g, na_kn_g, df_qn_g, df_kn_g, rg_wa, rg_wx, rg_ba, rg_bx,
                          df_lq1, df_lk1, df_lq2, df_lk2, df_subln_g, w_out, router_w)
        mod = mod_all[l]
        g1 = norm1_g[l].reshape(1, D)
        g2 = norm2_g[l].reshape(1, D)
        mlg = ml_norm_g[l].reshape(1, GW)
        cb = rg_conv_b[l].reshape(1, GW)
        tb = _rpb_table(na_rpb[l])

        nt_c = nb * seq // tm
        ml, gates, na, rg, df = _proj_in(xc, mod, g1, p["w_r"], p["gate_b"], p["qkg"], None,
                                         row_base=0, row_div=nt_c, tm=tm)
        hf, hb, cn_c, m_c = _mlstm(ml, gates, jnp.zeros((nb, 2, NH, HD, 128), F32), jnp.zeros((nb, 8, 128), F32),
                                   nreq=nb, nc=seq // CHUNK)
        y_na, o_df = _ctx_attn(na, df, p["lamp"], lam_init, nreq=nb, n=seq)
        y_rg, rg_fin = _rglru(rg, rg_conv_w[l], cb, p["wbd"], p["rg_bias"], rg_lam[l],
                              jnp.zeros((nb, 2, GW), F32), nreq=nb, n=seq)
        x1_c, hn2_c, lg_c = _merge(hf, hb, ml, y_na, y_rg, o_df, xc, mod, mlg, p["sub"], p["wout"], g2, p["rw"],
                                   lam_init, row_base=0, row_div=nt_c, tm=tm)
        pos_c, aff_c = _route(lg_c, nreq=nb, n=seq, cap=cap_c)
        xs_c, w_c = _gather(pos_c, aff_c, hn2_c, nreq=nb, n=seq, cap=cap_c, eb=NE)
        ctx_out.append((na, df, cn_c, m_c, rg_fin))

        tiles_req = dseq // tm
        ml, gates, na, rg, df = _proj_in(xl, mod, g1, p["w_r"], p["gate_b"], p["qkg"], rope,
                                         row_base=1, row_div=tiles_req, tm=tm)
        cn0 = jnp.concatenate([state_ml_c[:, l], state_ml_n[:, l][..., None],
                               jnp.zeros((db, 2, NH, HD, 128 - HD - 1), F32)], axis=-1)
        m0 = jnp.broadcast_to(state_ml_m[:, l].reshape(db, 8, 1), (db, 8, 128))
        hf, hb, _, _ = _mlstm(ml, gates, cn0, m0, nreq=db, nc=dseq // CHUNK)
        y_na = _na_lat(na, cache_na_k[:, l].reshape(db, past, GW), cache_na_v[:, l].reshape(db, past, GW), tb,
                       nreq=db, n=dseq)
        o_df = _df_lat(df, cache_df_k[:, l].reshape(db, past, GW), cache_df_v[:, l].reshape(db, past, GW),
                       p["lamp"], lam_init, nreq=db, n=dseq)
        y_rg, _ = _rglru(rg, rg_conv_w[l], cb, p["wbd"], p["rg_bias"], rg_lam[l], state_rg_h[:, l],
                         nreq=db, n=dseq)
        x1_l, hn2_l, lg_l = _merge(hf, hb, ml, y_na, y_rg, o_df, xl, mod, mlg, p["sub"], p["wout"], g2, p["rw"],
                                   lam_init, row_base=1, row_div=tiles_req, tm=tm)
        pos_l, aff_l = _route(lg_l, nreq=db, n=dseq, cap=cap_l)
        xs_l, w_l = _gather(pos_l, aff_l, hn2_l, nreq=db, n=dseq, cap=cap_l, eb=1)

        y_c, y_l = _experts(xs_c, xs_l, w_c, w_l, moe_wg, moe_wu, moe_wd, l)
        xc = _scatter(pos_c, y_c, x1_c, mod, nreq=nb, n=seq, cap=cap_c, eb=NE, row_base=0, row_mul=0)
        xl = _scatter(pos_l, y_l, x1_l, mod, nreq=db, n=dseq, cap=cap_l, eb=1, row_base=1, row_mul=1)

    y_prompt = xc.reshape(nb, seq, D)
    y_sample = xl.reshape(db, dseq, D)
    st = lambda f: jnp.stack([f(o) for o in ctx_out], axis=1)
    na_k = st(lambda o: o[0][:, 256:512].reshape(nb, seq, NH, HD))
    na_v = st(lambda o: o[0][:, 512:768].reshape(nb, seq, NH, HD))
    df_k = st(lambda o: o[1][:, 256:512].reshape(nb, seq, NH, 2, HD // 2))
    df_v = st(lambda o: o[1][:, 512:768].reshape(nb, seq, NH, HD))
    ml_c = st(lambda o: o[2][..., 0:HD])
    ml_n = st(lambda o: o[2][..., HD])
    ml_m = st(lambda o: o[3][:, :, 0].reshape(nb, 2, NH))
    rg_h = st(lambda o: o[4])
    return (y_prompt, y_sample, na_k, na_v, df_k, df_v, ml_c, ml_n, ml_m, rg_h)
```

```python
import functools
import math

import numpy as np
import jax
import jax.numpy as jnp
from jax import lax
from jax.experimental import pallas as pl
from jax.experimental.pallas import tpu as pltpu

F32 = jnp.float32
BF16 = jnp.bfloat16

D = 1024
GW = 256
NH = 4
HD = 64
NE = 16
EPS = 1e-6
CHUNK = 128
GRID_W = 64
NA_ROWS = 8
NA_COLS = 16
RG_C = 8.0
ROPE_BASE = 10000.0
PROJ_PAD = 3200
VMEM_LIMIT_BYTES = 56 * 1024 * 1024
NEG_BIG = -1e30
LOG2E = 1.4426950408889634


def _bf(x):
    return x.astype(BF16)


def _dot(a, b):
    return jnp.dot(a, b, preferred_element_type=F32)


def _dot_nt(a, b):
    return lax.dot_general(a, b, (((1,), (1,)), ((), ())), preferred_element_type=F32)


def _dot_tn(a, b):
    return lax.dot_general(a, b, (((0,), (0,)), ((), ())), preferred_element_type=F32)


def _dot_f32(a, b):
    return jnp.dot(a, b, precision=lax.Precision.HIGHEST, preferred_element_type=F32)


def _params(n_axes):
    return pltpu.CompilerParams(dimension_semantics=("arbitrary",) * n_axes,
                                vmem_limit_bytes=VMEM_LIMIT_BYTES)


def _full(shape):
    return pl.BlockSpec(shape, lambda *_: (0,) * len(shape))


def _sds(shape, dtype=F32):
    return jax.ShapeDtypeStruct(shape, dtype)


def _softplus(x):
    return jnp.maximum(x, 0.0) + jnp.log1p(jnp.exp(-jnp.abs(x)))


def _log_sigmoid(x):
    return jnp.minimum(x, 0.0) - jnp.log1p(jnp.exp(-jnp.abs(x)))


def _seg_rms(x, nseg, g_row):
    seg = x.shape[-1] // nseg
    lane = lax.broadcasted_iota(jnp.int32, x.shape, 1)
    x2 = x * x
    tot = jnp.zeros_like(x)
    for s in range(nseg):
        m = (lane >= s * seg) & (lane < (s + 1) * seg)
        t = jnp.sum(jnp.where(m, x2, 0.0), axis=-1, keepdims=True)
        tot = jnp.where(m, t, tot)
    return x * lax.rsqrt(tot * (1.0 / seg) + EPS) * g_row


def _mod_row(pid, row_base, row_div):
    return row_base + pid // row_div


def _mod_kernel(cv_ref, w_ref, b_ref, o_ref):
    cv = cv_ref[...]
    s = cv * jax.nn.sigmoid(cv)
    o_ref[0] = _dot(_bf(s), _bf(w_ref[0])) + b_ref[0]


def _modulation(cv, w_mod, b_mod):
    nl = w_mod.shape[0]
    tn = 1536
    return pl.pallas_call(
        _mod_kernel,
        grid=(nl, 6 * D // tn),
        in_specs=[_full((8, D)),
                  pl.BlockSpec((1, D, tn), lambda l, j: (l, 0, j)),
                  pl.BlockSpec((1, 1, tn), lambda l, j: (l, 0, j))],
        out_specs=pl.BlockSpec((1, 8, tn), lambda l, j: (l, 0, j)),
        out_shape=_sds((nl, 8, 6 * D)),
        compiler_params=_params(2),
        name="modulation",
    )(cv, w_mod, b_mod.reshape(nl, 1, 6 * D))


def _rope(x, cos_t, sin_t):
    lane = lax.broadcasted_iota(jnp.int32, x.shape, 1)
    first = (lane & 15) < 8
    sw = jnp.where(first, pltpu.roll(x, GW - 8, axis=1), pltpu.roll(x, 8, axis=1))
    return x * cos_t + sw * sin_t


def _proj_kernel(*refs, row_base, row_div, use_rope):
    if use_rope:
        (x_ref, mod_ref, g1_ref, w_ref, gb_ref, qkg_ref, cos_ref, sin_ref,
         ml_ref, gate_ref, na_ref, rg_ref, df_ref) = refs
    else:
        (x_ref, mod_ref, g1_ref, w_ref, gb_ref, qkg_ref,
         ml_ref, gate_ref, na_ref, rg_ref, df_ref) = refs
    r = _mod_row(pl.program_id(0), row_base, row_div)
    x = x_ref[...]
    ms = jnp.mean(x * x, axis=-1, keepdims=True)
    y = x * lax.rsqrt(ms + EPS) * g1_ref[...]
    sh = mod_ref[pl.ds(r, 1), 0:D]
    sc = mod_ref[pl.ds(r, 1), D:2 * D]
    hn = _bf(y * (1.0 + sc) + sh)

    ml = _dot(hn, w_ref[:, 0:1024])
    ml_ref[:, 0:256] = ml[:, 0:256]
    ml_ref[:, 256:512] = ml[:, 256:512] * (HD ** -0.5)
    ml_ref[:, 512:1024] = ml[:, 512:1024]

    gz = _dot(hn, w_ref[:, 3072:3200]) + gb_ref[...]
    lane = lax.broadcasted_iota(jnp.int32, gz.shape, 1)
    gate_ref[...] = jnp.where(((lane >> 2) & 1) == 1, _log_sigmoid(gz), gz)

    nz = _dot(hn, w_ref[:, 1024:1792])
    na_ref[:, 0:256] = _seg_rms(nz[:, 0:256], NH, qkg_ref[0:1, :])
    na_ref[:, 256:512] = _seg_rms(nz[:, 256:512], NH, qkg_ref[1:2, :])
    na_ref[:, 512:768] = nz[:, 512:768]

    rg_ref[...] = _dot(hn, w_ref[:, 1792:2304])

    dz = _dot(hn, w_ref[:, 2304:3072])
    dq = _seg_rms(dz[:, 0:256], 2 * NH, qkg_ref[2:3, :])
    dk = _seg_rms(dz[:, 256:512], 2 * NH, qkg_ref[3:4, :])
    if use_rope:
        cos_t = cos_ref[...]
        sin_t = sin_ref[...]
        dq = _rope(dq, cos_t, sin_t)
        dk = _rope(dk, cos_t, sin_t)
    df_ref[:, 0:256] = dq
    df_ref[:, 256:512] = dk
    df_ref[:, 512:768] = dz[:, 512:768]


def _proj_in(x, mod, g1, w_r, gate_b, qkg, rope, *, row_base, row_div, tm=512):
    tp = x.shape[0]
    use_rope = rope is not None
    in_specs = [pl.BlockSpec((tm, D), lambda i: (i, 0)), _full((8, 6 * D)), _full((1, D)),
                _full((D, PROJ_PAD)), _full((1, 128)), _full((4, GW))]
    args = [x, mod, g1, w_r, gate_b, qkg]
    if use_rope:
        tiles = rope[0].shape[0] // tm
        in_specs += [pl.BlockSpec((tm, GW), lambda i: (i % tiles, 0))] * 2
        args += [rope[0], rope[1]]
    widths = (1024, 128, 768, 512, 768)
    return pl.pallas_call(
        functools.partial(_proj_kernel, row_base=row_base, row_div=row_div, use_rope=use_rope),
        grid=(tp // tm,),
        in_specs=in_specs,
        out_specs=[pl.BlockSpec((tm, n), lambda i: (i, 0)) for n in widths],
        out_shape=[_sds((tp, n)) for n in widths],
        compiler_params=_params(1),
        name="proj_in",
    )(*args)


def _mlstm_dir(d, q_ref, k_ref, v_ref, g_ref, h_ref, st_s, m_s):
    lc = CHUNK
    g = g_ref[...]
    gt = g.T
    row = lax.broadcasted_iota(jnp.int32, (lc, lc), 0)
    col = lax.broadcasted_iota(jnp.int32, (lc, lc), 1)
    tri = (row >= col) if d == 0 else (row <= col)
    tri_t = (col >= row) if d == 0 else (col <= row)
    bc_col = _dot_f32(tri.astype(F32), g)
    bc_row = _dot_f32(gt, tri_t.astype(F32))
    r_rows = gt[d * 8:d * 8 + NH, :] - bc_row[d * 8 + NH:d * 8 + 2 * NH, :]
    k_t = k_ref[...].T
    k_tb = _bf(k_t)
    q_all = _bf(q_ref[...])
    v_all = _bf(v_ref[...])
    ones = jnp.ones((lc, 128), BF16)
    last = lc - 1 if d == 0 else 0
    hs = [slice(h * HD, (h + 1) * HD) for h in range(NH)]
    ms = [m_s[d * NH + h:d * NH + h + 1, 0:1] for h in range(NH)]
    rms = [jnp.where(tri, r_rows[h:h + 1, :], -jnp.inf) for h in range(NH)]
    big_rs = [jnp.maximum(jnp.max(rms[h], axis=-1, keepdims=True), ms[h]) for h in range(NH)]
    ss = [_bf(_dot(q_all[:, hs[h]], k_tb[hs[h], :]) * jnp.exp(rms[h] - big_rs[h])) for h in range(NH)]
    outs = []
    for h in range(NH):
        v2 = v_all[:, (h // 2) * 128:(h // 2 + 1) * 128]
        e = jnp.exp(ms[h] - big_rs[h])
        qh = q_all[:, hs[h]]
        tot = _dot(ss[h], v2) + e * _dot(qh, _bf(st_s[d, h, 0]))
        den = _dot(ss[h], ones) + e * _dot(qh, _bf(st_s[d, h, 1]))
        c_col = bc_col[:, d * 8 + NH + h:d * 8 + NH + h + 1]
        floor = jnp.exp(-(c_col + big_rs[h]))
        outs.append(tot / jnp.maximum(jnp.abs(den), floor))

        r_last = big_rs[h][last:last + 1, :]
        kw = _bf(k_t[hs[h], :] * jnp.exp(r_rows[h:h + 1, :] - r_last))
        gdec = jnp.exp(ms[h] - r_last)
        st_s[d, h, 0] = gdec * st_s[d, h, 0] + _dot(kw, v2)
        st_s[d, h, 1] = gdec * st_s[d, h, 1] + _dot(kw, ones)
        m_s[d * NH + h:d * NH + h + 1, :] = jnp.broadcast_to(c_col[last:last + 1, :] + r_last, (1, 128))
    lane = lax.broadcasted_iota(jnp.int32, (lc, 128), 1)
    h_ref[...] = jnp.concatenate([jnp.where(lane < HD, outs[2 * p], outs[2 * p + 1]) for p in range(NH // 2)],
                                 axis=-1)


def _mlstm_kernel(qf, kf, vf, gf, qb, kb, vb, gb, st0_ref, m0_ref,
                  hf_ref, hb_ref, st_out, m_out, st_s, m_s):
    j = pl.program_id(1)

    @pl.when(j == 0)
    def _():
        st_s[...] = st0_ref[0]
        m_s[...] = m0_ref[0]

    _mlstm_dir(0, qf, kf, vf, gf, hf_ref, st_s, m_s)
    _mlstm_dir(1, qb, kb, vb, gb, hb_ref, st_s, m_s)

    @pl.when(j == pl.num_programs(1) - 1)
    def _():
        st_out[0] = st_s[...]
        m_out[0] = m_s[...]


def _mlstm_pack_state(c, n):
    z = jnp.zeros_like(c)
    odd = (jnp.arange(NH) % 2 == 1)[None, None, :, None, None]
    c_pair = jnp.where(odd, jnp.concatenate([z, c], axis=-1), jnp.concatenate([c, z], axis=-1))
    n_rep = jnp.broadcast_to(n[..., None], n.shape + (128,))
    return jnp.stack([c_pair, n_rep], axis=3)


def _mlstm_unpack_state(st):
    odd = (jnp.arange(NH) % 2 == 1)[None, None, :, None, None]
    c = jnp.where(odd, st[:, :, :, 0, :, HD:2 * HD], st[:, :, :, 0, :, 0:HD])
    return c, st[:, :, :, 1, :, 0]


def _mlstm(mlz, gates, st0, m0, *, nreq, nc):
    tp = mlz.shape[0]
    st_spec = pl.BlockSpec((1, 2, NH, 2, HD, 128), lambda r, j: (r, 0, 0, 0, 0, 0))

    def fwd(col):
        return lambda r, j: (r * nc + j, col)

    def bwd(col):
        return lambda r, j: (r * nc + nc - 1 - j, col)

    in_specs = []
    for mk in (fwd, bwd):
        in_specs += [pl.BlockSpec((CHUNK, GW), mk(0)), pl.BlockSpec((CHUNK, GW), mk(1)),
                     pl.BlockSpec((CHUNK, GW), mk(2)), pl.BlockSpec((CHUNK, 128), mk(0))]
    in_specs += [st_spec, pl.BlockSpec((1, 8, 128), lambda r, j: (r, 0, 0))]
    return pl.pallas_call(
        _mlstm_kernel,
        grid=(nreq, nc),
        in_specs=in_specs,
        out_specs=[pl.BlockSpec((CHUNK, GW), fwd(0)), pl.BlockSpec((CHUNK, GW), bwd(0)),
                   st_spec, pl.BlockSpec((1, 8, 128), lambda r, j: (r, 0, 0))],
        out_shape=[_sds((tp, GW)), _sds((tp, GW)), _sds((nreq, 2, NH, 2, HD, 128)), _sds((nreq, 8, 128))],
        scratch_shapes=[pltpu.VMEM((2, NH, 2, HD, 128), F32), pltpu.VMEM((8, 128), F32)],
        compiler_params=_params(2),
        name="mlstm",
    )(mlz, mlz, mlz, gates, mlz, mlz, mlz, gates, st0, m0)


def _diff_lambda(lam_ref, lam_init):
    lp = lam_ref[...]
    a = jnp.exp(jnp.sum(lp[0:1, :] * lp[1:2, :], axis=-1, keepdims=True))
    b = jnp.exp(jnp.sum(lp[2:3, :] * lp[3:4, :], axis=-1, keepdims=True))
    return a - b + lam_init


def _softmax_pv(q, k_t, v):
    s = _dot(q, k_t)
    e = jnp.exp2(s - jnp.max(s, axis=-1, keepdims=True))
    return _dot(_bf(e), v) / jnp.sum(e, axis=-1, keepdims=True)


def _ctx_attn_kernel(nq, nk, nv, dq, dk, dv, lam_ref, ona_ref, odf_ref, *, lam_init):
    lam = _diff_lambda(lam_ref, lam_init)
    nk_t = _bf(nk[...].T)
    dk_t = _bf(dk[...].T)
    na_out, df_out = [], []
    for h in range(NH):
        hs = slice(h * HD, (h + 1) * HD)
        na_out.append(_softmax_pv(_bf(nq[:, hs] * (HD ** -0.5 * LOG2E)), nk_t[hs, :], _bf(nv[:, hs])))
        vh = _bf(dv[:, hs])
        os = []
        for c in range(2):
            cs = slice(h * HD + c * 32, h * HD + (c + 1) * 32)
            os.append(_softmax_pv(_bf(dq[:, cs] * (32 ** -0.5 * LOG2E)), dk_t[cs, :], vh))
        df_out.append(os[0] - lam * os[1])
    ona_ref[...] = jnp.concatenate(na_out, axis=-1)
    odf_ref[...] = jnp.concatenate(df_out, axis=-1)


def _ctx_attn(na, df, lamp, lam_init, *, nreq, n):
    tp = na.shape[0]
    blk = lambda c: pl.BlockSpec((n, GW), lambda b: (b, c))
    return pl.pallas_call(
        functools.partial(_ctx_attn_kernel, lam_init=lam_init),
        grid=(nreq,),
        in_specs=[blk(0), blk(1), blk(2), blk(0), blk(1), blk(2), _full((4, 32))],
        out_specs=[blk(0), blk(0)],
        out_shape=[_sds((tp, GW)), _sds((tp, GW))],
        compiler_params=_params(1),
        name="ctx_attn",
    )(na, na, na, df, df, df, lamp)


def _rpb_kernel(rpb_ref, o_ref):
    h = pl.program_id(0)
    s = pl.program_id(1)
    shape = (GRID_W, NA_ROWS * GRID_W)
    lane = lax.broadcasted_iota(jnp.int32, shape, 1)
    cq = lax.broadcasted_iota(jnp.int32, shape, 0)
    ck = lane & (GRID_W - 1)
    dc = jnp.clip(ck - cq, -(NA_COLS - 1), NA_COLS - 1) + (NA_COLS - 1)
    c0 = jnp.clip(cq - NA_COLS // 2, 0, GRID_W - NA_COLS)
    ok = (ck >= c0) & (ck < c0 + NA_COLS)
    jrow = lax.broadcasted_iota(jnp.int32, (1, shape[1]), 1) >> 6
    ncol = 2 * NA_COLS - 1
    acc = jnp.zeros(shape, F32)
    for b in range(ncol):
        vrow = jnp.zeros((1, shape[1]), F32)
        for j in range(NA_ROWS):
            val = rpb_ref[(h * (2 * NA_ROWS - 1) + s + j) * ncol + b]
            vrow = jnp.where(jrow == j, val, vrow)
        acc = jnp.where(dc == b, vrow, acc)
    o_ref[0, 0] = jnp.where(ok, acc * LOG2E, NEG_BIG)


def _rpb_table(rpb):
    return pl.pallas_call(
        _rpb_kernel,
        grid=(NH, NA_ROWS),
        in_specs=[pl.BlockSpec(memory_space=pltpu.SMEM)],
        out_specs=pl.BlockSpec((1, 1, GRID_W, NA_ROWS * GRID_W), lambda h, s: (h, s, 0, 0)),
        out_shape=_sds((NH, NA_ROWS, GRID_W, NA_ROWS * GRID_W)),
        compiler_params=_params(2),
        name="rpb_table",
    )(rpb.reshape(-1))


def _na_lat_kernel(q_ref, k_ref, v_ref, kc_ref, vc_ref, tb_ref, o_ref, k_s, v_s, kct_s, vc_s, *, rows, rps):
    @pl.when(pl.program_id(1) == 0)
    def _():
        k_s[...] = _bf(k_ref[...])
        v_s[...] = _bf(v_ref[...])
        kct_s[...] = _bf(kc_ref[0].T)
        vc_s[...] = _bf(vc_ref[0])

    win = NA_ROWS * GRID_W
    for a in range(rps):
        r = pl.program_id(1) * rps + a
        r0 = jnp.clip(r - NA_ROWS // 2, 0, rows - NA_ROWS)
        sidx = r0 - r + (NA_ROWS - 1)
        start = pl.multiple_of(r0 * GRID_W, GRID_W)
        outs = []
        for h in range(NH):
            hs = slice(h * HD, (h + 1) * HD)
            q = _bf(q_ref[a * GRID_W:(a + 1) * GRID_W, hs] * (HD ** -0.5 * LOG2E))
            s_c = _dot(q, kct_s[hs, :])
            s_l = _dot_nt(q, k_s[pl.ds(start, win), hs]) + tb_ref[h, sidx]
            m = jnp.maximum(jnp.max(s_c, axis=-1, keepdims=True), jnp.max(s_l, axis=-1, keepdims=True))
            e_c = jnp.exp2(s_c - m)
            e_l = jnp.exp2(s_l - m)
            den = jnp.sum(e_c, axis=-1, keepdims=True) + jnp.sum(e_l, axis=-1, keepdims=True)
            o = _dot(_bf(e_c), vc_s[:, hs]) + _dot(_bf(e_l), v_s[pl.ds(start, win), hs])
            outs.append(o / den)
        o_ref[a * GRID_W:(a + 1) * GRID_W, :] = jnp.concatenate(outs, axis=-1)


def _na_lat(na, kc, vc, tb, *, nreq, n, rps=4):
    tp = na.shape[0]
    rows = n // GRID_W
    past = kc.shape[1]
    steps = rows // rps
    tq = rps * GRID_W
    return pl.pallas_call(
        functools.partial(_na_lat_kernel, rows=rows, rps=rps),
        grid=(nreq, steps),
        in_specs=[pl.BlockSpec((tq, GW), lambda b, r: (b * steps + r, 0)),
                  pl.BlockSpec((n, GW), lambda b, r: (b, 1)),
                  pl.BlockSpec((n, GW), lambda b, r: (b, 2)),
                  pl.BlockSpec((1, past, GW), lambda b, r: (b, 0, 0)),
                  pl.BlockSpec((1, past, GW), lambda b, r: (b, 0, 0)),
                  _full(tb.shape)],
        out_specs=pl.BlockSpec((tq, GW), lambda b, r: (b * steps + r, 0)),
        out_shape=_sds((tp, GW)),
        scratch_shapes=[pltpu.VMEM((n, GW), BF16), pltpu.VMEM((n, GW), BF16),
                        pltpu.VMEM((GW, past), BF16), pltpu.VMEM((past, GW), BF16)],
        compiler_params=_params(2),
        name="na_latent",
    )(na, na, na, kc, vc, tb)


def _df_lat_kernel(q_ref, k_ref, v_ref, kc_ref, vc_ref, lam_ref, o_ref, kt_s, v_s, kct_s, vc_s, *, lam_init):
    @pl.when(pl.program_id(1) == 0)
    def _():
        kt_s[...] = _bf(k_ref[...].T)
        v_s[...] = _bf(v_ref[...])
        kct_s[...] = _bf(kc_ref[0].T)
        vc_s[...] = _bf(vc_ref[0])

    lam = _diff_lambda(lam_ref, lam_init)
    outs = []
    for h in range(NH):
        hs = slice(h * HD, (h + 1) * HD)
        os = []
        for c in range(2):
            cs = slice(h * HD + c * 32, h * HD + (c + 1) * 32)
            q = _bf(q_ref[:, cs] * (32 ** -0.5 * LOG2E))
            s_c = _dot(q, kct_s[cs, :])
            s_l = _dot(q, kt_s[cs, :])
            m = jnp.maximum(jnp.max(s_c, axis=-1, keepdims=True), jnp.max(s_l, axis=-1, keepdims=True))
            e_c = jnp.exp2(s_c - m)
            e_l = jnp.exp2(s_l - m)
            den = jnp.sum(e_c, axis=-1, keepdims=True) + jnp.sum(e_l, axis=-1, keepdims=True)
            os.append((_dot(_bf(e_c), vc_s[:, hs]) + _dot(_bf(e_l), v_s[:, hs])) / den)
        outs.append(os[0] - lam * os[1])
    o_ref[...] = jnp.concatenate(outs, axis=-1)


def _df_lat(df, kc, vc, lamp, lam_init, *, nreq, n, tq=256):
    tp = df.shape[0]
    nq = n // tq
    past = kc.shape[1]
    return pl.pallas_call(
        functools.partial(_df_lat_kernel, lam_init=lam_init),
        grid=(nreq, nq),
        in_specs=[pl.BlockSpec((tq, GW), lambda b, j: (b * nq + j, 0)),
                  pl.BlockSpec((n, GW), lambda b, j: (b, 1)),
                  pl.BlockSpec((n, GW), lambda b, j: (b, 2)),
                  pl.BlockSpec((1, past, GW), lambda b, j: (b, 0, 0)),
                  pl.BlockSpec((1, past, GW), lambda b, j: (b, 0, 0)),
                  _full((4, 32))],
        out_specs=pl.BlockSpec((tq, GW), lambda b, j: (b * nq + j, 0)),
        out_shape=_sds((tp, GW)),
        scratch_shapes=[pltpu.VMEM((GW, n), BF16), pltpu.VMEM((n, GW), BF16),
                        pltpu.VMEM((GW, past), BF16), pltpu.VMEM((past, GW), BF16)],
        compiler_params=_params(2),
        name="df_latent",
    )(df, df, df, kc, vc, lamp)


def _rg_kernel(x_ref, g_ref, cw_ref, cb_ref, wbd_ref, bias_ref, lam_ref, h0_ref,
               y_ref, fin_ref, a_s, b_s, *, n):
    x = x_ref[...]
    row = lax.broadcasted_iota(jnp.int32, x.shape, 0)
    xc = cb_ref[...] + jnp.where(row >= 2, pltpu.roll(x, 2, axis=0), 0.0) * cw_ref[0:1, :]
    xc = xc + jnp.where(row >= 1, pltpu.roll(x, 1, axis=0), 0.0) * cw_ref[1:2, :]
    xc = xc + x * cw_ref[2:3, :]
    xc = xc + jnp.where(row < n - 1, pltpu.roll(x, n - 1, axis=0), 0.0) * cw_ref[3:4, :]
    z = _dot(_bf(xc), wbd_ref[...]) + bias_ref[...]
    sub = row & 7
    for d in range(2):
        rgate = jax.nn.sigmoid(z[:, 512 * d:512 * d + GW])
        igate = jax.nn.sigmoid(z[:, 512 * d + GW:512 * d + 2 * GW])
        la = -RG_C * rgate * _softplus(-lam_ref[d:d + 1, :])
        a = jnp.exp(la)
        t = jnp.tanh(la)
        b = jnp.sqrt(-2.0 * t / (1.0 - t)) * igate * xc
        for dd in (1, 2, 4):
            if d == 0:
                keep = sub >= dd
                a_sh = jnp.where(keep, pltpu.roll(a, dd, axis=0), 1.0)
                b_sh = jnp.where(keep, pltpu.roll(b, dd, axis=0), 0.0)
            else:
                keep = sub < 8 - dd
                a_sh = jnp.where(keep, pltpu.roll(a, n - dd, axis=0), 1.0)
                b_sh = jnp.where(keep, pltpu.roll(b, n - dd, axis=0), 0.0)
            b = b + a * b_sh
            a = a * a_sh
        a_s[d] = a
        b_s[d] = b

    nt = n // 8

    def body(t, carry):
        hf, hb = carry
        sf = pl.multiple_of(t * 8, 8)
        sb = pl.multiple_of((nt - 1 - t) * 8, 8)
        tf = a_s[0, pl.ds(sf, 8), :] * hf + b_s[0, pl.ds(sf, 8), :]
        tb = a_s[1, pl.ds(sb, 8), :] * hb + b_s[1, pl.ds(sb, 8), :]
        b_s[0, pl.ds(sf, 8), :] = tf
        b_s[1, pl.ds(sb, 8), :] = tb
        return tf[7:8, :], tb[0:1, :]

    hf, hb = lax.fori_loop(0, nt, body, (h0_ref[0, 0:1, :], h0_ref[0, 1:2, :]))
    fin_ref[0, 0:1, :] = hf
    fin_ref[0, 1:2, :] = hb
    gg = g_ref[...]
    cdf = 0.5 * (1.0 + jnp.tanh(math.sqrt(2.0 / math.pi) * (gg + 0.044715 * (gg * gg * gg))))
    y_ref[...] = (b_s[0] + b_s[1]) * (gg * cdf)


def _rglru(rg, cw, cb, wbd, bias, lam, h0, *, nreq, n):
    tp = rg.shape[0]
    return pl.pallas_call(
        functools.partial(_rg_kernel, n=n),
        grid=(nreq,),
        in_specs=[pl.BlockSpec((n, GW), lambda b: (b, 0)), pl.BlockSpec((n, GW), lambda b: (b, 1)),
                  _full((4, GW)), _full((1, GW)), _full((GW, 4 * GW)), _full((1, 4 * GW)), _full((2, GW)),
                  pl.BlockSpec((1, 2, GW), lambda b: (b, 0, 0))],
        out_specs=[pl.BlockSpec((n, GW), lambda b: (b, 0)), pl.BlockSpec((1, 2, GW), lambda b: (b, 0, 0))],
        out_shape=[_sds((tp, GW)), _sds((nreq, 2, GW))],
        scratch_shapes=[pltpu.VMEM((2, n, GW), F32), pltpu.VMEM((2, n, GW), F32)],
        compiler_params=_params(1),
        name="rglru",
    )(rg, rg, cw, cb, wbd, bias, lam, h0)


def _merge_kernel(hf_ref, hb_ref, mlo_ref, yna_ref, yrg_ref, odf_ref, x_ref, mod_ref, mlg_ref, sub_ref,
                  wout_ref, n2_ref, rw_ref, x1_ref, hn2_ref, lg_ref, *, row_base, row_div, lam_init):
    r = _mod_row(pl.program_id(0), row_base, row_div)
    y_ml = _seg_rms(hf_ref[...] + hb_ref[...], NH, mlg_ref[...]) * jax.nn.sigmoid(mlo_ref[...])
    y_df = _seg_rms(odf_ref[...], NH, sub_ref[...]) * (1.0 - lam_init)
    y = jnp.concatenate([_bf(y_ml), _bf(yna_ref[...]), _bf(yrg_ref[...]), _bf(y_df)], axis=-1)
    o = _dot(y, wout_ref[...])
    x1 = x_ref[...] + mod_ref[pl.ds(r, 1), 2 * D:3 * D] * o
    x1_ref[...] = x1
    ms = jnp.mean(x1 * x1, axis=-1, keepdims=True)
    hn = x1 * lax.rsqrt(ms + EPS) * n2_ref[...]
    hn = _bf(hn * (1.0 + mod_ref[pl.ds(r, 1), 4 * D:5 * D]) + mod_ref[pl.ds(r, 1), 3 * D:4 * D])
    hn2_ref[...] = hn
    lg_ref[...] = _dot(hn, rw_ref[...])


def _merge(hf, hb, mlz, yna, yrg, odf, x, mod, mlg, sub, wout, n2, rw, lam_init, *, row_base, row_div, tm=512):
    tp = x.shape[0]
    g = lambda c: pl.BlockSpec((tm, GW), lambda i: (i, c))
    return pl.pallas_call(
        functools.partial(_merge_kernel, row_base=row_base, row_div=row_div, lam_init=lam_init),
        grid=(tp // tm,),
        in_specs=[g(0), g(0), g(3), g(0), g(0), g(0),
                  pl.BlockSpec((tm, D), lambda i: (i, 0)), _full((8, 6 * D)), _full((1, GW)), _full((1, GW)),
                  _full((D, D)), _full((1, D)), _full((D, 128))],
        out_specs=[pl.BlockSpec((tm, D), lambda i: (i, 0)), pl.BlockSpec((tm, D), lambda i: (i, 0)),
                   pl.BlockSpec((tm, 128), lambda i: (i, 0))],
        out_shape=[_sds((tp, D)), _sds((tp, D), BF16), _sds((tp, 128))],
        compiler_params=_params(1),
        name="merge",
    )(hf, hb, mlz, yna, yrg, odf, x, mod, mlg, sub, wout, n2, rw)


def _excl_cumsum_lanes(mask):
    blk = 256
    r = lax.broadcasted_iota(jnp.int32, (blk, blk), 0)
    c = lax.broadcasted_iota(jnp.int32, (blk, blk), 1)
    tri = (r < c).astype(F32).astype(BF16)
    off = jnp.zeros((mask.shape[0], 1), F32)
    outs = []
    for i in range(mask.shape[1] // blk):
        mb = mask[:, i * blk:(i + 1) * blk]
        outs.append(_dot(_bf(mb), tri) + off)
        off = off + jnp.sum(mb, axis=-1, keepdims=True)
    return jnp.concatenate(outs, axis=-1)


def _route_kernel(lg_ref, pos_ref, aff_ref, *, cap, n, rb):
    lg = lg_ref[...].T[0:NE, :]
    ex = jnp.exp(lg - jnp.max(lg, axis=0, keepdims=True))
    aff = ex / jnp.sum(ex, axis=0, keepdims=True)
    aff_ref[...] = aff
    aff = jnp.concatenate([aff[:, i * n:(i + 1) * n] for i in range(rb)], axis=0)
    bits = pltpu.bitcast(aff, jnp.int32)
    thr = jnp.zeros((rb * NE, 1), jnp.int32)
    for bit in range(30, -1, -1):
        cand = thr | (1 << bit)
        cnt = jnp.sum((bits >= cand).astype(jnp.int32), axis=-1, keepdims=True)
        thr = jnp.where(cnt >= cap, cand, thr)
    gt = bits > thr
    eq = bits == thr
    need = (cap - jnp.sum(gt.astype(jnp.int32), axis=-1, keepdims=True)).astype(F32)
    eq_rank = _excl_cumsum_lanes(eq.astype(F32))
    sel = gt | (eq & (eq_rank < need))
    slot = _excl_cumsum_lanes(sel.astype(F32))
    pos = jnp.where(sel, slot.astype(jnp.int32), -1)
    for i in range(rb):
        pos_ref[:, i * n:(i + 1) * n] = pos[i * NE:(i + 1) * NE, :]


def _route(lg, *, nreq, n, cap, rb):
    tp = lg.shape[0]
    return pl.pallas_call(
        functools.partial(_route_kernel, cap=cap, n=n, rb=rb),
        grid=(nreq // rb,),
        in_specs=[pl.BlockSpec((rb * n, 128), lambda b: (b, 0))],
        out_specs=[pl.BlockSpec((NE, rb * n), lambda b: (0, b)), pl.BlockSpec((NE, rb * n), lambda b: (0, b))],
        out_shape=[_sds((NE, tp), jnp.int32), _sds((NE, tp))],
        compiler_params=_params(1),
        name="route",
    )(lg)


def _gather_kernel(pos_ref, aff_ref, h_ref, xs_ref, w_ref, *, eb, cap, n):
    eg = pl.program_id(1)
    io = lax.broadcasted_iota(jnp.int32, (cap, n), 0)
    sels = []
    for k in range(eb):
        e = eg * eb + k
        sel = pos_ref[pl.ds(e, 1), :] == io
        sels.append(sel.astype(F32).astype(BF16))
        w = jnp.sum(jnp.where(sel, aff_ref[pl.ds(e, 1), :], 0.0), axis=-1, keepdims=True)
        w_ref[k] = jnp.broadcast_to(w, (cap, 128))
    xs = _dot(jnp.concatenate(sels, axis=0), h_ref[...]).astype(BF16)
    for k in range(eb):
        xs_ref[k] = xs[k * cap:(k + 1) * cap, :]


def _gather(pos, aff, hn2, *, nreq, n, cap, eb):
    return pl.pallas_call(
        functools.partial(_gather_kernel, eb=eb, cap=cap, n=n),
        grid=(nreq, NE // eb),
        in_specs=[pl.BlockSpec((NE, n), lambda b, g: (0, b)), pl.BlockSpec((NE, n), lambda b, g: (0, b)),
                  pl.BlockSpec((n, D), lambda b, g: (b, 0))],
        out_specs=[pl.BlockSpec((eb, cap, D), lambda b, g: (g, b, 0)),
                   pl.BlockSpec((eb, cap, 128), lambda b, g: (g, b, 0))],
        out_shape=[_sds((NE, nreq * cap, D), BF16), _sds((NE, nreq * cap, 128))],
        compiler_params=_params(2),
        name="moe_gather",
    )(pos, aff, hn2)


def _expert_kernel(xc_ref, xl_ref, wc_ref, wl_ref, wg_ref, wu_ref, wd_ref, yc_ref, yl_ref, acc_ref, *, rows):
    f = pl.program_id(1)

    @pl.when(f == 0)
    def _():
        acc_ref[...] = jnp.zeros_like(acc_ref)

    wg = _bf(wg_ref[0, 0])
    wu = _bf(wu_ref[0, 0])
    wd = _bf(wd_ref[0, 0])
    tm = 512
    for part, x_ref in enumerate((xc_ref, xl_ref)):
        for ch in range(rows // tm):
            x = x_ref[0, ch * tm:(ch + 1) * tm, :]
            g = _dot(x, wg)
            u = _dot(x, wu)
            a = _bf(g * jax.nn.sigmoid(g) * u)
            lo = part * rows + ch * tm
            acc_ref[lo:lo + tm, :] += _dot(a, wd)

    @pl.when(f == pl.num_programs(1) - 1)
    def _():
        yc_ref[0] = _bf(acc_ref[0:rows, :] * wc_ref[0][:, 0:1])
        yl_ref[0] = _bf(acc_ref[rows:2 * rows, :] * wl_ref[0][:, 0:1])


def _experts(xs_c, xs_l, w_c, w_l, wg, wu, wd, layer, *, fb=512):
    rows = xs_c.shape[1]
    dff = wg.shape[-1]
    xspec = pl.BlockSpec((1, rows, D), lambda e, f: (e, 0, 0))
    wspec = pl.BlockSpec((1, rows, 128), lambda e, f: (e, 0, 0))
    return pl.pallas_call(
        functools.partial(_expert_kernel, rows=rows),
        grid=(NE, dff // fb),
        in_specs=[xspec, xspec, wspec, wspec,
                  pl.BlockSpec((1, 1, D, fb), lambda e, f: (layer, e, 0, f)),
                  pl.BlockSpec((1, 1, D, fb), lambda e, f: (layer, e, 0, f)),
                  pl.BlockSpec((1, 1, fb, D), lambda e, f: (layer, e, f, 0))],
        out_specs=[xspec, xspec],
        out_shape=[_sds((NE, rows, D), BF16), _sds((NE, rows, D), BF16)],
        scratch_shapes=[pltpu.VMEM((2 * rows, D), F32)],
        compiler_params=_params(2),
        name="moe_experts",
    )(xs_c, xs_l, w_c, w_l, wg, wu, wd)


def _scatter_kernel(pos_ref, y_ref, x1_ref, mod_ref, o_ref, acc_ref, *, eb, cap, n, row_base, row_mul):
    eg = pl.program_id(1)

    @pl.when(eg == 0)
    def _():
        acc_ref[...] = jnp.zeros_like(acc_ref)

    io = lax.broadcasted_iota(jnp.int32, (cap, n), 0)
    sels = [(pos_ref[pl.ds(eg * eb + k, 1), :] == io).astype(F32).astype(BF16) for k in range(eb)]
    ys = [y_ref[k] for k in range(eb)]
    acc_ref[...] += _dot_tn(jnp.concatenate(sels, axis=0), jnp.concatenate(ys, axis=0))

    @pl.when(eg == pl.num_programs(1) - 1)
    def _():
        r = row_base + row_mul * pl.program_id(0)
        o_ref[...] = x1_ref[...] + mod_ref[pl.ds(r, 1), 5 * D:6 * D] * acc_ref[...]


def _scatter(pos, y, x1, mod, *, nreq, n, cap, eb, row_base, row_mul):
    tp = x1.shape[0]
    return pl.pallas_call(
        functools.partial(_scatter_kernel, eb=eb, cap=cap, n=n, row_base=row_base, row_mul=row_mul),
        grid=(nreq, NE // eb),
        in_specs=[pl.BlockSpec((NE, n), lambda b, g: (0, b)),
                  pl.BlockSpec((eb, cap, D), lambda b, g: (g, b, 0)),
                  pl.BlockSpec((n, D), lambda b, g: (b, 0)), _full((8, 6 * D))],
        out_specs=pl.BlockSpec((n, D), lambda b, g: (b, 0)),
        out_shape=_sds((tp, D)),
        scratch_shapes=[pltpu.VMEM((n, D), F32)],
        compiler_params=_params(2),
        name="moe_scatter",
    )(pos, y, x1, mod)


def _rope_tables(n):
    nf = 8
    t = np.arange(n)
    rowp = (t // GRID_W).astype(np.float32)
    colp = (t % GRID_W).astype(np.float32)
    inv = (np.float32(ROPE_BASE) ** (-np.arange(nf, dtype=np.float32) / np.float32(nf))).astype(np.float32)
    lane = np.arange(GW)
    c32 = lane % 32
    pos = np.where((c32 < 16)[None, :], rowp[:, None], colp[:, None]).astype(np.float32)
    ang = (pos * inv[(c32 % 8)][None, :]).astype(np.float32).astype(np.float64)
    sign = np.where((lane % 16) < 8, -1.0, 1.0)[None, :]
    return jnp.asarray(np.cos(ang), F32), jnp.asarray(np.sin(ang) * sign, F32)


def _block_diag(w):
    nb, bi, bo = w.shape
    return (jnp.eye(nb, dtype=w.dtype)[:, None, :, None] * w[:, :, None, :]).reshape(nb * bi, nb * bo)


def _layer_params(l, w_in, ml_gate_b, na_qn_g, na_kn_g, df_qn_g, df_kn_g, rg_wa, rg_wx, rg_ba, rg_bx,
                  df_lq1, df_lk1, df_lq2, df_lk2, df_subln_g, w_out, router_w):
    wi = w_in[l]
    w_r = jnp.concatenate([wi[:, 0:1024], wi[:, 1040:3088], wi[:, 1024:1040],
                           jnp.zeros((D, PROJ_PAD - 3088), F32)], axis=1).astype(BF16)
    gate_b = jnp.pad(ml_gate_b[l], (0, 128 - 16)).reshape(1, 128)
    qkg = jnp.stack([jnp.tile(na_qn_g[l], NH), jnp.tile(na_kn_g[l], NH),
                     jnp.tile(df_qn_g[l], 2 * NH), jnp.tile(df_kn_g[l], 2 * NH)])
    wbd = jnp.concatenate([_block_diag(rg_wa[l, 0]), _block_diag(rg_wx[l, 0]),
                           _block_diag(rg_wa[l, 1]), _block_diag(rg_wx[l, 1])], axis=1).astype(BF16)
    rg_bias = jnp.concatenate([rg_ba[l, 0], rg_bx[l, 0], rg_ba[l, 1], rg_bx[l, 1]]).reshape(1, 4 * GW)
    lamp = jnp.stack([df_lq1[l], df_lk1[l], df_lq2[l], df_lk2[l]])
    sub = jnp.tile(df_subln_g[l], NH).reshape(1, GW)
    rw = jnp.pad(router_w[l], ((0, 0), (0, 128 - NE))).astype(BF16)
    return dict(w_r=w_r, gate_b=gate_b, qkg=qkg, wbd=wbd, rg_bias=rg_bias, lamp=lamp, sub=sub,
                wout=w_out[l].astype(BF16), rw=rw)


def kernel(x_prompt, x_sample, cache_na_k, cache_na_v, cache_df_k, cache_df_v, state_ml_c, state_ml_n, state_ml_m, state_rg_h, c, c_ctx, norm1_g, norm2_g, w_mod, b_mod, w_in, ml_gate_b, ml_norm_g, na_qn_g, na_kn_g, na_rpb, rg_conv_w, rg_conv_b, rg_wa, rg_ba, rg_wx, rg_bx, rg_lam, df_qn_g, df_kn_g, df_lq1, df_lk1, df_lq2, df_lk2, df_subln_g, w_out, router_w, moe_wg, moe_wu, moe_wd):
    nb, seq, _ = x_prompt.shape
    db, dseq, _ = x_sample.shape
    depth = w_in.shape[0]
    past = cache_na_k.shape[2]
    tm = 512
    cap_c = 2 * seq // NE
    cap_l = 2 * dseq // NE

    cv = jnp.concatenate([c_ctx[None, :], c, jnp.zeros((8 - 1 - db, D), F32)], axis=0)
    mod_all = _modulation(cv, w_mod, b_mod)
    rope = _rope_tables(dseq)

    xc = x_prompt.reshape(nb * seq, D)
    xl = x_sample.reshape(db * dseq, D)
    ctx_out = []
    for l in range(depth):
        lam_init = 0.8 - 0.6 * math.exp(-0.3 * l)
        p = _layer_params(l, w_in, ml_gate_b, na_qn_g, na_kn_g, df_qn_g, df_kn_g, rg_wa, rg_wx, rg_ba, rg_bx,
                          df_lq1, df_lk1, df_lq2, df_lk2, df_subln_g, w_out, router_w)
        mod = mod_all[l]
        g1 = norm1_g[l].reshape(1, D)
        g2 = norm2_g[l].reshape(1, D)
        mlg = ml_norm_g[l].reshape(1, GW)
        cb = rg_conv_b[l].reshape(1, GW)
        tb = _rpb_table(na_rpb[l])

        nt_c = nb * seq // tm
        ml, gates, na, rg, df = _proj_in(xc, mod, g1, p["w_r"], p["gate_b"], p["qkg"], None,
                                         row_base=0, row_div=nt_c, tm=tm)
        hf, hb, cn_c, m_c = _mlstm(ml, gates, jnp.zeros((nb, 2, NH, 2, HD, 128), F32), jnp.zeros((nb, 8, 128), F32),
                                   nreq=nb, nc=seq // CHUNK)
        y_na, o_df = _ctx_attn(na, df, p["lamp"], lam_init, nreq=nb, n=seq)
        y_rg, rg_fin = _rglru(rg, rg_conv_w[l], cb, p["wbd"], p["rg_bias"], rg_lam[l],
                              jnp.zeros((nb, 2, GW), F32), nreq=nb, n=seq)
        x1_c, hn2_c, lg_c = _merge(hf, hb, ml, y_na, y_rg, o_df, xc, mod, mlg, p["sub"], p["wout"], g2, p["rw"],
                                   lam_init, row_base=0, row_div=nt_c, tm=tm)
        pos_c, aff_c = _route(lg_c, nreq=nb, n=seq, cap=cap_c, rb=8)
        xs_c, w_c = _gather(pos_c, aff_c, hn2_c, nreq=nb, n=seq, cap=cap_c, eb=NE)
        ctx_out.append((na, df, cn_c, m_c, rg_fin))

        tiles_req = dseq // tm
        ml, gates, na, rg, df = _proj_in(xl, mod, g1, p["w_r"], p["gate_b"], p["qkg"], rope,
                                         row_base=1, row_div=tiles_req, tm=tm)
        cn0 = _mlstm_pack_state(state_ml_c[:, l], state_ml_n[:, l])
        m0 =jnp.broadcast_to(state_ml_m[:, l].reshape(db, 8, 1), (db, 8, 128))
        hf, hb, _, _ = _mlstm(ml, gates, cn0, m0, nreq=db, nc=dseq // CHUNK)
        y_na = _na_lat(na, cache_na_k[:, l].reshape(db, past, GW), cache_na_v[:, l].reshape(db, past, GW), tb,
                       nreq=db, n=dseq)
        o_df = _df_lat(df, cache_df_k[:, l].reshape(db, past, GW), cache_df_v[:, l].reshape(db, past, GW),
                       p["lamp"], lam_init, nreq=db, n=dseq)
        y_rg, _ = _rglru(rg, rg_conv_w[l], cb, p["wbd"], p["rg_bias"], rg_lam[l], state_rg_h[:, l],
                         nreq=db, n=dseq)
        x1_l, hn2_l, lg_l = _merge(hf, hb, ml, y_na, y_rg, o_df, xl, mod, mlg, p["sub"], p["wout"], g2, p["rw"],
                                   lam_init, row_base=1, row_div=tiles_req, tm=tm)
        pos_l, aff_l = _route(lg_l, nreq=db, n=dseq, cap=cap_l, rb=1)
        xs_l, w_l = _gather(pos_l, aff_l, hn2_l, nreq=db, n=dseq, cap=cap_l, eb=1)

        y_c, y_l = _experts(xs_c, xs_l, w_c, w_l, moe_wg, moe_wu, moe_wd, l)
        xc = _scatter(pos_c, y_c, x1_c, mod, nreq=nb, n=seq, cap=cap_c, eb=NE, row_base=0, row_mul=0)
        xl = _scatter(pos_l, y_l, x1_l, mod, nreq=db, n=dseq, cap=cap_l, eb=1, row_base=1, row_mul=1)

    y_prompt = xc.reshape(nb, seq, D)
    y_sample = xl.reshape(db, dseq, D)
    st = lambda f: jnp.stack([f(o) for o in ctx_out], axis=1)
    na_k = st(lambda o: o[0][:, 256:512].reshape(nb, seq, NH, HD))
    na_v = st(lambda o: o[0][:, 512:768].reshape(nb, seq, NH, HD))
    df_k = st(lambda o: o[1][:, 256:512].reshape(nb, seq, NH, 2, HD // 2))
    df_v = st(lambda o: o[1][:, 512:768].reshape(nb, seq, NH, HD))
    ml_c = st(lambda o: _mlstm_unpack_state(o[2])[0])
    ml_n = st(lambda o: _mlstm_unpack_state(o[2])[1])
    ml_m = st(lambda o: o[3][:, :, 0].reshape(nb, 2, NH))
    rg_h = st(lambda o: o[4])
    return (y_prompt, y_sample, na_k, na_v, df_k, df_v, ml_c, ml_n, ml_m, rg_h)
```

```python
import functools
import math

import numpy as np
import jax
import jax.numpy as jnp
from jax import lax
from jax.experimental import pallas as pl
from jax.experimental.pallas import tpu as pltpu

F32 = jnp.float32
BF16 = jnp.bfloat16

D = 1024
GW = 256
NH = 4
HD = 64
NE = 16
EPS = 1e-6
CHUNK = 128
GRID_W = 64
NA_ROWS = 8
NA_COLS = 16
SUB_ROWS = 256
NA_RPS = 4
RG_C = 8.0
ROPE_BASE = 10000.0
PROJ_PAD = 3200
VMEM_LIMIT_BYTES = 56 * 1024 * 1024
NEG_BIG = -1e30
LOG2E = 1.4426950408889634


def _bf(x):
    return x.astype(BF16)


def _dot(a, b):
    return jnp.dot(a, b, preferred_element_type=F32)


def _dot_nt(a, b):
    return lax.dot_general(a, b, (((1,), (1,)), ((), ())), preferred_element_type=F32)


def _dot_tn(a, b):
    return lax.dot_general(a, b, (((0,), (0,)), ((), ())), preferred_element_type=F32)


def _split3(x):
    p0 = _bf(x)
    r1 = x - p0.astype(F32)
    p1 = _bf(r1)
    return p0, p1, _bf(r1 - p1.astype(F32))


def _params(n_axes):
    return pltpu.CompilerParams(dimension_semantics=("arbitrary",) * n_axes,
                                vmem_limit_bytes=VMEM_LIMIT_BYTES)


def _full(shape):
    return pl.BlockSpec(shape, lambda *_: (0,) * len(shape))


def _sds(shape, dtype=F32):
    return jax.ShapeDtypeStruct(shape, dtype)


def _softplus(x):
    return jnp.maximum(x, 0.0) + jnp.log1p(jnp.exp(-jnp.abs(x)))


def _log_sigmoid(x):
    return jnp.minimum(x, 0.0) - jnp.log1p(jnp.exp(-jnp.abs(x)))


def _seg_rms(x, nseg, g_row):
    seg = x.shape[-1] // nseg
    lane = lax.broadcasted_iota(jnp.int32, x.shape, 1)
    x2 = x * x
    tot = jnp.zeros_like(x)
    for s in range(nseg):
        m = (lane >= s * seg) & (lane < (s + 1) * seg)
        t = jnp.sum(jnp.where(m, x2, 0.0), axis=-1, keepdims=True)
        tot = jnp.where(m, t, tot)
    return x * lax.rsqrt(tot * (1.0 / seg) + EPS) * g_row


def _mod_row(pid, row_base, row_div):
    return row_base + pid // row_div


def _mod_kernel(cv_ref, w_ref, b_ref, o_ref):
    cv = cv_ref[...]
    s = cv * jax.nn.sigmoid(cv)
    o_ref[0] = _dot(_bf(s), _bf(w_ref[0])) + b_ref[0]


def _modulation(cv, w_mod, b_mod):
    nl = w_mod.shape[0]
    tn = 1536
    return pl.pallas_call(
        _mod_kernel,
        grid=(nl, 6 * D // tn),
        in_specs=[_full((8, D)),
                  pl.BlockSpec((1, D, tn), lambda l, j: (l, 0, j)),
                  pl.BlockSpec((1, 1, tn), lambda l, j: (l, 0, j))],
        out_specs=pl.BlockSpec((1, 8, tn), lambda l, j: (l, 0, j)),
        out_shape=_sds((nl, 8, 6 * D)),
        compiler_params=_params(2),
        name="modulation",
    )(cv, w_mod, b_mod.reshape(nl, 1, 6 * D))


def _rope(x, cos_t, sin_t):
    lane = lax.broadcasted_iota(jnp.int32, x.shape, 1)
    first = (lane & 15) < 8
    sw = jnp.where(first, pltpu.roll(x, GW - 8, axis=1), pltpu.roll(x, 8, axis=1))
    return x * cos_t + sw * sin_t


def _proj_kernel(*refs, row_base, row_div, layer):
    ctx = layer is not None
    if ctx:
        (x_ref, mod_ref, g1_ref, w_ref, gb_ref, qkg_ref, _, _, _, _,
         ml_ref, gate_ref, naq_ref, rg_ref, dfq_ref, nk_ref, nv_ref, dk_ref, dv_ref) = refs
    else:
        (x_ref, mod_ref, g1_ref, w_ref, gb_ref, qkg_ref, cos_ref, sin_ref,
         ml_ref, gate_ref, na_ref, rg_ref, df_ref) = refs
    r = _mod_row(pl.program_id(0), row_base, row_div)
    sh = mod_ref[pl.ds(r, 1), 0:D]
    sc = mod_ref[pl.ds(r, 1), D:2 * D]
    for sub in range(x_ref.shape[0] // SUB_ROWS):
        rows = slice(sub * SUB_ROWS, (sub + 1) * SUB_ROWS)
        x = x_ref[rows, :]
        ms = jnp.mean(x * x, axis=-1, keepdims=True)
        y = x * lax.rsqrt(ms + EPS) * g1_ref[...]
        hn = _bf(y * (1.0 + sc) + sh)

        ml = _dot(hn, w_ref[:, 0:1024])
        ml_ref[rows, 0:256] = ml[:, 0:256]
        ml_ref[rows, 256:512] = ml[:, 256:512] * (HD ** -0.5)
        ml_ref[rows, 512:1024] = ml[:, 512:1024]

        gz = _dot(hn, w_ref[:, 3072:3200]) + gb_ref[...]
        lane = lax.broadcasted_iota(jnp.int32, gz.shape, 1)
        gate_ref[rows, :] = jnp.where(((lane >> 2) & 1) == 1, _log_sigmoid(gz), gz)

        rg_ref[rows, :] = _dot(hn, w_ref[:, 1792:2304])

        nz = _dot(hn, w_ref[:, 1024:1792])
        nq = _seg_rms(nz[:, 0:256], NH, qkg_ref[0:1, :])
        nk = _seg_rms(nz[:, 256:512], NH, qkg_ref[1:2, :])
        dz = _dot(hn, w_ref[:, 2304:3072])
        dq = _seg_rms(dz[:, 0:256], 2 * NH, qkg_ref[2:3, :])
        dk = _seg_rms(dz[:, 256:512], 2 * NH, qkg_ref[3:4, :])
        if ctx:
            naq_ref[rows, :] = nq
            dfq_ref[rows, :] = dq
            nk_ref[sub, 0] = nk
            nv_ref[sub, 0] = nz[:, 512:768]
            dk_ref[sub, 0] = dk
            dv_ref[sub, 0] = dz[:, 512:768]
        else:
            cos_t = cos_ref[rows, :]
            sin_t = sin_ref[rows, :]
            na_ref[rows, 0:256] = nq
            na_ref[rows, 256:512] = nk
            na_ref[rows, 512:768] = nz[:, 512:768]
            df_ref[rows, 0:256] = _rope(dq, cos_t, sin_t)
            df_ref[rows, 256:512] = _rope(dk, cos_t, sin_t)
            df_ref[rows, 512:768] = dz[:, 512:768]


def _proj_in(x, mod, g1, w_r, gate_b, qkg, *, rope=None, caches=None, layer=None, row_base, row_div, tm=512):
    tp = x.shape[0]
    in_specs = [pl.BlockSpec((tm, D), lambda i: (i, 0)), _full((8, 6 * D)), _full((1, D)),
                _full((D, PROJ_PAD)), _full((1, 128)), _full((4, GW))]
    args = [x, mod, g1, w_r, gate_b, qkg]
    tok = lambda n: pl.BlockSpec((tm, n), lambda i: (i, 0))
    if caches is None:
        tiles = rope[0].shape[0] // tm
        in_specs += [pl.BlockSpec((tm, GW), lambda i: (i % tiles, 0))] * 2
        args += list(rope)
        out_specs = [tok(1024), tok(128), tok(768), tok(512), tok(768)]
        out_shape = [_sds((tp, n)) for n in (1024, 128, 768, 512, 768)]
        aliases = {}
    else:
        seq = caches[0].shape[2]
        assert seq == SUB_ROWS
        cspec = pl.BlockSpec((tm // seq, 1, seq, GW), lambda i: (i, layer, 0, 0))
        in_specs += [pl.BlockSpec(memory_space=pl.ANY)] * 4
        args += list(caches)
        out_specs = [tok(1024), tok(128), tok(GW), tok(512), tok(GW)] + [cspec] * 4
        out_shape = [_sds((tp, n)) for n in (1024, 128, GW, 512, GW)] + [_sds(c.shape) for c in caches]
        aliases = {6 + i: 5 + i for i in range(4)}
    return pl.pallas_call(
        functools.partial(_proj_kernel, row_base=row_base, row_div=row_div, layer=layer),
        grid=(tp // tm,),
        in_specs=in_specs,
        out_specs=out_specs,
        out_shape=out_shape,
        input_output_aliases=aliases,
        compiler_params=_params(1),
        name="proj_in",
    )(*args)


def _mlstm_dir(d, q_ref, k_ref, v_ref, g_ref, h_ref, st_s, m_s):
    lc = CHUNK
    g = g_ref[...]
    gt = g.T
    row = lax.broadcasted_iota(jnp.int32, (lc, lc), 0)
    col = lax.broadcasted_iota(jnp.int32, (lc, lc), 1)
    tri = (row >= col) if d == 0 else (row <= col)
    tri_t = (col >= row) if d == 0 else (col <= row)
    g_parts = _split3(g)
    gt_parts = _split3(gt)
    tri_b = tri.astype(F32).astype(BF16)
    tri_tb = tri_t.astype(F32).astype(BF16)
    bc_col = _dot(tri_b, g_parts[0]) + _dot(tri_b, g_parts[1]) + _dot(tri_b, g_parts[2])
    bc_row = _dot(gt_parts[0], tri_tb) + _dot(gt_parts[1], tri_tb) + _dot(gt_parts[2], tri_tb)
    r_rows = gt[d * 8:d * 8 + NH, :] - bc_row[d * 8 + NH:d * 8 + 2 * NH, :]
    k_t = k_ref[...].T
    k_tb = _bf(k_t)
    q_all = _bf(q_ref[...])
    v_all = _bf(v_ref[...])
    ones = jnp.ones((lc, 128), BF16)
    last = lc - 1 if d == 0 else 0
    hs = [slice(h * HD, (h + 1) * HD) for h in range(NH)]
    ms = [m_s[d * NH + h:d * NH + h + 1, 0:1] for h in range(NH)]
    rms = [jnp.where(tri, r_rows[h:h + 1, :], -jnp.inf) for h in range(NH)]
    big_rs = [jnp.maximum(jnp.max(rms[h], axis=-1, keepdims=True), ms[h]) for h in range(NH)]
    ss = [_bf(_dot(q_all[:, hs[h]], k_tb[hs[h], :]) * jnp.exp(rms[h] - big_rs[h])) for h in range(NH)]
    outs = []
    for h in range(NH):
        v2 = v_all[:, (h // 2) * 128:(h // 2 + 1) * 128]
        e = jnp.exp(ms[h] - big_rs[h])
        qh = q_all[:, hs[h]]
        tot = _dot(ss[h], v2) + e * _dot(qh, _bf(st_s[d, h, 0]))
        den = _dot(ss[h], ones) + e * _dot(qh, _bf(st_s[d, h, 1]))
        c_col = bc_col[:, d * 8 + NH + h:d * 8 + NH + h + 1]
        floor = jnp.exp(-(c_col + big_rs[h]))
        outs.append(tot / jnp.maximum(jnp.abs(den), floor))

        r_last = big_rs[h][last:last + 1, :]
        kw = _bf(k_t[hs[h], :] * jnp.exp(r_rows[h:h + 1, :] - r_last))
        gdec = jnp.exp(ms[h] - r_last)
        st_s[d, h, 0] = gdec * st_s[d, h, 0] + _dot(kw, v2)
        st_s[d, h, 1] = gdec * st_s[d, h, 1] + _dot(kw, ones)
        m_s[d * NH + h:d * NH + h + 1, :] = jnp.broadcast_to(c_col[last:last + 1, :] + r_last, (1, 128))
    lane = lax.broadcasted_iota(jnp.int32, (lc, 128), 1)
    h_ref[...] = jnp.concatenate([jnp.where(lane < HD, outs[2 * p], outs[2 * p + 1]) for p in range(NH // 2)],
                                 axis=-1)


def _mlstm_kernel(qf, kf, vf, gf, qb, kb, vb, gb, st0_ref, m0_ref,
                  hf_ref, hb_ref, st_out, m_out, st_s, m_s):
    j = pl.program_id(1)

    @pl.when(j == 0)
    def _():
        st_s[...] = st0_ref[0]
        m_s[...] = m0_ref[0]

    _mlstm_dir(0, qf, kf, vf, gf, hf_ref, st_s, m_s)
    _mlstm_dir(1, qb, kb, vb, gb, hb_ref, st_s, m_s)

    @pl.when(j == pl.num_programs(1) - 1)
    def _():
        st_out[0] = st_s[...]
        m_out[0] = m_s[...]


def _mlstm_pack_state(c, n):
    z = jnp.zeros_like(c)
    odd = (jnp.arange(NH) % 2 == 1)[None, None, :, None, None]
    c_pair = jnp.where(odd, jnp.concatenate([z, c], axis=-1), jnp.concatenate([c, z], axis=-1))
    n_rep = jnp.broadcast_to(n[..., None], n.shape + (128,))
    return jnp.stack([c_pair, n_rep], axis=3)


def _mlstm_unpack_state(st):
    odd = (jnp.arange(NH) % 2 == 1)[None, None, :, None, None]
    c = jnp.where(odd, st[:, :, :, 0, :, HD:2 * HD], st[:, :, :, 0, :, 0:HD])
    return c, st[:, :, :, 1, :, 0]


def _mlstm(mlz, gates, st0, m0, *, nreq, nc):
    tp = mlz.shape[0]
    st_spec = pl.BlockSpec((1, 2, NH, 2, HD, 128), lambda r, j: (r, 0, 0, 0, 0, 0))

    def fwd(col):
        return lambda r, j: (r * nc + j, col)

    def bwd(col):
        return lambda r, j: (r * nc + nc - 1 - j, col)

    in_specs = []
    for mk in (fwd, bwd):
        in_specs += [pl.BlockSpec((CHUNK, GW), mk(0)), pl.BlockSpec((CHUNK, GW), mk(1)),
                     pl.BlockSpec((CHUNK, GW), mk(2)), pl.BlockSpec((CHUNK, 128), mk(0))]
    in_specs += [st_spec, pl.BlockSpec((1, 8, 128), lambda r, j: (r, 0, 0))]
    return pl.pallas_call(
        _mlstm_kernel,
        grid=(nreq, nc),
        in_specs=in_specs,
        out_specs=[pl.BlockSpec((CHUNK, GW), fwd(0)), pl.BlockSpec((CHUNK, GW), bwd(0)),
                   st_spec, pl.BlockSpec((1, 8, 128), lambda r, j: (r, 0, 0))],
        out_shape=[_sds((tp, GW)), _sds((tp, GW)), _sds((nreq, 2, NH, 2, HD, 128)), _sds((nreq, 8, 128))],
        scratch_shapes=[pltpu.VMEM((2, NH, 2, HD, 128), F32), pltpu.VMEM((8, 128), F32)],
        compiler_params=_params(2),
        name="mlstm",
    )(mlz, mlz, mlz, gates, mlz, mlz, mlz, gates, st0, m0)


def _diff_lambda(lam_ref, lam_init):
    lp = lam_ref[...]
    a = jnp.exp(jnp.sum(lp[0:1, :] * lp[1:2, :], axis=-1, keepdims=True))
    b = jnp.exp(jnp.sum(lp[2:3, :] * lp[3:4, :], axis=-1, keepdims=True))
    return a - b + lam_init


def _softmax_pv(q, k_t, v):
    s = _dot(q, k_t)
    e = jnp.exp2(s - jnp.max(s, axis=-1, keepdims=True))
    return _dot(_bf(e), v) / jnp.sum(e, axis=-1, keepdims=True)


def _ctx_attn_kernel(nq, nk, nv, dq, dk, dv, lam_ref, ona_ref, odf_ref, *, lam_init):
    lam = _diff_lambda(lam_ref, lam_init)
    nk_t = _bf(nk[0, 0].T)
    dk_t = _bf(dk[0, 0].T)
    na_out, df_out = [], []
    for h in range(NH):
        hs = slice(h * HD, (h + 1) * HD)
        na_out.append(_softmax_pv(_bf(nq[:, hs] * (HD ** -0.5 * LOG2E)), nk_t[hs, :], _bf(nv[0, 0, :, hs])))
        vh = _bf(dv[0, 0, :, hs])
        os = []
        for c in range(2):
            cs = slice(h * HD + c * 32, h * HD + (c + 1) * 32)
            os.append(_softmax_pv(_bf(dq[:, cs] * (32 ** -0.5 * LOG2E)), dk_t[cs, :], vh))
        df_out.append(os[0] - lam * os[1])
    ona_ref[...] = _bf(jnp.concatenate(na_out, axis=-1))
    odf_ref[...] = jnp.concatenate(df_out, axis=-1)


def _ctx_attn(naq, dfq, caches, layer, lamp, lam_init, *, nreq, n):
    tp = naq.shape[0]
    qs = pl.BlockSpec((n, GW), lambda b: (b, 0))
    cs = pl.BlockSpec((1, 1, n, GW), lambda b: (b, layer, 0, 0))
    return pl.pallas_call(
        functools.partial(_ctx_attn_kernel, lam_init=lam_init),
        grid=(nreq,),
        in_specs=[qs, cs, cs, qs, cs, cs, _full((4, 32))],
        out_specs=[qs, qs],
        out_shape=[_sds((tp, GW), BF16), _sds((tp, GW))],
        compiler_params=_params(1),
        name="ctx_attn",
    )(naq, caches[0], caches[1], dfq, caches[2], caches[3], lamp)


def _rpb_kernel(rpb_ref, o_ref, *, rows, rps):
    h = pl.program_id(0)
    shape = (GRID_W, NA_ROWS * GRID_W)
    lane = lax.broadcasted_iota(jnp.int32, shape, 1)
    cq = lax.broadcasted_iota(jnp.int32, shape, 0)
    ck = lane & (GRID_W - 1)
    dc = jnp.clip(ck - cq, -(NA_COLS - 1), NA_COLS - 1) + (NA_COLS - 1)
    c0 = jnp.clip(cq - NA_COLS // 2, 0, GRID_W - NA_COLS)
    ok = (ck >= c0) & (ck < c0 + NA_COLS)
    jrow = lax.broadcasted_iota(jnp.int32, (1, shape[1]), 1) >> 6
    ncol = 2 * NA_COLS - 1
    ndr = 2 * NA_ROWS - 1

    def one_window(s):
        def body(b, acc):
            vrow = jnp.zeros((1, shape[1]), F32)
            for j in range(NA_ROWS):
                vrow = jnp.where(jrow == j, rpb_ref[(h * ndr + s + j) * ncol + b], vrow)
            return jnp.where(dc == b, vrow, acc)
        acc = lax.fori_loop(0, ncol, body, jnp.zeros(shape, F32))
        return jnp.where(ok, acc * LOG2E, NEG_BIG)

    tbs = [one_window(s) for s in range(NA_ROWS)]
    wrows = NA_ROWS + rps
    steps = rows // rps
    for t, j in enumerate((0, 1, steps - 1)):
        w0 = min(max(j * rps - NA_ROWS // 2, 0), rows - wrows)
        for a in range(rps):
            r = j * rps + a
            r0 = min(max(r - NA_ROWS // 2, 0), rows - NA_ROWS)
            off = (r0 - w0) * GRID_W
            rest = rps * GRID_W - off
            pieces = [tbs[r0 - r + NA_ROWS - 1]]
            if off:
                pieces = [jnp.full((GRID_W, off), NEG_BIG, F32)] + pieces
            if rest:
                pieces = pieces + [jnp.full((GRID_W, rest), NEG_BIG, F32)]
            o_ref[0, t, a * GRID_W:(a + 1) * GRID_W, :] = jnp.concatenate(pieces, axis=-1)


def _na_step_offsets(j, rows, rps):
    w0 = min(max(j * rps - NA_ROWS // 2, 0), rows - NA_ROWS - rps)
    return [(min(max(r - NA_ROWS // 2, 0), rows - NA_ROWS) - r,
             min(max(r - NA_ROWS // 2, 0), rows - NA_ROWS) - w0) for r in range(j * rps, (j + 1) * rps)]


def _rpb_table(rpb, *, rows, rps):
    steps = rows // rps
    assert all(_na_step_offsets(j, rows, rps) == _na_step_offsets(1, rows, rps) for j in range(1, steps - 1))
    shape = (rps * GRID_W, (NA_ROWS + rps) * GRID_W)
    return pl.pallas_call(
        functools.partial(_rpb_kernel, rows=rows, rps=rps),
        grid=(NH,),
        in_specs=[pl.BlockSpec(memory_space=pltpu.SMEM)],
        out_specs=pl.BlockSpec((1, 3) + shape, lambda h: (h, 0, 0, 0)),
        out_shape=_sds((NH, 3) + shape),
        compiler_params=_params(1),
        name="rpb_table",
    )(rpb.reshape(-1))


def _na_lat_kernel(q_ref, k_ref, v_ref, kc_ref, vc_ref, tb_ref, o_ref, k_s, v_s, kct_s, vc_s, *, rows, rps):
    j = pl.program_id(1)

    @pl.when(j == 0)
    def _():
        k_s[...] = _bf(k_ref[...])
        v_s[...] = _bf(v_ref[...])
        kct_s[...] = _bf(kc_ref[0].T)
        vc_s[...] = _bf(vc_ref[0])

    wrows = NA_ROWS + rps
    win = wrows * GRID_W
    w0 = jnp.clip(j * rps - NA_ROWS // 2, 0, rows - wrows)
    start = pl.multiple_of(w0 * GRID_W, GRID_W)
    outs = []
    for h in range(NH):
        hs = slice(h * HD, (h + 1) * HD)
        q = _bf(q_ref[:, hs] * (HD ** -0.5 * LOG2E))
        s_c = _dot(q, kct_s[hs, :])
        s_l = _dot_nt(q, k_s[pl.ds(start, win), hs]) + tb_ref[h, 0]
        m = jnp.maximum(jnp.max(s_c, axis=-1, keepdims=True), jnp.max(s_l, axis=-1, keepdims=True))
        e_c = jnp.exp2(s_c - m)
        e_l = jnp.exp2(s_l - m)
        den = jnp.sum(e_c, axis=-1, keepdims=True) + jnp.sum(e_l, axis=-1, keepdims=True)
        o = _dot(_bf(e_c), vc_s[:, hs]) + _dot(_bf(e_l), v_s[pl.ds(start, win), hs])
        outs.append(o / den)
    o_ref[...] = _bf(jnp.concatenate(outs, axis=-1))


def _na_lat(na, kc, vc, tb, *, nreq, n, rps):
    tp = na.shape[0]
    rows = n // GRID_W
    past = kc.shape[1]
    steps = rows // rps
    tq = rps * GRID_W
    return pl.pallas_call(
        functools.partial(_na_lat_kernel, rows=rows, rps=rps),
        grid=(nreq, steps),
        in_specs=[pl.BlockSpec((tq, GW), lambda b, r: (b * steps + r, 0)),
                  pl.BlockSpec((n, GW), lambda b, r: (b, 1)),
                  pl.BlockSpec((n, GW), lambda b, r: (b, 2)),
                  pl.BlockSpec((1, past, GW), lambda b, r: (b, 0, 0)),
                  pl.BlockSpec((1, past, GW), lambda b, r: (b, 0, 0)),
                  pl.BlockSpec((NH, 1) + tb.shape[2:],
                               lambda b, r: (0, jnp.minimum(r, 1) + r // (steps - 1), 0, 0))],
        out_specs=pl.BlockSpec((tq, GW), lambda b, r: (b * steps + r, 0)),
        out_shape=_sds((tp, GW), BF16),
        scratch_shapes=[pltpu.VMEM((n, GW), BF16), pltpu.VMEM((n, GW), BF16),
                        pltpu.VMEM((GW, past), BF16), pltpu.VMEM((past, GW), BF16)],
        compiler_params=_params(2),
        name="na_latent",
    )(na, na, na, kc, vc, tb)


def _df_lat_kernel(q_ref, k_ref, v_ref, kc_ref, vc_ref, lam_ref, o_ref, kt_s, v_s, kct_s, vc_s, *, lam_init):
    @pl.when(pl.program_id(1) == 0)
    def _():
        kt_s[...] = _bf(k_ref[...].T)
        v_s[...] = _bf(v_ref[...])
        kct_s[...] = _bf(kc_ref[0].T)
        vc_s[...] = _bf(vc_ref[0])

    lam = _diff_lambda(lam_ref, lam_init)
    outs = []
    for h in range(NH):
        hs = slice(h * HD, (h + 1) * HD)
        os = []
        for c in range(2):
            cs = slice(h * HD + c * 32, h * HD + (c + 1) * 32)
            q = _bf(q_ref[:, cs] * (32 ** -0.5 * LOG2E))
            s_c = _dot(q, kct_s[cs, :])
            s_l = _dot(q, kt_s[cs, :])
            m = jnp.maximum(jnp.max(s_c, axis=-1, keepdims=True), jnp.max(s_l, axis=-1, keepdims=True))
            e_c = jnp.exp2(s_c - m)
            e_l = jnp.exp2(s_l - m)
            den = jnp.sum(e_c, axis=-1, keepdims=True) + jnp.sum(e_l, axis=-1, keepdims=True)
            os.append((_dot(_bf(e_c), vc_s[:, hs]) + _dot(_bf(e_l), v_s[:, hs])) / den)
        outs.append(os[0] - lam * os[1])
    o_ref[...] = jnp.concatenate(outs, axis=-1)


def _df_lat(df, kc, vc, lamp, lam_init, *, nreq, n, tq=256):
    tp = df.shape[0]
    nq = n // tq
    past = kc.shape[1]
    return pl.pallas_call(
        functools.partial(_df_lat_kernel, lam_init=lam_init),
        grid=(nreq, nq),
        in_specs=[pl.BlockSpec((tq, GW), lambda b, j: (b * nq + j, 0)),
                  pl.BlockSpec((n, GW), lambda b, j: (b, 1)),
                  pl.BlockSpec((n, GW), lambda b, j: (b, 2)),
                  pl.BlockSpec((1, past, GW), lambda b, j: (b, 0, 0)),
                  pl.BlockSpec((1, past, GW), lambda b, j: (b, 0, 0)),
                  _full((4, 32))],
        out_specs=pl.BlockSpec((tq, GW), lambda b, j: (b * nq + j, 0)),
        out_shape=_sds((tp, GW)),
        scratch_shapes=[pltpu.VMEM((GW, n), BF16), pltpu.VMEM((n, GW), BF16),
                        pltpu.VMEM((GW, past), BF16), pltpu.VMEM((past, GW), BF16)],
        compiler_params=_params(2),
        name="df_latent",
    )(df, df, df, kc, vc, lamp)


def _rg_kernel(x_ref, g_ref, cw_ref, cb_ref, wbd_ref, bias_ref, lam_ref, h0_ref,
               y_ref, fin_ref, a_s, b_s, *, n):
    x = x_ref[...]
    row = lax.broadcasted_iota(jnp.int32, x.shape, 0)
    xc = cb_ref[...] + jnp.where(row >= 2, pltpu.roll(x, 2, axis=0), 0.0) * cw_ref[0:1, :]
    xc = xc + jnp.where(row >= 1, pltpu.roll(x, 1, axis=0), 0.0) * cw_ref[1:2, :]
    xc = xc + x * cw_ref[2:3, :]
    xc = xc + jnp.where(row < n - 1, pltpu.roll(x, n - 1, axis=0), 0.0) * cw_ref[3:4, :]
    z = _dot(_bf(xc), wbd_ref[...]) + bias_ref[...]
    sub = row & 7
    for d in range(2):
        rgate = jax.nn.sigmoid(z[:, 512 * d:512 * d + GW])
        igate = jax.nn.sigmoid(z[:, 512 * d + GW:512 * d + 2 * GW])
        la = -RG_C * rgate * _softplus(-lam_ref[d:d + 1, :])
        a = jnp.exp(la)
        t = jnp.tanh(la)
        b = jnp.sqrt(-2.0 * t / (1.0 - t)) * igate * xc
        for dd in (1, 2, 4):
            if d == 0:
                keep = sub >= dd
                a_sh = jnp.where(keep, pltpu.roll(a, dd, axis=0), 1.0)
                b_sh = jnp.where(keep, pltpu.roll(b, dd, axis=0), 0.0)
            else:
                keep = sub < 8 - dd
                a_sh = jnp.where(keep, pltpu.roll(a, n - dd, axis=0), 1.0)
                b_sh = jnp.where(keep, pltpu.roll(b, n - dd, axis=0), 0.0)
            b = b + a * b_sh
            a = a * a_sh
        a_s[d] = a
        b_s[d] = b

    nt = n // 8

    def body(t, carry):
        hf, hb = carry
        sf = pl.multiple_of(t * 8, 8)
        sb = pl.multiple_of((nt - 1 - t) * 8, 8)
        tf = a_s[0, pl.ds(sf, 8), :] * hf + b_s[0, pl.ds(sf, 8), :]
        tb = a_s[1, pl.ds(sb, 8), :] * hb + b_s[1, pl.ds(sb, 8), :]
        b_s[0, pl.ds(sf, 8), :] = tf
        b_s[1, pl.ds(sb, 8), :] = tb
        return tf[7:8, :], tb[0:1, :]

    hf, hb = lax.fori_loop(0, nt, body, (h0_ref[0, 0:1, :], h0_ref[0, 1:2, :]))
    fin_ref[0, 0:1, :] = hf
    fin_ref[0, 1:2, :] = hb
    gg = g_ref[...]
    cdf = 0.5 * (1.0 + jnp.tanh(math.sqrt(2.0 / math.pi) * (gg + 0.044715 * (gg * gg * gg))))
    y_ref[...] = _bf((b_s[0] + b_s[1]) * (gg * cdf))


def _rglru(rg, cw, cb, wbd, bias, lam, h0, *, nreq, n):
    tp = rg.shape[0]
    return pl.pallas_call(
        functools.partial(_rg_kernel, n=n),
        grid=(nreq,),
        in_specs=[pl.BlockSpec((n, GW), lambda b: (b, 0)), pl.BlockSpec((n, GW), lambda b: (b, 1)),
                  _full((4, GW)), _full((1, GW)), _full((GW, 4 * GW)), _full((1, 4 * GW)), _full((2, GW)),
                  pl.BlockSpec((1, 2, GW), lambda b: (b, 0, 0))],
        out_specs=[pl.BlockSpec((n, GW), lambda b: (b, 0)), pl.BlockSpec((1, 2, GW), lambda b: (b, 0, 0))],
        out_shape=[_sds((tp, GW), BF16), _sds((nreq, 2, GW))],
        scratch_shapes=[pltpu.VMEM((2, n, GW), F32), pltpu.VMEM((2, n, GW), F32)],
        compiler_params=_params(1),
        name="rglru",
    )(rg, rg, cw, cb, wbd, bias, lam, h0)


def _merge_kernel(hf_ref, hb_ref, mlo_ref, yna_ref, yrg_ref, odf_ref, x_ref, mod_ref, mlg_ref, sub_ref,
                  wout_ref, n2_ref, rw_ref, x1_ref, hn2_ref, lg_ref, *, row_base, row_div, lam_init):
    r = _mod_row(pl.program_id(0), row_base, row_div)
    y_ml = _seg_rms(hf_ref[...] + hb_ref[...], NH, mlg_ref[...]) * jax.nn.sigmoid(mlo_ref[...])
    y_df = _seg_rms(odf_ref[...], NH, sub_ref[...]) * (1.0 - lam_init)
    y = jnp.concatenate([_bf(y_ml), _bf(yna_ref[...]), _bf(yrg_ref[...]), _bf(y_df)], axis=-1)
    o = _dot(y, wout_ref[...])
    x1 = x_ref[...] + mod_ref[pl.ds(r, 1), 2 * D:3 * D] * o
    x1_ref[...] = x1
    ms = jnp.mean(x1 * x1, axis=-1, keepdims=True)
    hn = x1 * lax.rsqrt(ms + EPS) * n2_ref[...]
    hn = _bf(hn * (1.0 + mod_ref[pl.ds(r, 1), 4 * D:5 * D]) + mod_ref[pl.ds(r, 1), 3 * D:4 * D])
    hn2_ref[...] = hn
    lg_ref[...] = _dot(hn, rw_ref[...])


def _merge(hf, hb, mlz, yna, yrg, odf, x, mod, mlg, sub, wout, n2, rw, lam_init, *, row_base, row_div, tm=512):
    tp = x.shape[0]
    g = lambda c: pl.BlockSpec((tm, GW), lambda i: (i, c))
    return pl.pallas_call(
        functools.partial(_merge_kernel, row_base=row_base, row_div=row_div, lam_init=lam_init),
        grid=(tp // tm,),
        in_specs=[g(0), g(0), g(3), g(0), g(0), g(0),
                  pl.BlockSpec((tm, D), lambda i: (i, 0)), _full((8, 6 * D)), _full((1, GW)), _full((1, GW)),
                  _full((D, D)), _full((1, D)), _full((D, 128))],
        out_specs=[pl.BlockSpec((tm, D), lambda i: (i, 0)), pl.BlockSpec((tm, D), lambda i: (i, 0)),
                   pl.BlockSpec((tm, 128), lambda i: (i, 0))],
        out_shape=[_sds((tp, D)), _sds((tp, D), BF16), _sds((tp, 128))],
        compiler_params=_params(1),
        name="merge",
    )(hf, hb, mlz, yna, yrg, odf, x, mod, mlg, sub, wout, n2, rw)


def _excl_cumsum_lanes(mask):
    blk = 256
    r = lax.broadcasted_iota(jnp.int32, (blk, blk), 0)
    c = lax.broadcasted_iota(jnp.int32, (blk, blk), 1)
    tri = (r < c).astype(F32).astype(BF16)
    off = jnp.zeros((mask.shape[0], 1), F32)
    outs = []
    for i in range(mask.shape[1] // blk):
        mb = mask[:, i * blk:(i + 1) * blk]
        outs.append(_dot(_bf(mb), tri) + off)
        off = off + jnp.sum(mb, axis=-1, keepdims=True)
    return jnp.concatenate(outs, axis=-1)


def _route_kernel(lg_ref, pos_ref, aff_ref, *, cap, n, rb):
    lg = lg_ref[...].T[0:NE, :]
    ex = jnp.exp(lg - jnp.max(lg, axis=0, keepdims=True))
    aff = ex / jnp.sum(ex, axis=0, keepdims=True)
    aff_ref[...] = aff
    aff = jnp.concatenate([aff[:, i * n:(i + 1) * n] for i in range(rb)], axis=0)
    thr = jnp.zeros((rb * NE, 1), jnp.int32)
    for bit in range(30, -1, -1):
        cand = thr | (1 << bit)
        cnt = jnp.sum((aff >= pltpu.bitcast(cand, F32)).astype(jnp.int32), axis=-1, keepdims=True)
        thr = jnp.where(cnt >= cap, cand, thr)
    gt = aff >= pltpu.bitcast(thr + 1, F32)
    eq = (aff >= pltpu.bitcast(thr, F32)) & jnp.logical_not(gt)
    need = (cap - jnp.sum(gt.astype(jnp.int32), axis=-1, keepdims=True)).astype(F32)
    eq_rank = _excl_cumsum_lanes(eq.astype(F32))
    sel = gt | (eq & (eq_rank < need))
    slot = _excl_cumsum_lanes(sel.astype(F32))
    pos = jnp.where(sel, slot.astype(jnp.int32), -1)
    for i in range(rb):
        pos_ref[:, i * n:(i + 1) * n] = pos[i * NE:(i + 1) * NE, :]


def _route(lg, *, nreq, n, cap, rb):
    tp = lg.shape[0]
    return pl.pallas_call(
        functools.partial(_route_kernel, cap=cap, n=n, rb=rb),
        grid=(nreq // rb,),
        in_specs=[pl.BlockSpec((rb * n, 128), lambda b: (b, 0))],
        out_specs=[pl.BlockSpec((NE, rb * n), lambda b: (0, b)), pl.BlockSpec((NE, rb * n), lambda b: (0, b))],
        out_shape=[_sds((NE, tp), jnp.int32), _sds((NE, tp))],
        compiler_params=_params(1),
        name="route",
    )(lg)


def _gather_kernel(pos_ref, aff_ref, h_ref, xs_ref, w_ref, *, eb, cap, n):
    eg = pl.program_id(1)
    io = lax.broadcasted_iota(jnp.int32, (cap, n), 0)
    sels = []
    for k in range(eb):
        e = eg * eb + k
        sel = pos_ref[pl.ds(e, 1), :] == io
        sels.append(sel.astype(F32).astype(BF16))
        w = jnp.sum(jnp.where(sel, aff_ref[pl.ds(e, 1), :], 0.0), axis=-1, keepdims=True)
        w_ref[k] = jnp.broadcast_to(w, (cap, 128))
    xs = _dot(jnp.concatenate(sels, axis=0), h_ref[...]).astype(BF16)
    for k in range(eb):
        xs_ref[k] = xs[k * cap:(k + 1) * cap, :]


def _gather(pos, aff, hn2, *, nreq, n, cap, eb):
    return pl.pallas_call(
        functools.partial(_gather_kernel, eb=eb, cap=cap, n=n),
        grid=(nreq, NE // eb),
        in_specs=[pl.BlockSpec((NE, n), lambda b, g: (0, b)), pl.BlockSpec((NE, n), lambda b, g: (0, b)),
                  pl.BlockSpec((n, D), lambda b, g: (b, 0))],
        out_specs=[pl.BlockSpec((eb, cap, D), lambda b, g: (g, b, 0)),
                   pl.BlockSpec((eb, cap, 128), lambda b, g: (g, b, 0))],
        out_shape=[_sds((NE, nreq * cap, D), BF16), _sds((NE, nreq * cap, 128))],
        compiler_params=_params(2),
        name="moe_gather",
    )(pos, aff, hn2)


def _expert_kernel(xc_ref, xl_ref, wc_ref, wl_ref, wg_ref, wu_ref, wd_ref, yc_ref, yl_ref, acc_ref, *, rows):
    f = pl.program_id(1)

    @pl.when(f == 0)
    def _():
        acc_ref[...] = jnp.zeros_like(acc_ref)

    wg = _bf(wg_ref[0, 0])
    wu = _bf(wu_ref[0, 0])
    wd = _bf(wd_ref[0, 0])
    tm = 512
    for part, x_ref in enumerate((xc_ref, xl_ref)):
        for ch in range(rows // tm):
            x = x_ref[0, ch * tm:(ch + 1) * tm, :]
            g = _dot(x, wg)
            u = _dot(x, wu)
            a = _bf(g * jax.nn.sigmoid(g) * u)
            lo = part * rows + ch * tm
            acc_ref[lo:lo + tm, :] += _dot(a, wd)

    @pl.when(f == pl.num_programs(1) - 1)
    def _():
        yc_ref[0] = _bf(acc_ref[0:rows, :] * wc_ref[0][:, 0:1])
        yl_ref[0] = _bf(acc_ref[rows:2 * rows, :] * wl_ref[0][:, 0:1])


def _experts(xs_c, xs_l, w_c, w_l, wg, wu, wd, layer, *, fb=512):
    rows = xs_c.shape[1]
    dff = wg.shape[-1]
    xspec = pl.BlockSpec((1, rows, D), lambda e, f: (e, 0, 0))
    wspec = pl.BlockSpec((1, rows, 128), lambda e, f: (e, 0, 0))
    return pl.pallas_call(
        functools.partial(_expert_kernel, rows=rows),
        grid=(NE, dff // fb),
        in_specs=[xspec, xspec, wspec, wspec,
                  pl.BlockSpec((1, 1, D, fb), lambda e, f: (layer, e, 0, f)),
                  pl.BlockSpec((1, 1, D, fb), lambda e, f: (layer, e, 0, f)),
                  pl.BlockSpec((1, 1, fb, D), lambda e, f: (layer, e, f, 0))],
        out_specs=[xspec, xspec],
        out_shape=[_sds((NE, rows, D), BF16), _sds((NE, rows, D), BF16)],
        scratch_shapes=[pltpu.VMEM((2 * rows, D), F32)],
        compiler_params=_params(2),
        name="moe_experts",
    )(xs_c, xs_l, w_c, w_l, wg, wu, wd)


def _scatter_kernel(pos_ref, y_ref, x1_ref, mod_ref, o_ref, acc_ref, *, eb, cap, n, row_base, row_mul):
    eg = pl.program_id(1)

    @pl.when(eg == 0)
    def _():
        acc_ref[...] = jnp.zeros_like(acc_ref)

    io = lax.broadcasted_iota(jnp.int32, (cap, n), 0)
    sels = [(pos_ref[pl.ds(eg * eb + k, 1), :] == io).astype(F32).astype(BF16) for k in range(eb)]
    ys = [y_ref[k] for k in range(eb)]
    acc_ref[...] += _dot_tn(jnp.concatenate(sels, axis=0), jnp.concatenate(ys, axis=0))

    @pl.when(eg == pl.num_programs(1) - 1)
    def _():
        r = row_base + row_mul * pl.program_id(0)
        o_ref[...] = x1_ref[...] + mod_ref[pl.ds(r, 1), 5 * D:6 * D] * acc_ref[...]


def _scatter(pos, y, x1, mod, *, nreq, n, cap, eb, row_base, row_mul):
    tp = x1.shape[0]
    return pl.pallas_call(
        functools.partial(_scatter_kernel, eb=eb, cap=cap, n=n, row_base=row_base, row_mul=row_mul),
        grid=(nreq, NE // eb),
        in_specs=[pl.BlockSpec((NE, n), lambda b, g: (0, b)),
                  pl.BlockSpec((eb, cap, D), lambda b, g: (g, b, 0)),
                  pl.BlockSpec((n, D), lambda b, g: (b, 0)), _full((8, 6 * D))],
        out_specs=pl.BlockSpec((n, D), lambda b, g: (b, 0)),
        out_shape=_sds((tp, D)),
        scratch_shapes=[pltpu.VMEM((n, D), F32)],
        compiler_params=_params(2),
        name="moe_scatter",
    )(pos, y, x1, mod)


def _rope_tables(n):
    nf = 8
    t = np.arange(n)
    rowp = (t // GRID_W).astype(np.float32)
    colp = (t % GRID_W).astype(np.float32)
    inv = (np.float32(ROPE_BASE) ** (-np.arange(nf, dtype=np.float32) / np.float32(nf))).astype(np.float32)
    lane = np.arange(GW)
    c32 = lane % 32
    pos = np.where((c32 < 16)[None, :], rowp[:, None], colp[:, None]).astype(np.float32)
    ang = (pos * inv[(c32 % 8)][None, :]).astype(np.float32).astype(np.float64)
    sign = np.where((lane % 16) < 8, -1.0, 1.0)[None, :]
    return jnp.asarray(np.cos(ang), F32), jnp.asarray(np.sin(ang) * sign, F32)


def _block_diag(w):
    nb, bi, bo = w.shape
    return (jnp.eye(nb, dtype=w.dtype)[:, None, :, None] * w[:, :, None, :]).reshape(nb * bi, nb * bo)


def _layer_params(l, w_in, ml_gate_b, na_qn_g, na_kn_g, df_qn_g, df_kn_g, rg_wa, rg_wx, rg_ba, rg_bx,
                  df_lq1, df_lk1, df_lq2, df_lk2, df_subln_g, w_out, router_w):
    wi = w_in[l]
    w_r = jnp.concatenate([wi[:, 0:1024], wi[:, 1040:3088], wi[:, 1024:1040],
                           jnp.zeros((D, PROJ_PAD - 3088), F32)], axis=1).astype(BF16)
    gate_b = jnp.pad(ml_gate_b[l], (0, 128 - 16)).reshape(1, 128)
    qkg = jnp.stack([jnp.tile(na_qn_g[l], NH), jnp.tile(na_kn_g[l], NH),
                     jnp.tile(df_qn_g[l], 2 * NH), jnp.tile(df_kn_g[l], 2 * NH)])
    wbd = jnp.concatenate([_block_diag(rg_wa[l, 0]), _block_diag(rg_wx[l, 0]),
                           _block_diag(rg_wa[l, 1]), _block_diag(rg_wx[l, 1])], axis=1).astype(BF16)
    rg_bias = jnp.concatenate([rg_ba[l, 0], rg_bx[l, 0], rg_ba[l, 1], rg_bx[l, 1]]).reshape(1, 4 * GW)
    lamp = jnp.stack([df_lq1[l], df_lk1[l], df_lq2[l], df_lk2[l]])
    sub = jnp.tile(df_subln_g[l], NH).reshape(1, GW)
    rw = jnp.pad(router_w[l], ((0, 0), (0, 128 - NE))).astype(BF16)
    return dict(w_r=w_r, gate_b=gate_b, qkg=qkg, wbd=wbd, rg_bias=rg_bias, lamp=lamp, sub=sub,
                wout=w_out[l].astype(BF16), rw=rw)


def kernel(x_prompt, x_sample, cache_na_k, cache_na_v, cache_df_k, cache_df_v, state_ml_c, state_ml_n, state_ml_m, state_rg_h, c, c_ctx, norm1_g, norm2_g, w_mod, b_mod, w_in, ml_gate_b, ml_norm_g, na_qn_g, na_kn_g, na_rpb, rg_conv_w, rg_conv_b, rg_wa, rg_ba, rg_wx, rg_bx, rg_lam, df_qn_g, df_kn_g, df_lq1, df_lk1, df_lq2, df_lk2, df_subln_g, w_out, router_w, moe_wg, moe_wu, moe_wd):
    nb, seq, _ = x_prompt.shape
    db, dseq, _ = x_sample.shape
    depth = w_in.shape[0]
    past = cache_na_k.shape[2]
    tm = 512
    cap_c = 2 * seq // NE
    cap_l = 2 * dseq // NE

    cv = jnp.concatenate([c_ctx[None, :], c, jnp.zeros((8 - 1 - db, D), F32)], axis=0)
    mod_all = _modulation(cv, w_mod, b_mod)
    rope = _rope_tables(dseq)

    xc = x_prompt.reshape(nb * seq, D)
    xl = x_sample.reshape(db * dseq, D)
    ctx_out = []
    caches = [jnp.zeros((nb, depth, seq, GW), F32) for _ in range(4)]
    for l in range(depth):
        lam_init = 0.8 - 0.6 * math.exp(-0.3 * l)
        p = _layer_params(l, w_in, ml_gate_b, na_qn_g, na_kn_g, df_qn_g, df_kn_g, rg_wa, rg_wx, rg_ba, rg_bx,
                          df_lq1, df_lk1, df_lq2, df_lk2, df_subln_g, w_out, router_w)
        mod = mod_all[l]
        g1 = norm1_g[l].reshape(1, D)
        g2 = norm2_g[l].reshape(1, D)
        mlg = ml_norm_g[l].reshape(1, GW)
        cb = rg_conv_b[l].reshape(1, GW)
        tb = _rpb_table(na_rpb[l], rows=dseq // GRID_W, rps=NA_RPS)

        nt_c = nb * seq // tm
        ml, gates, naq, rg, dfq, *caches = _proj_in(xc, mod, g1, p["w_r"], p["gate_b"], p["qkg"], caches=caches,
                                                    layer=l, row_base=0, row_div=nt_c, tm=tm)
        hf, hb, cn_c, m_c = _mlstm(ml, gates, jnp.zeros((nb, 2, NH, 2, HD, 128), F32), jnp.zeros((nb, 8, 128), F32),
                                   nreq=nb, nc=seq // CHUNK)
        y_na, o_df = _ctx_attn(naq, dfq, caches, l, p["lamp"], lam_init, nreq=nb, n=seq)
        y_rg, rg_fin = _rglru(rg, rg_conv_w[l], cb, p["wbd"], p["rg_bias"], rg_lam[l],
                              jnp.zeros((nb, 2, GW), F32), nreq=nb, n=seq)
        x1_c, hn2_c, lg_c = _merge(hf, hb, ml, y_na, y_rg, o_df, xc, mod, mlg, p["sub"], p["wout"], g2, p["rw"],
                                   lam_init, row_base=0, row_div=nt_c, tm=tm)
        pos_c, aff_c = _route(lg_c, nreq=nb, n=seq, cap=cap_c, rb=8)
        xs_c, w_c = _gather(pos_c, aff_c, hn2_c, nreq=nb, n=seq, cap=cap_c, eb=NE)
        ctx_out.append((cn_c, m_c, rg_fin))

        tiles_req = dseq // tm
        ml, gates, na, rg, df = _proj_in(xl, mod, g1, p["w_r"], p["gate_b"], p["qkg"], rope=rope,
                                         row_base=1, row_div=tiles_req, tm=tm)
        cn0 = _mlstm_pack_state(state_ml_c[:, l], state_ml_n[:, l])
        m0 = jnp.broadcast_to(state_ml_m[:, l].reshape(db, 8, 1), (db, 8, 128))
        hf, hb, _, _ = _mlstm(ml, gates, cn0, m0, nreq=db, nc=dseq // CHUNK)
        y_na = _na_lat(na, cache_na_k[:, l].reshape(db, past, GW), cache_na_v[:, l].reshape(db, past, GW), tb,
                       nreq=db, n=dseq, rps=NA_RPS)
        o_df = _df_lat(df, cache_df_k[:, l].reshape(db, past, GW), cache_df_v[:, l].reshape(db, past, GW),
                       p["lamp"], lam_init, nreq=db, n=dseq)
        y_rg, _ = _rglru(rg, rg_conv_w[l], cb, p["wbd"], p["rg_bias"], rg_lam[l], state_rg_h[:, l],
                         nreq=db, n=dseq)
        x1_l, hn2_l, lg_l = _merge(hf, hb, ml, y_na, y_rg, o_df, xl, mod, mlg, p["sub"], p["wout"], g2, p["rw"],
                                   lam_init, row_base=1, row_div=tiles_req, tm=tm)
        pos_l, aff_l = _route(lg_l, nreq=db, n=dseq, cap=cap_l, rb=1)
        xs_l, w_l = _gather(pos_l, aff_l, hn2_l, nreq=db, n=dseq, cap=cap_l, eb=1)

        y_c, y_l = _experts(xs_c, xs_l, w_c, w_l, moe_wg, moe_wu, moe_wd, l)
        xc = _scatter(pos_c, y_c, x1_c, mod, nreq=nb, n=seq, cap=cap_c, eb=NE, row_base=0, row_mul=0)
        xl = _scatter(pos_l, y_l, x1_l, mod, nreq=db, n=dseq, cap=cap_l, eb=1, row_base=1, row_mul=1)

    y_prompt = xc.reshape(nb, seq, D)
    y_sample = xl.reshape(db, dseq, D)
    st = lambda f: jnp.stack([f(o) for o in ctx_out], axis=1)
    na_k = caches[0].reshape(nb, depth, seq, NH, HD)
    na_v = caches[1].reshape(nb, depth, seq, NH, HD)
    df_k = caches[2].reshape(nb, depth, seq, NH, 2, HD // 2)
    df_v = caches[3].reshape(nb, depth, seq, NH, HD)
    ml_c = st(lambda o: _mlstm_unpack_state(o[0])[0])
    ml_n = st(lambda o: _mlstm_unpack_state(o[0])[1])
    ml_m = st(lambda o: o[1][:, :, 0].reshape(nb, 2, NH))
    rg_h = st(lambda o: o[2])
    return (y_prompt, y_sample, na_k, na_v, df_k, df_v, ml_c, ml_n, ml_m, rg_h)
```

```python
import functools
import math

import numpy as np
import jax
import jax.numpy as jnp
from jax import lax
from jax.experimental import pallas as pl
from jax.experimental.pallas import tpu as pltpu

F32 = jnp.float32
BF16 = jnp.bfloat16

D = 1024
GW = 256
NH = 4
HD = 64
NE = 16
EPS = 1e-6
CHUNK = 256
GRID_W = 64
NA_ROWS = 8
NA_COLS = 16
SUB_ROWS = 256
NA_RPS = 4
RG_C = 8.0
ROPE_BASE = 10000.0
PROJ_PAD = 3200
VMEM_LIMIT_BYTES = 56 * 1024 * 1024
NEG_BIG = -1e30
LOG2E = 1.4426950408889634


def _bf(x):
    return x.astype(BF16)


def _dot(a, b):
    return jnp.dot(a, b, preferred_element_type=F32)


def _dot_nt(a, b):
    return lax.dot_general(a, b, (((1,), (1,)), ((), ())), preferred_element_type=F32)


def _dot_tn(a, b):
    return lax.dot_general(a, b, (((0,), (0,)), ((), ())), preferred_element_type=F32)


def _split3(x):
    p0 = _bf(x)
    r1 = x - p0.astype(F32)
    p1 = _bf(r1)
    return p0, p1, _bf(r1 - p1.astype(F32))


def _params(n_axes):
    return pltpu.CompilerParams(dimension_semantics=("arbitrary",) * n_axes,
                                vmem_limit_bytes=VMEM_LIMIT_BYTES)


def _full(shape):
    return pl.BlockSpec(shape, lambda *_: (0,) * len(shape))


def _sds(shape, dtype=F32):
    return jax.ShapeDtypeStruct(shape, dtype)


def _softplus(x):
    return jnp.maximum(x, 0.0) + jnp.log1p(jnp.exp(-jnp.abs(x)))


def _log_sigmoid(x):
    return jnp.minimum(x, 0.0) - jnp.log1p(jnp.exp(-jnp.abs(x)))


def _seg_rms(x, nseg, g_row):
    seg = x.shape[-1] // nseg
    lane = lax.broadcasted_iota(jnp.int32, x.shape, 1)
    x2 = x * x
    tot = jnp.zeros_like(x)
    for s in range(nseg):
        m = (lane >= s * seg) & (lane < (s + 1) * seg)
        t = jnp.sum(jnp.where(m, x2, 0.0), axis=-1, keepdims=True)
        tot = jnp.where(m, t, tot)
    return x * lax.rsqrt(tot * (1.0 / seg) + EPS) * g_row


def _mod_row(pid, row_base, row_div):
    return row_base + pid // row_div


def _mod_kernel(cv_ref, w_ref, b_ref, o_ref):
    cv = cv_ref[...]
    s = cv * jax.nn.sigmoid(cv)
    o_ref[0] = _dot(_bf(s), _bf(w_ref[0])) + b_ref[0]


def _modulation(cv, w_mod, b_mod):
    nl = w_mod.shape[0]
    tn = 1536
    return pl.pallas_call(
        _mod_kernel,
        grid=(nl, 6 * D // tn),
        in_specs=[_full((8, D)),
                  pl.BlockSpec((1, D, tn), lambda l, j: (l, 0, j)),
                  pl.BlockSpec((1, 1, tn), lambda l, j: (l, 0, j))],
        out_specs=pl.BlockSpec((1, 8, tn), lambda l, j: (l, 0, j)),
        out_shape=_sds((nl, 8, 6 * D)),
        compiler_params=_params(2),
        name="modulation",
    )(cv, w_mod, b_mod.reshape(nl, 1, 6 * D))


def _rope(x, cos_t, sin_t):
    lane = lax.broadcasted_iota(jnp.int32, x.shape, 1)
    first = (lane & 15) < 8
    sw = jnp.where(first, pltpu.roll(x, GW - 8, axis=1), pltpu.roll(x, 8, axis=1))
    return x * cos_t + sw * sin_t


def _proj_kernel(*refs, row_base, row_div, layer):
    ctx = layer is not None
    if ctx:
        (x_ref, mod_ref, g1_ref, w_ref, gb_ref, qkg_ref, _, _, _, _,
         ml_ref, gate_ref, naq_ref, rg_ref, dfq_ref, nk_ref, nv_ref, dk_ref, dv_ref) = refs
    else:
        (x_ref, mod_ref, g1_ref, w_ref, gb_ref, qkg_ref, cos_ref, sin_ref,
         ml_ref, gate_ref, na_ref, rg_ref, df_ref) = refs
    r = _mod_row(pl.program_id(0), row_base, row_div)
    sh = mod_ref[pl.ds(r, 1), 0:D]
    sc = mod_ref[pl.ds(r, 1), D:2 * D]
    for sub in range(x_ref.shape[0] // SUB_ROWS):
        rows = slice(sub * SUB_ROWS, (sub + 1) * SUB_ROWS)
        x = x_ref[rows, :]
        ms = jnp.mean(x * x, axis=-1, keepdims=True)
        y = x * lax.rsqrt(ms + EPS) * g1_ref[...]
        hn = _bf(y * (1.0 + sc) + sh)

        ml = _dot(hn, w_ref[:, 0:1024])
        ml_ref[rows, 0:256] = ml[:, 0:256]
        ml_ref[rows, 256:512] = ml[:, 256:512] * (HD ** -0.5)
        ml_ref[rows, 512:1024] = ml[:, 512:1024]

        gz = _dot(hn, w_ref[:, 3072:3200]) + gb_ref[...]
        lane = lax.broadcasted_iota(jnp.int32, gz.shape, 1)
        gate_ref[rows, :] = jnp.where(((lane >> 2) & 1) == 1, _log_sigmoid(gz), gz)

        rg_ref[rows, :] = _dot(hn, w_ref[:, 1792:2304])

        nz = _dot(hn, w_ref[:, 1024:1792])
        nq = _seg_rms(nz[:, 0:256], NH, qkg_ref[0:1, :])
        nk = _seg_rms(nz[:, 256:512], NH, qkg_ref[1:2, :])
        dz = _dot(hn, w_ref[:, 2304:3072])
        dq = _seg_rms(dz[:, 0:256], 2 * NH, qkg_ref[2:3, :])
        dk = _seg_rms(dz[:, 256:512], 2 * NH, qkg_ref[3:4, :])
        if ctx:
            naq_ref[rows, :] = nq
            dfq_ref[rows, :] = dq
            nk_ref[sub, 0] = nk
            nv_ref[sub, 0] = nz[:, 512:768]
            dk_ref[sub, 0] = dk
            dv_ref[sub, 0] = dz[:, 512:768]
        else:
            cos_t = cos_ref[rows, :]
            sin_t = sin_ref[rows, :]
            na_ref[rows, 0:256] = nq
            na_ref[rows, 256:512] = nk
            na_ref[rows, 512:768] = nz[:, 512:768]
            df_ref[rows, 0:256] = _rope(dq, cos_t, sin_t)
            df_ref[rows, 256:512] = _rope(dk, cos_t, sin_t)
            df_ref[rows, 512:768] = dz[:, 512:768]


def _proj_in(x, mod, g1, w_r, gate_b, qkg, *, rope=None, caches=None, layer=None, row_base, row_div, tm=512):
    tp = x.shape[0]
    in_specs = [pl.BlockSpec((tm, D), lambda i: (i, 0)), _full((8, 6 * D)), _full((1, D)),
                _full((D, PROJ_PAD)), _full((1, 128)), _full((4, GW))]
    args = [x, mod, g1, w_r, gate_b, qkg]
    tok = lambda n: pl.BlockSpec((tm, n), lambda i: (i, 0))
    if caches is None:
        tiles = rope[0].shape[0] // tm
        in_specs += [pl.BlockSpec((tm, GW), lambda i: (i % tiles, 0))] * 2
        args += list(rope)
        out_specs = [tok(1024), tok(128), tok(768), tok(512), tok(768)]
        out_shape = [_sds((tp, n)) for n in (1024, 128, 768, 512, 768)]
        aliases = {}
    else:
        seq = caches[0].shape[2]
        assert seq == SUB_ROWS
        cspec = pl.BlockSpec((tm // seq, 1, seq, GW), lambda i: (i, layer, 0, 0))
        in_specs += [pl.BlockSpec(memory_space=pl.ANY)] * 4
        args += list(caches)
        out_specs = [tok(1024), tok(128), tok(GW), tok(512), tok(GW)] + [cspec] * 4
        out_shape = [_sds((tp, n)) for n in (1024, 128, GW, 512, GW)] + [_sds(c.shape) for c in caches]
        aliases = {6 + i: 5 + i for i in range(4)}
    return pl.pallas_call(
        functools.partial(_proj_kernel, row_base=row_base, row_div=row_div, layer=layer),
        grid=(tp // tm,),
        in_specs=in_specs,
        out_specs=out_specs,
        out_shape=out_shape,
        input_output_aliases=aliases,
        compiler_params=_params(1),
        name="proj_in",
    )(*args)


def _mlstm_dir(d, q_ref, k_ref, v_ref, g_ref, h_ref, st_s, m_s):
    lc = CHUNK
    g = g_ref[...]
    gt = g.T
    row = lax.broadcasted_iota(jnp.int32, (lc, lc), 0)
    col = lax.broadcasted_iota(jnp.int32, (lc, lc), 1)
    tri = (row >= col) if d == 0 else (row <= col)
    tri_t = (col >= row) if d == 0 else (col <= row)
    g_parts = _split3(g)
    gt_parts = _split3(gt)
    tri_b = tri.astype(F32).astype(BF16)
    tri_tb = tri_t.astype(F32).astype(BF16)
    bc_col = _dot(tri_b, g_parts[0]) + _dot(tri_b, g_parts[1]) + _dot(tri_b, g_parts[2])
    bc_row = _dot(gt_parts[0], tri_tb) + _dot(gt_parts[1], tri_tb) + _dot(gt_parts[2], tri_tb)
    r_rows = gt[d * 8:d * 8 + NH, :] - bc_row[d * 8 + NH:d * 8 + 2 * NH, :]
    k_t = k_ref[...].T
    k_tb = _bf(k_t)
    q_all = _bf(q_ref[...])
    v_all = _bf(v_ref[...])
    ones = jnp.ones((lc, 128), BF16)
    last = lc - 1 if d == 0 else 0
    hs = [slice(h * HD, (h + 1) * HD) for h in range(NH)]
    ms = [m_s[d * NH + h:d * NH + h + 1, 0:1] for h in range(NH)]
    rms = [jnp.where(tri, r_rows[h:h + 1, :], -jnp.inf) for h in range(NH)]
    big_rs = [jnp.maximum(jnp.max(rms[h], axis=-1, keepdims=True), ms[h]) for h in range(NH)]
    ss = [_bf(_dot(q_all[:, hs[h]], k_tb[hs[h], :]) * jnp.exp(rms[h] - big_rs[h])) for h in range(NH)]
    outs = []
    for h in range(NH):
        v2 = v_all[:, (h // 2) * 128:(h // 2 + 1) * 128]
        e = jnp.exp(ms[h] - big_rs[h])
        qh = q_all[:, hs[h]]
        tot = _dot(ss[h], v2) + e * _dot(qh, _bf(st_s[d, h, 0]))
        den = _dot(ss[h], ones) + e * _dot(qh, _bf(st_s[d, h, 1]))
        c_col = bc_col[:, d * 8 + NH + h:d * 8 + NH + h + 1]
        floor = jnp.exp(-(c_col + big_rs[h]))
        outs.append(tot / jnp.maximum(jnp.abs(den), floor))

        r_last = big_rs[h][last:last + 1, :]
        kw = _bf(k_t[hs[h], :] * jnp.exp(r_rows[h:h + 1, :] - r_last))
        gdec = jnp.exp(ms[h] - r_last)
        st_s[d, h, 0] = gdec * st_s[d, h, 0] + _dot(kw, v2)
        st_s[d, h, 1] = gdec * st_s[d, h, 1] + _dot(kw, ones)
        m_s[d * NH + h:d * NH + h + 1, :] = jnp.broadcast_to(c_col[last:last + 1, :] + r_last, (1, 128))
    lane = lax.broadcasted_iota(jnp.int32, (lc, 128), 1)
    h_ref[...] = jnp.concatenate([jnp.where(lane < HD, outs[2 * p], outs[2 * p + 1]) for p in range(NH // 2)],
                                 axis=-1)


def _mlstm_kernel(qf, kf, vf, gf, qb, kb, vb, gb, st0_ref, m0_ref,
                  hf_ref, hb_ref, st_out, m_out, st_s, m_s):
    j = pl.program_id(1)

    @pl.when(j == 0)
    def _():
        st_s[...] = st0_ref[0]
        m_s[...] = m0_ref[0]

    _mlstm_dir(0, qf, kf, vf, gf, hf_ref, st_s, m_s)
    _mlstm_dir(1, qb, kb, vb, gb, hb_ref, st_s, m_s)

    @pl.when(j == pl.num_programs(1) - 1)
    def _():
        st_out[0] = st_s[...]
        m_out[0] = m_s[...]


def _mlstm_pack_state(c, n):
    z = jnp.zeros_like(c)
    odd = (jnp.arange(NH) % 2 == 1)[None, None, :, None, None]
    c_pair = jnp.where(odd, jnp.concatenate([z, c], axis=-1), jnp.concatenate([c, z], axis=-1))
    n_rep = jnp.broadcast_to(n[..., None], n.shape + (128,))
    return jnp.stack([c_pair, n_rep], axis=3)


def _mlstm_unpack_state(st):
    odd = (jnp.arange(NH) % 2 == 1)[None, None, :, None, None]
    c = jnp.where(odd, st[:, :, :, 0, :, HD:2 * HD], st[:, :, :, 0, :, 0:HD])
    return c, st[:, :, :, 1, :, 0]


def _mlstm(mlz, gates, st0, m0, *, nreq, nc):
    tp = mlz.shape[0]
    st_spec = pl.BlockSpec((1, 2, NH, 2, HD, 128), lambda r, j: (r, 0, 0, 0, 0, 0))

    def fwd(col):
        return lambda r, j: (r * nc + j, col)

    def bwd(col):
        return lambda r, j: (r * nc + nc - 1 - j, col)

    in_specs = []
    for mk in (fwd, bwd):
        in_specs += [pl.BlockSpec((CHUNK, GW), mk(0)), pl.BlockSpec((CHUNK, GW), mk(1)),
                     pl.BlockSpec((CHUNK, GW), mk(2)), pl.BlockSpec((CHUNK, 128), mk(0))]
    in_specs += [st_spec, pl.BlockSpec((1, 8, 128), lambda r, j: (r, 0, 0))]
    return pl.pallas_call(
        _mlstm_kernel,
        grid=(nreq, nc),
        in_specs=in_specs,
        out_specs=[pl.BlockSpec((CHUNK, GW), fwd(0)), pl.BlockSpec((CHUNK, GW), bwd(0)),
                   st_spec, pl.BlockSpec((1, 8, 128), lambda r, j: (r, 0, 0))],
        out_shape=[_sds((tp, GW)), _sds((tp, GW)), _sds((nreq, 2, NH, 2, HD, 128)), _sds((nreq, 8, 128))],
        scratch_shapes=[pltpu.VMEM((2, NH, 2, HD, 128), F32), pltpu.VMEM((8, 128), F32)],
        compiler_params=_params(2),
        name="mlstm",
    )(mlz, mlz, mlz, gates, mlz, mlz, mlz, gates, st0, m0)


def _diff_lambda(lam_ref, lam_init):
    lp = lam_ref[...]
    a = jnp.exp(jnp.sum(lp[0:1, :] * lp[1:2, :], axis=-1, keepdims=True))
    b = jnp.exp(jnp.sum(lp[2:3, :] * lp[3:4, :], axis=-1, keepdims=True))
    return a - b + lam_init


def _softmax_pv(q, k_t, v):
    s = _dot(q, k_t)
    e = jnp.exp2(s - jnp.max(s, axis=-1, keepdims=True))
    return _dot(_bf(e), v) / jnp.sum(e, axis=-1, keepdims=True)


def _ctx_attn_kernel(nq, nk, nv, dq, dk, dv, lam_ref, ona_ref, odf_ref, *, lam_init):
    lam = _diff_lambda(lam_ref, lam_init)
    nk_t = _bf(nk[0, 0].T)
    dk_t = _bf(dk[0, 0].T)
    na_out, df_out = [], []
    for h in range(NH):
        hs = slice(h * HD, (h + 1) * HD)
        na_out.append(_softmax_pv(_bf(nq[:, hs] * (HD ** -0.5 * LOG2E)), nk_t[hs, :], _bf(nv[0, 0, :, hs])))
        vh = _bf(dv[0, 0, :, hs])
        os = []
        for c in range(2):
            cs = slice(h * HD + c * 32, h * HD + (c + 1) * 32)
            os.append(_softmax_pv(_bf(dq[:, cs] * (32 ** -0.5 * LOG2E)), dk_t[cs, :], vh))
        df_out.append(os[0] - lam * os[1])
    ona_ref[...] = _bf(jnp.concatenate(na_out, axis=-1))
    odf_ref[...] = jnp.concatenate(df_out, axis=-1)


def _ctx_attn(naq, dfq, caches, layer, lamp, lam_init, *, nreq, n):
    tp = naq.shape[0]
    qs = pl.BlockSpec((n, GW), lambda b: (b, 0))
    cs = pl.BlockSpec((1, 1, n, GW), lambda b: (b, layer, 0, 0))
    return pl.pallas_call(
        functools.partial(_ctx_attn_kernel, lam_init=lam_init),
        grid=(nreq,),
        in_specs=[qs, cs, cs, qs, cs, cs, _full((4, 32))],
        out_specs=[qs, qs],
        out_shape=[_sds((tp, GW), BF16), _sds((tp, GW))],
        compiler_params=_params(1),
        name="ctx_attn",
    )(naq, caches[0], caches[1], dfq, caches[2], caches[3], lamp)


def _rpb_kernel(rpb_ref, o_ref, *, rows, rps):
    h = pl.program_id(0)
    shape = (GRID_W, NA_ROWS * GRID_W)
    lane = lax.broadcasted_iota(jnp.int32, shape, 1)
    cq = lax.broadcasted_iota(jnp.int32, shape, 0)
    ck = lane & (GRID_W - 1)
    c0 = jnp.clip(cq - NA_COLS // 2, 0, GRID_W - NA_COLS)
    ok = (ck >= c0) & (ck < c0 + NA_COLS)
    ncol = 2 * NA_COLS - 1
    ndr = 2 * NA_ROWS - 1
    x = lax.broadcasted_iota(jnp.int32, (1, 2 * GRID_W), 1)
    didx = jnp.clip(x - GRID_W, -(NA_COLS - 1), NA_COLS - 1) + (NA_COLS - 1)
    tiles = []
    for dr in range(ndr):
        frow = jnp.zeros((1, 2 * GRID_W), F32)
        for b in range(ncol):
            frow = jnp.where(didx == b, rpb_ref[(h * ndr + dr) * ncol + b], frow)
        rolled = pltpu.roll(jnp.broadcast_to(frow, (GRID_W, 2 * GRID_W)), GRID_W, axis=1, stride=1, stride_axis=0)
        tiles.append(rolled[:, 0:GRID_W])
    tbs = [jnp.where(ok, jnp.concatenate(tiles[s:s + NA_ROWS], axis=-1) * LOG2E, NEG_BIG) for s in range(NA_ROWS)]
    wrows = NA_ROWS + rps
    steps = rows // rps
    for t, j in enumerate((0, 1, steps - 1)):
        w0 = min(max(j * rps - NA_ROWS // 2, 0), rows - wrows)
        for a in range(rps):
            r = j * rps + a
            r0 = min(max(r - NA_ROWS // 2, 0), rows - NA_ROWS)
            off = (r0 - w0) * GRID_W
            rest = rps * GRID_W - off
            pieces = [tbs[r0 - r + NA_ROWS - 1]]
            if off:
                pieces = [jnp.full((GRID_W, off), NEG_BIG, F32)] + pieces
            if rest:
                pieces = pieces + [jnp.full((GRID_W, rest), NEG_BIG, F32)]
            o_ref[0, t, a * GRID_W:(a + 1) * GRID_W, :] = jnp.concatenate(pieces, axis=-1)


def _na_step_offsets(j, rows, rps):
    w0 = min(max(j * rps - NA_ROWS // 2, 0), rows - NA_ROWS - rps)
    return [(min(max(r - NA_ROWS // 2, 0), rows - NA_ROWS) - r,
             min(max(r - NA_ROWS // 2, 0), rows - NA_ROWS) - w0) for r in range(j * rps, (j + 1) * rps)]


def _rpb_table(rpb, *, rows, rps):
    steps = rows // rps
    assert all(_na_step_offsets(j, rows, rps) == _na_step_offsets(1, rows, rps) for j in range(1, steps - 1))
    shape = (rps * GRID_W, (NA_ROWS + rps) * GRID_W)
    return pl.pallas_call(
        functools.partial(_rpb_kernel, rows=rows, rps=rps),
        grid=(NH,),
        in_specs=[pl.BlockSpec(memory_space=pltpu.SMEM)],
        out_specs=pl.BlockSpec((1, 3) + shape, lambda h: (h, 0, 0, 0)),
        out_shape=_sds((NH, 3) + shape),
        compiler_params=_params(1),
        name="rpb_table",
    )(rpb.reshape(-1))


def _na_lat_kernel(q_ref, k_ref, v_ref, kc_ref, vc_ref, tb_ref, o_ref, k_s, v_s, kct_s, vc_s, *, rows, rps):
    j = pl.program_id(1)

    @pl.when(j == 0)
    def _():
        k_s[...] = _bf(k_ref[...])
        v_s[...] = _bf(v_ref[...])
        kct_s[...] = _bf(kc_ref[0].T)
        vc_s[...] = _bf(vc_ref[0])

    wrows = NA_ROWS + rps
    win = wrows * GRID_W
    w0 = jnp.clip(j * rps - NA_ROWS // 2, 0, rows - wrows)
    start = pl.multiple_of(w0 * GRID_W, GRID_W)
    outs = []
    for h in range(NH):
        hs = slice(h * HD, (h + 1) * HD)
        q = _bf(q_ref[:, hs] * (HD ** -0.5 * LOG2E))
        s_c = _dot(q, kct_s[hs, :])
        s_l = _dot_nt(q, k_s[pl.ds(start, win), hs]) + tb_ref[h, 0]
        m = jnp.maximum(jnp.max(s_c, axis=-1, keepdims=True), jnp.max(s_l, axis=-1, keepdims=True))
        e_c = jnp.exp2(s_c - m)
        e_l = jnp.exp2(s_l - m)
        den = jnp.sum(e_c, axis=-1, keepdims=True) + jnp.sum(e_l, axis=-1, keepdims=True)
        o = _dot(_bf(e_c), vc_s[:, hs]) + _dot(_bf(e_l), v_s[pl.ds(start, win), hs])
        outs.append(o / den)
    o_ref[...] = _bf(jnp.concatenate(outs, axis=-1))


def _na_lat(na, kc, vc, tb, *, nreq, n, rps):
    tp = na.shape[0]
    rows = n // GRID_W
    past = kc.shape[1]
    steps = rows // rps
    tq = rps * GRID_W
    return pl.pallas_call(
        functools.partial(_na_lat_kernel, rows=rows, rps=rps),
        grid=(nreq, steps),
        in_specs=[pl.BlockSpec((tq, GW), lambda b, r: (b * steps + r, 0)),
                  pl.BlockSpec((n, GW), lambda b, r: (b, 1)),
                  pl.BlockSpec((n, GW), lambda b, r: (b, 2)),
                  pl.BlockSpec((1, past, GW), lambda b, r: (b, 0, 0)),
                  pl.BlockSpec((1, past, GW), lambda b, r: (b, 0, 0)),
                  pl.BlockSpec((NH, 1) + tb.shape[2:],
                               lambda b, r: (0, jnp.minimum(r, 1) + r // (steps - 1), 0, 0))],
        out_specs=pl.BlockSpec((tq, GW), lambda b, r: (b * steps + r, 0)),
        out_shape=_sds((tp, GW), BF16),
        scratch_shapes=[pltpu.VMEM((n, GW), BF16), pltpu.VMEM((n, GW), BF16),
                        pltpu.VMEM((GW, past), BF16), pltpu.VMEM((past, GW), BF16)],
        compiler_params=_params(2),
        name="na_latent",
    )(na, na, na, kc, vc, tb)


def _df_lat_kernel(q_ref, k_ref, v_ref, kc_ref, vc_ref, lam_ref, o_ref, kt_s, v_s, kct_s, vc_s, *, lam_init):
    @pl.when(pl.program_id(1) == 0)
    def _():
        kt_s[...] = _bf(k_ref[...].T)
        v_s[...] = _bf(v_ref[...])
        kct_s[...] = _bf(kc_ref[0].T)
        vc_s[...] = _bf(vc_ref[0])

    lam = _diff_lambda(lam_ref, lam_init)
    outs = []
    for h in range(NH):
        hs = slice(h * HD, (h + 1) * HD)
        os = []
        for c in range(2):
            cs = slice(h * HD + c * 32, h * HD + (c + 1) * 32)
            q = _bf(q_ref[:, cs] * (32 ** -0.5 * LOG2E))
            s_c = _dot(q, kct_s[cs, :])
            s_l = _dot(q, kt_s[cs, :])
            m = jnp.maximum(jnp.max(s_c, axis=-1, keepdims=True), jnp.max(s_l, axis=-1, keepdims=True))
            e_c = jnp.exp2(s_c - m)
            e_l = jnp.exp2(s_l - m)
            den = jnp.sum(e_c, axis=-1, keepdims=True) + jnp.sum(e_l, axis=-1, keepdims=True)
            os.append((_dot(_bf(e_c), vc_s[:, hs]) + _dot(_bf(e_l), v_s[:, hs])) / den)
        outs.append(os[0] - lam * os[1])
    o_ref[...] = jnp.concatenate(outs, axis=-1)


def _df_lat(df, kc, vc, lamp, lam_init, *, nreq, n, tq=256):
    tp = df.shape[0]
    nq = n // tq
    past = kc.shape[1]
    return pl.pallas_call(
        functools.partial(_df_lat_kernel, lam_init=lam_init),
        grid=(nreq, nq),
        in_specs=[pl.BlockSpec((tq, GW), lambda b, j: (b * nq + j, 0)),
                  pl.BlockSpec((n, GW), lambda b, j: (b, 1)),
                  pl.BlockSpec((n, GW), lambda b, j: (b, 2)),
                  pl.BlockSpec((1, past, GW), lambda b, j: (b, 0, 0)),
                  pl.BlockSpec((1, past, GW), lambda b, j: (b, 0, 0)),
                  _full((4, 32))],
        out_specs=pl.BlockSpec((tq, GW), lambda b, j: (b * nq + j, 0)),
        out_shape=_sds((tp, GW)),
        scratch_shapes=[pltpu.VMEM((GW, n), BF16), pltpu.VMEM((n, GW), BF16),
                        pltpu.VMEM((GW, past), BF16), pltpu.VMEM((past, GW), BF16)],
        compiler_params=_params(2),
        name="df_latent",
    )(df, df, df, kc, vc, lamp)


def _rg_kernel(x_ref, g_ref, cw_ref, cb_ref, wbd_ref, bias_ref, lam_ref, h0_ref,
               y_ref, fin_ref, a_s, b_s, *, n):
    x = x_ref[...]
    row = lax.broadcasted_iota(jnp.int32, x.shape, 0)
    xc = cb_ref[...] + jnp.where(row >= 2, pltpu.roll(x, 2, axis=0), 0.0) * cw_ref[0:1, :]
    xc = xc + jnp.where(row >= 1, pltpu.roll(x, 1, axis=0), 0.0) * cw_ref[1:2, :]
    xc = xc + x * cw_ref[2:3, :]
    xc = xc + jnp.where(row < n - 1, pltpu.roll(x, n - 1, axis=0), 0.0) * cw_ref[3:4, :]
    z = _dot(_bf(xc), wbd_ref[...]) + bias_ref[...]
    sub = row & 7
    for d in range(2):
        rgate = jax.nn.sigmoid(z[:, 512 * d:512 * d + GW])
        igate = jax.nn.sigmoid(z[:, 512 * d + GW:512 * d + 2 * GW])
        la = -RG_C * rgate * _softplus(-lam_ref[d:d + 1, :])
        a = jnp.exp(la)
        t = jnp.tanh(la)
        b = jnp.sqrt(-2.0 * t / (1.0 - t)) * igate * xc
        for dd in (1, 2, 4):
            if d == 0:
                keep = sub >= dd
                a_sh = jnp.where(keep, pltpu.roll(a, dd, axis=0), 1.0)
                b_sh = jnp.where(keep, pltpu.roll(b, dd, axis=0), 0.0)
            else:
                keep = sub < 8 - dd
                a_sh = jnp.where(keep, pltpu.roll(a, n - dd, axis=0), 1.0)
                b_sh = jnp.where(keep, pltpu.roll(b, n - dd, axis=0), 0.0)
            b = b + a * b_sh
            a = a * a_sh
        a_s[d] = a
        b_s[d] = b

    nt = n // 8

    def body(t, carry):
        hf, hb = carry
        sf = pl.multiple_of(t * 8, 8)
        sb = pl.multiple_of((nt - 1 - t) * 8, 8)
        tf = a_s[0, pl.ds(sf, 8), :] * hf + b_s[0, pl.ds(sf, 8), :]
        tb = a_s[1, pl.ds(sb, 8), :] * hb + b_s[1, pl.ds(sb, 8), :]
        b_s[0, pl.ds(sf, 8), :] = tf
        b_s[1, pl.ds(sb, 8), :] = tb
        return tf[7:8, :], tb[0:1, :]

    hf, hb = lax.fori_loop(0, nt, body, (h0_ref[0, 0:1, :], h0_ref[0, 1:2, :]))
    fin_ref[0, 0:1, :] = hf
    fin_ref[0, 1:2, :] = hb
    gg = g_ref[...]
    cdf = 0.5 * (1.0 + jnp.tanh(math.sqrt(2.0 / math.pi) * (gg + 0.044715 * (gg * gg * gg))))
    y_ref[...] = _bf((b_s[0] + b_s[1]) * (gg * cdf))


def _rglru(rg, cw, cb, wbd, bias, lam, h0, *, nreq, n):
    tp = rg.shape[0]
    return pl.pallas_call(
        functools.partial(_rg_kernel, n=n),
        grid=(nreq,),
        in_specs=[pl.BlockSpec((n, GW), lambda b: (b, 0)), pl.BlockSpec((n, GW), lambda b: (b, 1)),
                  _full((4, GW)), _full((1, GW)), _full((GW, 4 * GW)), _full((1, 4 * GW)), _full((2, GW)),
                  pl.BlockSpec((1, 2, GW), lambda b: (b, 0, 0))],
        out_specs=[pl.BlockSpec((n, GW), lambda b: (b, 0)), pl.BlockSpec((1, 2, GW), lambda b: (b, 0, 0))],
        out_shape=[_sds((tp, GW), BF16), _sds((nreq, 2, GW))],
        scratch_shapes=[pltpu.VMEM((2, n, GW), F32), pltpu.VMEM((2, n, GW), F32)],
        compiler_params=_params(1),
        name="rglru",
    )(rg, rg, cw, cb, wbd, bias, lam, h0)


def _merge_kernel(hf_ref, hb_ref, mlo_ref, yna_ref, yrg_ref, odf_ref, x_ref, mod_ref, mlg_ref, sub_ref,
                  wout_ref, n2_ref, rw_ref, x1_ref, hn2_ref, lg_ref, *, row_base, row_div, lam_init):
    r = _mod_row(pl.program_id(0), row_base, row_div)
    y_ml = _seg_rms(hf_ref[...] + hb_ref[...], NH, mlg_ref[...]) * jax.nn.sigmoid(mlo_ref[...])
    y_df = _seg_rms(odf_ref[...], NH, sub_ref[...]) * (1.0 - lam_init)
    y = jnp.concatenate([_bf(y_ml), _bf(yna_ref[...]), _bf(yrg_ref[...]), _bf(y_df)], axis=-1)
    o = _dot(y, wout_ref[...])
    x1 = x_ref[...] + mod_ref[pl.ds(r, 1), 2 * D:3 * D] * o
    x1_ref[...] = x1
    ms = jnp.mean(x1 * x1, axis=-1, keepdims=True)
    hn = x1 * lax.rsqrt(ms + EPS) * n2_ref[...]
    hn = _bf(hn * (1.0 + mod_ref[pl.ds(r, 1), 4 * D:5 * D]) + mod_ref[pl.ds(r, 1), 3 * D:4 * D])
    hn2_ref[...] = hn
    lg_ref[...] = _dot(hn, rw_ref[...])


def _merge(hf, hb, mlz, yna, yrg, odf, x, mod, mlg, sub, wout, n2, rw, lam_init, *, row_base, row_div, tm=512):
    tp = x.shape[0]
    g = lambda c: pl.BlockSpec((tm, GW), lambda i: (i, c))
    return pl.pallas_call(
        functools.partial(_merge_kernel, row_base=row_base, row_div=row_div, lam_init=lam_init),
        grid=(tp // tm,),
        in_specs=[g(0), g(0), g(3), g(0), g(0), g(0),
                  pl.BlockSpec((tm, D), lambda i: (i, 0)), _full((8, 6 * D)), _full((1, GW)), _full((1, GW)),
                  _full((D, D)), _full((1, D)), _full((D, 128))],
        out_specs=[pl.BlockSpec((tm, D), lambda i: (i, 0)), pl.BlockSpec((tm, D), lambda i: (i, 0)),
                   pl.BlockSpec((tm, 128), lambda i: (i, 0))],
        out_shape=[_sds((tp, D)), _sds((tp, D), BF16), _sds((tp, 128))],
        compiler_params=_params(1),
        name="merge",
    )(hf, hb, mlz, yna, yrg, odf, x, mod, mlg, sub, wout, n2, rw)


def _excl_cumsum_lanes(mask):
    blk = 256
    r = lax.broadcasted_iota(jnp.int32, (blk, blk), 0)
    c = lax.broadcasted_iota(jnp.int32, (blk, blk), 1)
    tri = (r < c).astype(F32).astype(BF16)
    off = jnp.zeros((mask.shape[0], 1), F32)
    outs = []
    for i in range(mask.shape[1] // blk):
        mb = mask[:, i * blk:(i + 1) * blk]
        outs.append(_dot(_bf(mb), tri) + off)
        off = off + jnp.sum(mb, axis=-1, keepdims=True)
    return jnp.concatenate(outs, axis=-1)


def _route_kernel(lg_ref, pos_ref, aff_ref, *, cap, n, rb):
    lg = lg_ref[...].T[0:NE, :]
    ex = jnp.exp(lg - jnp.max(lg, axis=0, keepdims=True))
    aff = ex / jnp.sum(ex, axis=0, keepdims=True)
    aff_ref[...] = aff
    aff = jnp.concatenate([aff[:, i * n:(i + 1) * n] for i in range(rb)], axis=0)
    thr = jnp.zeros((rb * NE, 1), jnp.int32)
    for bit in range(30, -1, -1):
        cand = thr | (1 << bit)
        cnt = jnp.sum((aff >= pltpu.bitcast(cand, F32)).astype(jnp.int32), axis=-1, keepdims=True)
        thr = jnp.where(cnt >= cap, cand, thr)
    gt = aff >= pltpu.bitcast(thr + 1, F32)
    eq = (aff >= pltpu.bitcast(thr, F32)) & jnp.logical_not(gt)
    need = (cap - jnp.sum(gt.astype(jnp.int32), axis=-1, keepdims=True)).astype(F32)
    eq_rank = _excl_cumsum_lanes(eq.astype(F32))
    sel = gt | (eq & (eq_rank < need))
    slot = _excl_cumsum_lanes(sel.astype(F32))
    pos = jnp.where(sel, slot.astype(jnp.int32), -1)
    for i in range(rb):
        pos_ref[:, i * n:(i + 1) * n] = pos[i * NE:(i + 1) * NE, :]


def _route(lg, *, nreq, n, cap, rb):
    tp = lg.shape[0]
    return pl.pallas_call(
        functools.partial(_route_kernel, cap=cap, n=n, rb=rb),
        grid=(nreq // rb,),
        in_specs=[pl.BlockSpec((rb * n, 128), lambda b: (b, 0))],
        out_specs=[pl.BlockSpec((NE, rb * n), lambda b: (0, b)), pl.BlockSpec((NE, rb * n), lambda b: (0, b))],
        out_shape=[_sds((NE, tp), jnp.int32), _sds((NE, tp))],
        compiler_params=_params(1),
        name="route",
    )(lg)


def _gather_kernel(pos_ref, aff_ref, h_ref, xs_ref, w_ref, *, eb, cap, n):
    eg = pl.program_id(1)
    io = lax.broadcasted_iota(jnp.int32, (cap, n), 0)
    sels = []
    for k in range(eb):
        e = eg * eb + k
        sel = pos_ref[pl.ds(e, 1), :] == io
        sels.append(sel.astype(F32).astype(BF16))
        w = jnp.sum(jnp.where(sel, aff_ref[pl.ds(e, 1), :], 0.0), axis=-1, keepdims=True)
        w_ref[k] = jnp.broadcast_to(w, (cap, 128))
    xs = _dot(jnp.concatenate(sels, axis=0), h_ref[...]).astype(BF16)
    for k in range(eb):
        xs_ref[k] = xs[k * cap:(k + 1) * cap, :]


def _gather(pos, aff, hn2, *, nreq, n, cap, eb):
    return pl.pallas_call(
        functools.partial(_gather_kernel, eb=eb, cap=cap, n=n),
        grid=(nreq, NE // eb),
        in_specs=[pl.BlockSpec((NE, n), lambda b, g: (0, b)), pl.BlockSpec((NE, n), lambda b, g: (0, b)),
                  pl.BlockSpec((n, D), lambda b, g: (b, 0))],
        out_specs=[pl.BlockSpec((eb, cap, D), lambda b, g: (g, b, 0)),
                   pl.BlockSpec((eb, cap, 128), lambda b, g: (g, b, 0))],
        out_shape=[_sds((NE, nreq * cap, D), BF16), _sds((NE, nreq * cap, 128))],
        compiler_params=_params(2),
        name="moe_gather",
    )(pos, aff, hn2)


def _expert_kernel(xc_ref, xl_ref, wc_ref, wl_ref, wg_ref, wu_ref, wd_ref, yc_ref, yl_ref, acc_ref, *, rows):
    f = pl.program_id(1)

    @pl.when(f == 0)
    def _():
        acc_ref[...] = jnp.zeros_like(acc_ref)

    wg = _bf(wg_ref[0, 0])
    wu = _bf(wu_ref[0, 0])
    wd = _bf(wd_ref[0, 0])
    tm = 512
    for part, x_ref in enumerate((xc_ref, xl_ref)):
        for ch in range(rows // tm):
            x = x_ref[0, ch * tm:(ch + 1) * tm, :]
            g = _dot(x, wg)
            u = _dot(x, wu)
            a = _bf(g * jax.nn.sigmoid(g) * u)
            lo = part * rows + ch * tm
            acc_ref[lo:lo + tm, :] += _dot(a, wd)

    @pl.when(f == pl.num_programs(1) - 1)
    def _():
        yc_ref[0] = _bf(acc_ref[0:rows, :] * wc_ref[0][:, 0:1])
        yl_ref[0] = _bf(acc_ref[rows:2 * rows, :] * wl_ref[0][:, 0:1])


def _experts(xs_c, xs_l, w_c, w_l, wg, wu, wd, layer, *, fb=512):
    rows = xs_c.shape[1]
    dff = wg.shape[-1]
    xspec = pl.BlockSpec((1, rows, D), lambda e, f: (e, 0, 0))
    wspec = pl.BlockSpec((1, rows, 128), lambda e, f: (e, 0, 0))
    return pl.pallas_call(
        functools.partial(_expert_kernel, rows=rows),
        grid=(NE, dff // fb),
        in_specs=[xspec, xspec, wspec, wspec,
                  pl.BlockSpec((1, 1, D, fb), lambda e, f: (layer, e, 0, f)),
                  pl.BlockSpec((1, 1, D, fb), lambda e, f: (layer, e, 0, f)),
                  pl.BlockSpec((1, 1, fb, D), lambda e, f: (layer, e, f, 0))],
        out_specs=[xspec, xspec],
        out_shape=[_sds((NE, rows, D), BF16), _sds((NE, rows, D), BF16)],
        scratch_shapes=[pltpu.VMEM((2 * rows, D), F32)],
        compiler_params=_params(2),
        name="moe_experts",
    )(xs_c, xs_l, w_c, w_l, wg, wu, wd)


def _scatter_kernel(pos_ref, y_ref, x1_ref, mod_ref, o_ref, *, eb, cap, n, row_base, row_mul):
    eg = pl.program_id(1)
    last = pl.num_programs(1) - 1
    r = row_base + row_mul * pl.program_id(0)
    g2 = mod_ref[pl.ds(r, 1), 5 * D:6 * D]
    tn = min(n, 512)
    io = lax.broadcasted_iota(jnp.int32, (cap, tn), 0)
    ys = jnp.concatenate([y_ref[k] for k in range(eb)], axis=0)
    for t in range(n // tn):
        cols = slice(t * tn, (t + 1) * tn)
        sels = [(pos_ref[pl.ds(eg * eb + k, 1), cols] == io).astype(F32).astype(BF16) for k in range(eb)]
        part = _dot_tn(jnp.concatenate(sels, axis=0), ys)

        @pl.when(eg == 0)
        def _():
            o_ref[cols, :] = part

        @pl.when(eg > 0)
        def _():
            o_ref[cols, :] += part

    @pl.when(eg == last)
    def _():
        o_ref[...] = x1_ref[...] + g2 * o_ref[...]


def _scatter(pos, y, x1, mod, *, nreq, n, cap, eb, row_base, row_mul):
    tp = x1.shape[0]
    return pl.pallas_call(
        functools.partial(_scatter_kernel, eb=eb, cap=cap, n=n, row_base=row_base, row_mul=row_mul),
        grid=(nreq, NE // eb),
        in_specs=[pl.BlockSpec((NE, n), lambda b, g: (0, b)),
                  pl.BlockSpec((eb, cap, D), lambda b, g: (g, b, 0)),
                  pl.BlockSpec((n, D), lambda b, g: (b, 0)), _full((8, 6 * D))],
        out_specs=pl.BlockSpec((n, D), lambda b, g: (b, 0)),
        out_shape=_sds((tp, D)),
        compiler_params=_params(2),
        name="moe_scatter",
    )(pos, y, x1, mod)


def _rope_tables(n):
    nf = 8
    t = np.arange(n)
    rowp = (t // GRID_W).astype(np.float32)
    colp = (t % GRID_W).astype(np.float32)
    inv = (np.float32(ROPE_BASE) ** (-np.arange(nf, dtype=np.float32) / np.float32(nf))).astype(np.float32)
    lane = np.arange(GW)
    c32 = lane % 32
    pos = np.where((c32 < 16)[None, :], rowp[:, None], colp[:, None]).astype(np.float32)
    ang = (pos * inv[(c32 % 8)][None, :]).astype(np.float32).astype(np.float64)
    sign = np.where((lane % 16) < 8, -1.0, 1.0)[None, :]
    return jnp.asarray(np.cos(ang), F32), jnp.asarray(np.sin(ang) * sign, F32)


def _block_diag(w):
    nb, bi, bo = w.shape
    return (jnp.eye(nb, dtype=w.dtype)[:, None, :, None] * w[:, :, None, :]).reshape(nb * bi, nb * bo)


def _layer_params(l, w_in, ml_gate_b, na_qn_g, na_kn_g, df_qn_g, df_kn_g, rg_wa, rg_wx, rg_ba, rg_bx,
                  df_lq1, df_lk1, df_lq2, df_lk2, df_subln_g, w_out, router_w):
    wi = w_in[l]
    w_r = jnp.concatenate([wi[:, 0:1024], wi[:, 1040:3088], wi[:, 1024:1040],
                           jnp.zeros((D, PROJ_PAD - 3088), F32)], axis=1).astype(BF16)
    gate_b = jnp.pad(ml_gate_b[l], (0, 128 - 16)).reshape(1, 128)
    qkg = jnp.stack([jnp.tile(na_qn_g[l], NH), jnp.tile(na_kn_g[l], NH),
                     jnp.tile(df_qn_g[l], 2 * NH), jnp.tile(df_kn_g[l], 2 * NH)])
    wbd = jnp.concatenate([_block_diag(rg_wa[l, 0]), _block_diag(rg_wx[l, 0]),
                           _block_diag(rg_wa[l, 1]), _block_diag(rg_wx[l, 1])], axis=1).astype(BF16)
    rg_bias = jnp.concatenate([rg_ba[l, 0], rg_bx[l, 0], rg_ba[l, 1], rg_bx[l, 1]]).reshape(1, 4 * GW)
    lamp = jnp.stack([df_lq1[l], df_lk1[l], df_lq2[l], df_lk2[l]])
    sub = jnp.tile(df_subln_g[l], NH).reshape(1, GW)
    rw = jnp.pad(router_w[l], ((0, 0), (0, 128 - NE))).astype(BF16)
    return dict(w_r=w_r, gate_b=gate_b, qkg=qkg, wbd=wbd, rg_bias=rg_bias, lamp=lamp, sub=sub,
                wout=w_out[l].astype(BF16), rw=rw)


def kernel(x_prompt, x_sample, cache_na_k, cache_na_v, cache_df_k, cache_df_v, state_ml_c, state_ml_n, state_ml_m, state_rg_h, c, c_ctx, norm1_g, norm2_g, w_mod, b_mod, w_in, ml_gate_b, ml_norm_g, na_qn_g, na_kn_g, na_rpb, rg_conv_w, rg_conv_b, rg_wa, rg_ba, rg_wx, rg_bx, rg_lam, df_qn_g, df_kn_g, df_lq1, df_lk1, df_lq2, df_lk2, df_subln_g, w_out, router_w, moe_wg, moe_wu, moe_wd):
    nb, seq, _ = x_prompt.shape
    db, dseq, _ = x_sample.shape
    depth = w_in.shape[0]
    past = cache_na_k.shape[2]
    tm = 512
    cap_c = 2 * seq // NE
    cap_l = 2 * dseq // NE

    cv = jnp.concatenate([c_ctx[None, :], c, jnp.zeros((8 - 1 - db, D), F32)], axis=0)
    mod_all = _modulation(cv, w_mod, b_mod)
    rope = _rope_tables(dseq)

    xc = x_prompt.reshape(nb * seq, D)
    xl = x_sample.reshape(db * dseq, D)
    ctx_out = []
    caches = [jnp.zeros((nb, depth, seq, GW), F32) for _ in range(4)]
    for l in range(depth):
        lam_init = 0.8 - 0.6 * math.exp(-0.3 * l)
        p = _layer_params(l, w_in, ml_gate_b, na_qn_g, na_kn_g, df_qn_g, df_kn_g, rg_wa, rg_wx, rg_ba, rg_bx,
                          df_lq1, df_lk1, df_lq2, df_lk2, df_subln_g, w_out, router_w)
        mod = mod_all[l]
        g1 = norm1_g[l].reshape(1, D)
        g2 = norm2_g[l].reshape(1, D)
        mlg = ml_norm_g[l].reshape(1, GW)
        cb = rg_conv_b[l].reshape(1, GW)
        tb = _rpb_table(na_rpb[l], rows=dseq // GRID_W, rps=NA_RPS)

        nt_c = nb * seq // tm
        ml, gates, naq, rg, dfq, *caches = _proj_in(xc, mod, g1, p["w_r"], p["gate_b"], p["qkg"], caches=caches,
                                                    layer=l, row_base=0, row_div=nt_c, tm=tm)
        hf, hb, cn_c, m_c = _mlstm(ml, gates, jnp.zeros((nb, 2, NH, 2, HD, 128), F32), jnp.zeros((nb, 8, 128), F32),
                                   nreq=nb, nc=seq // CHUNK)
        y_na, o_df = _ctx_attn(naq, dfq, caches, l, p["lamp"], lam_init, nreq=nb, n=seq)
        y_rg, rg_fin = _rglru(rg, rg_conv_w[l], cb, p["wbd"], p["rg_bias"], rg_lam[l],
                              jnp.zeros((nb, 2, GW), F32), nreq=nb, n=seq)
        x1_c, hn2_c, lg_c = _merge(hf, hb, ml, y_na, y_rg, o_df, xc, mod, mlg, p["sub"], p["wout"], g2, p["rw"],
                                   lam_init, row_base=0, row_div=nt_c, tm=tm)
        pos_c, aff_c = _route(lg_c, nreq=nb, n=seq, cap=cap_c, rb=8)
        xs_c, w_c = _gather(pos_c, aff_c, hn2_c, nreq=nb, n=seq, cap=cap_c, eb=NE)
        ctx_out.append((cn_c, m_c, rg_fin))

        tiles_req = dseq // tm
        ml, gates, na, rg, df = _proj_in(xl, mod, g1, p["w_r"], p["gate_b"], p["qkg"], rope=rope,
                                         row_base=1, row_div=tiles_req, tm=tm)
        cn0 = _mlstm_pack_state(state_ml_c[:, l], state_ml_n[:, l])
        m0 = jnp.broadcast_to(state_ml_m[:, l].reshape(db, 8, 1), (db, 8, 128))
        hf, hb, _, _ = _mlstm(ml, gates, cn0, m0, nreq=db, nc=dseq // CHUNK)
        y_na = _na_lat(na, cache_na_k[:, l].reshape(db, past, GW), cache_na_v[:, l].reshape(db, past, GW), tb,
                       nreq=db, n=dseq, rps=NA_RPS)
        o_df = _df_lat(df, cache_df_k[:, l].reshape(db, past, GW), cache_df_v[:, l].reshape(db, past, GW),
                       p["lamp"], lam_init, nreq=db, n=dseq)
        y_rg, _ = _rglru(rg, rg_conv_w[l], cb, p["wbd"], p["rg_bias"], rg_lam[l], state_rg_h[:, l],
                         nreq=db, n=dseq)
        x1_l, hn2_l, lg_l = _merge(hf, hb, ml, y_na, y_rg, o_df, xl, mod, mlg, p["sub"], p["wout"], g2, p["rw"],
                                   lam_init, row_base=1, row_div=tiles_req, tm=tm)
        pos_l, aff_l = _route(lg_l, nreq=db, n=dseq, cap=cap_l, rb=1)
        xs_l, w_l = _gather(pos_l, aff_l, hn2_l, nreq=db, n=dseq, cap=cap_l, eb=1)

        y_c, y_l = _experts(xs_c, xs_l, w_c, w_l, moe_wg, moe_wu, moe_wd, l)
        xc = _scatter(pos_c, y_c, x1_c, mod, nreq=nb, n=seq, cap=cap_c, eb=NE, row_base=0, row_mul=0)
        xl = _scatter(pos_l, y_l, x1_l, mod, nreq=db, n=dseq, cap=cap_l, eb=4, row_base=1, row_mul=1)

    y_prompt = xc.reshape(nb, seq, D)
    y_sample = xl.reshape(db, dseq, D)
    st = lambda f: jnp.stack([f(o) for o in ctx_out], axis=1)
    na_k = caches[0].reshape(nb, depth, seq, NH, HD)
    na_v = caches[1].reshape(nb, depth, seq, NH, HD)
    df_k = caches[2].reshape(nb, depth, seq, NH, 2, HD // 2)
    df_v = caches[3].reshape(nb, depth, seq, NH, HD)
    ml_c = st(lambda o: _mlstm_unpack_state(o[0])[0])
    ml_n = st(lambda o: _mlstm_unpack_state(o[0])[1])
    ml_m = st(lambda o: o[1][:, :, 0].reshape(nb, 2, NH))
    rg_h = st(lambda o: o[2])
    return (y_prompt, y_sample, na_k, na_v, df_k, df_v, ml_c, ml_n, ml_m, rg_h)
```

```python
import functools
import math

import numpy as np
import jax
import jax.numpy as jnp
from jax import lax
from jax.experimental import pallas as pl
from jax.experimental.pallas import tpu as pltpu

F32 = jnp.float32
BF16 = jnp.bfloat16

D = 1024
GW = 256
NH = 4
HD = 64
NE = 16
EPS = 1e-6
CHUNK = 256
GRID_W = 64
NA_ROWS = 8
NA_COLS = 16
SUB_ROWS = 256
NA_RPS = 4
RG_C = 8.0
ROPE_BASE = 10000.0
PROJ_PAD = 3200
VMEM_LIMIT_BYTES = 56 * 1024 * 1024
NEG_BIG = -1e30
LOG2E = 1.4426950408889634


def _bf(x):
    return x.astype(BF16)


def _dot(a, b):
    return jnp.dot(a, b, preferred_element_type=F32)


def _dot_nt(a, b):
    return lax.dot_general(a, b, (((1,), (1,)), ((), ())), preferred_element_type=F32)


def _dot_tn(a, b):
    return lax.dot_general(a, b, (((0,), (0,)), ((), ())), preferred_element_type=F32)


def _split3(x):
    p0 = _bf(x)
    r1 = x - p0.astype(F32)
    p1 = _bf(r1)
    return p0, p1, _bf(r1 - p1.astype(F32))


def _params(n_axes):
    return pltpu.CompilerParams(dimension_semantics=("arbitrary",) * n_axes,
                                vmem_limit_bytes=VMEM_LIMIT_BYTES)


def _full(shape):
    return pl.BlockSpec(shape, lambda *_: (0,) * len(shape))


def _sds(shape, dtype=F32):
    return jax.ShapeDtypeStruct(shape, dtype)


def _softplus(x):
    return jnp.maximum(x, 0.0) + jnp.log1p(jnp.exp(-jnp.abs(x)))


def _log_sigmoid(x):
    return jnp.minimum(x, 0.0) - jnp.log1p(jnp.exp(-jnp.abs(x)))


def _seg_rms(x, nseg, g_row):
    seg = x.shape[-1] // nseg
    lane = lax.broadcasted_iota(jnp.int32, x.shape, 1)
    x2 = x * x
    tot = jnp.zeros_like(x)
    for s in range(nseg):
        m = (lane >= s * seg) & (lane < (s + 1) * seg)
        t = jnp.sum(jnp.where(m, x2, 0.0), axis=-1, keepdims=True)
        tot = jnp.where(m, t, tot)
    return x * lax.rsqrt(tot * (1.0 / seg) + EPS) * g_row


def _mod_row(pid, row_base, row_div):
    return row_base + pid // row_div


def _mod_kernel(cv_ref, w_ref, b_ref, o_ref):
    cv = cv_ref[...]
    s = cv * jax.nn.sigmoid(cv)
    o_ref[0] = _dot(_bf(s), _bf(w_ref[0])) + b_ref[0]


def _modulation(cv, w_mod, b_mod):
    nl = w_mod.shape[0]
    tn = 1536
    return pl.pallas_call(
        _mod_kernel,
        grid=(nl, 6 * D // tn),
        in_specs=[_full((8, D)),
                  pl.BlockSpec((1, D, tn), lambda l, j: (l, 0, j)),
                  pl.BlockSpec((1, 1, tn), lambda l, j: (l, 0, j))],
        out_specs=pl.BlockSpec((1, 8, tn), lambda l, j: (l, 0, j)),
        out_shape=_sds((nl, 8, 6 * D)),
        compiler_params=_params(2),
        name="modulation",
    )(cv, w_mod, b_mod.reshape(nl, 1, 6 * D))


def _rope(x, cos_t, sin_t):
    lane = lax.broadcasted_iota(jnp.int32, x.shape, 1)
    first = (lane & 15) < 8
    sw = jnp.where(first, pltpu.roll(x, GW - 8, axis=1), pltpu.roll(x, 8, axis=1))
    return x * cos_t + sw * sin_t


def _proj_kernel(*refs, row_base, row_div, layer):
    ctx = layer is not None
    if ctx:
        (x_ref, mod_ref, g1_ref, w_ref, gb_ref, qkg_ref) = refs[:6]
        (ml_ref, gate_ref, naq_ref, rg_ref, dfq_ref, nk_ref, nv_ref, dk_ref, dv_ref) = refs[-9:]
        cache_layer = layer if nk_ref.shape[1] > 1 else 0
    else:
        (x_ref, mod_ref, g1_ref, w_ref, gb_ref, qkg_ref, cos_ref, sin_ref,
         ml_ref, gate_ref, na_ref, rg_ref, df_ref) = refs
    r = _mod_row(pl.program_id(0), row_base, row_div)
    sh = mod_ref[pl.ds(r, 1), 0:D]
    sc = mod_ref[pl.ds(r, 1), D:2 * D]
    for sub in range(x_ref.shape[0] // SUB_ROWS):
        rows = slice(sub * SUB_ROWS, (sub + 1) * SUB_ROWS)
        x = x_ref[rows, :]
        ms = jnp.mean(x * x, axis=-1, keepdims=True)
        y = x * lax.rsqrt(ms + EPS) * g1_ref[...]
        hn = _bf(y * (1.0 + sc) + sh)

        ml = _dot(hn, w_ref[:, 0:1024])
        ml_ref[rows, 0:256] = ml[:, 0:256]
        ml_ref[rows, 256:512] = ml[:, 256:512] * (HD ** -0.5)
        ml_ref[rows, 512:1024] = ml[:, 512:1024]

        gz = _dot(hn, w_ref[:, 3072:3200]) + gb_ref[...]
        lane = lax.broadcasted_iota(jnp.int32, gz.shape, 1)
        gate_ref[rows, :] = jnp.where(((lane >> 2) & 1) == 1, _log_sigmoid(gz), gz)

        rg_ref[rows, :] = _dot(hn, w_ref[:, 1792:2304])

        nz = _dot(hn, w_ref[:, 1024:1792])
        nq = _seg_rms(nz[:, 0:256], NH, qkg_ref[0:1, :])
        nk = _seg_rms(nz[:, 256:512], NH, qkg_ref[1:2, :])
        dz = _dot(hn, w_ref[:, 2304:3072])
        dq = _seg_rms(dz[:, 0:256], 2 * NH, qkg_ref[2:3, :])
        dk = _seg_rms(dz[:, 256:512], 2 * NH, qkg_ref[3:4, :])
        if ctx:
            naq_ref[rows, :] = nq
            dfq_ref[rows, :] = dq
            for c_ref, val in ((nk_ref, nk), (nv_ref, nz[:, 512:768]), (dk_ref, dk), (dv_ref, dz[:, 512:768])):
                c_ref[sub, cache_layer] = val
                for other in range(c_ref.shape[1]):
                    if other != cache_layer:
                        c_ref[sub, other] = jnp.zeros_like(val)
        else:
            cos_t = cos_ref[rows, :]
            sin_t = sin_ref[rows, :]
            na_ref[rows, 0:256] = nq
            na_ref[rows, 256:512] = nk
            na_ref[rows, 512:768] = nz[:, 512:768]
            df_ref[rows, 0:256] = _rope(dq, cos_t, sin_t)
            df_ref[rows, 256:512] = _rope(dk, cos_t, sin_t)
            df_ref[rows, 512:768] = dz[:, 512:768]


def _proj_in(x, mod, g1, w_r, gate_b, qkg, *, rope=None, caches=None, cache_shape=None, layer=None,
             row_base, row_div, tm=512):
    tp = x.shape[0]
    in_specs = [pl.BlockSpec((tm, D), lambda i: (i, 0)), _full((8, 6 * D)), _full((1, D)),
                _full((D, PROJ_PAD)), _full((1, 128)), _full((4, GW))]
    args = [x, mod, g1, w_r, gate_b, qkg]
    tok = lambda n: pl.BlockSpec((tm, n), lambda i: (i, 0))
    aliases = {}
    if layer is None:
        tiles = rope[0].shape[0] // tm
        in_specs += [pl.BlockSpec((tm, GW), lambda i: (i % tiles, 0))] * 2
        args += list(rope)
        out_specs = [tok(1024), tok(128), tok(768), tok(512), tok(768)]
        out_shape = [_sds((tp, n)) for n in (1024, 128, 768, 512, 768)]
    else:
        if caches is None:
            depth, seq = cache_shape[1], cache_shape[2]
            cspec = pl.BlockSpec((tm // seq, depth, seq, GW), lambda i: (i, 0, 0, 0))
        else:
            cache_shape, seq = caches[0].shape, caches[0].shape[2]
            cspec = pl.BlockSpec((tm // seq, 1, seq, GW), lambda i: (i, layer, 0, 0))
            in_specs += [pl.BlockSpec(memory_space=pl.ANY)] * 4
            args += list(caches)
            aliases = {6 + i: 5 + i for i in range(4)}
        assert seq == SUB_ROWS
        out_specs = [tok(1024), tok(128), tok(GW), tok(512), tok(GW)] + [cspec] * 4
        out_shape = [_sds((tp, n)) for n in (1024, 128, GW, 512, GW)] + [_sds(cache_shape)] * 4
    return pl.pallas_call(
        functools.partial(_proj_kernel, row_base=row_base, row_div=row_div, layer=layer),
        grid=(tp // tm,),
        in_specs=in_specs,
        out_specs=out_specs,
        out_shape=out_shape,
        input_output_aliases=aliases,
        compiler_params=_params(1),
        name="proj_in",
    )(*args)


def _mlstm_dir(d, q_ref, k_ref, v_ref, g_ref, h_ref, st_s, m_s):
    lc = CHUNK
    g = g_ref[...]
    gt = g.T
    row = lax.broadcasted_iota(jnp.int32, (lc, lc), 0)
    col = lax.broadcasted_iota(jnp.int32, (lc, lc), 1)
    tri = (row >= col) if d == 0 else (row <= col)
    tri_t = (col >= row) if d == 0 else (col <= row)
    g_parts = _split3(g)
    gt_parts = _split3(gt)
    tri_b = tri.astype(F32).astype(BF16)
    tri_tb = tri_t.astype(F32).astype(BF16)
    bc_col = _dot(tri_b, g_parts[0]) + _dot(tri_b, g_parts[1]) + _dot(tri_b, g_parts[2])
    bc_row = _dot(gt_parts[0], tri_tb) + _dot(gt_parts[1], tri_tb) + _dot(gt_parts[2], tri_tb)
    r_rows = gt[d * 8:d * 8 + NH, :] - bc_row[d * 8 + NH:d * 8 + 2 * NH, :]
    k_t = k_ref[...].T
    k_tb = _bf(k_t)
    q_all = _bf(q_ref[...])
    v_all = _bf(v_ref[...])
    ones = jnp.ones((lc, 128), BF16)
    last = lc - 1 if d == 0 else 0
    hs = [slice(h * HD, (h + 1) * HD) for h in range(NH)]
    ms = [m_s[d * NH + h:d * NH + h + 1, 0:1] for h in range(NH)]
    rms = [jnp.where(tri, r_rows[h:h + 1, :], -jnp.inf) for h in range(NH)]
    big_rs = [jnp.maximum(jnp.max(rms[h], axis=-1, keepdims=True), ms[h]) for h in range(NH)]
    ss = [_bf(_dot(q_all[:, hs[h]], k_tb[hs[h], :]) * jnp.exp(rms[h] - big_rs[h])) for h in range(NH)]
    outs = []
    for h in range(NH):
        v2 = v_all[:, (h // 2) * 128:(h // 2 + 1) * 128]
        e = jnp.exp(ms[h] - big_rs[h])
        qh = q_all[:, hs[h]]
        tot = _dot(ss[h], v2) + e * _dot(qh, _bf(st_s[d, h, 0]))
        den = _dot(ss[h], ones) + e * _dot(qh, _bf(st_s[d, h, 1]))
        c_col = bc_col[:, d * 8 + NH + h:d * 8 + NH + h + 1]
        floor = jnp.exp(-(c_col + big_rs[h]))
        outs.append(tot / jnp.maximum(jnp.abs(den), floor))

        r_last = big_rs[h][last:last + 1, :]
        kw = _bf(k_t[hs[h], :] * jnp.exp(r_rows[h:h + 1, :] - r_last))
        gdec = jnp.exp(ms[h] - r_last)
        st_s[d, h, 0] = gdec * st_s[d, h, 0] + _dot(kw, v2)
        st_s[d, h, 1] = gdec * st_s[d, h, 1] + _dot(kw, ones)
        m_s[d * NH + h:d * NH + h + 1, :] = jnp.broadcast_to(c_col[last:last + 1, :] + r_last, (1, 128))
    lane = lax.broadcasted_iota(jnp.int32, (lc, 128), 1)
    h_ref[...] = jnp.concatenate([jnp.where(lane < HD, outs[2 * p], outs[2 * p + 1]) for p in range(NH // 2)],
                                 axis=-1)


def _mlstm_kernel(qf, kf, vf, gf, qb, kb, vb, gb, *rest):
    hf_ref, hb_ref, st_out, m_out, st_s, m_s = rest[-6:]
    j = pl.program_id(1)

    @pl.when(j == 0)
    def _():
        if len(rest) == 8:
            st_s[...] = rest[0][0]
            m_s[...] = rest[1][0]
        else:
            st_s[...] = jnp.zeros_like(st_s)
            m_s[...] = jnp.zeros_like(m_s)

    _mlstm_dir(0, qf, kf, vf, gf, hf_ref, st_s, m_s)
    _mlstm_dir(1, qb, kb, vb, gb, hb_ref, st_s, m_s)

    @pl.when(j == pl.num_programs(1) - 1)
    def _():
        st_out[0] = st_s[...]
        m_out[0] = m_s[...]


def _mlstm_pack_state(c, n):
    z = jnp.zeros_like(c)
    odd = (jnp.arange(NH) % 2 == 1)[None, None, :, None, None]
    c_pair = jnp.where(odd, jnp.concatenate([z, c], axis=-1), jnp.concatenate([c, z], axis=-1))
    n_rep = jnp.broadcast_to(n[..., None], n.shape + (128,))
    return jnp.stack([c_pair, n_rep], axis=3)


def _mlstm_unpack_state(st):
    odd = (jnp.arange(NH) % 2 == 1)[None, None, :, None, None]
    c = jnp.where(odd, st[:, :, :, 0, :, HD:2 * HD], st[:, :, :, 0, :, 0:HD])
    return c, st[:, :, :, 1, :, 0]


def _mlstm(mlz, gates, state=None, *, nreq, nc):
    tp = mlz.shape[0]
    st_spec = pl.BlockSpec((1, 2, NH, 2, HD, 128), lambda r, j: (r, 0, 0, 0, 0, 0))

    def fwd(col):
        return lambda r, j: (r * nc + j, col)

    def bwd(col):
        return lambda r, j: (r * nc + nc - 1 - j, col)

    in_specs = []
    for mk in (fwd, bwd):
        in_specs += [pl.BlockSpec((CHUNK, GW), mk(0)), pl.BlockSpec((CHUNK, GW), mk(1)),
                     pl.BlockSpec((CHUNK, GW), mk(2)), pl.BlockSpec((CHUNK, 128), mk(0))]
    args = [mlz, mlz, mlz, gates, mlz, mlz, mlz, gates]
    if state is not None:
        in_specs += [st_spec, pl.BlockSpec((1, 8, 128), lambda r, j: (r, 0, 0))]
        args += list(state)
    return pl.pallas_call(
        _mlstm_kernel,
        grid=(nreq, nc),
        in_specs=in_specs,
        out_specs=[pl.BlockSpec((CHUNK, GW), fwd(0)), pl.BlockSpec((CHUNK, GW), bwd(0)),
                   st_spec, pl.BlockSpec((1, 8, 128), lambda r, j: (r, 0, 0))],
        out_shape=[_sds((tp, GW)), _sds((tp, GW)), _sds((nreq, 2, NH, 2, HD, 128)), _sds((nreq, 8, 128))],
        scratch_shapes=[pltpu.VMEM((2, NH, 2, HD, 128), F32), pltpu.VMEM((8, 128), F32)],
        compiler_params=_params(2),
        name="mlstm",
    )(*args)


def _diff_lambda(lam_ref, lam_init):
    lp = lam_ref[...]
    a = jnp.exp(jnp.sum(lp[0:1, :] * lp[1:2, :], axis=-1, keepdims=True))
    b = jnp.exp(jnp.sum(lp[2:3, :] * lp[3:4, :], axis=-1, keepdims=True))
    return a - b + lam_init


def _softmax_pv(q, k_t, v):
    s = _dot(q, k_t)
    e = jnp.exp2(s - jnp.max(s, axis=-1, keepdims=True))
    return _dot(_bf(e), v) / jnp.sum(e, axis=-1, keepdims=True)


def _ctx_attn_kernel(nq, nk, nv, dq, dk, dv, lam_ref, ona_ref, odf_ref, *, lam_init):
    lam = _diff_lambda(lam_ref, lam_init)
    nk_t = _bf(nk[0, 0].T)
    dk_t = _bf(dk[0, 0].T)
    na_out, df_out = [], []
    for h in range(NH):
        hs = slice(h * HD, (h + 1) * HD)
        na_out.append(_softmax_pv(_bf(nq[:, hs] * (HD ** -0.5 * LOG2E)), nk_t[hs, :], _bf(nv[0, 0, :, hs])))
        vh = _bf(dv[0, 0, :, hs])
        os = []
        for c in range(2):
            cs = slice(h * HD + c * 32, h * HD + (c + 1) * 32)
            os.append(_softmax_pv(_bf(dq[:, cs] * (32 ** -0.5 * LOG2E)), dk_t[cs, :], vh))
        df_out.append(os[0] - lam * os[1])
    ona_ref[...] = _bf(jnp.concatenate(na_out, axis=-1))
    odf_ref[...] = jnp.concatenate(df_out, axis=-1)


def _ctx_attn(naq, dfq, caches, layer, lamp, lam_init, *, nreq, n):
    tp = naq.shape[0]
    qs = pl.BlockSpec((n, GW), lambda b: (b, 0))
    cs = pl.BlockSpec((1, 1, n, GW), lambda b: (b, layer, 0, 0))
    return pl.pallas_call(
        functools.partial(_ctx_attn_kernel, lam_init=lam_init),
        grid=(nreq,),
        in_specs=[qs, cs, cs, qs, cs, cs, _full((4, 32))],
        out_specs=[qs, qs],
        out_shape=[_sds((tp, GW), BF16), _sds((tp, GW))],
        compiler_params=_params(1),
        name="ctx_attn",
    )(naq, caches[0], caches[1], dfq, caches[2], caches[3], lamp)


def _rpb_kernel(rpb_ref, o_ref, *, rows, rps):
    h = pl.program_id(0)
    shape = (GRID_W, NA_ROWS * GRID_W)
    lane = lax.broadcasted_iota(jnp.int32, shape, 1)
    cq = lax.broadcasted_iota(jnp.int32, shape, 0)
    ck = lane & (GRID_W - 1)
    c0 = jnp.clip(cq - NA_COLS // 2, 0, GRID_W - NA_COLS)
    ok = (ck >= c0) & (ck < c0 + NA_COLS)
    ncol = 2 * NA_COLS - 1
    ndr = 2 * NA_ROWS - 1
    x = lax.broadcasted_iota(jnp.int32, (1, 2 * GRID_W), 1)
    didx = jnp.clip(x - GRID_W, -(NA_COLS - 1), NA_COLS - 1) + (NA_COLS - 1)
    tiles = []
    for dr in range(ndr):
        frow = jnp.zeros((1, 2 * GRID_W), F32)
        for b in range(ncol):
            frow = jnp.where(didx == b, rpb_ref[(h * ndr + dr) * ncol + b], frow)
        rolled = pltpu.roll(jnp.broadcast_to(frow, (GRID_W, 2 * GRID_W)), GRID_W, axis=1, stride=1, stride_axis=0)
        tiles.append(rolled[:, 0:GRID_W])
    tbs = [jnp.where(ok, jnp.concatenate(tiles[s:s + NA_ROWS], axis=-1) * LOG2E, NEG_BIG) for s in range(NA_ROWS)]
    wrows = NA_ROWS + rps
    steps = rows // rps
    for t, j in enumerate((0, 1, steps - 1)):
        w0 = min(max(j * rps - NA_ROWS // 2, 0), rows - wrows)
        for a in range(rps):
            r = j * rps + a
            r0 = min(max(r - NA_ROWS // 2, 0), rows - NA_ROWS)
            off = (r0 - w0) * GRID_W
            rest = rps * GRID_W - off
            pieces = [tbs[r0 - r + NA_ROWS - 1]]
            if off:
                pieces = [jnp.full((GRID_W, off), NEG_BIG, F32)] + pieces
            if rest:
                pieces = pieces + [jnp.full((GRID_W, rest), NEG_BIG, F32)]
            o_ref[0, t, a * GRID_W:(a + 1) * GRID_W, :] = jnp.concatenate(pieces, axis=-1)


def _na_step_offsets(j, rows, rps):
    w0 = min(max(j * rps - NA_ROWS // 2, 0), rows - NA_ROWS - rps)
    return [(min(max(r - NA_ROWS // 2, 0), rows - NA_ROWS) - r,
             min(max(r - NA_ROWS // 2, 0), rows - NA_ROWS) - w0) for r in range(j * rps, (j + 1) * rps)]


def _rpb_table(rpb, *, rows, rps):
    steps = rows // rps
    assert all(_na_step_offsets(j, rows, rps) == _na_step_offsets(1, rows, rps) for j in range(1, steps - 1))
    shape = (rps * GRID_W, (NA_ROWS + rps) * GRID_W)
    return pl.pallas_call(
        functools.partial(_rpb_kernel, rows=rows, rps=rps),
        grid=(NH,),
        in_specs=[pl.BlockSpec(memory_space=pltpu.SMEM)],
        out_specs=pl.BlockSpec((1, 3) + shape, lambda h: (h, 0, 0, 0)),
        out_shape=_sds((NH, 3) + shape),
        compiler_params=_params(1),
        name="rpb_table",
    )(rpb.reshape(-1))


def _na_lat_kernel(q_ref, k_ref, v_ref, kc_ref, vc_ref, tb_ref, o_ref, k_s, v_s, kct_s, vc_s, *, rows, rps):
    j = pl.program_id(1)

    @pl.when(j == 0)
    def _():
        k_s[...] = _bf(k_ref[...])
        v_s[...] = _bf(v_ref[...])
        kct_s[...] = _bf(kc_ref[0].T)
        vc_s[...] = _bf(vc_ref[0])

    wrows = NA_ROWS + rps
    win = wrows * GRID_W
    w0 = jnp.clip(j * rps - NA_ROWS // 2, 0, rows - wrows)
    start = pl.multiple_of(w0 * GRID_W, GRID_W)
    outs = []
    for h in range(NH):
        hs = slice(h * HD, (h + 1) * HD)
        q = _bf(q_ref[:, hs] * (HD ** -0.5 * LOG2E))
        s_c = _dot(q, kct_s[hs, :])
        s_l = _dot_nt(q, k_s[pl.ds(start, win), hs]) + tb_ref[h, 0]
        m = jnp.maximum(jnp.max(s_c, axis=-1, keepdims=True), jnp.max(s_l, axis=-1, keepdims=True))
        e_c = jnp.exp2(s_c - m)
        e_l = jnp.exp2(s_l - m)
        den = jnp.sum(e_c, axis=-1, keepdims=True) + jnp.sum(e_l, axis=-1, keepdims=True)
        o = _dot(_bf(e_c), vc_s[:, hs]) + _dot(_bf(e_l), v_s[pl.ds(start, win), hs])
        outs.append(o / den)
    o_ref[...] = _bf(jnp.concatenate(outs, axis=-1))


def _na_lat(na, kc, vc, tb, *, nreq, n, rps):
    tp = na.shape[0]
    rows = n // GRID_W
    past = kc.shape[1]
    steps = rows // rps
    tq = rps * GRID_W
    return pl.pallas_call(
        functools.partial(_na_lat_kernel, rows=rows, rps=rps),
        grid=(nreq, steps),
        in_specs=[pl.BlockSpec((tq, GW), lambda b, r: (b * steps + r, 0)),
                  pl.BlockSpec((n, GW), lambda b, r: (b, 1)),
                  pl.BlockSpec((n, GW), lambda b, r: (b, 2)),
                  pl.BlockSpec((1, past, GW), lambda b, r: (b, 0, 0)),
                  pl.BlockSpec((1, past, GW), lambda b, r: (b, 0, 0)),
                  pl.BlockSpec((NH, 1) + tb.shape[2:],
                               lambda b, r: (0, jnp.minimum(r, 1) + r // (steps - 1), 0, 0))],
        out_specs=pl.BlockSpec((tq, GW), lambda b, r: (b * steps + r, 0)),
        out_shape=_sds((tp, GW), BF16),
        scratch_shapes=[pltpu.VMEM((n, GW), BF16), pltpu.VMEM((n, GW), BF16),
                        pltpu.VMEM((GW, past), BF16), pltpu.VMEM((past, GW), BF16)],
        compiler_params=_params(2),
        name="na_latent",
    )(na, na, na, kc, vc, tb)


def _df_lat_kernel(q_ref, k_ref, v_ref, kc_ref, vc_ref, lam_ref, o_ref, kt_s, v_s, kct_s, vc_s, *, lam_init):
    @pl.when(pl.program_id(1) == 0)
    def _():
        kt_s[...] = _bf(k_ref[...].T)
        v_s[...] = _bf(v_ref[...])
        kct_s[...] = _bf(kc_ref[0].T)
        vc_s[...] = _bf(vc_ref[0])

    lam = _diff_lambda(lam_ref, lam_init)
    outs = []
    for h in range(NH):
        hs = slice(h * HD, (h + 1) * HD)
        os = []
        for c in range(2):
            cs = slice(h * HD + c * 32, h * HD + (c + 1) * 32)
            q = _bf(q_ref[:, cs] * (32 ** -0.5 * LOG2E))
            s_c = _dot(q, kct_s[cs, :])
            s_l = _dot(q, kt_s[cs, :])
            m = jnp.maximum(jnp.max(s_c, axis=-1, keepdims=True), jnp.max(s_l, axis=-1, keepdims=True))
            e_c = jnp.exp2(s_c - m)
            e_l = jnp.exp2(s_l - m)
            den = jnp.sum(e_c, axis=-1, keepdims=True) + jnp.sum(e_l, axis=-1, keepdims=True)
            os.append((_dot(_bf(e_c), vc_s[:, hs]) + _dot(_bf(e_l), v_s[:, hs])) / den)
        outs.append(os[0] - lam * os[1])
    o_ref[...] = jnp.concatenate(outs, axis=-1)


def _df_lat(df, kc, vc, lamp, lam_init, *, nreq, n, tq=512):
    tp = df.shape[0]
    nq = n // tq
    past = kc.shape[1]
    return pl.pallas_call(
        functools.partial(_df_lat_kernel, lam_init=lam_init),
        grid=(nreq, nq),
        in_specs=[pl.BlockSpec((tq, GW), lambda b, j: (b * nq + j, 0)),
                  pl.BlockSpec((n, GW), lambda b, j: (b, 1)),
                  pl.BlockSpec((n, GW), lambda b, j: (b, 2)),
                  pl.BlockSpec((1, past, GW), lambda b, j: (b, 0, 0)),
                  pl.BlockSpec((1, past, GW), lambda b, j: (b, 0, 0)),
                  _full((4, 32))],
        out_specs=pl.BlockSpec((tq, GW), lambda b, j: (b * nq + j, 0)),
        out_shape=_sds((tp, GW)),
        scratch_shapes=[pltpu.VMEM((GW, n), BF16), pltpu.VMEM((n, GW), BF16),
                        pltpu.VMEM((GW, past), BF16), pltpu.VMEM((past, GW), BF16)],
        compiler_params=_params(2),
        name="df_latent",
    )(df, df, df, kc, vc, lamp)


def _rg_kernel(x_ref, g_ref, cw_ref, cb_ref, wbd_ref, bias_ref, lam_ref, h0_ref,
               y_ref, fin_ref, a_s, b_s, *, n):
    x = x_ref[...]
    row = lax.broadcasted_iota(jnp.int32, x.shape, 0)
    xc = cb_ref[...] + jnp.where(row >= 2, pltpu.roll(x, 2, axis=0), 0.0) * cw_ref[0:1, :]
    xc = xc + jnp.where(row >= 1, pltpu.roll(x, 1, axis=0), 0.0) * cw_ref[1:2, :]
    xc = xc + x * cw_ref[2:3, :]
    xc = xc + jnp.where(row < n - 1, pltpu.roll(x, n - 1, axis=0), 0.0) * cw_ref[3:4, :]
    z = _dot(_bf(xc), wbd_ref[...]) + bias_ref[...]
    sub = row & 7
    for d in range(2):
        rgate = jax.nn.sigmoid(z[:, 512 * d:512 * d + GW])
        igate = jax.nn.sigmoid(z[:, 512 * d + GW:512 * d + 2 * GW])
        la = -RG_C * rgate * _softplus(-lam_ref[d:d + 1, :])
        a = jnp.exp(la)
        t = jnp.tanh(la)
        b = jnp.sqrt(-2.0 * t / (1.0 - t)) * igate * xc
        for dd in (1, 2, 4):
            if d == 0:
                keep = sub >= dd
                a_sh = jnp.where(keep, pltpu.roll(a, dd, axis=0), 1.0)
                b_sh = jnp.where(keep, pltpu.roll(b, dd, axis=0), 0.0)
            else:
                keep = sub < 8 - dd
                a_sh = jnp.where(keep, pltpu.roll(a, n - dd, axis=0), 1.0)
                b_sh = jnp.where(keep, pltpu.roll(b, n - dd, axis=0), 0.0)
            b = b + a * b_sh
            a = a * a_sh
        a_s[d] = a
        b_s[d] = b

    nt = n // 8

    def body(t, carry):
        hf, hb = carry
        sf = pl.multiple_of(t * 8, 8)
        sb = pl.multiple_of((nt - 1 - t) * 8, 8)
        tf = a_s[0, pl.ds(sf, 8), :] * hf + b_s[0, pl.ds(sf, 8), :]
        tb = a_s[1, pl.ds(sb, 8), :] * hb + b_s[1, pl.ds(sb, 8), :]
        b_s[0, pl.ds(sf, 8), :] = tf
        b_s[1, pl.ds(sb, 8), :] = tb
        return tf[7:8, :], tb[0:1, :]

    hf, hb = lax.fori_loop(0, nt, body, (h0_ref[0, 0:1, :], h0_ref[0, 1:2, :]))
    fin_ref[0, 0:1, :] = hf
    fin_ref[0, 1:2, :] = hb
    gg = g_ref[...]
    cdf = 0.5 * (1.0 + jnp.tanh(math.sqrt(2.0 / math.pi) * (gg + 0.044715 * (gg * gg * gg))))
    y_ref[...] = _bf((b_s[0] + b_s[1]) * (gg * cdf))


def _rglru(rg, cw, cb, wbd, bias, lam, h0, *, nreq, n):
    tp = rg.shape[0]
    return pl.pallas_call(
        functools.partial(_rg_kernel, n=n),
        grid=(nreq,),
        in_specs=[pl.BlockSpec((n, GW), lambda b: (b, 0)), pl.BlockSpec((n, GW), lambda b: (b, 1)),
                  _full((4, GW)), _full((1, GW)), _full((GW, 4 * GW)), _full((1, 4 * GW)), _full((2, GW)),
                  pl.BlockSpec((1, 2, GW), lambda b: (b, 0, 0))],
        out_specs=[pl.BlockSpec((n, GW), lambda b: (b, 0)), pl.BlockSpec((1, 2, GW), lambda b: (b, 0, 0))],
        out_shape=[_sds((tp, GW), BF16), _sds((nreq, 2, GW))],
        scratch_shapes=[pltpu.VMEM((2, n, GW), F32), pltpu.VMEM((2, n, GW), F32)],
        compiler_params=_params(1),
        name="rglru",
    )(rg, rg, cw, cb, wbd, bias, lam, h0)


def _merge_kernel(hf_ref, hb_ref, mlo_ref, yna_ref, yrg_ref, odf_ref, x_ref, mod_ref, mlg_ref, sub_ref,
                  wout_ref, n2_ref, rw_ref, x1_ref, hn2_ref, lg_ref, *, row_base, row_div, lam_init):
    r = _mod_row(pl.program_id(0), row_base, row_div)
    y_ml = _seg_rms(hf_ref[...] + hb_ref[...], NH, mlg_ref[...]) * jax.nn.sigmoid(mlo_ref[...])
    y_df = _seg_rms(odf_ref[...], NH, sub_ref[...]) * (1.0 - lam_init)
    y = jnp.concatenate([_bf(y_ml), _bf(yna_ref[...]), _bf(yrg_ref[...]), _bf(y_df)], axis=-1)
    o = _dot(y, wout_ref[...])
    x1 = x_ref[...] + mod_ref[pl.ds(r, 1), 2 * D:3 * D] * o
    x1_ref[...] = x1
    ms = jnp.mean(x1 * x1, axis=-1, keepdims=True)
    hn = x1 * lax.rsqrt(ms + EPS) * n2_ref[...]
    hn = _bf(hn * (1.0 + mod_ref[pl.ds(r, 1), 4 * D:5 * D]) + mod_ref[pl.ds(r, 1), 3 * D:4 * D])
    hn2_ref[...] = hn
    lg_ref[...] = _dot(hn, rw_ref[...])


def _merge(hf, hb, mlz, yna, yrg, odf, x, mod, mlg, sub, wout, n2, rw, lam_init, *, row_base, row_div, tm=512):
    tp = x.shape[0]
    g = lambda c: pl.BlockSpec((tm, GW), lambda i: (i, c))
    return pl.pallas_call(
        functools.partial(_merge_kernel, row_base=row_base, row_div=row_div, lam_init=lam_init),
        grid=(tp // tm,),
        in_specs=[g(0), g(0), g(3), g(0), g(0), g(0),
                  pl.BlockSpec((tm, D), lambda i: (i, 0)), _full((8, 6 * D)), _full((1, GW)), _full((1, GW)),
                  _full((D, D)), _full((1, D)), _full((D, 128))],
        out_specs=[pl.BlockSpec((tm, D), lambda i: (i, 0)), pl.BlockSpec((tm, D), lambda i: (i, 0)),
                   pl.BlockSpec((tm, 128), lambda i: (i, 0))],
        out_shape=[_sds((tp, D)), _sds((tp, D), BF16), _sds((tp, 128))],
        compiler_params=_params(1),
        name="merge",
    )(hf, hb, mlz, yna, yrg, odf, x, mod, mlg, sub, wout, n2, rw)


def _excl_cumsum_lanes(mask):
    blk = 256
    r = lax.broadcasted_iota(jnp.int32, (blk, blk), 0)
    c = lax.broadcasted_iota(jnp.int32, (blk, blk), 1)
    tri = (r < c).astype(F32).astype(BF16)
    off = jnp.zeros((mask.shape[0], 1), F32)
    outs = []
    for i in range(mask.shape[1] // blk):
        mb = mask[:, i * blk:(i + 1) * blk]
        outs.append(_dot(_bf(mb), tri) + off)
        off = off + jnp.sum(mb, axis=-1, keepdims=True)
    return jnp.concatenate(outs, axis=-1)


def _route_kernel(lg_ref, pos_ref, aff_ref, post_ref, *, cap, n, rb):
    lg = lg_ref[...].T[0:NE, :]
    ex = jnp.exp(lg - jnp.max(lg, axis=0, keepdims=True))
    aff = ex / jnp.sum(ex, axis=0, keepdims=True)
    aff_ref[...] = aff
    aff = jnp.concatenate([aff[:, i * n:(i + 1) * n] for i in range(rb)], axis=0)
    thr = jnp.zeros((rb * NE, 1), jnp.int32)
    for bit in range(30, -1, -1):
        cand = thr | (1 << bit)
        cnt = jnp.sum((aff >= pltpu.bitcast(cand, F32)).astype(jnp.int32), axis=-1, keepdims=True)
        thr = jnp.where(cnt >= cap, cand, thr)
    gt = aff >= pltpu.bitcast(thr + 1, F32)
    eq = (aff >= pltpu.bitcast(thr, F32)) & jnp.logical_not(gt)
    need = (cap - jnp.sum(gt.astype(jnp.int32), axis=-1, keepdims=True)).astype(F32)
    eq_rank = _excl_cumsum_lanes(eq.astype(F32))
    sel = gt | (eq & (eq_rank < need))
    slot = _excl_cumsum_lanes(sel.astype(F32))
    pos = jnp.where(sel, slot.astype(jnp.int32), -1)
    pos = jnp.concatenate([pos[i * NE:(i + 1) * NE, :] for i in range(rb)], axis=1)
    pos_ref[...] = pos
    post_ref[...] = jnp.concatenate([pos, jnp.full((128 - NE, rb * n), -1, jnp.int32)], axis=0).T


def _route(lg, *, nreq, n, cap, rb):
    tp = lg.shape[0]
    return pl.pallas_call(
        functools.partial(_route_kernel, cap=cap, n=n, rb=rb),
        grid=(nreq // rb,),
        in_specs=[pl.BlockSpec((rb * n, 128), lambda b: (b, 0))],
        out_specs=[pl.BlockSpec((NE, rb * n), lambda b: (0, b)), pl.BlockSpec((NE, rb * n), lambda b: (0, b)),
                   pl.BlockSpec((rb * n, 128), lambda b: (b, 0))],
        out_shape=[_sds((NE, tp), jnp.int32), _sds((NE, tp)), _sds((tp, 128), jnp.int32)],
        compiler_params=_params(1),
        name="route",
    )(lg)


def _gather_kernel(pos_ref, aff_ref, h_ref, xs_ref, w_ref, *, eb, cap, n):
    eg = pl.program_id(1)
    io = lax.broadcasted_iota(jnp.int32, (cap, n), 0)
    sels = []
    for k in range(eb):
        e = eg * eb + k
        sel = pos_ref[pl.ds(e, 1), :] == io
        sels.append(sel.astype(F32).astype(BF16))
        w = jnp.sum(jnp.where(sel, aff_ref[pl.ds(e, 1), :], 0.0), axis=-1, keepdims=True)
        w_ref[k] = jnp.broadcast_to(w, (cap, 128))
    xs = _dot(jnp.concatenate(sels, axis=0), h_ref[...]).astype(BF16)
    for k in range(eb):
        xs_ref[k] = xs[k * cap:(k + 1) * cap, :]


def _gather(pos, aff, hn2, *, nreq, n, cap, eb):
    return pl.pallas_call(
        functools.partial(_gather_kernel, eb=eb, cap=cap, n=n),
        grid=(nreq, NE // eb),
        in_specs=[pl.BlockSpec((NE, n), lambda b, g: (0, b)), pl.BlockSpec((NE, n), lambda b, g: (0, b)),
                  pl.BlockSpec((n, D), lambda b, g: (b, 0))],
        out_specs=[pl.BlockSpec((eb, cap, D), lambda b, g: (g, b, 0)),
                   pl.BlockSpec((eb, cap, 128), lambda b, g: (g, b, 0))],
        out_shape=[_sds((NE, nreq * cap, D), BF16), _sds((NE, nreq * cap, 128))],
        compiler_params=_params(2),
        name="moe_gather",
    )(pos, aff, hn2)


def _expert_kernel(xc_ref, xl_ref, wc_ref, wl_ref, wg_ref, wu_ref, wd_ref, yc_ref, yl_ref, acc_ref, *, rows):
    f = pl.program_id(1)

    @pl.when(f == 0)
    def _():
        acc_ref[...] = jnp.zeros_like(acc_ref)

    wg = _bf(wg_ref[0, 0])
    wu = _bf(wu_ref[0, 0])
    wd = _bf(wd_ref[0, 0])
    tm = 512
    for part, x_ref in enumerate((xc_ref, xl_ref)):
        for ch in range(rows // tm):
            x = x_ref[0, ch * tm:(ch + 1) * tm, :]
            g = _dot(x, wg)
            u = _dot(x, wu)
            a = _bf(g * jax.nn.sigmoid(g) * u)
            lo = part * rows + ch * tm
            acc_ref[lo:lo + tm, :] += _dot(a, wd)

    @pl.when(f == pl.num_programs(1) - 1)
    def _():
        yc_ref[0] = _bf(acc_ref[0:rows, :] * wc_ref[0][:, 0:1])
        yl_ref[0] = _bf(acc_ref[rows:2 * rows, :] * wl_ref[0][:, 0:1])


def _experts(xs_c, xs_l, w_c, w_l, wg, wu, wd, layer, *, fb=512):
    rows = xs_c.shape[1]
    dff = wg.shape[-1]
    xspec = pl.BlockSpec((1, rows, D), lambda e, f: (e, 0, 0))
    wspec = pl.BlockSpec((1, rows, 128), lambda e, f: (e, 0, 0))
    return pl.pallas_call(
        functools.partial(_expert_kernel, rows=rows),
        grid=(NE, dff // fb),
        in_specs=[xspec, xspec, wspec, wspec,
                  pl.BlockSpec((1, 1, D, fb), lambda e, f: (layer, e, 0, f)),
                  pl.BlockSpec((1, 1, D, fb), lambda e, f: (layer, e, 0, f)),
                  pl.BlockSpec((1, 1, fb, D), lambda e, f: (layer, e, f, 0))],
        out_specs=[xspec, xspec],
        out_shape=[_sds((NE, rows, D), BF16), _sds((NE, rows, D), BF16)],
        scratch_shapes=[pltpu.VMEM((2 * rows, D), F32)],
        compiler_params=_params(2),
        name="moe_experts",
    )(xs_c, xs_l, w_c, w_l, wg, wu, wd)


def _scatter_kernel(idx_ref, y_ref, x1_ref, mod_ref, o_ref, *, eb, cap, n, row_base, row_mul, token_major):
    eg = pl.program_id(1)
    last = pl.num_programs(1) - 1
    r = row_base + row_mul * pl.program_id(0)
    g2 = mod_ref[pl.ds(r, 1), 5 * D:6 * D]
    tn = min(n, 512)
    ys = jnp.concatenate([y_ref[k] for k in range(eb)], axis=0)
    for t in range(n // tn):
        rows = slice(t * tn, (t + 1) * tn)
        if token_major:
            io = lax.broadcasted_iota(jnp.int32, (tn, cap), 1)
            shift = (128 - eg * eb) & 127
            pt = pltpu.roll(idx_ref[rows, :], shift, axis=1)
            sels = [(pt[:, k:k + 1] == io).astype(F32).astype(BF16) for k in range(eb)]
            part = _dot(jnp.concatenate(sels, axis=1), ys)
        else:
            io = lax.broadcasted_iota(jnp.int32, (cap, tn), 0)
            sels = [(idx_ref[pl.ds(eg * eb + k, 1), rows] == io).astype(F32).astype(BF16) for k in range(eb)]
            part = _dot_tn(jnp.concatenate(sels, axis=0), ys)
        o_ref[rows, :] = jnp.where(eg == 0, part, o_ref[rows, :] + part)

    @pl.when(eg == last)
    def _():
        o_ref[...] = x1_ref[...] + g2 * o_ref[...]


def _scatter(idx, y, x1, mod, *, nreq, n, cap, eb, row_base, row_mul):
    tp = x1.shape[0]
    token_major = idx.shape[0] == tp
    idx_spec = (pl.BlockSpec((n, 128), lambda b, g: (b, 0)) if token_major
                else pl.BlockSpec((NE, n), lambda b, g: (0, b)))
    return pl.pallas_call(
        functools.partial(_scatter_kernel, eb=eb, cap=cap, n=n, row_base=row_base, row_mul=row_mul,
                          token_major=token_major),
        grid=(nreq, NE // eb),
        in_specs=[idx_spec,
                  pl.BlockSpec((eb, cap, D), lambda b, g: (g, b, 0)),
                  pl.BlockSpec((n, D), lambda b, g: (b, 0)), _full((8, 6 * D))],
        out_specs=pl.BlockSpec((n, D), lambda b, g: (b, 0)),
        out_shape=_sds((tp, D)),
        compiler_params=_params(2),
        name="moe_scatter",
    )(idx, y, x1, mod)


def _rope_tables(n):
    nf = 8
    t = np.arange(n)
    rowp = (t // GRID_W).astype(np.float32)
    colp = (t % GRID_W).astype(np.float32)
    inv = (np.float32(ROPE_BASE) ** (-np.arange(nf, dtype=np.float32) / np.float32(nf))).astype(np.float32)
    lane = np.arange(GW)
    c32 = lane % 32
    pos = np.where((c32 < 16)[None, :], rowp[:, None], colp[:, None]).astype(np.float32)
    ang = (pos * inv[(c32 % 8)][None, :]).astype(np.float32).astype(np.float64)
    sign = np.where((lane % 16) < 8, -1.0, 1.0)[None, :]
    return jnp.asarray(np.cos(ang), F32), jnp.asarray(np.sin(ang) * sign, F32)


def _block_diag(w):
    nb, bi, bo = w.shape
    return (jnp.eye(nb, dtype=w.dtype)[:, None, :, None] * w[:, :, None, :]).reshape(nb * bi, nb * bo)


def _layer_params(l, w_in, ml_gate_b, na_qn_g, na_kn_g, df_qn_g, df_kn_g, rg_wa, rg_wx, rg_ba, rg_bx,
                  df_lq1, df_lk1, df_lq2, df_lk2, df_subln_g, w_out, router_w):
    wi = w_in[l]
    w_r = jnp.concatenate([wi[:, 0:1024], wi[:, 1040:3088], wi[:, 1024:1040],
                           jnp.zeros((D, PROJ_PAD - 3088), F32)], axis=1).astype(BF16)
    gate_b = jnp.pad(ml_gate_b[l], (0, 128 - 16)).reshape(1, 128)
    qkg = jnp.stack([jnp.tile(na_qn_g[l], NH), jnp.tile(na_kn_g[l], NH),
                     jnp.tile(df_qn_g[l], 2 * NH), jnp.tile(df_kn_g[l], 2 * NH)])
    wbd = jnp.concatenate([_block_diag(rg_wa[l, 0]), _block_diag(rg_wx[l, 0]),
                           _block_diag(rg_wa[l, 1]), _block_diag(rg_wx[l, 1])], axis=1).astype(BF16)
    rg_bias = jnp.concatenate([rg_ba[l, 0], rg_bx[l, 0], rg_ba[l, 1], rg_bx[l, 1]]).reshape(1, 4 * GW)
    lamp = jnp.stack([df_lq1[l], df_lk1[l], df_lq2[l], df_lk2[l]])
    sub = jnp.tile(df_subln_g[l], NH).reshape(1, GW)
    rw = jnp.pad(router_w[l], ((0, 0), (0, 128 - NE))).astype(BF16)
    return dict(w_r=w_r, gate_b=gate_b, qkg=qkg, wbd=wbd, rg_bias=rg_bias, lamp=lamp, sub=sub,
                wout=w_out[l].astype(BF16), rw=rw)


def kernel(x_prompt, x_sample, cache_na_k, cache_na_v, cache_df_k, cache_df_v, state_ml_c, state_ml_n, state_ml_m, state_rg_h, c, c_ctx, norm1_g, norm2_g, w_mod, b_mod, w_in, ml_gate_b, ml_norm_g, na_qn_g, na_kn_g, na_rpb, rg_conv_w, rg_conv_b, rg_wa, rg_ba, rg_wx, rg_bx, rg_lam, df_qn_g, df_kn_g, df_lq1, df_lk1, df_lq2, df_lk2, df_subln_g, w_out, router_w, moe_wg, moe_wu, moe_wd):
    nb, seq, _ = x_prompt.shape
    db, dseq, _ = x_sample.shape
    depth = w_in.shape[0]
    past = cache_na_k.shape[2]
    tm = 512
    cap_c = 2 * seq // NE
    cap_l = 2 * dseq // NE

    cv = jnp.concatenate([c_ctx[None, :], c, jnp.zeros((8 - 1 - db, D), F32)], axis=0)
    mod_all = _modulation(cv, w_mod, b_mod)
    rope = _rope_tables(dseq)

    xc = x_prompt.reshape(nb * seq, D)
    xl = x_sample.reshape(db * dseq, D)
    ctx_out = []
    caches = None
    for l in range(depth):
        lam_init = 0.8 - 0.6 * math.exp(-0.3 * l)
        p = _layer_params(l, w_in, ml_gate_b, na_qn_g, na_kn_g, df_qn_g, df_kn_g, rg_wa, rg_wx, rg_ba, rg_bx,
                          df_lq1, df_lk1, df_lq2, df_lk2, df_subln_g, w_out, router_w)
        mod = mod_all[l]
        g1 = norm1_g[l].reshape(1, D)
        g2 = norm2_g[l].reshape(1, D)
        mlg = ml_norm_g[l].reshape(1, GW)
        cb = rg_conv_b[l].reshape(1, GW)
        tb = _rpb_table(na_rpb[l], rows=dseq // GRID_W, rps=NA_RPS)

        nt_c = nb * seq // tm
        ml, gates, naq, rg, dfq, *caches = _proj_in(xc, mod, g1, p["w_r"], p["gate_b"], p["qkg"], caches=caches,
                                                    cache_shape=(nb, depth, seq, GW), layer=l,
                                                    row_base=0, row_div=nt_c, tm=tm)
        hf, hb, cn_c, m_c = _mlstm(ml, gates, nreq=nb, nc=seq // CHUNK)
        y_na, o_df = _ctx_attn(naq, dfq, caches, l, p["lamp"], lam_init, nreq=nb, n=seq)
        y_rg, rg_fin = _rglru(rg, rg_conv_w[l], cb, p["wbd"], p["rg_bias"], rg_lam[l],
                              jnp.zeros((nb, 2, GW), F32), nreq=nb, n=seq)
        x1_c, hn2_c, lg_c = _merge(hf, hb, ml, y_na, y_rg, o_df, xc, mod, mlg, p["sub"], p["wout"], g2, p["rw"],
                                   lam_init, row_base=0, row_div=nt_c, tm=tm)
        pos_c, aff_c, _ = _route(lg_c, nreq=nb, n=seq, cap=cap_c, rb=8)
        xs_c, w_c = _gather(pos_c, aff_c, hn2_c, nreq=nb, n=seq, cap=cap_c, eb=NE)
        ctx_out.append((cn_c, m_c, rg_fin))

        tiles_req = dseq // tm
        ml, gates, na, rg, df = _proj_in(xl, mod, g1, p["w_r"], p["gate_b"], p["qkg"], rope=rope,
                                         row_base=1, row_div=tiles_req, tm=tm)
        cn0 = _mlstm_pack_state(state_ml_c[:, l], state_ml_n[:, l])
        m0 = jnp.broadcast_to(state_ml_m[:, l].reshape(db, 8, 1), (db, 8, 128))
        hf, hb, _, _ = _mlstm(ml, gates, (cn0, m0), nreq=db, nc=dseq // CHUNK)
        y_na = _na_lat(na, cache_na_k[:, l].reshape(db, past, GW), cache_na_v[:, l].reshape(db, past, GW), tb,
                       nreq=db, n=dseq, rps=NA_RPS)
        o_df = _df_lat(df, cache_df_k[:, l].reshape(db, past, GW), cache_df_v[:, l].reshape(db, past, GW),
                       p["lamp"], lam_init, nreq=db, n=dseq)
        y_rg, _ = _rglru(rg, rg_conv_w[l], cb, p["wbd"], p["rg_bias"], rg_lam[l], state_rg_h[:, l],
                         nreq=db, n=dseq)
        x1_l, hn2_l, lg_l = _merge(hf, hb, ml, y_na, y_rg, o_df, xl, mod, mlg, p["sub"], p["wout"], g2, p["rw"],
                                   lam_init, row_base=1, row_div=tiles_req, tm=tm)
        pos_l, aff_l, post_l = _route(lg_l, nreq=db, n=dseq, cap=cap_l, rb=1)
        xs_l, w_l = _gather(pos_l, aff_l, hn2_l, nreq=db, n=dseq, cap=cap_l, eb=2)

        y_c, y_l = _experts(xs_c, xs_l, w_c, w_l, moe_wg, moe_wu, moe_wd, l)
        xc = _scatter(pos_c, y_c, x1_c, mod, nreq=nb, n=seq, cap=cap_c, eb=NE, row_base=0, row_mul=0)
        xl = _scatter(post_l, y_l, x1_l, mod, nreq=db, n=dseq, cap=cap_l, eb=4, row_base=1, row_mul=1)

    y_prompt = xc.reshape(nb, seq, D)
    y_sample = xl.reshape(db, dseq, D)
    st = lambda f: jnp.stack([f(o) for o in ctx_out], axis=1)
    na_k = caches[0].reshape(nb, depth, seq, NH, HD)
    na_v = caches[1].reshape(nb, depth, seq, NH, HD)
    df_k = caches[2].reshape(nb, depth, seq, NH, 2, HD // 2)
    df_v = caches[3].reshape(nb, depth, seq, NH, HD)
    ml_c = st(lambda o: _mlstm_unpack_state(o[0])[0])
    ml_n = st(lambda o: _mlstm_unpack_state(o[0])[1])
    ml_m = st(lambda o: o[1][:, :, 0].reshape(nb, 2, NH))
    rg_h = st(lambda o: o[2])
    return (y_prompt, y_sample, na_k, na_v, df_k, df_v, ml_c, ml_n, ml_m, rg_h)
```

```python
import functools
import math

import numpy as np
import jax
import jax.numpy as jnp
from jax import lax
from jax.experimental import pallas as pl
from jax.experimental.pallas import tpu as pltpu

F32 = jnp.float32
BF16 = jnp.bfloat16

D = 1024
GW = 256
NH = 4
HD = 64
NE = 16
EPS = 1e-6
CHUNK = 256
GRID_W = 64
NA_ROWS = 8
NA_COLS = 16
SUB_ROWS = 256
NA_RPS = 4
RG_C = 8.0
ROPE_BASE = 10000.0
PROJ_PAD = 3200
VMEM_LIMIT_BYTES = 56 * 1024 * 1024
NEG_BIG = -1e30
LOG2E = 1.4426950408889634


def _bf(x):
    return x.astype(BF16)


def _dot(a, b):
    return jnp.dot(a, b, preferred_element_type=F32)


def _dot_nt(a, b):
    return lax.dot_general(a, b, (((1,), (1,)), ((), ())), preferred_element_type=F32)


def _dot_tn(a, b):
    return lax.dot_general(a, b, (((0,), (0,)), ((), ())), preferred_element_type=F32)


def _split3(x):
    p0 = _bf(x)
    r1 = x - p0.astype(F32)
    p1 = _bf(r1)
    return p0, p1, _bf(r1 - p1.astype(F32))


def _params(n_axes):
    return pltpu.CompilerParams(dimension_semantics=("arbitrary",) * n_axes,
                                vmem_limit_bytes=VMEM_LIMIT_BYTES)


def _full(shape):
    return pl.BlockSpec(shape, lambda *_: (0,) * len(shape))


def _sds(shape, dtype=F32):
    return jax.ShapeDtypeStruct(shape, dtype)


def _softplus(x):
    return jnp.maximum(x, 0.0) + jnp.log1p(jnp.exp(-jnp.abs(x)))


def _log_sigmoid(x):
    return jnp.minimum(x, 0.0) - jnp.log1p(jnp.exp(-jnp.abs(x)))


def _seg_rms(x, nseg, g_row):
    seg = x.shape[-1] // nseg
    lane = lax.broadcasted_iota(jnp.int32, x.shape, 1)
    x2 = x * x
    tot = jnp.zeros_like(x)
    for s in range(nseg):
        m = (lane >= s * seg) & (lane < (s + 1) * seg)
        t = jnp.sum(jnp.where(m, x2, 0.0), axis=-1, keepdims=True)
        tot = jnp.where(m, t, tot)
    return x * lax.rsqrt(tot * (1.0 / seg) + EPS) * g_row


def _mod_row(pid, row_base, row_div):
    return row_base + pid // row_div


def _mod_kernel(cv_ref, w_ref, b_ref, o_ref):
    cv = cv_ref[...]
    s = cv * jax.nn.sigmoid(cv)
    o_ref[0] = _dot(_bf(s), _bf(w_ref[0])) + b_ref[0]


def _modulation(cv, w_mod, b_mod):
    nl = w_mod.shape[0]
    tn = 1536
    return pl.pallas_call(
        _mod_kernel,
        grid=(nl, 6 * D // tn),
        in_specs=[_full((8, D)),
                  pl.BlockSpec((1, D, tn), lambda l, j: (l, 0, j)),
                  pl.BlockSpec((1, 1, tn), lambda l, j: (l, 0, j))],
        out_specs=pl.BlockSpec((1, 8, tn), lambda l, j: (l, 0, j)),
        out_shape=_sds((nl, 8, 6 * D)),
        compiler_params=_params(2),
        name="modulation",
    )(cv, w_mod, b_mod.reshape(nl, 1, 6 * D))


def _rope(x, cos_t, sin_t):
    lane = lax.broadcasted_iota(jnp.int32, x.shape, 1)
    first = (lane & 15) < 8
    sw = jnp.where(first, pltpu.roll(x, GW - 8, axis=1), pltpu.roll(x, 8, axis=1))
    return x * cos_t + sw * sin_t


def _proj_kernel(*refs, row_base, row_div, layer):
    ctx = layer is not None
    if ctx:
        (x_ref, mod_ref, g1_ref, w_ref, gb_ref, qkg_ref) = refs[:6]
        (ml_ref, gate_ref, naq_ref, rg_ref, dfq_ref, nk_ref, nv_ref, dk_ref, dv_ref) = refs[-9:]
        cache_layer = layer if nk_ref.shape[1] > 1 else 0
    else:
        (x_ref, mod_ref, g1_ref, w_ref, gb_ref, qkg_ref, cos_ref, sin_ref,
         ml_ref, gate_ref, na_ref, rg_ref, df_ref) = refs
    r = _mod_row(pl.program_id(0), row_base, row_div)
    sh = mod_ref[pl.ds(r, 1), 0:D]
    sc = mod_ref[pl.ds(r, 1), D:2 * D]
    for sub in range(x_ref.shape[0] // SUB_ROWS):
        rows = slice(sub * SUB_ROWS, (sub + 1) * SUB_ROWS)
        x = x_ref[rows, :]
        ms = jnp.mean(x * x, axis=-1, keepdims=True)
        y = x * lax.rsqrt(ms + EPS) * g1_ref[...]
        hn = _bf(y * (1.0 + sc) + sh)

        ml = _dot(hn, w_ref[:, 0:1024])
        ml_ref[rows, 0:256] = ml[:, 0:256]
        ml_ref[rows, 256:512] = ml[:, 256:512] * (HD ** -0.5)
        ml_ref[rows, 512:1024] = ml[:, 512:1024]

        gz = _dot(hn, w_ref[:, 3072:3200]) + gb_ref[...]
        lane = lax.broadcasted_iota(jnp.int32, gz.shape, 1)
        gate_ref[rows, :] = jnp.where(((lane >> 2) & 1) == 1, _log_sigmoid(gz), gz)

        rg_ref[rows, :] = _dot(hn, w_ref[:, 1792:2304])

        nz = _dot(hn, w_ref[:, 1024:1792])
        nq = _seg_rms(nz[:, 0:256], NH, qkg_ref[0:1, :])
        nk = _seg_rms(nz[:, 256:512], NH, qkg_ref[1:2, :])
        dz = _dot(hn, w_ref[:, 2304:3072])
        dq = _seg_rms(dz[:, 0:256], 2 * NH, qkg_ref[2:3, :])
        dk = _seg_rms(dz[:, 256:512], 2 * NH, qkg_ref[3:4, :])
        if ctx:
            naq_ref[rows, :] = nq
            dfq_ref[rows, :] = dq
            for c_ref, val in ((nk_ref, nk), (nv_ref, nz[:, 512:768]), (dk_ref, dk), (dv_ref, dz[:, 512:768])):
                c_ref[sub, cache_layer] = val
                for other in range(c_ref.shape[1]):
                    if other != cache_layer:
                        c_ref[sub, other] = jnp.zeros_like(val)
        else:
            cos_t = cos_ref[rows, :]
            sin_t = sin_ref[rows, :]
            na_ref[rows, 0:256] = nq
            na_ref[rows, 256:512] = nk
            na_ref[rows, 512:768] = nz[:, 512:768]
            df_ref[rows, 0:256] = _rope(dq, cos_t, sin_t)
            df_ref[rows, 256:512] = _rope(dk, cos_t, sin_t)
            df_ref[rows, 512:768] = dz[:, 512:768]


def _proj_in(x, mod, g1, w_r, gate_b, qkg, *, rope=None, caches=None, cache_shape=None, layer=None,
             row_base, row_div, tm=512):
    tp = x.shape[0]
    in_specs = [pl.BlockSpec((tm, D), lambda i: (i, 0)), _full((8, 6 * D)), _full((1, D)),
                _full((D, PROJ_PAD)), _full((1, 128)), _full((4, GW))]
    args = [x, mod, g1, w_r, gate_b, qkg]
    tok = lambda n: pl.BlockSpec((tm, n), lambda i: (i, 0))
    aliases = {}
    if layer is None:
        tiles = rope[0].shape[0] // tm
        in_specs += [pl.BlockSpec((tm, GW), lambda i: (i % tiles, 0))] * 2
        args += list(rope)
        out_specs = [tok(1024), tok(128), tok(768), tok(512), tok(768)]
        out_shape = [_sds((tp, n)) for n in (1024, 128, 768, 512, 768)]
    else:
        if caches is None:
            depth, seq = cache_shape[1], cache_shape[2]
            cspec = pl.BlockSpec((tm // seq, depth, seq, GW), lambda i: (i, 0, 0, 0))
        else:
            cache_shape, seq = caches[0].shape, caches[0].shape[2]
            cspec = pl.BlockSpec((tm // seq, 1, seq, GW), lambda i: (i, layer, 0, 0))
            in_specs += [pl.BlockSpec(memory_space=pl.ANY)] * 4
            args += list(caches)
            aliases = {6 + i: 5 + i for i in range(4)}
        assert seq == SUB_ROWS
        out_specs = [tok(1024), tok(128), tok(GW), tok(512), tok(GW)] + [cspec] * 4
        out_shape = [_sds((tp, n)) for n in (1024, 128, GW, 512, GW)] + [_sds(cache_shape)] * 4
    return pl.pallas_call(
        functools.partial(_proj_kernel, row_base=row_base, row_div=row_div, layer=layer),
        grid=(tp // tm,),
        in_specs=in_specs,
        out_specs=out_specs,
        out_shape=out_shape,
        input_output_aliases=aliases,
        compiler_params=_params(1),
        name="proj_in",
    )(*args)


def _mlstm_dir(d, q_ref, k_ref, v_ref, g_ref, h_ref, st_s, m_s):
    lc = CHUNK
    g = g_ref[...]
    gt = g.T
    row = lax.broadcasted_iota(jnp.int32, (lc, lc), 0)
    col = lax.broadcasted_iota(jnp.int32, (lc, lc), 1)
    tri = (row >= col) if d == 0 else (row <= col)
    tri_t = (col >= row) if d == 0 else (col <= row)
    g_parts = _split3(g)
    gt_parts = _split3(gt)
    tri_b = tri.astype(F32).astype(BF16)
    tri_tb = tri_t.astype(F32).astype(BF16)
    bc_col = _dot(tri_b, g_parts[0]) + _dot(tri_b, g_parts[1]) + _dot(tri_b, g_parts[2])
    bc_row = _dot(gt_parts[0], tri_tb) + _dot(gt_parts[1], tri_tb) + _dot(gt_parts[2], tri_tb)
    r_rows = gt[d * 8:d * 8 + NH, :] - bc_row[d * 8 + NH:d * 8 + 2 * NH, :]
    k_t = k_ref[...].T
    k_tb = _bf(k_t)
    q_all = _bf(q_ref[...])
    v_all = _bf(v_ref[...])
    ones = jnp.ones((lc, 128), BF16)
    last = lc - 1 if d == 0 else 0
    hs = [slice(h * HD, (h + 1) * HD) for h in range(NH)]
    ms = [m_s[d * NH + h:d * NH + h + 1, 0:1] for h in range(NH)]
    rms = [jnp.where(tri, r_rows[h:h + 1, :], -jnp.inf) for h in range(NH)]
    big_rs = [jnp.maximum(jnp.max(rms[h], axis=-1, keepdims=True), ms[h]) for h in range(NH)]
    ss = [_bf(_dot(q_all[:, hs[h]], k_tb[hs[h], :]) * jnp.exp(rms[h] - big_rs[h])) for h in range(NH)]
    outs = []
    for h in range(NH):
        v2 = v_all[:, (h // 2) * 128:(h // 2 + 1) * 128]
        e = jnp.exp(ms[h] - big_rs[h])
        qh = q_all[:, hs[h]]
        tot = _dot(ss[h], v2) + e * _dot(qh, _bf(st_s[d, h, 0]))
        den = _dot(ss[h], ones) + e * _dot(qh, _bf(st_s[d, h, 1]))
        c_col = bc_col[:, d * 8 + NH + h:d * 8 + NH + h + 1]
        floor = jnp.exp(-(c_col + big_rs[h]))
        outs.append(tot / jnp.maximum(jnp.abs(den), floor))

        r_last = big_rs[h][last:last + 1, :]
        kw = _bf(k_t[hs[h], :] * jnp.exp(r_rows[h:h + 1, :] - r_last))
        gdec = jnp.exp(ms[h] - r_last)
        st_s[d, h, 0] = gdec * st_s[d, h, 0] + _dot(kw, v2)
        st_s[d, h, 1] = gdec * st_s[d, h, 1] + _dot(kw, ones)
        m_s[d * NH + h:d * NH + h + 1, :] = jnp.broadcast_to(c_col[last:last + 1, :] + r_last, (1, 128))
    lane = lax.broadcasted_iota(jnp.int32, (lc, 128), 1)
    h_ref[...] = jnp.concatenate([jnp.where(lane < HD, outs[2 * p], outs[2 * p + 1]) for p in range(NH // 2)],
                                 axis=-1)


def _mlstm_kernel(qf, kf, vf, gf, qb, kb, vb, gb, *rest):
    hf_ref, hb_ref, st_out, m_out, st_s, m_s = rest[-6:]
    j = pl.program_id(1)

    @pl.when(j == 0)
    def _():
        if len(rest) == 8:
            st_s[...] = rest[0][0]
            m_s[...] = rest[1][0]
        else:
            st_s[...] = jnp.zeros_like(st_s)
            m_s[...] = jnp.zeros_like(m_s)

    _mlstm_dir(0, qf, kf, vf, gf, hf_ref, st_s, m_s)
    _mlstm_dir(1, qb, kb, vb, gb, hb_ref, st_s, m_s)

    @pl.when(j == pl.num_programs(1) - 1)
    def _():
        st_out[0] = st_s[...]
        m_out[0] = m_s[...]


def _mlstm_pack_state(c, n):
    z = jnp.zeros_like(c)
    odd = (jnp.arange(NH) % 2 == 1)[None, None, :, None, None]
    c_pair = jnp.where(odd, jnp.concatenate([z, c], axis=-1), jnp.concatenate([c, z], axis=-1))
    n_rep = jnp.broadcast_to(n[..., None], n.shape + (128,))
    return jnp.stack([c_pair, n_rep], axis=3)


def _mlstm_unpack_state(st):
    odd = (jnp.arange(NH) % 2 == 1)[None, None, :, None, None]
    c = jnp.where(odd, st[:, :, :, 0, :, HD:2 * HD], st[:, :, :, 0, :, 0:HD])
    return c, st[:, :, :, 1, :, 0]


def _mlstm(mlz, gates, state=None, *, nreq, nc):
    tp = mlz.shape[0]
    st_spec = pl.BlockSpec((1, 2, NH, 2, HD, 128), lambda r, j: (r, 0, 0, 0, 0, 0))

    def fwd(col):
        return lambda r, j: (r * nc + j, col)

    def bwd(col):
        return lambda r, j: (r * nc + nc - 1 - j, col)

    in_specs = []
    for mk in (fwd, bwd):
        in_specs += [pl.BlockSpec((CHUNK, GW), mk(0)), pl.BlockSpec((CHUNK, GW), mk(1)),
                     pl.BlockSpec((CHUNK, GW), mk(2)), pl.BlockSpec((CHUNK, 128), mk(0))]
    args = [mlz, mlz, mlz, gates, mlz, mlz, mlz, gates]
    if state is not None:
        in_specs += [st_spec, pl.BlockSpec((1, 8, 128), lambda r, j: (r, 0, 0))]
        args += list(state)
    return pl.pallas_call(
        _mlstm_kernel,
        grid=(nreq, nc),
        in_specs=in_specs,
        out_specs=[pl.BlockSpec((CHUNK, GW), fwd(0)), pl.BlockSpec((CHUNK, GW), bwd(0)),
                   st_spec, pl.BlockSpec((1, 8, 128), lambda r, j: (r, 0, 0))],
        out_shape=[_sds((tp, GW)), _sds((tp, GW)), _sds((nreq, 2, NH, 2, HD, 128)), _sds((nreq, 8, 128))],
        scratch_shapes=[pltpu.VMEM((2, NH, 2, HD, 128), F32), pltpu.VMEM((8, 128), F32)],
        compiler_params=_params(2),
        name="mlstm",
    )(*args)


def _diff_lambda(lam_ref, lam_init):
    lp = lam_ref[...]
    a = jnp.exp(jnp.sum(lp[0:1, :] * lp[1:2, :], axis=-1, keepdims=True))
    b = jnp.exp(jnp.sum(lp[2:3, :] * lp[3:4, :], axis=-1, keepdims=True))
    return a - b + lam_init


def _softmax_pv(q, k_t, v):
    s = _dot(q, k_t)
    e = jnp.exp2(s - jnp.max(s, axis=-1, keepdims=True))
    return _dot(_bf(e), v) / jnp.sum(e, axis=-1, keepdims=True)


def _ctx_attn_kernel(nq, nk, nv, dq, dk, dv, lam_ref, ona_ref, odf_ref, *, lam_init):
    lam = _diff_lambda(lam_ref, lam_init)
    n = nk.shape[2]
    for i in range(nk.shape[0]):
        rows = slice(i * n, (i + 1) * n)
        nk_t = _bf(nk[i, 0].T)
        dk_t = _bf(dk[i, 0].T)
        na_out, df_out = [], []
        for h in range(NH):
            hs = slice(h * HD, (h + 1) * HD)
            na_out.append(_softmax_pv(_bf(nq[rows, hs] * (HD ** -0.5 * LOG2E)), nk_t[hs, :], _bf(nv[i, 0, :, hs])))
            vh = _bf(dv[i, 0, :, hs])
            os = []
            for c in range(2):
                cs = slice(h * HD + c * 32, h * HD + (c + 1) * 32)
                os.append(_softmax_pv(_bf(dq[rows, cs] * (32 ** -0.5 * LOG2E)), dk_t[cs, :], vh))
            df_out.append(os[0] - lam * os[1])
        ona_ref[rows, :] = _bf(jnp.concatenate(na_out, axis=-1))
        odf_ref[rows, :] = jnp.concatenate(df_out, axis=-1)


def _ctx_attn(naq, dfq, caches, layer, lamp, lam_init, *, nreq, n, rb=1):
    tp = naq.shape[0]
    qs = pl.BlockSpec((rb * n, GW), lambda b: (b, 0))
    cs = pl.BlockSpec((rb, 1, n, GW), lambda b: (b, layer, 0, 0))
    return pl.pallas_call(
        functools.partial(_ctx_attn_kernel, lam_init=lam_init),
        grid=(nreq // rb,),
        in_specs=[qs, cs, cs, qs, cs, cs, _full((4, 32))],
        out_specs=[qs, qs],
        out_shape=[_sds((tp, GW), BF16), _sds((tp, GW))],
        compiler_params=_params(1),
        name="ctx_attn",
    )(naq, caches[0], caches[1], dfq, caches[2], caches[3], lamp)


def _rpb_kernel(rpb_ref, o_ref, *, rows, rps):
    h = pl.program_id(0)
    shape = (GRID_W, NA_ROWS * GRID_W)
    lane = lax.broadcasted_iota(jnp.int32, shape, 1)
    cq = lax.broadcasted_iota(jnp.int32, shape, 0)
    ck = lane & (GRID_W - 1)
    c0 = jnp.clip(cq - NA_COLS // 2, 0, GRID_W - NA_COLS)
    ok = (ck >= c0) & (ck < c0 + NA_COLS)
    ncol = 2 * NA_COLS - 1
    ndr = 2 * NA_ROWS - 1
    x = lax.broadcasted_iota(jnp.int32, (1, 2 * GRID_W), 1)
    didx = jnp.clip(x - GRID_W, -(NA_COLS - 1), NA_COLS - 1) + (NA_COLS - 1)
    tiles = []
    for dr in range(ndr):
        frow = jnp.zeros((1, 2 * GRID_W), F32)
        for b in range(ncol):
            frow = jnp.where(didx == b, rpb_ref[(h * ndr + dr) * ncol + b], frow)
        rolled = pltpu.roll(jnp.broadcast_to(frow, (GRID_W, 2 * GRID_W)), GRID_W, axis=1, stride=1, stride_axis=0)
        tiles.append(rolled[:, 0:GRID_W])
    tbs = [jnp.where(ok, jnp.concatenate(tiles[s:s + NA_ROWS], axis=-1) * LOG2E, NEG_BIG) for s in range(NA_ROWS)]
    wrows = NA_ROWS + rps
    steps = rows // rps
    for t, j in enumerate((0, 1, steps - 1)):
        w0 = min(max(j * rps - NA_ROWS // 2, 0), rows - wrows)
        for a in range(rps):
            r = j * rps + a
            r0 = min(max(r - NA_ROWS // 2, 0), rows - NA_ROWS)
            off = (r0 - w0) * GRID_W
            rest = rps * GRID_W - off
            pieces = [tbs[r0 - r + NA_ROWS - 1]]
            if off:
                pieces = [jnp.full((GRID_W, off), NEG_BIG, F32)] + pieces
            if rest:
                pieces = pieces + [jnp.full((GRID_W, rest), NEG_BIG, F32)]
            o_ref[0, t, a * GRID_W:(a + 1) * GRID_W, :] = jnp.concatenate(pieces, axis=-1)


def _na_step_offsets(j, rows, rps):
    w0 = min(max(j * rps - NA_ROWS // 2, 0), rows - NA_ROWS - rps)
    return [(min(max(r - NA_ROWS // 2, 0), rows - NA_ROWS) - r,
             min(max(r - NA_ROWS // 2, 0), rows - NA_ROWS) - w0) for r in range(j * rps, (j + 1) * rps)]


def _rpb_table(rpb, *, rows, rps):
    steps = rows // rps
    assert all(_na_step_offsets(j, rows, rps) == _na_step_offsets(1, rows, rps) for j in range(1, steps - 1))
    shape = (rps * GRID_W, (NA_ROWS + rps) * GRID_W)
    return pl.pallas_call(
        functools.partial(_rpb_kernel, rows=rows, rps=rps),
        grid=(NH,),
        in_specs=[pl.BlockSpec(memory_space=pltpu.SMEM)],
        out_specs=pl.BlockSpec((1, 3) + shape, lambda h: (h, 0, 0, 0)),
        out_shape=_sds((NH, 3) + shape),
        compiler_params=_params(1),
        name="rpb_table",
    )(rpb.reshape(-1))


def _na_lat_kernel(q_ref, k_ref, v_ref, kc_ref, vc_ref, tb_ref, o_ref, k_s, v_s, kct_s, vc_s, *, rows, rps):
    j = pl.program_id(1)

    @pl.when(j == 0)
    def _():
        k_s[...] = _bf(k_ref[...])
        v_s[...] = _bf(v_ref[...])
        kct_s[...] = _bf(kc_ref[0].T)
        vc_s[...] = _bf(vc_ref[0])

    wrows = NA_ROWS + rps
    win = wrows * GRID_W
    w0 = jnp.clip(j * rps - NA_ROWS // 2, 0, rows - wrows)
    start = pl.multiple_of(w0 * GRID_W, GRID_W)
    outs = []
    for h in range(NH):
        hs = slice(h * HD, (h + 1) * HD)
        q = _bf(q_ref[:, hs] * (HD ** -0.5 * LOG2E))
        s_c = _dot(q, kct_s[hs, :])
        s_l = _dot_nt(q, k_s[pl.ds(start, win), hs]) + tb_ref[h, 0]
        m = jnp.maximum(jnp.max(s_c, axis=-1, keepdims=True), jnp.max(s_l, axis=-1, keepdims=True))
        e_c = jnp.exp2(s_c - m)
        e_l = jnp.exp2(s_l - m)
        den = jnp.sum(e_c, axis=-1, keepdims=True) + jnp.sum(e_l, axis=-1, keepdims=True)
        o = _dot(_bf(e_c), vc_s[:, hs]) + _dot(_bf(e_l), v_s[pl.ds(start, win), hs])
        outs.append(o / den)
    o_ref[...] = _bf(jnp.concatenate(outs, axis=-1))


def _na_lat(na, kc, vc, tb, *, nreq, n, rps):
    tp = na.shape[0]
    rows = n // GRID_W
    past = kc.shape[1]
    steps = rows // rps
    tq = rps * GRID_W
    return pl.pallas_call(
        functools.partial(_na_lat_kernel, rows=rows, rps=rps),
        grid=(nreq, steps),
        in_specs=[pl.BlockSpec((tq, GW), lambda b, r: (b * steps + r, 0)),
                  pl.BlockSpec((n, GW), lambda b, r: (b, 1)),
                  pl.BlockSpec((n, GW), lambda b, r: (b, 2)),
                  pl.BlockSpec((1, past, GW), lambda b, r: (b, 0, 0)),
                  pl.BlockSpec((1, past, GW), lambda b, r: (b, 0, 0)),
                  pl.BlockSpec((NH, 1) + tb.shape[2:],
                               lambda b, r: (0, jnp.minimum(r, 1) + r // (steps - 1), 0, 0))],
        out_specs=pl.BlockSpec((tq, GW), lambda b, r: (b * steps + r, 0)),
        out_shape=_sds((tp, GW), BF16),
        scratch_shapes=[pltpu.VMEM((n, GW), BF16), pltpu.VMEM((n, GW), BF16),
                        pltpu.VMEM((GW, past), BF16), pltpu.VMEM((past, GW), BF16)],
        compiler_params=_params(2),
        name="na_latent",
    )(na, na, na, kc, vc, tb)


def _df_lat_kernel(q_ref, k_ref, v_ref, kc_ref, vc_ref, lam_ref, o_ref, kt_s, v_s, kct_s, vc_s, *, lam_init):
    @pl.when(pl.program_id(1) == 0)
    def _():
        kt_s[...] = _bf(k_ref[...].T)
        v_s[...] = _bf(v_ref[...])
        kct_s[...] = _bf(kc_ref[0].T)
        vc_s[...] = _bf(vc_ref[0])

    lam = _diff_lambda(lam_ref, lam_init)
    outs = []
    for h in range(NH):
        hs = slice(h * HD, (h + 1) * HD)
        os = []
        for c in range(2):
            cs = slice(h * HD + c * 32, h * HD + (c + 1) * 32)
            q = _bf(q_ref[:, cs] * (32 ** -0.5 * LOG2E))
            s_c = _dot(q, kct_s[cs, :])
            s_l = _dot(q, kt_s[cs, :])
            m = jnp.maximum(jnp.max(s_c, axis=-1, keepdims=True), jnp.max(s_l, axis=-1, keepdims=True))
            e_c = jnp.exp2(s_c - m)
            e_l = jnp.exp2(s_l - m)
            den = jnp.sum(e_c, axis=-1, keepdims=True) + jnp.sum(e_l, axis=-1, keepdims=True)
            os.append((_dot(_bf(e_c), vc_s[:, hs]) + _dot(_bf(e_l), v_s[:, hs])) / den)
        outs.append(os[0] - lam * os[1])
    o_ref[...] = jnp.concatenate(outs, axis=-1)


def _df_lat(df, kc, vc, lamp, lam_init, *, nreq, n, tq=512):
    tp = df.shape[0]
    nq = n // tq
    past = kc.shape[1]
    return pl.pallas_call(
        functools.partial(_df_lat_kernel, lam_init=lam_init),
        grid=(nreq, nq),
        in_specs=[pl.BlockSpec((tq, GW), lambda b, j: (b * nq + j, 0)),
                  pl.BlockSpec((n, GW), lambda b, j: (b, 1)),
                  pl.BlockSpec((n, GW), lambda b, j: (b, 2)),
                  pl.BlockSpec((1, past, GW), lambda b, j: (b, 0, 0)),
                  pl.BlockSpec((1, past, GW), lambda b, j: (b, 0, 0)),
                  _full((4, 32))],
        out_specs=pl.BlockSpec((tq, GW), lambda b, j: (b * nq + j, 0)),
        out_shape=_sds((tp, GW)),
        scratch_shapes=[pltpu.VMEM((GW, n), BF16), pltpu.VMEM((n, GW), BF16),
                        pltpu.VMEM((GW, past), BF16), pltpu.VMEM((past, GW), BF16)],
        compiler_params=_params(2),
        name="df_latent",
    )(df, df, df, kc, vc, lamp)


def _rg_kernel(x_ref, g_ref, cw_ref, cb_ref, wbd_ref, bias_ref, lam_ref, h0_ref,
               y_ref, fin_ref, a_s, b_s, *, n):
    x = x_ref[...]
    row = lax.broadcasted_iota(jnp.int32, x.shape, 0)
    xc = cb_ref[...] + jnp.where(row >= 2, pltpu.roll(x, 2, axis=0), 0.0) * cw_ref[0:1, :]
    xc = xc + jnp.where(row >= 1, pltpu.roll(x, 1, axis=0), 0.0) * cw_ref[1:2, :]
    xc = xc + x * cw_ref[2:3, :]
    xc = xc + jnp.where(row < n - 1, pltpu.roll(x, n - 1, axis=0), 0.0) * cw_ref[3:4, :]
    z = _dot(_bf(xc), wbd_ref[...]) + bias_ref[...]
    sub = row & 7
    for d in range(2):
        rgate = jax.nn.sigmoid(z[:, 512 * d:512 * d + GW])
        igate = jax.nn.sigmoid(z[:, 512 * d + GW:512 * d + 2 * GW])
        la = -RG_C * rgate * _softplus(-lam_ref[d:d + 1, :])
        a = jnp.exp(la)
        t = jnp.tanh(la)
        b = jnp.sqrt(-2.0 * t / (1.0 - t)) * igate * xc
        for dd in (1, 2, 4):
            if d == 0:
                keep = sub >= dd
                a_sh = jnp.where(keep, pltpu.roll(a, dd, axis=0), 1.0)
                b_sh = jnp.where(keep, pltpu.roll(b, dd, axis=0), 0.0)
            else:
                keep = sub < 8 - dd
                a_sh = jnp.where(keep, pltpu.roll(a, n - dd, axis=0), 1.0)
                b_sh = jnp.where(keep, pltpu.roll(b, n - dd, axis=0), 0.0)
            b = b + a * b_sh
            a = a * a_sh
        a_s[d] = a
        b_s[d] = b

    nt = n // 8

    def body(t, carry):
        hf, hb = carry
        sf = pl.multiple_of(t * 8, 8)
        sb = pl.multiple_of((nt - 1 - t) * 8, 8)
        tf = a_s[0, pl.ds(sf, 8), :] * hf + b_s[0, pl.ds(sf, 8), :]
        tb = a_s[1, pl.ds(sb, 8), :] * hb + b_s[1, pl.ds(sb, 8), :]
        b_s[0, pl.ds(sf, 8), :] = tf
        b_s[1, pl.ds(sb, 8), :] = tb
        return tf[7:8, :], tb[0:1, :]

    hf, hb = lax.fori_loop(0, nt, body, (h0_ref[0, 0:1, :], h0_ref[0, 1:2, :]))
    fin_ref[0, 0:1, :] = hf
    fin_ref[0, 1:2, :] = hb
    gg = g_ref[...]
    cdf = 0.5 * (1.0 + jnp.tanh(math.sqrt(2.0 / math.pi) * (gg + 0.044715 * (gg * gg * gg))))
    y_ref[...] = _bf((b_s[0] + b_s[1]) * (gg * cdf))


def _rglru(rg, cw, cb, wbd, bias, lam, h0, *, nreq, n):
    tp = rg.shape[0]
    return pl.pallas_call(
        functools.partial(_rg_kernel, n=n),
        grid=(nreq,),
        in_specs=[pl.BlockSpec((n, GW), lambda b: (b, 0)), pl.BlockSpec((n, GW), lambda b: (b, 1)),
                  _full((4, GW)), _full((1, GW)), _full((GW, 4 * GW)), _full((1, 4 * GW)), _full((2, GW)),
                  pl.BlockSpec((1, 2, GW), lambda b: (b, 0, 0))],
        out_specs=[pl.BlockSpec((n, GW), lambda b: (b, 0)), pl.BlockSpec((1, 2, GW), lambda b: (b, 0, 0))],
        out_shape=[_sds((tp, GW), BF16), _sds((nreq, 2, GW))],
        scratch_shapes=[pltpu.VMEM((2, n, GW), F32), pltpu.VMEM((2, n, GW), F32)],
        compiler_params=_params(1),
        name="rglru",
    )(rg, rg, cw, cb, wbd, bias, lam, h0)


def _merge_kernel(hf_ref, hb_ref, mlo_ref, yna_ref, yrg_ref, odf_ref, x_ref, mod_ref, mlg_ref, sub_ref,
                  wout_ref, n2_ref, rw_ref, x1_ref, hn2_ref, lg_ref, *, row_base, row_div, lam_init):
    r = _mod_row(pl.program_id(0), row_base, row_div)
    y_ml = _seg_rms(hf_ref[...] + hb_ref[...], NH, mlg_ref[...]) * jax.nn.sigmoid(mlo_ref[...])
    y_df = _seg_rms(odf_ref[...], NH, sub_ref[...]) * (1.0 - lam_init)
    y = jnp.concatenate([_bf(y_ml), _bf(yna_ref[...]), _bf(yrg_ref[...]), _bf(y_df)], axis=-1)
    o = _dot(y, wout_ref[...])
    x1 = x_ref[...] + mod_ref[pl.ds(r, 1), 2 * D:3 * D] * o
    x1_ref[...] = x1
    ms = jnp.mean(x1 * x1, axis=-1, keepdims=True)
    hn = x1 * lax.rsqrt(ms + EPS) * n2_ref[...]
    hn = _bf(hn * (1.0 + mod_ref[pl.ds(r, 1), 4 * D:5 * D]) + mod_ref[pl.ds(r, 1), 3 * D:4 * D])
    hn2_ref[...] = hn
    lg_ref[...] = _dot(hn, rw_ref[...])


def _merge(hf, hb, mlz, yna, yrg, odf, x, mod, mlg, sub, wout, n2, rw, lam_init, *, row_base, row_div, tm=512):
    tp = x.shape[0]
    g = lambda c: pl.BlockSpec((tm, GW), lambda i: (i, c))
    return pl.pallas_call(
        functools.partial(_merge_kernel, row_base=row_base, row_div=row_div, lam_init=lam_init),
        grid=(tp // tm,),
        in_specs=[g(0), g(0), g(3), g(0), g(0), g(0),
                  pl.BlockSpec((tm, D), lambda i: (i, 0)), _full((8, 6 * D)), _full((1, GW)), _full((1, GW)),
                  _full((D, D)), _full((1, D)), _full((D, 128))],
        out_specs=[pl.BlockSpec((tm, D), lambda i: (i, 0)), pl.BlockSpec((tm, D), lambda i: (i, 0)),
                   pl.BlockSpec((tm, 128), lambda i: (i, 0))],
        out_shape=[_sds((tp, D)), _sds((tp, D), BF16), _sds((tp, 128))],
        compiler_params=_params(1),
        name="merge",
    )(hf, hb, mlz, yna, yrg, odf, x, mod, mlg, sub, wout, n2, rw)


def _excl_cumsum_lanes(mask):
    blk = 256
    r = lax.broadcasted_iota(jnp.int32, (blk, blk), 0)
    c = lax.broadcasted_iota(jnp.int32, (blk, blk), 1)
    tri = (r < c).astype(F32).astype(BF16)
    off = jnp.zeros((mask.shape[0], 1), F32)
    outs = []
    for i in range(mask.shape[1] // blk):
        mb = mask[:, i * blk:(i + 1) * blk]
        outs.append(_dot(_bf(mb), tri) + off)
        off = off + jnp.sum(mb, axis=-1, keepdims=True)
    return jnp.concatenate(outs, axis=-1)


def _route_kernel(lg_ref, pos_ref, aff_ref, post_ref, *, cap, n, rb):
    lg = lg_ref[...].T[0:NE, :]
    ex = jnp.exp(lg - jnp.max(lg, axis=0, keepdims=True))
    aff = ex / jnp.sum(ex, axis=0, keepdims=True)
    aff_ref[...] = aff
    aff = jnp.concatenate([aff[:, i * n:(i + 1) * n] for i in range(rb)], axis=0)
    thr = jnp.zeros((rb * NE, 1), jnp.int32)
    for bit in range(30, -1, -1):
        cand = thr | (1 << bit)
        cnt = jnp.sum((aff >= pltpu.bitcast(cand, F32)).astype(jnp.int32), axis=-1, keepdims=True)
        thr = jnp.where(cnt >= cap, cand, thr)
    gt = aff >= pltpu.bitcast(thr + 1, F32)
    eq = (aff >= pltpu.bitcast(thr, F32)) & jnp.logical_not(gt)
    need = (cap - jnp.sum(gt.astype(jnp.int32), axis=-1, keepdims=True)).astype(F32)
    eq_rank = _excl_cumsum_lanes(eq.astype(F32))
    sel = gt | (eq & (eq_rank < need))
    slot = _excl_cumsum_lanes(sel.astype(F32))
    pos = jnp.where(sel, slot.astype(jnp.int32), -1)
    pos = jnp.concatenate([pos[i * NE:(i + 1) * NE, :] for i in range(rb)], axis=1)
    pos_ref[...] = pos
    post_ref[...] = jnp.concatenate([pos, jnp.full((128 - NE, rb * n), -1, jnp.int32)], axis=0).T


def _route(lg, *, nreq, n, cap, rb):
    tp = lg.shape[0]
    return pl.pallas_call(
        functools.partial(_route_kernel, cap=cap, n=n, rb=rb),
        grid=(nreq // rb,),
        in_specs=[pl.BlockSpec((rb * n, 128), lambda b: (b, 0))],
        out_specs=[pl.BlockSpec((NE, rb * n), lambda b: (0, b)), pl.BlockSpec((NE, rb * n), lambda b: (0, b)),
                   pl.BlockSpec((rb * n, 128), lambda b: (b, 0))],
        out_shape=[_sds((NE, tp), jnp.int32), _sds((NE, tp)), _sds((tp, 128), jnp.int32)],
        compiler_params=_params(1),
        name="route",
    )(lg)


def _gather_kernel(pos_ref, aff_ref, h_ref, xs_ref, w_ref, *, eb, cap, n):
    eg = pl.program_id(1)
    io = lax.broadcasted_iota(jnp.int32, (cap, n), 0)
    sels = []
    for k in range(eb):
        e = eg * eb + k
        sel = pos_ref[pl.ds(e, 1), :] == io
        sels.append(sel.astype(F32).astype(BF16))
        w = jnp.sum(jnp.where(sel, aff_ref[pl.ds(e, 1), :], 0.0), axis=-1, keepdims=True)
        w_ref[k] = jnp.broadcast_to(w, (cap, 128))
    xs = _dot(jnp.concatenate(sels, axis=0), h_ref[...]).astype(BF16)
    for k in range(eb):
        xs_ref[k] = xs[k * cap:(k + 1) * cap, :]


def _gather(pos, aff, hn2, *, nreq, n, cap, eb):
    return pl.pallas_call(
        functools.partial(_gather_kernel, eb=eb, cap=cap, n=n),
        grid=(nreq, NE // eb),
        in_specs=[pl.BlockSpec((NE, n), lambda b, g: (0, b)), pl.BlockSpec((NE, n), lambda b, g: (0, b)),
                  pl.BlockSpec((n, D), lambda b, g: (b, 0))],
        out_specs=[pl.BlockSpec((eb, cap, D), lambda b, g: (g, b, 0)),
                   pl.BlockSpec((eb, cap, 128), lambda b, g: (g, b, 0))],
        out_shape=[_sds((NE, nreq * cap, D), BF16), _sds((NE, nreq * cap, 128))],
        compiler_params=_params(2),
        name="moe_gather",
    )(pos, aff, hn2)


def _expert_kernel(xc_ref, xl_ref, wc_ref, wl_ref, wg_ref, wu_ref, wd_ref, yc_ref, yl_ref, *, rows):
    wg = _bf(wg_ref[0, 0])
    wu = _bf(wu_ref[0, 0])
    wd = _bf(wd_ref[0, 0])
    tm = 512
    for x_ref, w_ref, y_ref in ((xc_ref, wc_ref, yc_ref), (xl_ref, wl_ref, yl_ref)):
        for ch in range(rows // tm):
            sl = slice(ch * tm, (ch + 1) * tm)
            x = x_ref[0, sl, :]
            g = _dot(x, wg)
            u = _dot(x, wu)
            a = _bf(g * jax.nn.sigmoid(g) * u)
            y_ref[0, sl, :] = _bf(_dot(a, wd) * w_ref[0, sl, 0:1])


def _experts(xs_c, xs_l, w_c, w_l, wg, wu, wd, layer):
    rows = xs_c.shape[1]
    dff = wg.shape[-1]
    xspec = pl.BlockSpec((1, rows, D), lambda e: (e, 0, 0))
    wspec = pl.BlockSpec((1, rows, 128), lambda e: (e, 0, 0))
    return pl.pallas_call(
        functools.partial(_expert_kernel, rows=rows),
        grid=(NE,),
        in_specs=[xspec, xspec, wspec, wspec,
                  pl.BlockSpec((1, 1, D, dff), lambda e: (layer, e, 0, 0)),
                  pl.BlockSpec((1, 1, D, dff), lambda e: (layer, e, 0, 0)),
                  pl.BlockSpec((1, 1, dff, D), lambda e: (layer, e, 0, 0))],
        out_specs=[xspec, xspec],
        out_shape=[_sds((NE, rows, D), BF16), _sds((NE, rows, D), BF16)],
        compiler_params=_params(1),
        name="moe_experts",
    )(xs_c, xs_l, w_c, w_l, wg, wu, wd)


def _scatter_kernel(idx_ref, y_ref, x1_ref, mod_ref, o_ref, *, eb, cap, n, row_base, row_mul, token_major):
    eg = pl.program_id(1)
    last = pl.num_programs(1) - 1
    r = row_base + row_mul * pl.program_id(0)
    g2 = mod_ref[pl.ds(r, 1), 5 * D:6 * D]
    tn = min(n, 512)
    ys = jnp.concatenate([y_ref[k] for k in range(eb)], axis=0)
    for t in range(n // tn):
        rows = slice(t * tn, (t + 1) * tn)
        if token_major:
            io = lax.broadcasted_iota(jnp.int32, (tn, cap), 1)
            shift = (128 - eg * eb) & 127
            pt = pltpu.roll(idx_ref[rows, :], shift, axis=1)
            sels = [(pt[:, k:k + 1] == io).astype(F32).astype(BF16) for k in range(eb)]
            part = _dot(jnp.concatenate(sels, axis=1), ys)
        else:
            io = lax.broadcasted_iota(jnp.int32, (cap, tn), 0)
            sels = [(idx_ref[pl.ds(eg * eb + k, 1), rows] == io).astype(F32).astype(BF16) for k in range(eb)]
            part = _dot_tn(jnp.concatenate(sels, axis=0), ys)
        o_ref[rows, :] = jnp.where(eg == 0, part, o_ref[rows, :] + part)

    @pl.when(eg == last)
    def _():
        o_ref[...] = x1_ref[...] + g2 * o_ref[...]


def _scatter(idx, y, x1, mod, *, nreq, n, cap, eb, row_base, row_mul):
    tp = x1.shape[0]
    token_major = idx.shape[0] == tp
    idx_spec = (pl.BlockSpec((n, 128), lambda b, g: (b, 0)) if token_major
                else pl.BlockSpec((NE, n), lambda b, g: (0, b)))
    return pl.pallas_call(
        functools.partial(_scatter_kernel, eb=eb, cap=cap, n=n, row_base=row_base, row_mul=row_mul,
                          token_major=token_major),
        grid=(nreq, NE // eb),
        in_specs=[idx_spec,
                  pl.BlockSpec((eb, cap, D), lambda b, g: (g, b, 0)),
                  pl.BlockSpec((n, D), lambda b, g: (b, 0)), _full((8, 6 * D))],
        out_specs=pl.BlockSpec((n, D), lambda b, g: (b, 0)),
        out_shape=_sds((tp, D)),
        compiler_params=_params(2),
        name="moe_scatter",
    )(idx, y, x1, mod)


def _rope_tables(n):
    nf = 8
    t = np.arange(n)
    rowp = (t // GRID_W).astype(np.float32)
    colp = (t % GRID_W).astype(np.float32)
    inv = (np.float32(ROPE_BASE) ** (-np.arange(nf, dtype=np.float32) / np.float32(nf))).astype(np.float32)
    lane = np.arange(GW)
    c32 = lane % 32
    pos = np.where((c32 < 16)[None, :], rowp[:, None], colp[:, None]).astype(np.float32)
    ang = (pos * inv[(c32 % 8)][None, :]).astype(np.float32).astype(np.float64)
    sign = np.where((lane % 16) < 8, -1.0, 1.0)[None, :]
    return jnp.asarray(np.cos(ang), F32), jnp.asarray(np.sin(ang) * sign, F32)


def _block_diag(w):
    nb, bi, bo = w.shape
    return (jnp.eye(nb, dtype=w.dtype)[:, None, :, None] * w[:, :, None, :]).reshape(nb * bi, nb * bo)


def _layer_params(l, w_in, ml_gate_b, na_qn_g, na_kn_g, df_qn_g, df_kn_g, rg_wa, rg_wx, rg_ba, rg_bx,
                  df_lq1, df_lk1, df_lq2, df_lk2, df_subln_g, w_out, router_w):
    wi = w_in[l]
    w_r = jnp.concatenate([wi[:, 0:1024], wi[:, 1040:3088], wi[:, 1024:1040],
                           jnp.zeros((D, PROJ_PAD - 3088), F32)], axis=1).astype(BF16)
    gate_b = jnp.pad(ml_gate_b[l], (0, 128 - 16)).reshape(1, 128)
    qkg = jnp.stack([jnp.tile(na_qn_g[l], NH), jnp.tile(na_kn_g[l], NH),
                     jnp.tile(df_qn_g[l], 2 * NH), jnp.tile(df_kn_g[l], 2 * NH)])
    wbd = jnp.concatenate([_block_diag(rg_wa[l, 0]), _block_diag(rg_wx[l, 0]),
                           _block_diag(rg_wa[l, 1]), _block_diag(rg_wx[l, 1])], axis=1).astype(BF16)
    rg_bias = jnp.concatenate([rg_ba[l, 0], rg_bx[l, 0], rg_ba[l, 1], rg_bx[l, 1]]).reshape(1, 4 * GW)
    lamp = jnp.stack([df_lq1[l], df_lk1[l], df_lq2[l], df_lk2[l]])
    sub = jnp.tile(df_subln_g[l], NH).reshape(1, GW)
    rw = jnp.pad(router_w[l], ((0, 0), (0, 128 - NE))).astype(BF16)
    return dict(w_r=w_r, gate_b=gate_b, qkg=qkg, wbd=wbd, rg_bias=rg_bias, lamp=lamp, sub=sub,
                wout=w_out[l].astype(BF16), rw=rw)


def kernel(x_prompt, x_sample, cache_na_k, cache_na_v, cache_df_k, cache_df_v, state_ml_c, state_ml_n, state_ml_m, state_rg_h, c, c_ctx, norm1_g, norm2_g, w_mod, b_mod, w_in, ml_gate_b, ml_norm_g, na_qn_g, na_kn_g, na_rpb, rg_conv_w, rg_conv_b, rg_wa, rg_ba, rg_wx, rg_bx, rg_lam, df_qn_g, df_kn_g, df_lq1, df_lk1, df_lq2, df_lk2, df_subln_g, w_out, router_w, moe_wg, moe_wu, moe_wd):
    nb, seq, _ = x_prompt.shape
    db, dseq, _ = x_sample.shape
    depth = w_in.shape[0]
    past = cache_na_k.shape[2]
    tm = 512
    cap_c = 2 * seq // NE
    cap_l = 2 * dseq // NE

    cv = jnp.concatenate([c_ctx[None, :], c, jnp.zeros((8 - 1 - db, D), F32)], axis=0)
    mod_all = _modulation(cv, w_mod, b_mod)
    rope = _rope_tables(dseq)

    xc = x_prompt.reshape(nb * seq, D)
    xl = x_sample.reshape(db * dseq, D)
    ctx_out = []
    caches = None
    for l in range(depth):
        lam_init = 0.8 - 0.6 * math.exp(-0.3 * l)
        p = _layer_params(l, w_in, ml_gate_b, na_qn_g, na_kn_g, df_qn_g, df_kn_g, rg_wa, rg_wx, rg_ba, rg_bx,
                          df_lq1, df_lk1, df_lq2, df_lk2, df_subln_g, w_out, router_w)
        mod = mod_all[l]
        g1 = norm1_g[l].reshape(1, D)
        g2 = norm2_g[l].reshape(1, D)
        mlg = ml_norm_g[l].reshape(1, GW)
        cb = rg_conv_b[l].reshape(1, GW)
        tb = _rpb_table(na_rpb[l], rows=dseq // GRID_W, rps=NA_RPS)

        nt_c = nb * seq // tm
        ml, gates, naq, rg, dfq, *caches = _proj_in(xc, mod, g1, p["w_r"], p["gate_b"], p["qkg"], caches=caches,
                                                    cache_shape=(nb, depth, seq, GW), layer=l,
                                                    row_base=0, row_div=nt_c, tm=tm)
        hf, hb, cn_c, m_c = _mlstm(ml, gates, nreq=nb, nc=seq // CHUNK)
        y_na, o_df = _ctx_attn(naq, dfq, caches, l, p["lamp"], lam_init, nreq=nb, n=seq)
        y_rg, rg_fin = _rglru(rg, rg_conv_w[l], cb, p["wbd"], p["rg_bias"], rg_lam[l],
                              jnp.zeros((nb, 2, GW), F32), nreq=nb, n=seq)
        x1_c, hn2_c, lg_c = _merge(hf, hb, ml, y_na, y_rg, o_df, xc, mod, mlg, p["sub"], p["wout"], g2, p["rw"],
                                   lam_init, row_base=0, row_div=nt_c, tm=tm)
        pos_c, aff_c, _ = _route(lg_c, nreq=nb, n=seq, cap=cap_c, rb=8)
        xs_c, w_c = _gather(pos_c, aff_c, hn2_c, nreq=nb, n=seq, cap=cap_c, eb=NE)
        ctx_out.append((cn_c, m_c, rg_fin))

        tiles_req = dseq // tm
        ml, gates, na, rg, df = _proj_in(xl, mod, g1, p["w_r"], p["gate_b"], p["qkg"], rope=rope,
                                         row_base=1, row_div=tiles_req, tm=tm)
        cn0 = _mlstm_pack_state(state_ml_c[:, l], state_ml_n[:, l])
        m0 = jnp.broadcast_to(state_ml_m[:, l].reshape(db, 8, 1), (db, 8, 128))
        hf, hb, _, _ = _mlstm(ml, gates, (cn0, m0), nreq=db, nc=dseq // CHUNK)
        y_na = _na_lat(na, cache_na_k[:, l].reshape(db, past, GW), cache_na_v[:, l].reshape(db, past, GW), tb,
                       nreq=db, n=dseq, rps=NA_RPS)
        o_df = _df_lat(df, cache_df_k[:, l].reshape(db, past, GW), cache_df_v[:, l].reshape(db, past, GW),
                       p["lamp"], lam_init, nreq=db, n=dseq)
        y_rg, _ = _rglru(rg, rg_conv_w[l], cb, p["wbd"], p["rg_bias"], rg_lam[l], state_rg_h[:, l],
                         nreq=db, n=dseq)
        x1_l, hn2_l, lg_l = _merge(hf, hb, ml, y_na, y_rg, o_df, xl, mod, mlg, p["sub"], p["wout"], g2, p["rw"],
                                   lam_init, row_base=1, row_div=tiles_req, tm=tm)
        pos_l, aff_l, post_l = _route(lg_l, nreq=db, n=dseq, cap=cap_l, rb=1)
        xs_l, w_l = _gather(pos_l, aff_l, hn2_l, nreq=db, n=dseq, cap=cap_l, eb=2)

        y_c, y_l = _experts(xs_c, xs_l, w_c, w_l, moe_wg, moe_wu, moe_wd, l)
        xc = _scatter(pos_c, y_c, x1_c, mod, nreq=nb, n=seq, cap=cap_c, eb=NE, row_base=0, row_mul=0)
        xl = _scatter(post_l, y_l, x1_l, mod, nreq=db, n=dseq, cap=cap_l, eb=4, row_base=1, row_mul=1)

    y_prompt = xc.reshape(nb, seq, D)
    y_sample = xl.reshape(db, dseq, D)
    st = lambda f: jnp.stack([f(o) for o in ctx_out], axis=1)
    na_k = caches[0].reshape(nb, depth, seq, NH, HD)
    na_v = caches[1].reshape(nb, depth, seq, NH, HD)
    df_k = caches[2].reshape(nb, depth, seq, NH, 2, HD // 2)
    df_v = caches[3].reshape(nb, depth, seq, NH, HD)
    ml_c = st(lambda o: _mlstm_unpack_state(o[0])[0])
    ml_n = st(lambda o: _mlstm_unpack_state(o[0])[1])
    ml_m = st(lambda o: o[1][:, :, 0].reshape(nb, 2, NH))
    rg_h = st(lambda o: o[2])
    return (y_prompt, y_sample, na_k, na_v, df_k, df_v, ml_c, ml_n, ml_m, rg_h)
```

```python
import functools
import math

import numpy as np
import jax
import jax.numpy as jnp
from jax import lax
from jax.experimental import pallas as pl
from jax.experimental.pallas import tpu as pltpu

F32 = jnp.float32
BF16 = jnp.bfloat16

D = 1024
GW = 256
NH = 4
HD = 64
NE = 16
EPS = 1e-6
CHUNK = 256
GRID_W = 64
NA_ROWS = 8
NA_COLS = 16
SUB_ROWS = 256
NA_RPS = 4
RG_C = 8.0
ROPE_BASE = 10000.0
PROJ_PAD = 3200
VMEM_LIMIT_BYTES = 56 * 1024 * 1024
NEG_BIG = -1e30
LOG2E = 1.4426950408889634


def _bf(x):
    return x.astype(BF16)


def _dot(a, b):
    return jnp.dot(a, b, preferred_element_type=F32)


def _dot_nt(a, b):
    return lax.dot_general(a, b, (((1,), (1,)), ((), ())), preferred_element_type=F32)


def _dot_tn(a, b):
    return lax.dot_general(a, b, (((0,), (0,)), ((), ())), preferred_element_type=F32)


def _split3(x):
    p0 = _bf(x)
    r1 = x - p0.astype(F32)
    p1 = _bf(r1)
    return p0, p1, _bf(r1 - p1.astype(F32))


def _params(n_axes):
    return pltpu.CompilerParams(dimension_semantics=("arbitrary",) * n_axes,
                                vmem_limit_bytes=VMEM_LIMIT_BYTES)


def _full(shape):
    return pl.BlockSpec(shape, lambda *_: (0,) * len(shape))


def _sds(shape, dtype=F32):
    return jax.ShapeDtypeStruct(shape, dtype)


def _softplus(x):
    return jnp.maximum(x, 0.0) + jnp.log1p(jnp.exp(-jnp.abs(x)))


def _log_sigmoid(x):
    return jnp.minimum(x, 0.0) - jnp.log1p(jnp.exp(-jnp.abs(x)))


def _seg_rms(x, nseg, g_row):
    seg = x.shape[-1] // nseg
    lane = lax.broadcasted_iota(jnp.int32, x.shape, 1)
    x2 = x * x
    tot = jnp.zeros_like(x)
    for s in range(nseg):
        m = (lane >= s * seg) & (lane < (s + 1) * seg)
        t = jnp.sum(jnp.where(m, x2, 0.0), axis=-1, keepdims=True)
        tot = jnp.where(m, t, tot)
    return x * lax.rsqrt(tot * (1.0 / seg) + EPS) * g_row


def _mod_row(pid, row_base, row_div):
    return row_base + pid // row_div


def _mod_kernel(cv_ref, w_ref, b_ref, o_ref):
    cv = cv_ref[...]
    s = cv * jax.nn.sigmoid(cv)
    o_ref[0] = _dot(_bf(s), _bf(w_ref[0])) + b_ref[0]


def _modulation(cv, w_mod, b_mod):
    nl = w_mod.shape[0]
    tn = 1536
    return pl.pallas_call(
        _mod_kernel,
        grid=(nl, 6 * D // tn),
        in_specs=[_full((8, D)),
                  pl.BlockSpec((1, D, tn), lambda l, j: (l, 0, j)),
                  pl.BlockSpec((1, 1, tn), lambda l, j: (l, 0, j))],
        out_specs=pl.BlockSpec((1, 8, tn), lambda l, j: (l, 0, j)),
        out_shape=_sds((nl, 8, 6 * D)),
        compiler_params=_params(2),
        name="modulation",
    )(cv, w_mod, b_mod.reshape(nl, 1, 6 * D))


def _rope(x, cos_t, sin_t):
    lane = lax.broadcasted_iota(jnp.int32, x.shape, 1)
    first = (lane & 15) < 8
    sw = jnp.where(first, pltpu.roll(x, GW - 8, axis=1), pltpu.roll(x, 8, axis=1))
    return x * cos_t + sw * sin_t


def _proj_kernel(*refs, row_base, row_div, layer):
    ctx = layer is not None
    if ctx:
        (x_ref, mod_ref, g1_ref, w_ref, gb_ref, qkg_ref) = refs[:6]
        (ml_ref, gate_ref, naq_ref, rg_ref, dfq_ref, nk_ref, nv_ref, dk_ref, dv_ref) = refs[-9:]
        cache_layer = layer if nk_ref.shape[1] > 1 else 0
    else:
        (x_ref, mod_ref, g1_ref, w_ref, gb_ref, qkg_ref, cos_ref, sin_ref,
         ml_ref, gate_ref, na_ref, rg_ref, df_ref) = refs
    r = _mod_row(pl.program_id(0), row_base, row_div)
    sh = mod_ref[pl.ds(r, 1), 0:D]
    sc = mod_ref[pl.ds(r, 1), D:2 * D]
    for sub in range(x_ref.shape[0] // SUB_ROWS):
        rows = slice(sub * SUB_ROWS, (sub + 1) * SUB_ROWS)
        x = x_ref[rows, :]
        ms = jnp.mean(x * x, axis=-1, keepdims=True)
        y = x * lax.rsqrt(ms + EPS) * g1_ref[...]
        hn = _bf(y * (1.0 + sc) + sh)

        ml = _dot(hn, w_ref[:, 0:1024])
        ml_ref[rows, 0:256] = ml[:, 0:256]
        ml_ref[rows, 256:512] = ml[:, 256:512] * (HD ** -0.5)
        ml_ref[rows, 512:1024] = ml[:, 512:1024]

        gz = _dot(hn, w_ref[:, 3072:3200]) + gb_ref[...]
        lane = lax.broadcasted_iota(jnp.int32, gz.shape, 1)
        gate_ref[rows, :] = jnp.where(((lane >> 2) & 1) == 1, _log_sigmoid(gz), gz)

        rg_ref[rows, :] = _dot(hn, w_ref[:, 1792:2304])

        nz = _dot(hn, w_ref[:, 1024:1792])
        nq = _seg_rms(nz[:, 0:256], NH, qkg_ref[0:1, :])
        nk = _seg_rms(nz[:, 256:512], NH, qkg_ref[1:2, :])
        dz = _dot(hn, w_ref[:, 2304:3072])
        dq = _seg_rms(dz[:, 0:256], 2 * NH, qkg_ref[2:3, :])
        dk = _seg_rms(dz[:, 256:512], 2 * NH, qkg_ref[3:4, :])
        if ctx:
            naq_ref[rows, :] = nq
            dfq_ref[rows, :] = dq
            for c_ref, val in ((nk_ref, nk), (nv_ref, nz[:, 512:768]), (dk_ref, dk), (dv_ref, dz[:, 512:768])):
                c_ref[sub, cache_layer] = val
                for other in range(c_ref.shape[1]):
                    if other != cache_layer:
                        c_ref[sub, other] = jnp.zeros_like(val)
        else:
            cos_t = cos_ref[rows, :]
            sin_t = sin_ref[rows, :]
            na_ref[rows, 0:256] = nq
            na_ref[rows, 256:512] = nk
            na_ref[rows, 512:768] = nz[:, 512:768]
            df_ref[rows, 0:256] = _rope(dq, cos_t, sin_t)
            df_ref[rows, 256:512] = _rope(dk, cos_t, sin_t)
            df_ref[rows, 512:768] = dz[:, 512:768]


def _proj_in(x, mod, g1, w_r, gate_b, qkg, *, rope=None, caches=None, cache_shape=None, layer=None,
             row_base, row_div, tm=512):
    tp = x.shape[0]
    in_specs = [pl.BlockSpec((tm, D), lambda i: (i, 0)), _full((8, 6 * D)), _full((1, D)),
                _full((D, PROJ_PAD)), _full((1, 128)), _full((4, GW))]
    args = [x, mod, g1, w_r, gate_b, qkg]
    tok = lambda n: pl.BlockSpec((tm, n), lambda i: (i, 0))
    aliases = {}
    if layer is None:
        tiles = rope[0].shape[0] // tm
        in_specs += [pl.BlockSpec((tm, GW), lambda i: (i % tiles, 0))] * 2
        args += list(rope)
        out_specs = [tok(1024), tok(128), tok(768), tok(512), tok(768)]
        out_shape = [_sds((tp, n)) for n in (1024, 128, 768, 512, 768)]
    else:
        if caches is None:
            depth, seq = cache_shape[1], cache_shape[2]
            cspec = pl.BlockSpec((tm // seq, depth, seq, GW), lambda i: (i, 0, 0, 0))
        else:
            cache_shape, seq = caches[0].shape, caches[0].shape[2]
            cspec = pl.BlockSpec((tm // seq, 1, seq, GW), lambda i: (i, layer, 0, 0))
            in_specs += [pl.BlockSpec(memory_space=pl.ANY)] * 4
            args += list(caches)
            aliases = {6 + i: 5 + i for i in range(4)}
        assert seq == SUB_ROWS
        out_specs = [tok(1024), tok(128), tok(GW), tok(512), tok(GW)] + [cspec] * 4
        out_shape = [_sds((tp, n)) for n in (1024, 128, GW, 512, GW)] + [_sds(cache_shape)] * 4
    return pl.pallas_call(
        functools.partial(_proj_kernel, row_base=row_base, row_div=row_div, layer=layer),
        grid=(tp // tm,),
        in_specs=in_specs,
        out_specs=out_specs,
        out_shape=out_shape,
        input_output_aliases=aliases,
        compiler_params=_params(1),
        name="proj_in",
    )(*args)


def _mlstm_dir(d, q_ref, k_ref, v_ref, g_ref, h_ref, st_s, m_s):
    lc = CHUNK
    g = g_ref[...]
    gt = g.T
    row = lax.broadcasted_iota(jnp.int32, (lc, lc), 0)
    col = lax.broadcasted_iota(jnp.int32, (lc, lc), 1)
    tri = (row >= col) if d == 0 else (row <= col)
    tri_t = (col >= row) if d == 0 else (col <= row)
    g_parts = _split3(g)
    gt_parts = _split3(gt)
    tri_b = tri.astype(F32).astype(BF16)
    tri_tb = tri_t.astype(F32).astype(BF16)
    bc_col = _dot(tri_b, g_parts[0]) + _dot(tri_b, g_parts[1]) + _dot(tri_b, g_parts[2])
    bc_row = _dot(gt_parts[0], tri_tb) + _dot(gt_parts[1], tri_tb) + _dot(gt_parts[2], tri_tb)
    r_rows = gt[d * 8:d * 8 + NH, :] - bc_row[d * 8 + NH:d * 8 + 2 * NH, :]
    k_t = k_ref[...].T
    k_tb = _bf(k_t)
    q_all = _bf(q_ref[...])
    v_all = _bf(v_ref[...])
    ones = jnp.ones((lc, 128), BF16)
    last = lc - 1 if d == 0 else 0
    hs = [slice(h * HD, (h + 1) * HD) for h in range(NH)]
    ms = [m_s[d * NH + h:d * NH + h + 1, 0:1] for h in range(NH)]
    rms = [jnp.where(tri, r_rows[h:h + 1, :], -jnp.inf) for h in range(NH)]
    big_rs = [jnp.maximum(jnp.max(rms[h], axis=-1, keepdims=True), ms[h]) for h in range(NH)]
    ss = [_bf(_dot(q_all[:, hs[h]], k_tb[hs[h], :]) * jnp.exp(rms[h] - big_rs[h])) for h in range(NH)]
    outs = []
    for h in range(NH):
        v2 = v_all[:, (h // 2) * 128:(h // 2 + 1) * 128]
        e = jnp.exp(ms[h] - big_rs[h])
        qh = q_all[:, hs[h]]
        tot = _dot(ss[h], v2) + e * _dot(qh, _bf(st_s[d, h, 0]))
        den = _dot(ss[h], ones) + e * _dot(qh, _bf(st_s[d, h, 1]))
        c_col = bc_col[:, d * 8 + NH + h:d * 8 + NH + h + 1]
        floor = jnp.exp(-(c_col + big_rs[h]))
        outs.append(tot / jnp.maximum(jnp.abs(den), floor))

        r_last = big_rs[h][last:last + 1, :]
        kw = _bf(k_t[hs[h], :] * jnp.exp(r_rows[h:h + 1, :] - r_last))
        gdec = jnp.exp(ms[h] - r_last)
        st_s[d, h, 0] = gdec * st_s[d, h, 0] + _dot(kw, v2)
        st_s[d, h, 1] = gdec * st_s[d, h, 1] + _dot(kw, ones)
        m_s[d * NH + h:d * NH + h + 1, :] = jnp.broadcast_to(c_col[last:last + 1, :] + r_last, (1, 128))
    lane = lax.broadcasted_iota(jnp.int32, (lc, 128), 1)
    h_ref[...] = jnp.concatenate([jnp.where(lane < HD, outs[2 * p], outs[2 * p + 1]) for p in range(NH // 2)],
                                 axis=-1)


def _mlstm_kernel(qf, kf, vf, gf, qb, kb, vb, gb, *rest):
    hf_ref, hb_ref, st_out, m_out, st_s, m_s = rest[-6:]
    j = pl.program_id(1)

    @pl.when(j == 0)
    def _():
        if len(rest) == 8:
            st_s[...] = rest[0][0]
            m_s[...] = rest[1][0]
        else:
            st_s[...] = jnp.zeros_like(st_s)
            m_s[...] = jnp.zeros_like(m_s)

    _mlstm_dir(0, qf, kf, vf, gf, hf_ref, st_s, m_s)
    _mlstm_dir(1, qb, kb, vb, gb, hb_ref, st_s, m_s)

    @pl.when(j == pl.num_programs(1) - 1)
    def _():
        st_out[0] = st_s[...]
        m_out[0] = m_s[...]


def _mlstm_pack_state(c, n):
    z = jnp.zeros_like(c)
    odd = (jnp.arange(NH) % 2 == 1)[None, None, :, None, None]
    c_pair = jnp.where(odd, jnp.concatenate([z, c], axis=-1), jnp.concatenate([c, z], axis=-1))
    n_rep = jnp.broadcast_to(n[..., None], n.shape + (128,))
    return jnp.stack([c_pair, n_rep], axis=3)


def _mlstm_unpack_state(st):
    odd = (jnp.arange(NH) % 2 == 1)[None, None, :, None, None]
    c = jnp.where(odd, st[:, :, :, 0, :, HD:2 * HD], st[:, :, :, 0, :, 0:HD])
    return c, st[:, :, :, 1, :, 0]


def _mlstm(mlz, gates, state=None, *, nreq, nc):
    tp = mlz.shape[0]
    st_spec = pl.BlockSpec((1, 2, NH, 2, HD, 128), lambda r, j: (r, 0, 0, 0, 0, 0))

    def fwd(col):
        return lambda r, j: (r * nc + j, col)

    def bwd(col):
        return lambda r, j: (r * nc + nc - 1 - j, col)

    in_specs = []
    for mk in (fwd, bwd):
        in_specs += [pl.BlockSpec((CHUNK, GW), mk(0)), pl.BlockSpec((CHUNK, GW), mk(1)),
                     pl.BlockSpec((CHUNK, GW), mk(2)), pl.BlockSpec((CHUNK, 128), mk(0))]
    args = [mlz, mlz, mlz, gates, mlz, mlz, mlz, gates]
    if state is not None:
        in_specs += [st_spec, pl.BlockSpec((1, 8, 128), lambda r, j: (r, 0, 0))]
        args += list(state)
    return pl.pallas_call(
        _mlstm_kernel,
        grid=(nreq, nc),
        in_specs=in_specs,
        out_specs=[pl.BlockSpec((CHUNK, GW), fwd(0)), pl.BlockSpec((CHUNK, GW), bwd(0)),
                   st_spec, pl.BlockSpec((1, 8, 128), lambda r, j: (r, 0, 0))],
        out_shape=[_sds((tp, GW)), _sds((tp, GW)), _sds((nreq, 2, NH, 2, HD, 128)), _sds((nreq, 8, 128))],
        scratch_shapes=[pltpu.VMEM((2, NH, 2, HD, 128), F32), pltpu.VMEM((8, 128), F32)],
        compiler_params=_params(2),
        name="mlstm",
    )(*args)


def _diff_lambda(lam_ref, lam_init):
    lp = lam_ref[...]
    a = jnp.exp(jnp.sum(lp[0:1, :] * lp[1:2, :], axis=-1, keepdims=True))
    b = jnp.exp(jnp.sum(lp[2:3, :] * lp[3:4, :], axis=-1, keepdims=True))
    return a - b + lam_init


def _softmax_pv(q, k_t, v):
    s = _dot(q, k_t)
    e = jnp.exp2(s - jnp.max(s, axis=-1, keepdims=True))
    return _dot(_bf(e), v) / jnp.sum(e, axis=-1, keepdims=True)


def _ctx_attn_kernel(nq, nk, nv, dq, dk, dv, lam_ref, ona_ref, odf_ref, *, lam_init):
    lam = _diff_lambda(lam_ref, lam_init)
    n = nk.shape[2]
    for i in range(nk.shape[0]):
        rows = slice(i * n, (i + 1) * n)
        nk_t = _bf(nk[i, 0].T)
        dk_t = _bf(dk[i, 0].T)
        na_out, df_out = [], []
        for h in range(NH):
            hs = slice(h * HD, (h + 1) * HD)
            na_out.append(_softmax_pv(_bf(nq[rows, hs] * (HD ** -0.5 * LOG2E)), nk_t[hs, :], _bf(nv[i, 0, :, hs])))
            vh = _bf(dv[i, 0, :, hs])
            os = []
            for c in range(2):
                cs = slice(h * HD + c * 32, h * HD + (c + 1) * 32)
                os.append(_softmax_pv(_bf(dq[rows, cs] * (32 ** -0.5 * LOG2E)), dk_t[cs, :], vh))
            df_out.append(os[0] - lam * os[1])
        ona_ref[rows, :] = _bf(jnp.concatenate(na_out, axis=-1))
        odf_ref[rows, :] = jnp.concatenate(df_out, axis=-1)


def _ctx_attn(naq, dfq, caches, layer, lamp, lam_init, *, nreq, n, rb=1):
    tp = naq.shape[0]
    qs = pl.BlockSpec((rb * n, GW), lambda b: (b, 0))
    cs = pl.BlockSpec((rb, 1, n, GW), lambda b: (b, layer, 0, 0))
    return pl.pallas_call(
        functools.partial(_ctx_attn_kernel, lam_init=lam_init),
        grid=(nreq // rb,),
        in_specs=[qs, cs, cs, qs, cs, cs, _full((4, 32))],
        out_specs=[qs, qs],
        out_shape=[_sds((tp, GW), BF16), _sds((tp, GW))],
        compiler_params=_params(1),
        name="ctx_attn",
    )(naq, caches[0], caches[1], dfq, caches[2], caches[3], lamp)


def _rpb_kernel(rpb_ref, o_ref, *, rows, rps):
    h = pl.program_id(0)
    shape = (GRID_W, NA_ROWS * GRID_W)
    lane = lax.broadcasted_iota(jnp.int32, shape, 1)
    cq = lax.broadcasted_iota(jnp.int32, shape, 0)
    ck = lane & (GRID_W - 1)
    c0 = jnp.clip(cq - NA_COLS // 2, 0, GRID_W - NA_COLS)
    ok = (ck >= c0) & (ck < c0 + NA_COLS)
    ncol = 2 * NA_COLS - 1
    ndr = 2 * NA_ROWS - 1
    x = lax.broadcasted_iota(jnp.int32, (1, 2 * GRID_W), 1)
    didx = jnp.clip(x - GRID_W, -(NA_COLS - 1), NA_COLS - 1) + (NA_COLS - 1)
    tiles = []
    for dr in range(ndr):
        frow = jnp.zeros((1, 2 * GRID_W), F32)
        for b in range(ncol):
            frow = jnp.where(didx == b, rpb_ref[(h * ndr + dr) * ncol + b], frow)
        rolled = pltpu.roll(jnp.broadcast_to(frow, (GRID_W, 2 * GRID_W)), GRID_W, axis=1, stride=1, stride_axis=0)
        tiles.append(rolled[:, 0:GRID_W])
    tbs = [jnp.where(ok, jnp.concatenate(tiles[s:s + NA_ROWS], axis=-1) * LOG2E, NEG_BIG) for s in range(NA_ROWS)]
    wrows = NA_ROWS + rps
    steps = rows // rps
    for t, j in enumerate((0, 1, steps - 1)):
        w0 = min(max(j * rps - NA_ROWS // 2, 0), rows - wrows)
        for a in range(rps):
            r = j * rps + a
            r0 = min(max(r - NA_ROWS // 2, 0), rows - NA_ROWS)
            off = (r0 - w0) * GRID_W
            rest = rps * GRID_W - off
            pieces = [tbs[r0 - r + NA_ROWS - 1]]
            if off:
                pieces = [jnp.full((GRID_W, off), NEG_BIG, F32)] + pieces
            if rest:
                pieces = pieces + [jnp.full((GRID_W, rest), NEG_BIG, F32)]
            o_ref[0, t, a * GRID_W:(a + 1) * GRID_W, :] = jnp.concatenate(pieces, axis=-1)


def _na_step_offsets(j, rows, rps):
    w0 = min(max(j * rps - NA_ROWS // 2, 0), rows - NA_ROWS - rps)
    return [(min(max(r - NA_ROWS // 2, 0), rows - NA_ROWS) - r,
             min(max(r - NA_ROWS // 2, 0), rows - NA_ROWS) - w0) for r in range(j * rps, (j + 1) * rps)]


def _rpb_table(rpb, *, rows, rps):
    steps = rows // rps
    assert all(_na_step_offsets(j, rows, rps) == _na_step_offsets(1, rows, rps) for j in range(1, steps - 1))
    shape = (rps * GRID_W, (NA_ROWS + rps) * GRID_W)
    return pl.pallas_call(
        functools.partial(_rpb_kernel, rows=rows, rps=rps),
        grid=(NH,),
        in_specs=[pl.BlockSpec(memory_space=pltpu.SMEM)],
        out_specs=pl.BlockSpec((1, 3) + shape, lambda h: (h, 0, 0, 0)),
        out_shape=_sds((NH, 3) + shape),
        compiler_params=_params(1),
        name="rpb_table",
    )(rpb.reshape(-1))


def _na_lat_kernel(q_ref, k_ref, v_ref, kc_ref, vc_ref, tb_ref, o_ref, k_s, v_s, kct_s, vc_s, *, rows, rps):
    j = pl.program_id(1)

    @pl.when(j == 0)
    def _():
        k_s[...] = _bf(k_ref[...])
        v_s[...] = _bf(v_ref[...])
        kct_s[...] = _bf(kc_ref[0].T)
        vc_s[...] = _bf(vc_ref[0])

    wrows = NA_ROWS + rps
    win = wrows * GRID_W
    w0 = jnp.clip(j * rps - NA_ROWS // 2, 0, rows - wrows)
    start = pl.multiple_of(w0 * GRID_W, GRID_W)
    outs = []
    for h in range(NH):
        hs = slice(h * HD, (h + 1) * HD)
        q = _bf(q_ref[:, hs] * (HD ** -0.5 * LOG2E))
        s_c = _dot(q, kct_s[hs, :])
        s_l = _dot_nt(q, k_s[pl.ds(start, win), hs]) + tb_ref[h, 0]
        m = jnp.maximum(jnp.max(s_c, axis=-1, keepdims=True), jnp.max(s_l, axis=-1, keepdims=True))
        e_c = jnp.exp2(s_c - m)
        e_l = jnp.exp2(s_l - m)
        den = jnp.sum(e_c, axis=-1, keepdims=True) + jnp.sum(e_l, axis=-1, keepdims=True)
        o = _dot(_bf(e_c), vc_s[:, hs]) + _dot(_bf(e_l), v_s[pl.ds(start, win), hs])
        outs.append(o / den)
    o_ref[...] = _bf(jnp.concatenate(outs, axis=-1))


def _na_lat(na, kc, vc, tb, *, nreq, n, rps):
    tp = na.shape[0]
    rows = n // GRID_W
    past = kc.shape[1]
    steps = rows // rps
    tq = rps * GRID_W
    return pl.pallas_call(
        functools.partial(_na_lat_kernel, rows=rows, rps=rps),
        grid=(nreq, steps),
        in_specs=[pl.BlockSpec((tq, GW), lambda b, r: (b * steps + r, 0)),
                  pl.BlockSpec((n, GW), lambda b, r: (b, 1)),
                  pl.BlockSpec((n, GW), lambda b, r: (b, 2)),
                  pl.BlockSpec((1, past, GW), lambda b, r: (b, 0, 0)),
                  pl.BlockSpec((1, past, GW), lambda b, r: (b, 0, 0)),
                  pl.BlockSpec((NH, 1) + tb.shape[2:],
                               lambda b, r: (0, jnp.minimum(r, 1) + r // (steps - 1), 0, 0))],
        out_specs=pl.BlockSpec((tq, GW), lambda b, r: (b * steps + r, 0)),
        out_shape=_sds((tp, GW), BF16),
        scratch_shapes=[pltpu.VMEM((n, GW), BF16), pltpu.VMEM((n, GW), BF16),
                        pltpu.VMEM((GW, past), BF16), pltpu.VMEM((past, GW), BF16)],
        compiler_params=_params(2),
        name="na_latent",
    )(na, na, na, kc, vc, tb)


def _df_lat_kernel(q_ref, k_ref, v_ref, kc_ref, vc_ref, lam_ref, o_ref, kt_s, v_s, kct_s, vc_s, *, lam_init):
    @pl.when(pl.program_id(1) == 0)
    def _():
        kt_s[...] = _bf(k_ref[...].T)
        v_s[...] = _bf(v_ref[...])
        kct_s[...] = _bf(kc_ref[0].T)
        vc_s[...] = _bf(vc_ref[0])

    lam = _diff_lambda(lam_ref, lam_init)
    outs = []
    for h in range(NH):
        hs = slice(h * HD, (h + 1) * HD)
        os = []
        for c in range(2):
            cs = slice(h * HD + c * 32, h * HD + (c + 1) * 32)
            q = _bf(q_ref[:, cs] * (32 ** -0.5 * LOG2E))
            s_c = _dot(q, kct_s[cs, :])
            s_l = _dot(q, kt_s[cs, :])
            m = jnp.maximum(jnp.max(s_c, axis=-1, keepdims=True), jnp.max(s_l, axis=-1, keepdims=True))
            e_c = jnp.exp2(s_c - m)
            e_l = jnp.exp2(s_l - m)
            den = jnp.sum(e_c, axis=-1, keepdims=True) + jnp.sum(e_l, axis=-1, keepdims=True)
            os.append((_dot(_bf(e_c), vc_s[:, hs]) + _dot(_bf(e_l), v_s[:, hs])) / den)
        outs.append(os[0] - lam * os[1])
    o_ref[...] = jnp.concatenate(outs, axis=-1)


def _df_lat(df, kc, vc, lamp, lam_init, *, nreq, n, tq=512):
    tp = df.shape[0]
    nq = n // tq
    past = kc.shape[1]
    return pl.pallas_call(
        functools.partial(_df_lat_kernel, lam_init=lam_init),
        grid=(nreq, nq),
        in_specs=[pl.BlockSpec((tq, GW), lambda b, j: (b * nq + j, 0)),
                  pl.BlockSpec((n, GW), lambda b, j: (b, 1)),
                  pl.BlockSpec((n, GW), lambda b, j: (b, 2)),
                  pl.BlockSpec((1, past, GW), lambda b, j: (b, 0, 0)),
                  pl.BlockSpec((1, past, GW), lambda b, j: (b, 0, 0)),
                  _full((4, 32))],
        out_specs=pl.BlockSpec((tq, GW), lambda b, j: (b * nq + j, 0)),
        out_shape=_sds((tp, GW)),
        scratch_shapes=[pltpu.VMEM((GW, n), BF16), pltpu.VMEM((n, GW), BF16),
                        pltpu.VMEM((GW, past), BF16), pltpu.VMEM((past, GW), BF16)],
        compiler_params=_params(2),
        name="df_latent",
    )(df, df, df, kc, vc, lamp)


def _rg_kernel(x_ref, g_ref, cw_ref, cb_ref, wbd_ref, bias_ref, lam_ref, h0_ref,
               y_ref, fin_ref, a_s, b_s, *, n):
    x = x_ref[...]
    row = lax.broadcasted_iota(jnp.int32, x.shape, 0)
    xc = cb_ref[...] + jnp.where(row >= 2, pltpu.roll(x, 2, axis=0), 0.0) * cw_ref[0:1, :]
    xc = xc + jnp.where(row >= 1, pltpu.roll(x, 1, axis=0), 0.0) * cw_ref[1:2, :]
    xc = xc + x * cw_ref[2:3, :]
    xc = xc + jnp.where(row < n - 1, pltpu.roll(x, n - 1, axis=0), 0.0) * cw_ref[3:4, :]
    z = _dot(_bf(xc), wbd_ref[...]) + bias_ref[...]
    sub = row & 7
    for d in range(2):
        rgate = jax.nn.sigmoid(z[:, 512 * d:512 * d + GW])
        igate = jax.nn.sigmoid(z[:, 512 * d + GW:512 * d + 2 * GW])
        la = -RG_C * rgate * _softplus(-lam_ref[d:d + 1, :])
        a = jnp.exp(la)
        t = jnp.tanh(la)
        b = jnp.sqrt(-2.0 * t / (1.0 - t)) * igate * xc
        for dd in (1, 2, 4):
            if d == 0:
                keep = sub >= dd
                a_sh = jnp.where(keep, pltpu.roll(a, dd, axis=0), 1.0)
                b_sh = jnp.where(keep, pltpu.roll(b, dd, axis=0), 0.0)
            else:
                keep = sub < 8 - dd
                a_sh = jnp.where(keep, pltpu.roll(a, n - dd, axis=0), 1.0)
                b_sh = jnp.where(keep, pltpu.roll(b, n - dd, axis=0), 0.0)
            b = b + a * b_sh
            a = a * a_sh
        a_s[d] = a
        b_s[d] = b

    nt = n // 8

    def body(t, carry):
        hf, hb = carry
        sf = pl.multiple_of(t * 8, 8)
        sb = pl.multiple_of((nt - 1 - t) * 8, 8)
        tf = a_s[0, pl.ds(sf, 8), :] * hf + b_s[0, pl.ds(sf, 8), :]
        tb = a_s[1, pl.ds(sb, 8), :] * hb + b_s[1, pl.ds(sb, 8), :]
        b_s[0, pl.ds(sf, 8), :] = tf
        b_s[1, pl.ds(sb, 8), :] = tb
        return tf[7:8, :], tb[0:1, :]

    hf, hb = lax.fori_loop(0, nt, body, (h0_ref[0, 0:1, :], h0_ref[0, 1:2, :]))
    fin_ref[0, 0:1, :] = hf
    fin_ref[0, 1:2, :] = hb
    gg = g_ref[...]
    cdf = 0.5 * (1.0 + jnp.tanh(math.sqrt(2.0 / math.pi) * (gg + 0.044715 * (gg * gg * gg))))
    y_ref[...] = _bf((b_s[0] + b_s[1]) * (gg * cdf))


def _rglru(rg, cw, cb, wbd, bias, lam, h0, *, nreq, n):
    tp = rg.shape[0]
    return pl.pallas_call(
        functools.partial(_rg_kernel, n=n),
        grid=(nreq,),
        in_specs=[pl.BlockSpec((n, GW), lambda b: (b, 0)), pl.BlockSpec((n, GW), lambda b: (b, 1)),
                  _full((4, GW)), _full((1, GW)), _full((GW, 4 * GW)), _full((1, 4 * GW)), _full((2, GW)),
                  pl.BlockSpec((1, 2, GW), lambda b: (b, 0, 0))],
        out_specs=[pl.BlockSpec((n, GW), lambda b: (b, 0)), pl.BlockSpec((1, 2, GW), lambda b: (b, 0, 0))],
        out_shape=[_sds((tp, GW), BF16), _sds((nreq, 2, GW))],
        scratch_shapes=[pltpu.VMEM((2, n, GW), F32), pltpu.VMEM((2, n, GW), F32)],
        compiler_params=_params(1),
        name="rglru",
    )(rg, rg, cw, cb, wbd, bias, lam, h0)


def _merge_kernel(hf_ref, hb_ref, mlo_ref, yna_ref, yrg_ref, odf_ref, x_ref, mod_ref, mlg_ref, sub_ref,
                  wout_ref, n2_ref, rw_ref, x1_ref, hn2_ref, lg_ref, *, row_base, row_div, lam_init):
    r = _mod_row(pl.program_id(0), row_base, row_div)
    y_ml = _seg_rms(hf_ref[...] + hb_ref[...], NH, mlg_ref[...]) * jax.nn.sigmoid(mlo_ref[...])
    y_df = _seg_rms(odf_ref[...], NH, sub_ref[...]) * (1.0 - lam_init)
    y = jnp.concatenate([_bf(y_ml), _bf(yna_ref[...]), _bf(yrg_ref[...]), _bf(y_df)], axis=-1)
    o = _dot(y, wout_ref[...])
    x1 = x_ref[...] + mod_ref[pl.ds(r, 1), 2 * D:3 * D] * o
    x1_ref[...] = x1
    ms = jnp.mean(x1 * x1, axis=-1, keepdims=True)
    hn = x1 * lax.rsqrt(ms + EPS) * n2_ref[...]
    hn = _bf(hn * (1.0 + mod_ref[pl.ds(r, 1), 4 * D:5 * D]) + mod_ref[pl.ds(r, 1), 3 * D:4 * D])
    hn2_ref[...] = hn
    lg_ref[...] = _dot(hn, rw_ref[...])


def _merge(hf, hb, mlz, yna, yrg, odf, x, mod, mlg, sub, wout, n2, rw, lam_init, *, row_base, row_div, tm=512):
    tp = x.shape[0]
    g = lambda c: pl.BlockSpec((tm, GW), lambda i: (i, c))
    return pl.pallas_call(
        functools.partial(_merge_kernel, row_base=row_base, row_div=row_div, lam_init=lam_init),
        grid=(tp // tm,),
        in_specs=[g(0), g(0), g(3), g(0), g(0), g(0),
                  pl.BlockSpec((tm, D), lambda i: (i, 0)), _full((8, 6 * D)), _full((1, GW)), _full((1, GW)),
                  _full((D, D)), _full((1, D)), _full((D, 128))],
        out_specs=[pl.BlockSpec((tm, D), lambda i: (i, 0)), pl.BlockSpec((tm, D), lambda i: (i, 0)),
                   pl.BlockSpec((tm, 128), lambda i: (i, 0))],
        out_shape=[_sds((tp, D)), _sds((tp, D), BF16), _sds((tp, 128))],
        compiler_params=_params(1),
        name="merge",
    )(hf, hb, mlz, yna, yrg, odf, x, mod, mlg, sub, wout, n2, rw)


def _excl_cumsum_lanes(mask):
    blk = 256
    r = lax.broadcasted_iota(jnp.int32, (blk, blk), 0)
    c = lax.broadcasted_iota(jnp.int32, (blk, blk), 1)
    tri = (r < c).astype(F32).astype(BF16)
    off = jnp.zeros((mask.shape[0], 1), F32)
    outs = []
    for i in range(mask.shape[1] // blk):
        mb = mask[:, i * blk:(i + 1) * blk]
        outs.append(_dot(_bf(mb), tri) + off)
        off = off + jnp.sum(mb, axis=-1, keepdims=True)
    return jnp.concatenate(outs, axis=-1)


def _route_kernel(lg_ref, pos_ref, aff_ref, *post_ref, cap, n, rb):
    lg = lg_ref[...].T[0:NE, :]
    ex = jnp.exp(lg - jnp.max(lg, axis=0, keepdims=True))
    aff = ex / jnp.sum(ex, axis=0, keepdims=True)
    aff_ref[...] = aff
    aff = jnp.concatenate([aff[:, i * n:(i + 1) * n] for i in range(rb)], axis=0)
    thr = jnp.zeros((rb * NE, 1), jnp.int32)
    for bit in range(30, -1, -1):
        cand = thr | (1 << bit)
        cnt = jnp.sum((aff >= pltpu.bitcast(cand, F32)).astype(jnp.int32), axis=-1, keepdims=True)
        thr = jnp.where(cnt >= cap, cand, thr)
    thr_f = pltpu.bitcast(thr, F32)
    gt = aff > thr_f
    eq = aff == thr_f
    need = (cap - jnp.sum(gt.astype(jnp.int32), axis=-1, keepdims=True)).astype(F32)
    eq_rank = _excl_cumsum_lanes(eq.astype(F32))
    sel = gt | (eq & (eq_rank < need))
    slot = _excl_cumsum_lanes(sel.astype(F32))
    pos = jnp.where(sel, slot.astype(jnp.int32), -1)
    pos = jnp.concatenate([pos[i * NE:(i + 1) * NE, :] for i in range(rb)], axis=1)
    pos_ref[...] = pos
    if post_ref:
        post_ref[0][...] = jnp.concatenate([pos, jnp.full((128 - NE, rb * n), -1, jnp.int32)], axis=0).T


def _route(lg, *, nreq, n, cap, rb, token_major_copy):
    tp = lg.shape[0]
    espec = pl.BlockSpec((NE, rb * n), lambda b: (0, b))
    out_specs = [espec, espec]
    out_shape = [_sds((NE, tp), jnp.int32), _sds((NE, tp))]
    if token_major_copy:
        out_specs.append(pl.BlockSpec((rb * n, 128), lambda b: (b, 0)))
        out_shape.append(_sds((tp, 128), jnp.int32))
    return pl.pallas_call(
        functools.partial(_route_kernel, cap=cap, n=n, rb=rb),
        grid=(nreq // rb,),
        in_specs=[pl.BlockSpec((rb * n, 128), lambda b: (b, 0))],
        out_specs=out_specs,
        out_shape=out_shape,
        compiler_params=_params(1),
        name="route",
    )(lg)


def _gather_kernel(pos_ref, aff_ref, h_ref, xs_ref, w_ref, *, eb, cap, n, rb):
    eg = pl.program_id(1)
    io = lax.broadcasted_iota(jnp.int32, (cap, n), 0)
    for i in range(rb):
        toks = slice(i * n, (i + 1) * n)
        slots = slice(i * cap, (i + 1) * cap)
        sels = []
        for k in range(eb):
            e = eg * eb + k
            sel = pos_ref[pl.ds(e, 1), toks] == io
            sels.append(sel.astype(F32).astype(BF16))
            w = jnp.sum(jnp.where(sel, aff_ref[pl.ds(e, 1), toks], 0.0), axis=-1, keepdims=True)
            w_ref[k, slots, :] = jnp.broadcast_to(w, (cap, 128))
        xs = _dot(jnp.concatenate(sels, axis=0), h_ref[toks, :]).astype(BF16)
        for k in range(eb):
            xs_ref[k, slots, :] = xs[k * cap:(k + 1) * cap, :]


def _gather(pos, aff, hn2, *, nreq, n, cap, eb, rb=1):
    return pl.pallas_call(
        functools.partial(_gather_kernel, eb=eb, cap=cap, n=n, rb=rb),
        grid=(nreq // rb, NE // eb),
        in_specs=[pl.BlockSpec((NE, rb * n), lambda b, g: (0, b)), pl.BlockSpec((NE, rb * n), lambda b, g: (0, b)),
                  pl.BlockSpec((rb * n, D), lambda b, g: (b, 0))],
        out_specs=[pl.BlockSpec((eb, rb * cap, D), lambda b, g: (g, b, 0)),
                   pl.BlockSpec((eb, rb * cap, 128), lambda b, g: (g, b, 0))],
        out_shape=[_sds((NE, nreq * cap, D), BF16), _sds((NE, nreq * cap, 128))],
        compiler_params=_params(2),
        name="moe_gather",
    )(pos, aff, hn2)


def _expert_kernel(xc_ref, xl_ref, wc_ref, wl_ref, wg_ref, wu_ref, wd_ref, yc_ref, yl_ref, *, rows):
    wg = _bf(wg_ref[0, 0])
    wu = _bf(wu_ref[0, 0])
    wd = _bf(wd_ref[0, 0])
    tm = 512
    for x_ref, w_ref, y_ref in ((xc_ref, wc_ref, yc_ref), (xl_ref, wl_ref, yl_ref)):
        for ch in range(rows // tm):
            sl = slice(ch * tm, (ch + 1) * tm)
            x = x_ref[0, sl, :]
            g = _dot(x, wg)
            u = _dot(x, wu)
            a = _bf(g * jax.nn.sigmoid(g) * u)
            y_ref[0, sl, :] = _bf(_dot(a, wd) * w_ref[0, sl, 0:1])


def _experts(xs_c, xs_l, w_c, w_l, wg, wu, wd, layer):
    rows = xs_c.shape[1]
    dff = wg.shape[-1]
    xspec = pl.BlockSpec((1, rows, D), lambda e: (e, 0, 0))
    wspec = pl.BlockSpec((1, rows, 128), lambda e: (e, 0, 0))
    return pl.pallas_call(
        functools.partial(_expert_kernel, rows=rows),
        grid=(NE,),
        in_specs=[xspec, xspec, wspec, wspec,
                  pl.BlockSpec((1, 1, D, dff), lambda e: (layer, e, 0, 0)),
                  pl.BlockSpec((1, 1, D, dff), lambda e: (layer, e, 0, 0)),
                  pl.BlockSpec((1, 1, dff, D), lambda e: (layer, e, 0, 0))],
        out_specs=[xspec, xspec],
        out_shape=[_sds((NE, rows, D), BF16), _sds((NE, rows, D), BF16)],
        compiler_params=_params(1),
        name="moe_experts",
    )(xs_c, xs_l, w_c, w_l, wg, wu, wd)


def _scatter_kernel(idx_ref, y_ref, x1_ref, mod_ref, o_ref, *, eb, cap, n, rb, row_base, row_mul, token_major):
    eg = pl.program_id(1)
    last = pl.num_programs(1) - 1
    r = row_base + row_mul * pl.program_id(0)
    g2 = mod_ref[pl.ds(r, 1), 5 * D:6 * D]
    tn = min(n, 512)
    for i in range(rb):
        ys = jnp.concatenate([y_ref[k, i * cap:(i + 1) * cap, :] for k in range(eb)], axis=0)
        for t in range(n // tn):
            rows = slice(i * n + t * tn, i * n + (t + 1) * tn)
            if token_major:
                io = lax.broadcasted_iota(jnp.int32, (tn, cap), 1)
                shift = (128 - eg * eb) & 127
                pt = pltpu.roll(idx_ref[rows, :], shift, axis=1)
                sels = [(pt[:, k:k + 1] == io).astype(F32).astype(BF16) for k in range(eb)]
                part = _dot(jnp.concatenate(sels, axis=1), ys)
            else:
                io = lax.broadcasted_iota(jnp.int32, (cap, tn), 0)
                sels = [(idx_ref[pl.ds(eg * eb + k, 1), rows] == io).astype(F32).astype(BF16) for k in range(eb)]
                part = _dot_tn(jnp.concatenate(sels, axis=0), ys)
            o_ref[rows, :] = jnp.where(eg == 0, part, o_ref[rows, :] + part)

    @pl.when(eg == last)
    def _():
        o_ref[...] = x1_ref[...] + g2 * o_ref[...]


def _scatter(idx, y, x1, mod, *, nreq, n, cap, eb, row_base, row_mul, rb=1):
    tp = x1.shape[0]
    assert rb == 1 or row_mul == 0
    token_major = idx.shape[0] == tp
    idx_spec = (pl.BlockSpec((rb * n, 128), lambda b, g: (b, 0)) if token_major
                else pl.BlockSpec((NE, rb * n), lambda b, g: (0, b)))
    return pl.pallas_call(
        functools.partial(_scatter_kernel, eb=eb, cap=cap, n=n, rb=rb, row_base=row_base, row_mul=row_mul,
                          token_major=token_major),
        grid=(nreq // rb, NE // eb),
        in_specs=[idx_spec,
                  pl.BlockSpec((eb, rb * cap, D), lambda b, g: (g, b, 0)),
                  pl.BlockSpec((rb * n, D), lambda b, g: (b, 0)), _full((8, 6 * D))],
        out_specs=pl.BlockSpec((rb * n, D), lambda b, g: (b, 0)),
        out_shape=_sds((tp, D)),
        compiler_params=_params(2),
        name="moe_scatter",
    )(idx, y, x1, mod)


def _rope_tables(n):
    nf = 8
    t = np.arange(n)
    rowp = (t // GRID_W).astype(np.float32)
    colp = (t % GRID_W).astype(np.float32)
    inv = (np.float32(ROPE_BASE) ** (-np.arange(nf, dtype=np.float32) / np.float32(nf))).astype(np.float32)
    lane = np.arange(GW)
    c32 = lane % 32
    pos = np.where((c32 < 16)[None, :], rowp[:, None], colp[:, None]).astype(np.float32)
    ang = (pos * inv[(c32 % 8)][None, :]).astype(np.float32).astype(np.float64)
    sign = np.where((lane % 16) < 8, -1.0, 1.0)[None, :]
    return jnp.asarray(np.cos(ang), F32), jnp.asarray(np.sin(ang) * sign, F32)


def _block_diag(w):
    nb, bi, bo = w.shape
    return (jnp.eye(nb, dtype=w.dtype)[:, None, :, None] * w[:, :, None, :]).reshape(nb * bi, nb * bo)


def _layer_params(l, w_in, ml_gate_b, na_qn_g, na_kn_g, df_qn_g, df_kn_g, rg_wa, rg_wx, rg_ba, rg_bx,
                  df_lq1, df_lk1, df_lq2, df_lk2, df_subln_g, w_out, router_w):
    wi = w_in[l]
    w_r = jnp.concatenate([wi[:, 0:1024], wi[:, 1040:3088], wi[:, 1024:1040],
                           jnp.zeros((D, PROJ_PAD - 3088), F32)], axis=1).astype(BF16)
    gate_b = jnp.pad(ml_gate_b[l], (0, 128 - 16)).reshape(1, 128)
    qkg = jnp.stack([jnp.tile(na_qn_g[l], NH), jnp.tile(na_kn_g[l], NH),
                     jnp.tile(df_qn_g[l], 2 * NH), jnp.tile(df_kn_g[l], 2 * NH)])
    wbd = jnp.concatenate([_block_diag(rg_wa[l, 0]), _block_diag(rg_wx[l, 0]),
                           _block_diag(rg_wa[l, 1]), _block_diag(rg_wx[l, 1])], axis=1).astype(BF16)
    rg_bias = jnp.concatenate([rg_ba[l, 0], rg_bx[l, 0], rg_ba[l, 1], rg_bx[l, 1]]).reshape(1, 4 * GW)
    lamp = jnp.stack([df_lq1[l], df_lk1[l], df_lq2[l], df_lk2[l]])
    sub = jnp.tile(df_subln_g[l], NH).reshape(1, GW)
    rw = jnp.pad(router_w[l], ((0, 0), (0, 128 - NE))).astype(BF16)
    return dict(w_r=w_r, gate_b=gate_b, qkg=qkg, wbd=wbd, rg_bias=rg_bias, lamp=lamp, sub=sub,
                wout=w_out[l].astype(BF16), rw=rw)


def kernel(x_prompt, x_sample, cache_na_k, cache_na_v, cache_df_k, cache_df_v, state_ml_c, state_ml_n, state_ml_m, state_rg_h, c, c_ctx, norm1_g, norm2_g, w_mod, b_mod, w_in, ml_gate_b, ml_norm_g, na_qn_g, na_kn_g, na_rpb, rg_conv_w, rg_conv_b, rg_wa, rg_ba, rg_wx, rg_bx, rg_lam, df_qn_g, df_kn_g, df_lq1, df_lk1, df_lq2, df_lk2, df_subln_g, w_out, router_w, moe_wg, moe_wu, moe_wd):
    nb, seq, _ = x_prompt.shape
    db, dseq, _ = x_sample.shape
    depth = w_in.shape[0]
    past = cache_na_k.shape[2]
    tm = 512
    cap_c = 2 * seq // NE
    cap_l = 2 * dseq // NE

    cv = jnp.concatenate([c_ctx[None, :], c, jnp.zeros((8 - 1 - db, D), F32)], axis=0)
    mod_all = _modulation(cv, w_mod, b_mod)
    rope = _rope_tables(dseq)

    xc = x_prompt.reshape(nb * seq, D)
    xl = x_sample.reshape(db * dseq, D)
    ctx_out = []
    caches = None
    for l in range(depth):
        lam_init = 0.8 - 0.6 * math.exp(-0.3 * l)
        p = _layer_params(l, w_in, ml_gate_b, na_qn_g, na_kn_g, df_qn_g, df_kn_g, rg_wa, rg_wx, rg_ba, rg_bx,
                          df_lq1, df_lk1, df_lq2, df_lk2, df_subln_g, w_out, router_w)
        mod = mod_all[l]
        g1 = norm1_g[l].reshape(1, D)
        g2 = norm2_g[l].reshape(1, D)
        mlg = ml_norm_g[l].reshape(1, GW)
        cb = rg_conv_b[l].reshape(1, GW)
        tb = _rpb_table(na_rpb[l], rows=dseq // GRID_W, rps=NA_RPS)

        nt_c = nb * seq // tm
        ml, gates, naq, rg, dfq, *caches = _proj_in(xc, mod, g1, p["w_r"], p["gate_b"], p["qkg"], caches=caches,
                                                    cache_shape=(nb, depth, seq, GW), layer=l,
                                                    row_base=0, row_div=nt_c, tm=tm)
        hf, hb, cn_c, m_c = _mlstm(ml, gates, nreq=nb, nc=seq // CHUNK)
        y_na, o_df = _ctx_attn(naq, dfq, caches, l, p["lamp"], lam_init, nreq=nb, n=seq)
        y_rg, rg_fin = _rglru(rg, rg_conv_w[l], cb, p["wbd"], p["rg_bias"], rg_lam[l],
                              jnp.zeros((nb, 2, GW), F32), nreq=nb, n=seq)
        x1_c, hn2_c, lg_c = _merge(hf, hb, ml, y_na, y_rg, o_df, xc, mod, mlg, p["sub"], p["wout"], g2, p["rw"],
                                   lam_init, row_base=0, row_div=nt_c, tm=tm)
        pos_c, aff_c = _route(lg_c, nreq=nb, n=seq, cap=cap_c, rb=8, token_major_copy=False)
        xs_c, w_c = _gather(pos_c, aff_c, hn2_c, nreq=nb, n=seq, cap=cap_c, eb=NE, rb=4)
        ctx_out.append((cn_c, m_c, rg_fin))

        tiles_req = dseq // tm
        ml, gates, na, rg, df = _proj_in(xl, mod, g1, p["w_r"], p["gate_b"], p["qkg"], rope=rope,
                                         row_base=1, row_div=tiles_req, tm=tm)
        cn0 = _mlstm_pack_state(state_ml_c[:, l], state_ml_n[:, l])
        m0 = jnp.broadcast_to(state_ml_m[:, l].reshape(db, 8, 1), (db, 8, 128))
        hf, hb, _, _ = _mlstm(ml, gates, (cn0, m0), nreq=db, nc=dseq // CHUNK)
        y_na = _na_lat(na, cache_na_k[:, l].reshape(db, past, GW), cache_na_v[:, l].reshape(db, past, GW), tb,
                       nreq=db, n=dseq, rps=NA_RPS)
        o_df = _df_lat(df, cache_df_k[:, l].reshape(db, past, GW), cache_df_v[:, l].reshape(db, past, GW),
                       p["lamp"], lam_init, nreq=db, n=dseq)
        y_rg, _ = _rglru(rg, rg_conv_w[l], cb, p["wbd"], p["rg_bias"], rg_lam[l], state_rg_h[:, l],
                         nreq=db, n=dseq)
        x1_l, hn2_l, lg_l = _merge(hf, hb, ml, y_na, y_rg, o_df, xl, mod, mlg, p["sub"], p["wout"], g2, p["rw"],
                                   lam_init, row_base=1, row_div=tiles_req, tm=tm)
        pos_l, aff_l, post_l = _route(lg_l, nreq=db, n=dseq, cap=cap_l, rb=1, token_major_copy=True)
        xs_l, w_l = _gather(pos_l, aff_l, hn2_l, nreq=db, n=dseq, cap=cap_l, eb=2)

        y_c, y_l = _experts(xs_c, xs_l, w_c, w_l, moe_wg, moe_wu, moe_wd, l)
        xc = _scatter(pos_c, y_c, x1_c, mod, nreq=nb, n=seq, cap=cap_c, eb=NE, row_base=0, row_mul=0, rb=4)
        xl = _scatter(post_l, y_l, x1_l, mod, nreq=db, n=dseq, cap=cap_l, eb=4, row_base=1, row_mul=1)

    y_prompt = xc.reshape(nb, seq, D)
    y_sample = xl.reshape(db, dseq, D)
    st = lambda f: jnp.stack([f(o) for o in ctx_out], axis=1)
    na_k = caches[0].reshape(nb, depth, seq, NH, HD)
    na_v = caches[1].reshape(nb, depth, seq, NH, HD)
    df_k = caches[2].reshape(nb, depth, seq, NH, 2, HD // 2)
    df_v = caches[3].reshape(nb, depth, seq, NH, HD)
    ml_c = st(lambda o: _mlstm_unpack_state(o[0])[0])
    ml_n = st(lambda o: _mlstm_unpack_state(o[0])[1])
    ml_m = st(lambda o: o[1][:, :, 0].reshape(nb, 2, NH))
    rg_h = st(lambda o: o[2])
    return (y_prompt, y_sample, na_k, na_v, df_k, df_v, ml_c, ml_n, ml_m, rg_h)
```

```python
import functools
import math

import numpy as np
import jax
import jax.numpy as jnp
from jax import lax
from jax.experimental import pallas as pl
from jax.experimental.pallas import tpu as pltpu

F32 = jnp.float32
BF16 = jnp.bfloat16

D = 1024
GW = 256
NH = 4
HD = 64
NE = 16
EPS = 1e-6
CHUNK = 256
GRID_W = 64
NA_ROWS = 8
NA_COLS = 16
SUB_ROWS = 256
NA_RPS = 4
RG_C = 8.0
ROPE_BASE = 10000.0
PROJ_PAD = 3200
VMEM_LIMIT_BYTES = 56 * 1024 * 1024
NEG_BIG = -1e30
LOG2E = 1.4426950408889634


def _bf(x):
    return x.astype(BF16)


def _dot(a, b):
    return jnp.dot(a, b, preferred_element_type=F32)


def _dot_nt(a, b):
    return lax.dot_general(a, b, (((1,), (1,)), ((), ())), preferred_element_type=F32)


def _dot_tn(a, b):
    return lax.dot_general(a, b, (((0,), (0,)), ((), ())), preferred_element_type=F32)


def _split3(x):
    p0 = _bf(x)
    r1 = x - p0.astype(F32)
    p1 = _bf(r1)
    return p0, p1, _bf(r1 - p1.astype(F32))


def _params(n_axes):
    return pltpu.CompilerParams(dimension_semantics=("arbitrary",) * n_axes,
                                vmem_limit_bytes=VMEM_LIMIT_BYTES)


def _full(shape):
    return pl.BlockSpec(shape, lambda *_: (0,) * len(shape))


def _sds(shape, dtype=F32):
    return jax.ShapeDtypeStruct(shape, dtype)


def _softplus(x):
    return jnp.maximum(x, 0.0) + jnp.log1p(jnp.exp(-jnp.abs(x)))


def _log_sigmoid(x):
    return jnp.minimum(x, 0.0) - jnp.log1p(jnp.exp(-jnp.abs(x)))


def _seg_rms(x, nseg, g_row):
    seg = x.shape[-1] // nseg
    lane = lax.broadcasted_iota(jnp.int32, x.shape, 1)
    x2 = x * x
    tot = jnp.zeros_like(x)
    for s in range(nseg):
        m = (lane >= s * seg) & (lane < (s + 1) * seg)
        t = jnp.sum(jnp.where(m, x2, 0.0), axis=-1, keepdims=True)
        tot = jnp.where(m, t, tot)
    return x * lax.rsqrt(tot * (1.0 / seg) + EPS) * g_row


def _mod_row(pid, row_base, row_div):
    return row_base + pid // row_div


def _mod_kernel(cv_ref, w_ref, b_ref, o_ref):
    cv = cv_ref[...]
    s = cv * jax.nn.sigmoid(cv)
    o_ref[0] = _dot(_bf(s), _bf(w_ref[0])) + b_ref[0]


def _modulation(cv, w_mod, b_mod):
    nl = w_mod.shape[0]
    tn = 1536
    return pl.pallas_call(
        _mod_kernel,
        grid=(nl, 6 * D // tn),
        in_specs=[_full((8, D)),
                  pl.BlockSpec((1, D, tn), lambda l, j: (l, 0, j)),
                  pl.BlockSpec((1, 1, tn), lambda l, j: (l, 0, j))],
        out_specs=pl.BlockSpec((1, 8, tn), lambda l, j: (l, 0, j)),
        out_shape=_sds((nl, 8, 6 * D)),
        compiler_params=_params(2),
        name="modulation",
    )(cv, w_mod, b_mod.reshape(nl, 1, 6 * D))


def _rope(x, cos_t, sin_t):
    lane = lax.broadcasted_iota(jnp.int32, x.shape, 1)
    first = (lane & 15) < 8
    sw = jnp.where(first, pltpu.roll(x, GW - 8, axis=1), pltpu.roll(x, 8, axis=1))
    return x * cos_t + sw * sin_t


def _proj_kernel(*refs, row_base, row_div, layer):
    ctx = layer is not None
    if ctx:
        (x_ref, mod_ref, g1_ref, w_ref, gb_ref, qkg_ref) = refs[:6]
        (ml_ref, gate_ref, naq_ref, rg_ref, dfq_ref, nk_ref, nv_ref, dk_ref, dv_ref) = refs[-9:]
        cache_layer = layer if nk_ref.shape[1] > 1 else 0
    else:
        (x_ref, mod_ref, g1_ref, w_ref, gb_ref, qkg_ref, cos_ref, sin_ref,
         ml_ref, gate_ref, na_ref, rg_ref, df_ref) = refs
    r = _mod_row(pl.program_id(0), row_base, row_div)
    sh = mod_ref[pl.ds(r, 1), 0:D]
    sc = mod_ref[pl.ds(r, 1), D:2 * D]
    for sub in range(x_ref.shape[0] // SUB_ROWS):
        rows = slice(sub * SUB_ROWS, (sub + 1) * SUB_ROWS)
        x = x_ref[rows, :]
        ms = jnp.mean(x * x, axis=-1, keepdims=True)
        y = x * lax.rsqrt(ms + EPS) * g1_ref[...]
        hn = _bf(y * (1.0 + sc) + sh)

        ml = _dot(hn, w_ref[:, 0:1024])
        ml_ref[rows, 0:256] = ml[:, 0:256]
        ml_ref[rows, 256:512] = ml[:, 256:512] * (HD ** -0.5)
        ml_ref[rows, 512:1024] = ml[:, 512:1024]

        gz = _dot(hn, w_ref[:, 3072:3200]) + gb_ref[...]
        lane = lax.broadcasted_iota(jnp.int32, gz.shape, 1)
        gate_ref[rows, :] = jnp.where(((lane >> 2) & 1) == 1, _log_sigmoid(gz), gz)

        rg_ref[rows, :] = _dot(hn, w_ref[:, 1792:2304])

        nz = _dot(hn, w_ref[:, 1024:1792])
        nq = _seg_rms(nz[:, 0:256], NH, qkg_ref[0:1, :])
        nk = _seg_rms(nz[:, 256:512], NH, qkg_ref[1:2, :])
        dz = _dot(hn, w_ref[:, 2304:3072])
        dq = _seg_rms(dz[:, 0:256], 2 * NH, qkg_ref[2:3, :])
        dk = _seg_rms(dz[:, 256:512], 2 * NH, qkg_ref[3:4, :])
        if ctx:
            naq_ref[rows, :] = nq
            dfq_ref[rows, :] = dq
            for c_ref, val in ((nk_ref, nk), (nv_ref, nz[:, 512:768]), (dk_ref, dk), (dv_ref, dz[:, 512:768])):
                c_ref[sub, cache_layer] = val
                for other in range(c_ref.shape[1]):
                    if other != cache_layer:
                        c_ref[sub, other] = jnp.zeros_like(val)
        else:
            cos_t = cos_ref[rows, :]
            sin_t = sin_ref[rows, :]
            na_ref[rows, 0:256] = nq
            na_ref[rows, 256:512] = nk
            na_ref[rows, 512:768] = nz[:, 512:768]
            df_ref[rows, 0:256] = _rope(dq, cos_t, sin_t)
            df_ref[rows, 256:512] = _rope(dk, cos_t, sin_t)
            df_ref[rows, 512:768] = dz[:, 512:768]


def _proj_in(x, mod, g1, w_r, gate_b, qkg, *, rope=None, caches=None, cache_shape=None, layer=None,
             row_base, row_div, tm=512):
    tp = x.shape[0]
    in_specs = [pl.BlockSpec((tm, D), lambda i: (i, 0)), _full((8, 6 * D)), _full((1, D)),
                _full((D, PROJ_PAD)), _full((1, 128)), _full((4, GW))]
    args = [x, mod, g1, w_r, gate_b, qkg]
    tok = lambda n: pl.BlockSpec((tm, n), lambda i: (i, 0))
    aliases = {}
    if layer is None:
        tiles = rope[0].shape[0] // tm
        in_specs += [pl.BlockSpec((tm, GW), lambda i: (i % tiles, 0))] * 2
        args += list(rope)
        out_specs = [tok(1024), tok(128), tok(768), tok(512), tok(768)]
        out_shape = [_sds((tp, n)) for n in (1024, 128, 768, 512, 768)]
    else:
        if caches is None:
            depth, seq = cache_shape[1], cache_shape[2]
            cspec = pl.BlockSpec((tm // seq, depth, seq, GW), lambda i: (i, 0, 0, 0))
        else:
            cache_shape, seq = caches[0].shape, caches[0].shape[2]
            cspec = pl.BlockSpec((tm // seq, 1, seq, GW), lambda i: (i, layer, 0, 0))
            in_specs += [pl.BlockSpec(memory_space=pl.ANY)] * 4
            args += list(caches)
            aliases = {6 + i: 5 + i for i in range(4)}
        assert seq == SUB_ROWS
        out_specs = [tok(1024), tok(128), tok(GW), tok(512), tok(GW)] + [cspec] * 4
        out_shape = [_sds((tp, n)) for n in (1024, 128, GW, 512, GW)] + [_sds(cache_shape)] * 4
    return pl.pallas_call(
        functools.partial(_proj_kernel, row_base=row_base, row_div=row_div, layer=layer),
        grid=(tp // tm,),
        in_specs=in_specs,
        out_specs=out_specs,
        out_shape=out_shape,
        input_output_aliases=aliases,
        compiler_params=_params(1),
        name="proj_in",
    )(*args)


def _mlstm_dir(d, q_ref, k_ref, v_ref, g_ref, h_ref, st_s, m_s):
    lc = CHUNK
    g = g_ref[...]
    gt = g.T
    row = lax.broadcasted_iota(jnp.int32, (lc, lc), 0)
    col = lax.broadcasted_iota(jnp.int32, (lc, lc), 1)
    tri = (row >= col) if d == 0 else (row <= col)
    tri_t = (col >= row) if d == 0 else (col <= row)
    g_parts = _split3(g)
    gt_parts = _split3(gt)
    tri_b = tri.astype(F32).astype(BF16)
    tri_tb = tri_t.astype(F32).astype(BF16)
    bc_col = _dot(tri_b, g_parts[0]) + _dot(tri_b, g_parts[1]) + _dot(tri_b, g_parts[2])
    bc_row = _dot(gt_parts[0], tri_tb) + _dot(gt_parts[1], tri_tb) + _dot(gt_parts[2], tri_tb)
    r_rows = gt[d * 8:d * 8 + NH, :] - bc_row[d * 8 + NH:d * 8 + 2 * NH, :]
    k_t = k_ref[...].T
    k_tb = _bf(k_t)
    q_all = _bf(q_ref[...])
    v_all = _bf(v_ref[...])
    ones = jnp.ones((lc, 128), BF16)
    last = lc - 1 if d == 0 else 0
    hs = [slice(h * HD, (h + 1) * HD) for h in range(NH)]
    ms = [m_s[d * NH + h:d * NH + h + 1, 0:1] for h in range(NH)]
    rms = [jnp.where(tri, r_rows[h:h + 1, :], -jnp.inf) for h in range(NH)]
    big_rs = [jnp.maximum(jnp.max(rms[h], axis=-1, keepdims=True), ms[h]) for h in range(NH)]
    ss = [_bf(_dot(q_all[:, hs[h]], k_tb[hs[h], :]) * jnp.exp(rms[h] - big_rs[h])) for h in range(NH)]
    v2s = [v_all[:, (h // 2) * 128:(h // 2 + 1) * 128] for h in range(NH)]
    es = [jnp.exp(ms[h] - big_rs[h]) for h in range(NH)]
    c_cols = [bc_col[:, d * 8 + NH + h:d * 8 + NH + h + 1] for h in range(NH)]
    tots = [_dot(ss[h], v2s[h]) + es[h] * _dot(q_all[:, hs[h]], _bf(st_s[d, h, 0])) for h in range(NH)]
    dens = [_dot(ss[h], ones) + es[h] * _dot(q_all[:, hs[h]], _bf(st_s[d, h, 1])) for h in range(NH)]
    outs = [tots[h] / jnp.maximum(jnp.abs(dens[h]), jnp.exp(-(c_cols[h] + big_rs[h]))) for h in range(NH)]
    for h in range(NH):
        r_last = big_rs[h][last:last + 1, :]
        kw = _bf(k_t[hs[h], :] * jnp.exp(r_rows[h:h + 1, :] - r_last))
        gdec = jnp.exp(ms[h] - r_last)
        st_s[d, h, 0] = gdec * st_s[d, h, 0] + _dot(kw, v2s[h])
        st_s[d, h, 1] = gdec * st_s[d, h, 1] + _dot(kw, ones)
        m_s[d * NH + h:d * NH + h + 1, :] = jnp.broadcast_to(c_cols[h][last:last + 1, :] + r_last, (1, 128))
    lane = lax.broadcasted_iota(jnp.int32, (lc, 128), 1)
    h_ref[...] = jnp.concatenate([jnp.where(lane < HD, outs[2 * p], outs[2 * p + 1]) for p in range(NH // 2)],
                                 axis=-1)


def _mlstm_kernel(qf, kf, vf, gf, qb, kb, vb, gb, *rest):
    hf_ref, hb_ref, st_out, m_out, st_s, m_s = rest[-6:]
    j = pl.program_id(1)

    @pl.when(j == 0)
    def _():
        if len(rest) == 8:
            st_s[...] = rest[0][0]
            m_s[...] = rest[1][0]
        else:
            st_s[...] = jnp.zeros_like(st_s)
            m_s[...] = jnp.zeros_like(m_s)

    _mlstm_dir(0, qf, kf, vf, gf, hf_ref, st_s, m_s)
    _mlstm_dir(1, qb, kb, vb, gb, hb_ref, st_s, m_s)

    @pl.when(j == pl.num_programs(1) - 1)
    def _():
        st_out[0] = st_s[...]
        m_out[0] = m_s[...]


def _mlstm_pack_state(c, n):
    z = jnp.zeros_like(c)
    odd = (jnp.arange(NH) % 2 == 1)[None, None, :, None, None]
    c_pair = jnp.where(odd, jnp.concatenate([z, c], axis=-1), jnp.concatenate([c, z], axis=-1))
    n_rep = jnp.broadcast_to(n[..., None], n.shape + (128,))
    return jnp.stack([c_pair, n_rep], axis=3)


def _mlstm_unpack_state(st):
    odd = (jnp.arange(NH) % 2 == 1)[None, None, :, None, None]
    c = jnp.where(odd, st[:, :, :, 0, :, HD:2 * HD], st[:, :, :, 0, :, 0:HD])
    return c, st[:, :, :, 1, :, 0]


def _mlstm(mlz, gates, state=None, *, nreq, nc):
    tp = mlz.shape[0]
    st_spec = pl.BlockSpec((1, 2, NH, 2, HD, 128), lambda r, j: (r, 0, 0, 0, 0, 0))

    def fwd(col):
        return lambda r, j: (r * nc + j, col)

    def bwd(col):
        return lambda r, j: (r * nc + nc - 1 - j, col)

    in_specs = []
    for mk in (fwd, bwd):
        in_specs += [pl.BlockSpec((CHUNK, GW), mk(0)), pl.BlockSpec((CHUNK, GW), mk(1)),
                     pl.BlockSpec((CHUNK, GW), mk(2)), pl.BlockSpec((CHUNK, 128), mk(0))]
    args = [mlz, mlz, mlz, gates, mlz, mlz, mlz, gates]
    if state is not None:
        in_specs += [st_spec, pl.BlockSpec((1, 8, 128), lambda r, j: (r, 0, 0))]
        args += list(state)
    return pl.pallas_call(
        _mlstm_kernel,
        grid=(nreq, nc),
        in_specs=in_specs,
        out_specs=[pl.BlockSpec((CHUNK, GW), fwd(0)), pl.BlockSpec((CHUNK, GW), bwd(0)),
                   st_spec, pl.BlockSpec((1, 8, 128), lambda r, j: (r, 0, 0))],
        out_shape=[_sds((tp, GW)), _sds((tp, GW)), _sds((nreq, 2, NH, 2, HD, 128)), _sds((nreq, 8, 128))],
        scratch_shapes=[pltpu.VMEM((2, NH, 2, HD, 128), F32), pltpu.VMEM((8, 128), F32)],
        compiler_params=_params(2),
        name="mlstm",
    )(*args)


def _diff_lambda(lam_ref, lam_init):
    lp = lam_ref[...]
    a = jnp.exp(jnp.sum(lp[0:1, :] * lp[1:2, :], axis=-1, keepdims=True))
    b = jnp.exp(jnp.sum(lp[2:3, :] * lp[3:4, :], axis=-1, keepdims=True))
    return a - b + lam_init


def _softmax_pv(jobs):
    ss = [_dot(q, k_t) for q, k_t, _ in jobs]
    es = [jnp.exp2(s - jnp.max(s, axis=-1, keepdims=True)) for s in ss]
    return [_dot(_bf(e), v) / jnp.sum(e, axis=-1, keepdims=True) for e, (_, _, v) in zip(es, jobs)]


def _ctx_attn_kernel(nq, nk, nv, dq, dk, dv, lam_ref, ona_ref, odf_ref, *, lam_init):
    lam = _diff_lambda(lam_ref, lam_init)
    n = nk.shape[2]
    for i in range(nk.shape[0]):
        rows = slice(i * n, (i + 1) * n)
        nk_t = _bf(nk[i, 0].T)
        dk_t = _bf(dk[i, 0].T)
        jobs = []
        for h in range(NH):
            hs = slice(h * HD, (h + 1) * HD)
            jobs.append((_bf(nq[rows, hs] * (HD ** -0.5 * LOG2E)), nk_t[hs, :], _bf(nv[i, 0, :, hs])))
        for h in range(NH):
            vh = _bf(dv[i, 0, :, h * HD:(h + 1) * HD])
            for c in range(2):
                cs = slice(h * HD + c * 32, h * HD + (c + 1) * 32)
                jobs.append((_bf(dq[rows, cs] * (32 ** -0.5 * LOG2E)), dk_t[cs, :], vh))
        outs = _softmax_pv(jobs)
        ona_ref[rows, :] = _bf(jnp.concatenate(outs[0:NH], axis=-1))
        odf_ref[rows, :] = jnp.concatenate([outs[NH + 2 * h] - lam * outs[NH + 2 * h + 1] for h in range(NH)],
                                           axis=-1)


def _ctx_attn(naq, dfq, caches, layer, lamp, lam_init, *, nreq, n, rb=1):
    tp = naq.shape[0]
    qs = pl.BlockSpec((rb * n, GW), lambda b: (b, 0))
    cs = pl.BlockSpec((rb, 1, n, GW), lambda b: (b, layer, 0, 0))
    return pl.pallas_call(
        functools.partial(_ctx_attn_kernel, lam_init=lam_init),
        grid=(nreq // rb,),
        in_specs=[qs, cs, cs, qs, cs, cs, _full((4, 32))],
        out_specs=[qs, qs],
        out_shape=[_sds((tp, GW), BF16), _sds((tp, GW))],
        compiler_params=_params(1),
        name="ctx_attn",
    )(naq, caches[0], caches[1], dfq, caches[2], caches[3], lamp)


def _rpb_kernel(rpb_ref, o_ref, *, rows, rps):
    h = pl.program_id(0)
    shape = (GRID_W, NA_ROWS * GRID_W)
    lane = lax.broadcasted_iota(jnp.int32, shape, 1)
    cq = lax.broadcasted_iota(jnp.int32, shape, 0)
    ck = lane & (GRID_W - 1)
    c0 = jnp.clip(cq - NA_COLS // 2, 0, GRID_W - NA_COLS)
    ok = (ck >= c0) & (ck < c0 + NA_COLS)
    ncol = 2 * NA_COLS - 1
    ndr = 2 * NA_ROWS - 1
    x = lax.broadcasted_iota(jnp.int32, (1, 2 * GRID_W), 1)
    didx = jnp.clip(x - GRID_W, -(NA_COLS - 1), NA_COLS - 1) + (NA_COLS - 1)
    tiles = []
    for dr in range(ndr):
        frow = jnp.zeros((1, 2 * GRID_W), F32)
        for b in range(ncol):
            frow = jnp.where(didx == b, rpb_ref[(h * ndr + dr) * ncol + b], frow)
        rolled = pltpu.roll(jnp.broadcast_to(frow, (GRID_W, 2 * GRID_W)), GRID_W, axis=1, stride=1, stride_axis=0)
        tiles.append(rolled[:, 0:GRID_W])
    tbs = [jnp.where(ok, jnp.concatenate(tiles[s:s + NA_ROWS], axis=-1) * LOG2E, NEG_BIG) for s in range(NA_ROWS)]
    wrows = NA_ROWS + rps
    steps = rows // rps
    for t, j in enumerate((0, 1, steps - 1)):
        w0 = min(max(j * rps - NA_ROWS // 2, 0), rows - wrows)
        for a in range(rps):
            r = j * rps + a
            r0 = min(max(r - NA_ROWS // 2, 0), rows - NA_ROWS)
            off = (r0 - w0) * GRID_W
            rest = rps * GRID_W - off
            pieces = [tbs[r0 - r + NA_ROWS - 1]]
            if off:
                pieces = [jnp.full((GRID_W, off), NEG_BIG, F32)] + pieces
            if rest:
                pieces = pieces + [jnp.full((GRID_W, rest), NEG_BIG, F32)]
            o_ref[0, t, a * GRID_W:(a + 1) * GRID_W, :] = jnp.concatenate(pieces, axis=-1)


def _na_step_offsets(j, rows, rps):
    w0 = min(max(j * rps - NA_ROWS // 2, 0), rows - NA_ROWS - rps)
    return [(min(max(r - NA_ROWS // 2, 0), rows - NA_ROWS) - r,
             min(max(r - NA_ROWS // 2, 0), rows - NA_ROWS) - w0) for r in range(j * rps, (j + 1) * rps)]


def _rpb_table(rpb, *, rows, rps):
    steps = rows // rps
    assert all(_na_step_offsets(j, rows, rps) == _na_step_offsets(1, rows, rps) for j in range(1, steps - 1))
    shape = (rps * GRID_W, (NA_ROWS + rps) * GRID_W)
    return pl.pallas_call(
        functools.partial(_rpb_kernel, rows=rows, rps=rps),
        grid=(NH,),
        in_specs=[pl.BlockSpec(memory_space=pltpu.SMEM)],
        out_specs=pl.BlockSpec((1, 3) + shape, lambda h: (h, 0, 0, 0)),
        out_shape=_sds((NH, 3) + shape),
        compiler_params=_params(1),
        name="rpb_table",
    )(rpb.reshape(-1))


def _na_lat_kernel(q_ref, k_ref, v_ref, kc_ref, vc_ref, tb_ref, o_ref, k_s, v_s, kct_s, vc_s, *, rows, rps):
    j = pl.program_id(1)

    @pl.when(j == 0)
    def _():
        k_s[...] = _bf(k_ref[...])
        v_s[...] = _bf(v_ref[...])
        kct_s[...] = _bf(kc_ref[0].T)
        vc_s[...] = _bf(vc_ref[0])

    wrows = NA_ROWS + rps
    win = wrows * GRID_W
    w0 = jnp.clip(j * rps - NA_ROWS // 2, 0, rows - wrows)
    start = pl.multiple_of(w0 * GRID_W, GRID_W)
    hss = [slice(h * HD, (h + 1) * HD) for h in range(NH)]
    qs = [_bf(q_ref[:, hs] * (HD ** -0.5 * LOG2E)) for hs in hss]
    s_cs = [_dot(q, kct_s[hs, :]) for q, hs in zip(qs, hss)]
    s_ls = [_dot_nt(q, k_s[pl.ds(start, win), hs]) + tb_ref[h, 0] for h, (q, hs) in enumerate(zip(qs, hss))]
    ms = [jnp.maximum(jnp.max(s_c, axis=-1, keepdims=True), jnp.max(s_l, axis=-1, keepdims=True))
          for s_c, s_l in zip(s_cs, s_ls)]
    e_cs = [jnp.exp2(s_c - m) for s_c, m in zip(s_cs, ms)]
    e_ls = [jnp.exp2(s_l - m) for s_l, m in zip(s_ls, ms)]
    outs = []
    for e_c, e_l, hs in zip(e_cs, e_ls, hss):
        den = jnp.sum(e_c, axis=-1, keepdims=True) + jnp.sum(e_l, axis=-1, keepdims=True)
        o = _dot(_bf(e_c), vc_s[:, hs]) + _dot(_bf(e_l), v_s[pl.ds(start, win), hs])
        outs.append(o / den)
    o_ref[...] = _bf(jnp.concatenate(outs, axis=-1))


def _na_lat(na, kc, vc, tb, *, nreq, n, rps):
    tp = na.shape[0]
    rows = n // GRID_W
    past = kc.shape[1]
    steps = rows // rps
    tq = rps * GRID_W
    return pl.pallas_call(
        functools.partial(_na_lat_kernel, rows=rows, rps=rps),
        grid=(nreq, steps),
        in_specs=[pl.BlockSpec((tq, GW), lambda b, r: (b * steps + r, 0)),
                  pl.BlockSpec((n, GW), lambda b, r: (b, 1)),
                  pl.BlockSpec((n, GW), lambda b, r: (b, 2)),
                  pl.BlockSpec((1, past, GW), lambda b, r: (b, 0, 0)),
                  pl.BlockSpec((1, past, GW), lambda b, r: (b, 0, 0)),
                  pl.BlockSpec((NH, 1) + tb.shape[2:],
                               lambda b, r: (0, jnp.minimum(r, 1) + r // (steps - 1), 0, 0))],
        out_specs=pl.BlockSpec((tq, GW), lambda b, r: (b * steps + r, 0)),
        out_shape=_sds((tp, GW), BF16),
        scratch_shapes=[pltpu.VMEM((n, GW), BF16), pltpu.VMEM((n, GW), BF16),
                        pltpu.VMEM((GW, past), BF16), pltpu.VMEM((past, GW), BF16)],
        compiler_params=_params(2),
        name="na_latent",
    )(na, na, na, kc, vc, tb)


def _df_lat_kernel(q_ref, k_ref, v_ref, kc_ref, vc_ref, lam_ref, o_ref, kt_s, v_s, kct_s, vc_s, *, lam_init):
    @pl.when(pl.program_id(1) == 0)
    def _():
        kt_s[...] = _bf(k_ref[...].T)
        v_s[...] = _bf(v_ref[...])
        kct_s[...] = _bf(kc_ref[0].T)
        vc_s[...] = _bf(vc_ref[0])

    lam = _diff_lambda(lam_ref, lam_init)
    outs = []
    for h in range(NH):
        hs = slice(h * HD, (h + 1) * HD)
        css = [slice(h * HD + c * 32, h * HD + (c + 1) * 32) for c in range(2)]
        qs = [_bf(q_ref[:, cs] * (32 ** -0.5 * LOG2E)) for cs in css]
        s_cs = [_dot(q, kct_s[cs, :]) for q, cs in zip(qs, css)]
        s_ls = [_dot(q, kt_s[cs, :]) for q, cs in zip(qs, css)]
        ms = [jnp.maximum(jnp.max(s_c, axis=-1, keepdims=True), jnp.max(s_l, axis=-1, keepdims=True))
              for s_c, s_l in zip(s_cs, s_ls)]
        e_cs = [jnp.exp2(s_c - m) for s_c, m in zip(s_cs, ms)]
        e_ls = [jnp.exp2(s_l - m) for s_l, m in zip(s_ls, ms)]
        os = []
        for e_c, e_l in zip(e_cs, e_ls):
            den = jnp.sum(e_c, axis=-1, keepdims=True) + jnp.sum(e_l, axis=-1, keepdims=True)
            os.append((_dot(_bf(e_c), vc_s[:, hs]) + _dot(_bf(e_l), v_s[:, hs])) / den)
        outs.append(os[0] - lam * os[1])
    o_ref[...] = jnp.concatenate(outs, axis=-1)


def _df_lat(df, kc, vc, lamp, lam_init, *, nreq, n, tq=512):
    tp = df.shape[0]
    nq = n // tq
    past = kc.shape[1]
    return pl.pallas_call(
        functools.partial(_df_lat_kernel, lam_init=lam_init),
        grid=(nreq, nq),
        in_specs=[pl.BlockSpec((tq, GW), lambda b, j: (b * nq + j, 0)),
                  pl.BlockSpec((n, GW), lambda b, j: (b, 1)),
                  pl.BlockSpec((n, GW), lambda b, j: (b, 2)),
                  pl.BlockSpec((1, past, GW), lambda b, j: (b, 0, 0)),
                  pl.BlockSpec((1, past, GW), lambda b, j: (b, 0, 0)),
                  _full((4, 32))],
        out_specs=pl.BlockSpec((tq, GW), lambda b, j: (b * nq + j, 0)),
        out_shape=_sds((tp, GW)),
        scratch_shapes=[pltpu.VMEM((GW, n), BF16), pltpu.VMEM((n, GW), BF16),
                        pltpu.VMEM((GW, past), BF16), pltpu.VMEM((past, GW), BF16)],
        compiler_params=_params(2),
        name="df_latent",
    )(df, df, df, kc, vc, lamp)


def _rg_kernel(x_ref, g_ref, cw_ref, cb_ref, wbd_ref, bias_ref, lam_ref, h0_ref,
               y_ref, fin_ref, a_s, b_s, *, n):
    x = x_ref[...]
    row = lax.broadcasted_iota(jnp.int32, x.shape, 0)
    xc = cb_ref[...] + jnp.where(row >= 2, pltpu.roll(x, 2, axis=0), 0.0) * cw_ref[0:1, :]
    xc = xc + jnp.where(row >= 1, pltpu.roll(x, 1, axis=0), 0.0) * cw_ref[1:2, :]
    xc = xc + x * cw_ref[2:3, :]
    xc = xc + jnp.where(row < n - 1, pltpu.roll(x, n - 1, axis=0), 0.0) * cw_ref[3:4, :]
    z = _dot(_bf(xc), wbd_ref[...]) + bias_ref[...]
    sub = row & 7
    for d in range(2):
        rgate = jax.nn.sigmoid(z[:, 512 * d:512 * d + GW])
        igate = jax.nn.sigmoid(z[:, 512 * d + GW:512 * d + 2 * GW])
        la = -RG_C * rgate * _softplus(-lam_ref[d:d + 1, :])
        a = jnp.exp(la)
        t = jnp.tanh(la)
        b = jnp.sqrt(-2.0 * t / (1.0 - t)) * igate * xc
        for dd in (1, 2, 4):
            if d == 0:
                keep = sub >= dd
                a_sh = jnp.where(keep, pltpu.roll(a, dd, axis=0), 1.0)
                b_sh = jnp.where(keep, pltpu.roll(b, dd, axis=0), 0.0)
            else:
                keep = sub < 8 - dd
                a_sh = jnp.where(keep, pltpu.roll(a, n - dd, axis=0), 1.0)
                b_sh = jnp.where(keep, pltpu.roll(b, n - dd, axis=0), 0.0)
            b = b + a * b_sh
            a = a * a_sh
        a_s[d] = a
        b_s[d] = b

    nt = n // 8

    def body(t, carry):
        hf, hb = carry
        sf = pl.multiple_of(t * 8, 8)
        sb = pl.multiple_of((nt - 1 - t) * 8, 8)
        tf = a_s[0, pl.ds(sf, 8), :] * hf + b_s[0, pl.ds(sf, 8), :]
        tb = a_s[1, pl.ds(sb, 8), :] * hb + b_s[1, pl.ds(sb, 8), :]
        b_s[0, pl.ds(sf, 8), :] = tf
        b_s[1, pl.ds(sb, 8), :] = tb
        return tf[7:8, :], tb[0:1, :]

    hf, hb = lax.fori_loop(0, nt, body, (h0_ref[0, 0:1, :], h0_ref[0, 1:2, :]))
    fin_ref[0, 0:1, :] = hf
    fin_ref[0, 1:2, :] = hb
    gg = g_ref[...]
    cdf = 0.5 * (1.0 + jnp.tanh(math.sqrt(2.0 / math.pi) * (gg + 0.044715 * (gg * gg * gg))))
    y_ref[...] = _bf((b_s[0] + b_s[1]) * (gg * cdf))


def _rglru(rg, cw, cb, wbd, bias, lam, h0, *, nreq, n):
    tp = rg.shape[0]
    return pl.pallas_call(
        functools.partial(_rg_kernel, n=n),
        grid=(nreq,),
        in_specs=[pl.BlockSpec((n, GW), lambda b: (b, 0)), pl.BlockSpec((n, GW), lambda b: (b, 1)),
                  _full((4, GW)), _full((1, GW)), _full((GW, 4 * GW)), _full((1, 4 * GW)), _full((2, GW)),
                  pl.BlockSpec((1, 2, GW), lambda b: (b, 0, 0))],
        out_specs=[pl.BlockSpec((n, GW), lambda b: (b, 0)), pl.BlockSpec((1, 2, GW), lambda b: (b, 0, 0))],
        out_shape=[_sds((tp, GW), BF16), _sds((nreq, 2, GW))],
        scratch_shapes=[pltpu.VMEM((2, n, GW), F32), pltpu.VMEM((2, n, GW), F32)],
        compiler_params=_params(1),
        name="rglru",
    )(rg, rg, cw, cb, wbd, bias, lam, h0)


def _merge_kernel(hf_ref, hb_ref, mlo_ref, yna_ref, yrg_ref, odf_ref, x_ref, mod_ref, mlg_ref, sub_ref,
                  wout_ref, n2_ref, rw_ref, x1_ref, hn2_ref, lg_ref, *, row_base, row_div, lam_init):
    r = _mod_row(pl.program_id(0), row_base, row_div)
    y_ml = _seg_rms(hf_ref[...] + hb_ref[...], NH, mlg_ref[...]) * jax.nn.sigmoid(mlo_ref[...])
    y_df = _seg_rms(odf_ref[...], NH, sub_ref[...]) * (1.0 - lam_init)
    y = jnp.concatenate([_bf(y_ml), _bf(yna_ref[...]), _bf(yrg_ref[...]), _bf(y_df)], axis=-1)
    o = _dot(y, wout_ref[...])
    x1 = x_ref[...] + mod_ref[pl.ds(r, 1), 2 * D:3 * D] * o
    x1_ref[...] = x1
    ms = jnp.mean(x1 * x1, axis=-1, keepdims=True)
    hn = x1 * lax.rsqrt(ms + EPS) * n2_ref[...]
    hn = _bf(hn * (1.0 + mod_ref[pl.ds(r, 1), 4 * D:5 * D]) + mod_ref[pl.ds(r, 1), 3 * D:4 * D])
    hn2_ref[...] = hn
    lg_ref[...] = _dot(hn, rw_ref[...])


def _merge(hf, hb, mlz, yna, yrg, odf, x, mod, mlg, sub, wout, n2, rw, lam_init, *, row_base, row_div, tm=512):
    tp = x.shape[0]
    g = lambda c: pl.BlockSpec((tm, GW), lambda i: (i, c))
    return pl.pallas_call(
        functools.partial(_merge_kernel, row_base=row_base, row_div=row_div, lam_init=lam_init),
        grid=(tp // tm,),
        in_specs=[g(0), g(0), g(3), g(0), g(0), g(0),
                  pl.BlockSpec((tm, D), lambda i: (i, 0)), _full((8, 6 * D)), _full((1, GW)), _full((1, GW)),
                  _full((D, D)), _full((1, D)), _full((D, 128))],
        out_specs=[pl.BlockSpec((tm, D), lambda i: (i, 0)), pl.BlockSpec((tm, D), lambda i: (i, 0)),
                   pl.BlockSpec((tm, 128), lambda i: (i, 0))],
        out_shape=[_sds((tp, D)), _sds((tp, D), BF16), _sds((tp, 128))],
        compiler_params=_params(1),
        name="merge",
    )(hf, hb, mlz, yna, yrg, odf, x, mod, mlg, sub, wout, n2, rw)


def _excl_cumsum_lanes(mask):
    blk = 256
    r = lax.broadcasted_iota(jnp.int32, (blk, blk), 0)
    c = lax.broadcasted_iota(jnp.int32, (blk, blk), 1)
    tri = (r < c).astype(F32).astype(BF16)
    off = jnp.zeros((mask.shape[0], 1), F32)
    outs = []
    for i in range(mask.shape[1] // blk):
        mb = mask[:, i * blk:(i + 1) * blk]
        outs.append(_dot(_bf(mb), tri) + off)
        off = off + jnp.sum(mb, axis=-1, keepdims=True)
    return jnp.concatenate(outs, axis=-1)


def _route_kernel(lg_ref, pos_ref, aff_ref, *post_ref, cap, n, rb):
    lg = lg_ref[...].T[0:NE, :]
    ex = jnp.exp(lg - jnp.max(lg, axis=0, keepdims=True))
    aff = ex / jnp.sum(ex, axis=0, keepdims=True)
    aff_ref[...] = aff
    aff = jnp.concatenate([aff[:, i * n:(i + 1) * n] for i in range(rb)], axis=0)
    thr = jnp.zeros((rb * NE, 1), jnp.int32)
    for bit in range(30, -1, -1):
        cand = thr | (1 << bit)
        cnt = jnp.sum((aff >= pltpu.bitcast(cand, F32)).astype(jnp.int32), axis=-1, keepdims=True)
        thr = jnp.where(cnt >= cap, cand, thr)
    thr_f = pltpu.bitcast(thr, F32)
    gt = aff > thr_f
    eq = aff == thr_f
    need = (cap - jnp.sum(gt.astype(jnp.int32), axis=-1, keepdims=True)).astype(F32)
    eq_rank = _excl_cumsum_lanes(eq.astype(F32))
    sel = gt | (eq & (eq_rank < need))
    slot = _excl_cumsum_lanes(sel.astype(F32))
    pos = jnp.where(sel, slot.astype(jnp.int32), -1)
    pos = jnp.concatenate([pos[i * NE:(i + 1) * NE, :] for i in range(rb)], axis=1)
    pos_ref[...] = pos
    if post_ref:
        post_ref[0][...] = jnp.concatenate([pos, jnp.full((128 - NE, rb * n), -1, jnp.int32)], axis=0).T


def _route(lg, *, nreq, n, cap, rb, token_major_copy):
    tp = lg.shape[0]
    espec = pl.BlockSpec((NE, rb * n), lambda b: (0, b))
    out_specs = [espec, espec]
    out_shape = [_sds((NE, tp), jnp.int32), _sds((NE, tp))]
    if token_major_copy:
        out_specs.append(pl.BlockSpec((rb * n, 128), lambda b: (b, 0)))
        out_shape.append(_sds((tp, 128), jnp.int32))
    return pl.pallas_call(
        functools.partial(_route_kernel, cap=cap, n=n, rb=rb),
        grid=(nreq // rb,),
        in_specs=[pl.BlockSpec((rb * n, 128), lambda b: (b, 0))],
        out_specs=out_specs,
        out_shape=out_shape,
        compiler_params=_params(1),
        name="route",
    )(lg)


def _gather_kernel(pos_ref, aff_ref, h_ref, xs_ref, w_ref, *, eb, cap, n, rb):
    eg = pl.program_id(1)
    io = lax.broadcasted_iota(jnp.int32, (cap, n), 0)
    for i in range(rb):
        toks = slice(i * n, (i + 1) * n)
        slots = slice(i * cap, (i + 1) * cap)
        sels = []
        for k in range(eb):
            e = eg * eb + k
            sel = pos_ref[pl.ds(e, 1), toks] == io
            sels.append(sel.astype(F32).astype(BF16))
            w = jnp.sum(jnp.where(sel, aff_ref[pl.ds(e, 1), toks], 0.0), axis=-1, keepdims=True)
            w_ref[k, slots, :] = jnp.broadcast_to(w, (cap, 128))
        xs = _dot(jnp.concatenate(sels, axis=0), h_ref[toks, :]).astype(BF16)
        for k in range(eb):
            xs_ref[k, slots, :] = xs[k * cap:(k + 1) * cap, :]


def _gather(pos, aff, hn2, *, nreq, n, cap, eb, rb=1):
    return pl.pallas_call(
        functools.partial(_gather_kernel, eb=eb, cap=cap, n=n, rb=rb),
        grid=(nreq // rb, NE // eb),
        in_specs=[pl.BlockSpec((NE, rb * n), lambda b, g: (0, b)), pl.BlockSpec((NE, rb * n), lambda b, g: (0, b)),
                  pl.BlockSpec((rb * n, D), lambda b, g: (b, 0))],
        out_specs=[pl.BlockSpec((eb, rb * cap, D), lambda b, g: (g, b, 0)),
                   pl.BlockSpec((eb, rb * cap, 128), lambda b, g: (g, b, 0))],
        out_shape=[_sds((NE, nreq * cap, D), BF16), _sds((NE, nreq * cap, 128))],
        compiler_params=_params(2),
        name="moe_gather",
    )(pos, aff, hn2)


def _expert_kernel(xc_ref, xl_ref, wc_ref, wl_ref, wg_ref, wu_ref, wd_ref, yc_ref, yl_ref, *, rows):
    wg = _bf(wg_ref[0, 0])
    wu = _bf(wu_ref[0, 0])
    wd = _bf(wd_ref[0, 0])
    tm = 512
    for x_ref, w_ref, y_ref in ((xc_ref, wc_ref, yc_ref), (xl_ref, wl_ref, yl_ref)):
        for ch in range(rows // tm):
            sl = slice(ch * tm, (ch + 1) * tm)
            x = x_ref[0, sl, :]
            g = _dot(x, wg)
            u = _dot(x, wu)
            a = _bf(g * jax.nn.sigmoid(g) * u)
            y_ref[0, sl, :] = _bf(_dot(a, wd) * w_ref[0, sl, 0:1])


def _experts(xs_c, xs_l, w_c, w_l, wg, wu, wd, layer):
    rows = xs_c.shape[1]
    dff = wg.shape[-1]
    xspec = pl.BlockSpec((1, rows, D), lambda e: (e, 0, 0))
    wspec = pl.BlockSpec((1, rows, 128), lambda e: (e, 0, 0))
    return pl.pallas_call(
        functools.partial(_expert_kernel, rows=rows),
        grid=(NE,),
        in_specs=[xspec, xspec, wspec, wspec,
                  pl.BlockSpec((1, 1, D, dff), lambda e: (layer, e, 0, 0)),
                  pl.BlockSpec((1, 1, D, dff), lambda e: (layer, e, 0, 0)),
                  pl.BlockSpec((1, 1, dff, D), lambda e: (layer, e, 0, 0))],
        out_specs=[xspec, xspec],
        out_shape=[_sds((NE, rows, D), BF16), _sds((NE, rows, D), BF16)],
        compiler_params=_params(1),
        name="moe_experts",
    )(xs_c, xs_l, w_c, w_l, wg, wu, wd)


def _scatter_kernel(idx_ref, y_ref, x1_ref, mod_ref, o_ref, *, eb, cap, n, rb, row_base, row_mul, token_major):
    eg = pl.program_id(1)
    last = pl.num_programs(1) - 1
    r = row_base + row_mul * pl.program_id(0)
    g2 = mod_ref[pl.ds(r, 1), 5 * D:6 * D]
    tn = min(n, 512)
    for i in range(rb):
        ys = jnp.concatenate([y_ref[k, i * cap:(i + 1) * cap, :] for k in range(eb)], axis=0)
        for t in range(n // tn):
            rows = slice(i * n + t * tn, i * n + (t + 1) * tn)
            if token_major:
                io = lax.broadcasted_iota(jnp.int32, (tn, cap), 1)
                shift = (128 - eg * eb) & 127
                pt = pltpu.roll(idx_ref[rows, :], shift, axis=1)
                sels = [(pt[:, k:k + 1] == io).astype(F32).astype(BF16) for k in range(eb)]
                part = _dot(jnp.concatenate(sels, axis=1), ys)
            else:
                io = lax.broadcasted_iota(jnp.int32, (cap, tn), 0)
                sels = [(idx_ref[pl.ds(eg * eb + k, 1), rows] == io).astype(F32).astype(BF16) for k in range(eb)]
                part = _dot_tn(jnp.concatenate(sels, axis=0), ys)
            o_ref[rows, :] = jnp.where(eg == 0, part, o_ref[rows, :] + part)

    @pl.when(eg == last)
    def _():
        o_ref[...] = x1_ref[...] + g2 * o_ref[...]


def _scatter(idx, y, x1, mod, *, nreq, n, cap, eb, row_base, row_mul, rb=1):
    tp = x1.shape[0]
    assert rb == 1 or row_mul == 0
    token_major = idx.shape[0] == tp
    idx_spec = (pl.BlockSpec((rb * n, 128), lambda b, g: (b, 0)) if token_major
                else pl.BlockSpec((NE, rb * n), lambda b, g: (0, b)))
    return pl.pallas_call(
        functools.partial(_scatter_kernel, eb=eb, cap=cap, n=n, rb=rb, row_base=row_base, row_mul=row_mul,
                          token_major=token_major),
        grid=(nreq // rb, NE // eb),
        in_specs=[idx_spec,
                  pl.BlockSpec((eb, rb * cap, D), lambda b, g: (g, b, 0)),
                  pl.BlockSpec((rb * n, D), lambda b, g: (b, 0)), _full((8, 6 * D))],
        out_specs=pl.BlockSpec((rb * n, D), lambda b, g: (b, 0)),
        out_shape=_sds((tp, D)),
        compiler_params=_params(2),
        name="moe_scatter",
    )(idx, y, x1, mod)


def _rope_tables(n):
    nf = 8
    t = np.arange(n)
    rowp = (t // GRID_W).astype(np.float32)
    colp = (t % GRID_W).astype(np.float32)
    inv = (np.float32(ROPE_BASE) ** (-np.arange(nf, dtype=np.float32) / np.float32(nf))).astype(np.float32)
    lane = np.arange(GW)
    c32 = lane % 32
    pos = np.where((c32 < 16)[None, :], rowp[:, None], colp[:, None]).astype(np.float32)
    ang = (pos * inv[(c32 % 8)][None, :]).astype(np.float32).astype(np.float64)
    sign = np.where((lane % 16) < 8, -1.0, 1.0)[None, :]
    return jnp.asarray(np.cos(ang), F32), jnp.asarray(np.sin(ang) * sign, F32)


def _block_diag(w):
    nb, bi, bo = w.shape
    return (jnp.eye(nb, dtype=w.dtype)[:, None, :, None] * w[:, :, None, :]).reshape(nb * bi, nb * bo)


def _layer_params(l, w_in, ml_gate_b, na_qn_g, na_kn_g, df_qn_g, df_kn_g, rg_wa, rg_wx, rg_ba, rg_bx,
                  df_lq1, df_lk1, df_lq2, df_lk2, df_subln_g, w_out, router_w):
    wi = w_in[l]
    w_r = jnp.concatenate([wi[:, 0:1024], wi[:, 1040:3088], wi[:, 1024:1040],
                           jnp.zeros((D, PROJ_PAD - 3088), F32)], axis=1).astype(BF16)
    gate_b = jnp.pad(ml_gate_b[l], (0, 128 - 16)).reshape(1, 128)
    qkg = jnp.stack([jnp.tile(na_qn_g[l], NH), jnp.tile(na_kn_g[l], NH),
                     jnp.tile(df_qn_g[l], 2 * NH), jnp.tile(df_kn_g[l], 2 * NH)])
    wbd = jnp.concatenate([_block_diag(rg_wa[l, 0]), _block_diag(rg_wx[l, 0]),
                           _block_diag(rg_wa[l, 1]), _block_diag(rg_wx[l, 1])], axis=1).astype(BF16)
    rg_bias = jnp.concatenate([rg_ba[l, 0], rg_bx[l, 0], rg_ba[l, 1], rg_bx[l, 1]]).reshape(1, 4 * GW)
    lamp = jnp.stack([df_lq1[l], df_lk1[l], df_lq2[l], df_lk2[l]])
    sub = jnp.tile(df_subln_g[l], NH).reshape(1, GW)
    rw = jnp.pad(router_w[l], ((0, 0), (0, 128 - NE))).astype(BF16)
    return dict(w_r=w_r, gate_b=gate_b, qkg=qkg, wbd=wbd, rg_bias=rg_bias, lamp=lamp, sub=sub,
                wout=w_out[l].astype(BF16), rw=rw)


def kernel(x_prompt, x_sample, cache_na_k, cache_na_v, cache_df_k, cache_df_v, state_ml_c, state_ml_n, state_ml_m, state_rg_h, c, c_ctx, norm1_g, norm2_g, w_mod, b_mod, w_in, ml_gate_b, ml_norm_g, na_qn_g, na_kn_g, na_rpb, rg_conv_w, rg_conv_b, rg_wa, rg_ba, rg_wx, rg_bx, rg_lam, df_qn_g, df_kn_g, df_lq1, df_lk1, df_lq2, df_lk2, df_subln_g, w_out, router_w, moe_wg, moe_wu, moe_wd):
    nb, seq, _ = x_prompt.shape
    db, dseq, _ = x_sample.shape
    depth = w_in.shape[0]
    past = cache_na_k.shape[2]
    tm = 512
    cap_c = 2 * seq // NE
    cap_l = 2 * dseq // NE

    cv = jnp.concatenate([c_ctx[None, :], c, jnp.zeros((8 - 1 - db, D), F32)], axis=0)
    mod_all = _modulation(cv, w_mod, b_mod)
    rope = _rope_tables(dseq)

    xc = x_prompt.reshape(nb * seq, D)
    xl = x_sample.reshape(db * dseq, D)
    ctx_out = []
    caches = None
    for l in range(depth):
        lam_init = 0.8 - 0.6 * math.exp(-0.3 * l)
        p = _layer_params(l, w_in, ml_gate_b, na_qn_g, na_kn_g, df_qn_g, df_kn_g, rg_wa, rg_wx, rg_ba, rg_bx,
                          df_lq1, df_lk1, df_lq2, df_lk2, df_subln_g, w_out, router_w)
        mod = mod_all[l]
        g1 = norm1_g[l].reshape(1, D)
        g2 = norm2_g[l].reshape(1, D)
        mlg = ml_norm_g[l].reshape(1, GW)
        cb = rg_conv_b[l].reshape(1, GW)
        tb = _rpb_table(na_rpb[l], rows=dseq // GRID_W, rps=NA_RPS)

        nt_c = nb * seq // tm
        ml, gates, naq, rg, dfq, *caches = _proj_in(xc, mod, g1, p["w_r"], p["gate_b"], p["qkg"], caches=caches,
                                                    cache_shape=(nb, depth, seq, GW), layer=l,
                                                    row_base=0, row_div=nt_c, tm=tm)
        hf, hb, cn_c, m_c = _mlstm(ml, gates, nreq=nb, nc=seq // CHUNK)
        y_na, o_df = _ctx_attn(naq, dfq, caches, l, p["lamp"], lam_init, nreq=nb, n=seq)
        y_rg, rg_fin = _rglru(rg, rg_conv_w[l], cb, p["wbd"], p["rg_bias"], rg_lam[l],
                              jnp.zeros((nb, 2, GW), F32), nreq=nb, n=seq)
        x1_c, hn2_c, lg_c = _merge(hf, hb, ml, y_na, y_rg, o_df, xc, mod, mlg, p["sub"], p["wout"], g2, p["rw"],
                                   lam_init, row_base=0, row_div=nt_c, tm=tm)
        pos_c, aff_c = _route(lg_c, nreq=nb, n=seq, cap=cap_c, rb=8, token_major_copy=False)
        xs_c, w_c = _gather(pos_c, aff_c, hn2_c, nreq=nb, n=seq, cap=cap_c, eb=NE, rb=4)
        ctx_out.append((cn_c, m_c, rg_fin))

        tiles_req = dseq // tm
        ml, gates, na, rg, df = _proj_in(xl, mod, g1, p["w_r"], p["gate_b"], p["qkg"], rope=rope,
                                         row_base=1, row_div=tiles_req, tm=tm)
        cn0 = _mlstm_pack_state(state_ml_c[:, l], state_ml_n[:, l])
        m0 = jnp.broadcast_to(state_ml_m[:, l].reshape(db, 8, 1), (db, 8, 128))
        hf, hb, _, _ = _mlstm(ml, gates, (cn0, m0), nreq=db, nc=dseq // CHUNK)
        y_na = _na_lat(na, cache_na_k[:, l].reshape(db, past, GW), cache_na_v[:, l].reshape(db, past, GW), tb,
                       nreq=db, n=dseq, rps=NA_RPS)
        o_df = _df_lat(df, cache_df_k[:, l].reshape(db, past, GW), cache_df_v[:, l].reshape(db, past, GW),
                       p["lamp"], lam_init, nreq=db, n=dseq)
        y_rg, _ = _rglru(rg, rg_conv_w[l], cb, p["wbd"], p["rg_bias"], rg_lam[l], state_rg_h[:, l],
                         nreq=db, n=dseq)
        x1_l, hn2_l, lg_l = _merge(hf, hb, ml, y_na, y_rg, o_df, xl, mod, mlg, p["sub"], p["wout"], g2, p["rw"],
                                   lam_init, row_base=1, row_div=tiles_req, tm=tm)
        pos_l, aff_l, post_l = _route(lg_l, nreq=db, n=dseq, cap=cap_l, rb=1, token_major_copy=True)
        xs_l, w_l = _gather(pos_l, aff_l, hn2_l, nreq=db, n=dseq, cap=cap_l, eb=2)

        y_c, y_l = _experts(xs_c, xs_l, w_c, w_l, moe_wg, moe_wu, moe_wd, l)
        xc = _scatter(pos_c, y_c, x1_c, mod, nreq=nb, n=seq, cap=cap_c, eb=NE, row_base=0, row_mul=0, rb=4)
        xl = _scatter(post_l, y_l, x1_l, mod, nreq=db, n=dseq, cap=cap_l, eb=4, row_base=1, row_mul=1)

    y_prompt = xc.reshape(nb, seq, D)
    y_sample = xl.reshape(db, dseq, D)
    st = lambda f: jnp.stack([f(o) for o in ctx_out], axis=1)
    na_k = caches[0].reshape(nb, depth, seq, NH, HD)
    na_v = caches[1].reshape(nb, depth, seq, NH, HD)
    df_k = caches[2].reshape(nb, depth, seq, NH, 2, HD // 2)
    df_v = caches[3].reshape(nb, depth, seq, NH, HD)
    ml_c = st(lambda o: _mlstm_unpack_state(o[0])[0])
    ml_n = st(lambda o: _mlstm_unpack_state(o[0])[1])
    ml_m = st(lambda o: o[1][:, :, 0].reshape(nb, 2, NH))
    rg_h = st(lambda o: o[2])
    return (y_prompt, y_sample, na_k, na_v, df_k, df_v, ml_c, ml_n, ml_m, rg_h)
```

```python
import functools
import math

import numpy as np
import jax
import jax.numpy as jnp
from jax import lax
from jax.experimental import pallas as pl
from jax.experimental.pallas import tpu as pltpu

F32 = jnp.float32
BF16 = jnp.bfloat16

D = 1024
GW = 256
NH = 4
HD = 64
NE = 16
EPS = 1e-6
CHUNK = 256
GRID_W = 64
NA_ROWS = 8
NA_COLS = 16
DF_HEAD_GROUP = 1
SUB_ROWS = 256
NA_RPS = 4
RG_C = 8.0
ROPE_BASE = 10000.0
PROJ_PAD = 3200
VMEM_LIMIT_BYTES = 56 * 1024 * 1024
NEG_BIG = -1e30
LOG2E = 1.4426950408889634


def _bf(x):
    return x.astype(BF16)


def _dot(a, b):
    return jnp.dot(a, b, preferred_element_type=F32)


def _dot_nt(a, b):
    return lax.dot_general(a, b, (((1,), (1,)), ((), ())), preferred_element_type=F32)


def _dot_tn(a, b):
    return lax.dot_general(a, b, (((0,), (0,)), ((), ())), preferred_element_type=F32)


def _split3(x):
    p0 = _bf(x)
    r1 = x - p0.astype(F32)
    p1 = _bf(r1)
    return p0, p1, _bf(r1 - p1.astype(F32))


def _params(n_axes):
    return pltpu.CompilerParams(dimension_semantics=("arbitrary",) * n_axes,
                                vmem_limit_bytes=VMEM_LIMIT_BYTES)


def _full(shape):
    return pl.BlockSpec(shape, lambda *_: (0,) * len(shape))


def _sds(shape, dtype=F32):
    return jax.ShapeDtypeStruct(shape, dtype)


def _softplus(x):
    return jnp.maximum(x, 0.0) + jnp.log1p(jnp.exp(-jnp.abs(x)))


def _log_sigmoid(x):
    return jnp.minimum(x, 0.0) - jnp.log1p(jnp.exp(-jnp.abs(x)))


def _seg_rms(x, nseg, g_row):
    seg = x.shape[-1] // nseg
    lane = lax.broadcasted_iota(jnp.int32, x.shape, 1)
    x2 = x * x
    tot = jnp.zeros_like(x)
    for s in range(nseg):
        m = (lane >= s * seg) & (lane < (s + 1) * seg)
        t = jnp.sum(jnp.where(m, x2, 0.0), axis=-1, keepdims=True)
        tot = jnp.where(m, t, tot)
    return x * lax.rsqrt(tot * (1.0 / seg) + EPS) * g_row


def _mod_row(pid, row_base, row_div):
    return row_base + pid // row_div


def _mod_kernel(cv_ref, w_ref, b_ref, o_ref):
    cv = cv_ref[...]
    s = cv * jax.nn.sigmoid(cv)
    o_ref[0] = _dot(_bf(s), _bf(w_ref[0])) + b_ref[0]


def _modulation(cv, w_mod, b_mod):
    nl = w_mod.shape[0]
    tn = 1536
    return pl.pallas_call(
        _mod_kernel,
        grid=(nl, 6 * D // tn),
        in_specs=[_full((8, D)),
                  pl.BlockSpec((1, D, tn), lambda l, j: (l, 0, j)),
                  pl.BlockSpec((1, 1, tn), lambda l, j: (l, 0, j))],
        out_specs=pl.BlockSpec((1, 8, tn), lambda l, j: (l, 0, j)),
        out_shape=_sds((nl, 8, 6 * D)),
        compiler_params=_params(2),
        name="modulation",
    )(cv, w_mod, b_mod.reshape(nl, 1, 6 * D))


def _rope(x, cos_t, sin_t):
    lane = lax.broadcasted_iota(jnp.int32, x.shape, 1)
    first = (lane & 15) < 8
    sw = jnp.where(first, pltpu.roll(x, GW - 8, axis=1), pltpu.roll(x, 8, axis=1))
    return x * cos_t + sw * sin_t


def _proj_kernel(*refs, row_base, row_div, layer):
    ctx = layer is not None
    if ctx:
        (x_ref, mod_ref, g1_ref, w_ref, gb_ref, qkg_ref) = refs[:6]
        (ml_ref, gate_ref, naq_ref, rg_ref, dfq_ref, nk_ref, nv_ref, dk_ref, dv_ref) = refs[-9:]
        cache_layer = layer if nk_ref.shape[1] > 1 else 0
    else:
        (x_ref, mod_ref, g1_ref, w_ref, gb_ref, qkg_ref, cos_ref, sin_ref,
         ml_ref, gate_ref, na_ref, rg_ref, df_ref) = refs
    r = _mod_row(pl.program_id(0), row_base, row_div)
    sh = mod_ref[pl.ds(r, 1), 0:D]
    sc = mod_ref[pl.ds(r, 1), D:2 * D]
    for sub in range(x_ref.shape[0] // SUB_ROWS):
        rows = slice(sub * SUB_ROWS, (sub + 1) * SUB_ROWS)
        x = x_ref[rows, :]
        ms = jnp.mean(x * x, axis=-1, keepdims=True)
        y = x * lax.rsqrt(ms + EPS) * g1_ref[...]
        hn = _bf(y * (1.0 + sc) + sh)

        ml = _dot(hn, w_ref[:, 0:1024])
        ml_ref[rows, 0:256] = ml[:, 0:256]
        ml_ref[rows, 256:512] = ml[:, 256:512] * (HD ** -0.5)
        ml_ref[rows, 512:1024] = ml[:, 512:1024]

        gz = _dot(hn, w_ref[:, 3072:3200]) + gb_ref[...]
        lane = lax.broadcasted_iota(jnp.int32, gz.shape, 1)
        gate_ref[rows, :] = jnp.where(((lane >> 2) & 1) == 1, _log_sigmoid(gz), gz)

        rg_ref[rows, :] = _dot(hn, w_ref[:, 1792:2304])

        nz = _dot(hn, w_ref[:, 1024:1792])
        nq = _seg_rms(nz[:, 0:256], NH, qkg_ref[0:1, :])
        nk = _seg_rms(nz[:, 256:512], NH, qkg_ref[1:2, :])
        dz = _dot(hn, w_ref[:, 2304:3072])
        dq = _seg_rms(dz[:, 0:256], 2 * NH, qkg_ref[2:3, :])
        dk = _seg_rms(dz[:, 256:512], 2 * NH, qkg_ref[3:4, :])
        if ctx:
            naq_ref[rows, :] = nq
            dfq_ref[rows, :] = dq
            for c_ref, val in ((nk_ref, nk), (nv_ref, nz[:, 512:768]), (dk_ref, dk), (dv_ref, dz[:, 512:768])):
                c_ref[sub, cache_layer] = val
                for other in range(c_ref.shape[1]):
                    if other != cache_layer:
                        c_ref[sub, other] = jnp.zeros_like(val)
        else:
            cos_t = cos_ref[rows, :]
            sin_t = sin_ref[rows, :]
            na_ref[rows, 0:256] = nq
            na_ref[rows, 256:512] = nk
            na_ref[rows, 512:768] = nz[:, 512:768]
            df_ref[rows, 0:256] = _rope(dq, cos_t, sin_t)
            df_ref[rows, 256:512] = _rope(dk, cos_t, sin_t)
            df_ref[rows, 512:768] = dz[:, 512:768]


def _proj_in(x, mod, g1, w_r, gate_b, qkg, *, rope=None, caches=None, cache_shape=None, layer=None,
             row_base, row_div, tm=512):
    tp = x.shape[0]
    in_specs = [pl.BlockSpec((tm, D), lambda i: (i, 0)), _full((8, 6 * D)), _full((1, D)),
                _full((D, PROJ_PAD)), _full((1, 128)), _full((4, GW))]
    args = [x, mod, g1, w_r, gate_b, qkg]
    tok = lambda n: pl.BlockSpec((tm, n), lambda i: (i, 0))
    aliases = {}
    if layer is None:
        tiles = rope[0].shape[0] // tm
        in_specs += [pl.BlockSpec((tm, GW), lambda i: (i % tiles, 0))] * 2
        args += list(rope)
        out_specs = [tok(1024), tok(128), tok(768), tok(512), tok(768)]
        out_shape = [_sds((tp, n)) for n in (1024, 128, 768, 512, 768)]
    else:
        if caches is None:
            depth, seq = cache_shape[1], cache_shape[2]
            cspec = pl.BlockSpec((tm // seq, depth, seq, GW), lambda i: (i, 0, 0, 0))
        else:
            cache_shape, seq = caches[0].shape, caches[0].shape[2]
            cspec = pl.BlockSpec((tm // seq, 1, seq, GW), lambda i: (i, layer, 0, 0))
            in_specs += [pl.BlockSpec(memory_space=pl.ANY)] * 4
            args += list(caches)
            aliases = {6 + i: 5 + i for i in range(4)}
        assert seq == SUB_ROWS
        out_specs = [tok(1024), tok(128), tok(GW), tok(512), tok(GW)] + [cspec] * 4
        out_shape = [_sds((tp, n)) for n in (1024, 128, GW, 512, GW)] + [_sds(cache_shape)] * 4
    return pl.pallas_call(
        functools.partial(_proj_kernel, row_base=row_base, row_div=row_div, layer=layer),
        grid=(tp // tm,),
        in_specs=in_specs,
        out_specs=out_specs,
        out_shape=out_shape,
        input_output_aliases=aliases,
        compiler_params=_params(1),
        name="proj_in",
    )(*args)


def _mlstm_chunk(dir_refs, st_s, m_s):
    lc = CHUNK
    row = lax.broadcasted_iota(jnp.int32, (lc, lc), 0)
    col = lax.broadcasted_iota(jnp.int32, (lc, lc), 1)
    ones = jnp.ones((lc, 128), BF16)
    hs = [slice(h * HD, (h + 1) * HD) for h in range(NH)]
    chains = [(d, h) for d in range(2) for h in range(NH)]
    tris, bc_cols, r_rows, k_ts, k_tbs, q_alls, v_alls = [], [], [], [], [], [], []
    for d, (q_ref, k_ref, v_ref, g_ref, _) in enumerate(dir_refs):
        g = g_ref[...]
        gt = g.T
        tri = (row >= col) if d == 0 else (row <= col)
        tri_t = (col >= row) if d == 0 else (col <= row)
        g_parts = _split3(g)
        gt_parts = _split3(gt)
        tri_b = tri.astype(F32).astype(BF16)
        tri_tb = tri_t.astype(F32).astype(BF16)
        bc_cols.append(_dot(tri_b, g_parts[0]) + _dot(tri_b, g_parts[1]) + _dot(tri_b, g_parts[2]))
        bc_row = _dot(gt_parts[0], tri_tb) + _dot(gt_parts[1], tri_tb) + _dot(gt_parts[2], tri_tb)
        r_rows.append(gt[d * 8:d * 8 + NH, :] - bc_row[d * 8 + NH:d * 8 + 2 * NH, :])
        tris.append(tri)
        k_t = k_ref[...].T
        k_ts.append(k_t)
        k_tbs.append(_bf(k_t))
        q_alls.append(_bf(q_ref[...]))
        v_alls.append(_bf(v_ref[...]))
    ms = [m_s[d * NH + h:d * NH + h + 1, 0:1] for d, h in chains]
    rms = [jnp.where(tris[d], r_rows[d][h:h + 1, :], -jnp.inf) for d, h in chains]
    big_rs = [jnp.maximum(jnp.max(rm, axis=-1, keepdims=True), m) for rm, m in zip(rms, ms)]
    ss = [_bf(_dot(q_alls[d][:, hs[h]], k_tbs[d][hs[h], :]) * jnp.exp(rms[i] - big_rs[i]))
          for i, (d, h) in enumerate(chains)]
    v2s = [v_alls[d][:, (h // 2) * 128:(h // 2 + 1) * 128] for d, h in chains]
    es = [jnp.exp(m - big_r) for m, big_r in zip(ms, big_rs)]
    c_cols = [bc_cols[d][:, d * 8 + NH + h:d * 8 + NH + h + 1] for d, h in chains]
    tots = [_dot(ss[i], v2s[i]) + es[i] * _dot(q_alls[d][:, hs[h]], _bf(st_s[d, h, 0]))
            for i, (d, h) in enumerate(chains)]
    dens = [_dot(ss[i], ones) + es[i] * _dot(q_alls[d][:, hs[h]], _bf(st_s[d, h, 1]))
            for i, (d, h) in enumerate(chains)]
    outs = [tots[i] / jnp.maximum(jnp.abs(dens[i]), jnp.exp(-(c_cols[i] + big_rs[i]))) for i in range(len(chains))]
    for i, (d, h) in enumerate(chains):
        last = lc - 1 if d == 0 else 0
        r_last = big_rs[i][last:last + 1, :]
        kw = _bf(k_ts[d][hs[h], :] * jnp.exp(r_rows[d][h:h + 1, :] - r_last))
        gdec = jnp.exp(ms[i] - r_last)
        st_s[d, h, 0] = gdec * st_s[d, h, 0] + _dot(kw, v2s[i])
        st_s[d, h, 1] = gdec * st_s[d, h, 1] + _dot(kw, ones)
        m_s[d * NH + h:d * NH + h + 1, :] = jnp.broadcast_to(c_cols[i][last:last + 1, :] + r_last, (1, 128))
    lane = lax.broadcasted_iota(jnp.int32, (lc, 128), 1)
    for d, refs in enumerate(dir_refs):
        o = outs[d * NH:(d + 1) * NH]
        refs[4][...] = jnp.concatenate([jnp.where(lane < HD, o[2 * p], o[2 * p + 1]) for p in range(NH // 2)],
                                       axis=-1)


def _mlstm_kernel(qf, kf, vf, gf, qb, kb, vb, gb, *rest):
    hf_ref, hb_ref, st_out, m_out, st_s, m_s = rest[-6:]
    j = pl.program_id(1)

    @pl.when(j == 0)
    def _():
        if len(rest) == 8:
            st_s[...] = rest[0][0]
            m_s[...] = rest[1][0]
        else:
            st_s[...] = jnp.zeros_like(st_s)
            m_s[...] = jnp.zeros_like(m_s)

    _mlstm_chunk(((qf, kf, vf, gf, hf_ref), (qb, kb, vb, gb, hb_ref)), st_s, m_s)

    @pl.when(j == pl.num_programs(1) - 1)
    def _():
        st_out[0] = st_s[...]
        m_out[0] = m_s[...]


def _mlstm_pack_state(c, n):
    z = jnp.zeros_like(c)
    odd = (jnp.arange(NH) % 2 == 1)[None, None, :, None, None]
    c_pair = jnp.where(odd, jnp.concatenate([z, c], axis=-1), jnp.concatenate([c, z], axis=-1))
    n_rep = jnp.broadcast_to(n[..., None], n.shape + (128,))
    return jnp.stack([c_pair, n_rep], axis=3)


def _mlstm_unpack_state(st):
    odd = (jnp.arange(NH) % 2 == 1)[None, None, :, None, None]
    c = jnp.where(odd, st[:, :, :, 0, :, HD:2 * HD], st[:, :, :, 0, :, 0:HD])
    return c, st[:, :, :, 1, :, 0]


def _mlstm(mlz, gates, state=None, *, nreq, nc):
    tp = mlz.shape[0]
    st_spec = pl.BlockSpec((1, 2, NH, 2, HD, 128), lambda r, j: (r, 0, 0, 0, 0, 0))

    def fwd(col):
        return lambda r, j: (r * nc + j, col)

    def bwd(col):
        return lambda r, j: (r * nc + nc - 1 - j, col)

    in_specs = []
    for mk in (fwd, bwd):
        in_specs += [pl.BlockSpec((CHUNK, GW), mk(0)), pl.BlockSpec((CHUNK, GW), mk(1)),
                     pl.BlockSpec((CHUNK, GW), mk(2)), pl.BlockSpec((CHUNK, 128), mk(0))]
    args = [mlz, mlz, mlz, gates, mlz, mlz, mlz, gates]
    if state is not None:
        in_specs += [st_spec, pl.BlockSpec((1, 8, 128), lambda r, j: (r, 0, 0))]
        args += list(state)
    return pl.pallas_call(
        _mlstm_kernel,
        grid=(nreq, nc),
        in_specs=in_specs,
        out_specs=[pl.BlockSpec((CHUNK, GW), fwd(0)), pl.BlockSpec((CHUNK, GW), bwd(0)),
                   st_spec, pl.BlockSpec((1, 8, 128), lambda r, j: (r, 0, 0))],
        out_shape=[_sds((tp, GW)), _sds((tp, GW)), _sds((nreq, 2, NH, 2, HD, 128)), _sds((nreq, 8, 128))],
        scratch_shapes=[pltpu.VMEM((2, NH, 2, HD, 128), F32), pltpu.VMEM((8, 128), F32)],
        compiler_params=_params(2),
        name="mlstm",
    )(*args)


def _diff_lambda(lam_ref, lam_init):
    lp = lam_ref[...]
    a = jnp.exp(jnp.sum(lp[0:1, :] * lp[1:2, :], axis=-1, keepdims=True))
    b = jnp.exp(jnp.sum(lp[2:3, :] * lp[3:4, :], axis=-1, keepdims=True))
    return a - b + lam_init


def _softmax_pv(jobs):
    ss = [_dot(q, k_t) for q, k_t, _ in jobs]
    es = [jnp.exp2(s - jnp.max(s, axis=-1, keepdims=True)) for s in ss]
    return [_dot(_bf(e), v) / jnp.sum(e, axis=-1, keepdims=True) for e, (_, _, v) in zip(es, jobs)]


def _ctx_attn_kernel(nq, nk, nv, dq, dk, dv, lam_ref, ona_ref, odf_ref, *, lam_init):
    lam = _diff_lambda(lam_ref, lam_init)
    n = nk.shape[2]
    for i in range(nk.shape[0]):
        rows = slice(i * n, (i + 1) * n)
        nk_t = _bf(nk[i, 0].T)
        dk_t = _bf(dk[i, 0].T)
        jobs = []
        for h in range(NH):
            hs = slice(h * HD, (h + 1) * HD)
            jobs.append((_bf(nq[rows, hs] * (HD ** -0.5 * LOG2E)), nk_t[hs, :], _bf(nv[i, 0, :, hs])))
        for h in range(NH):
            vh = _bf(dv[i, 0, :, h * HD:(h + 1) * HD])
            for c in range(2):
                cs = slice(h * HD + c * 32, h * HD + (c + 1) * 32)
                jobs.append((_bf(dq[rows, cs] * (32 ** -0.5 * LOG2E)), dk_t[cs, :], vh))
        outs = _softmax_pv(jobs)
        ona_ref[rows, :] = _bf(jnp.concatenate(outs[0:NH], axis=-1))
        odf_ref[rows, :] = jnp.concatenate([outs[NH + 2 * h] - lam * outs[NH + 2 * h + 1] for h in range(NH)],
                                           axis=-1)


def _ctx_attn(naq, dfq, caches, layer, lamp, lam_init, *, nreq, n, rb=1):
    tp = naq.shape[0]
    qs = pl.BlockSpec((rb * n, GW), lambda b: (b, 0))
    cs = pl.BlockSpec((rb, 1, n, GW), lambda b: (b, layer, 0, 0))
    return pl.pallas_call(
        functools.partial(_ctx_attn_kernel, lam_init=lam_init),
        grid=(nreq // rb,),
        in_specs=[qs, cs, cs, qs, cs, cs, _full((4, 32))],
        out_specs=[qs, qs],
        out_shape=[_sds((tp, GW), BF16), _sds((tp, GW))],
        compiler_params=_params(1),
        name="ctx_attn",
    )(naq, caches[0], caches[1], dfq, caches[2], caches[3], lamp)


def _rpb_kernel(rpb_ref, o_ref, *, rows, rps):
    h = pl.program_id(0)
    shape = (GRID_W, NA_ROWS * GRID_W)
    lane = lax.broadcasted_iota(jnp.int32, shape, 1)
    cq = lax.broadcasted_iota(jnp.int32, shape, 0)
    ck = lane & (GRID_W - 1)
    c0 = jnp.clip(cq - NA_COLS // 2, 0, GRID_W - NA_COLS)
    ok = (ck >= c0) & (ck < c0 + NA_COLS)
    ncol = 2 * NA_COLS - 1
    ndr = 2 * NA_ROWS - 1
    x = lax.broadcasted_iota(jnp.int32, (1, 2 * GRID_W), 1)
    didx = jnp.clip(x - GRID_W, -(NA_COLS - 1), NA_COLS - 1) + (NA_COLS - 1)
    tiles = []
    for dr in range(ndr):
        frow = jnp.zeros((1, 2 * GRID_W), F32)
        for b in range(ncol):
            frow = jnp.where(didx == b, rpb_ref[(h * ndr + dr) * ncol + b], frow)
        rolled = pltpu.roll(jnp.broadcast_to(frow, (GRID_W, 2 * GRID_W)), GRID_W, axis=1, stride=1, stride_axis=0)
        tiles.append(rolled[:, 0:GRID_W])
    tbs = [jnp.where(ok, jnp.concatenate(tiles[s:s + NA_ROWS], axis=-1) * LOG2E, NEG_BIG) for s in range(NA_ROWS)]
    wrows = NA_ROWS + rps
    steps = rows // rps
    for t, j in enumerate((0, 1, steps - 1)):
        w0 = min(max(j * rps - NA_ROWS // 2, 0), rows - wrows)
        for a in range(rps):
            r = j * rps + a
            r0 = min(max(r - NA_ROWS // 2, 0), rows - NA_ROWS)
            off = (r0 - w0) * GRID_W
            rest = rps * GRID_W - off
            pieces = [tbs[r0 - r + NA_ROWS - 1]]
            if off:
                pieces = [jnp.full((GRID_W, off), NEG_BIG, F32)] + pieces
            if rest:
                pieces = pieces + [jnp.full((GRID_W, rest), NEG_BIG, F32)]
            o_ref[0, t, a * GRID_W:(a + 1) * GRID_W, :] = jnp.concatenate(pieces, axis=-1)


def _na_step_offsets(j, rows, rps):
    w0 = min(max(j * rps - NA_ROWS // 2, 0), rows - NA_ROWS - rps)
    return [(min(max(r - NA_ROWS // 2, 0), rows - NA_ROWS) - r,
             min(max(r - NA_ROWS // 2, 0), rows - NA_ROWS) - w0) for r in range(j * rps, (j + 1) * rps)]


def _rpb_table(rpb, *, rows, rps):
    steps = rows // rps
    assert all(_na_step_offsets(j, rows, rps) == _na_step_offsets(1, rows, rps) for j in range(1, steps - 1))
    shape = (rps * GRID_W, (NA_ROWS + rps) * GRID_W)
    return pl.pallas_call(
        functools.partial(_rpb_kernel, rows=rows, rps=rps),
        grid=(NH,),
        in_specs=[pl.BlockSpec(memory_space=pltpu.SMEM)],
        out_specs=pl.BlockSpec((1, 3) + shape, lambda h: (h, 0, 0, 0)),
        out_shape=_sds((NH, 3) + shape),
        compiler_params=_params(1),
        name="rpb_table",
    )(rpb.reshape(-1))


def _na_lat_kernel(q_ref, k_ref, v_ref, kc_ref, vc_ref, tb_ref, o_ref, k_s, v_s, kct_s, vc_s, *, rows, rps):
    j = pl.program_id(1)

    @pl.when(j == 0)
    def _():
        k_s[...] = _bf(k_ref[...])
        v_s[...] = _bf(v_ref[...])
        kct_s[...] = _bf(kc_ref[0].T)
        vc_s[...] = _bf(vc_ref[0])

    wrows = NA_ROWS + rps
    win = wrows * GRID_W
    w0 = jnp.clip(j * rps - NA_ROWS // 2, 0, rows - wrows)
    start = pl.multiple_of(w0 * GRID_W, GRID_W)
    hss = [slice(h * HD, (h + 1) * HD) for h in range(NH)]
    qs = [_bf(q_ref[:, hs] * (HD ** -0.5 * LOG2E)) for hs in hss]
    s_cs = [_dot(q, kct_s[hs, :]) for q, hs in zip(qs, hss)]
    s_ls = [_dot_nt(q, k_s[pl.ds(start, win), hs]) + tb_ref[h, 0] for h, (q, hs) in enumerate(zip(qs, hss))]
    ms = [jnp.maximum(jnp.max(s_c, axis=-1, keepdims=True), jnp.max(s_l, axis=-1, keepdims=True))
          for s_c, s_l in zip(s_cs, s_ls)]
    e_cs = [jnp.exp2(s_c - m) for s_c, m in zip(s_cs, ms)]
    e_ls = [jnp.exp2(s_l - m) for s_l, m in zip(s_ls, ms)]
    outs = []
    for e_c, e_l, hs in zip(e_cs, e_ls, hss):
        den = jnp.sum(e_c, axis=-1, keepdims=True) + jnp.sum(e_l, axis=-1, keepdims=True)
        o = _dot(_bf(e_c), vc_s[:, hs]) + _dot(_bf(e_l), v_s[pl.ds(start, win), hs])
        outs.append(o / den)
    o_ref[...] = _bf(jnp.concatenate(outs, axis=-1))


def _na_lat(na, kc, vc, tb, *, nreq, n, rps):
    tp = na.shape[0]
    rows = n // GRID_W
    past = kc.shape[1]
    steps = rows // rps
    tq = rps * GRID_W
    return pl.pallas_call(
        functools.partial(_na_lat_kernel, rows=rows, rps=rps),
        grid=(nreq, steps),
        in_specs=[pl.BlockSpec((tq, GW), lambda b, r: (b * steps + r, 0)),
                  pl.BlockSpec((n, GW), lambda b, r: (b, 1)),
                  pl.BlockSpec((n, GW), lambda b, r: (b, 2)),
                  pl.BlockSpec((1, past, GW), lambda b, r: (b, 0, 0)),
                  pl.BlockSpec((1, past, GW), lambda b, r: (b, 0, 0)),
                  pl.BlockSpec((NH, 1) + tb.shape[2:],
                               lambda b, r: (0, jnp.minimum(r, 1) + r // (steps - 1), 0, 0))],
        out_specs=pl.BlockSpec((tq, GW), lambda b, r: (b * steps + r, 0)),
        out_shape=_sds((tp, GW), BF16),
        scratch_shapes=[pltpu.VMEM((n, GW), BF16), pltpu.VMEM((n, GW), BF16),
                        pltpu.VMEM((GW, past), BF16), pltpu.VMEM((past, GW), BF16)],
        compiler_params=_params(2),
        name="na_latent",
    )(na, na, na, kc, vc, tb)


def _df_lat_kernel(q_ref, k_ref, v_ref, kc_ref, vc_ref, lam_ref, o_ref, kt_s, v_s, kct_s, vc_s, *, lam_init):
    @pl.when(pl.program_id(1) == 0)
    def _():
        kt_s[...] = _bf(k_ref[...].T)
        v_s[...] = _bf(v_ref[...])
        kct_s[...] = _bf(kc_ref[0].T)
        vc_s[...] = _bf(vc_ref[0])

    lam = _diff_lambda(lam_ref, lam_init)
    outs = []
    for h0 in range(0, NH, DF_HEAD_GROUP):
        heads = range(h0, h0 + DF_HEAD_GROUP)
        css = [slice(h * HD + c * 32, h * HD + (c + 1) * 32) for h in heads for c in range(2)]
        qs = [_bf(q_ref[:, cs] * (32 ** -0.5 * LOG2E)) for cs in css]
        s_cs = [_dot(q, kct_s[cs, :]) for q, cs in zip(qs, css)]
        s_ls = [_dot(q, kt_s[cs, :]) for q, cs in zip(qs, css)]
        ms = [jnp.maximum(jnp.max(s_c, axis=-1, keepdims=True), jnp.max(s_l, axis=-1, keepdims=True))
              for s_c, s_l in zip(s_cs, s_ls)]
        e_cs = [jnp.exp2(s_c - m) for s_c, m in zip(s_cs, ms)]
        e_ls = [jnp.exp2(s_l - m) for s_l, m in zip(s_ls, ms)]
        os = []
        for i, (e_c, e_l) in enumerate(zip(e_cs, e_ls)):
            hs = slice((h0 + i // 2) * HD, (h0 + i // 2 + 1) * HD)
            den = jnp.sum(e_c, axis=-1, keepdims=True) + jnp.sum(e_l, axis=-1, keepdims=True)
            os.append((_dot(_bf(e_c), vc_s[:, hs]) + _dot(_bf(e_l), v_s[:, hs])) / den)
        outs += [os[2 * i] - lam * os[2 * i + 1] for i in range(DF_HEAD_GROUP)]
    o_ref[...] = jnp.concatenate(outs, axis=-1)


def _df_lat(df, kc, vc, lamp, lam_init, *, nreq, n, tq=512):
    tp = df.shape[0]
    nq = n // tq
    past = kc.shape[1]
    return pl.pallas_call(
        functools.partial(_df_lat_kernel, lam_init=lam_init),
        grid=(nreq, nq),
        in_specs=[pl.BlockSpec((tq, GW), lambda b, j: (b * nq + j, 0)),
                  pl.BlockSpec((n, GW), lambda b, j: (b, 1)),
                  pl.BlockSpec((n, GW), lambda b, j: (b, 2)),
                  pl.BlockSpec((1, past, GW), lambda b, j: (b, 0, 0)),
                  pl.BlockSpec((1, past, GW), lambda b, j: (b, 0, 0)),
                  _full((4, 32))],
        out_specs=pl.BlockSpec((tq, GW), lambda b, j: (b * nq + j, 0)),
        out_shape=_sds((tp, GW)),
        scratch_shapes=[pltpu.VMEM((GW, n), BF16), pltpu.VMEM((n, GW), BF16),
                        pltpu.VMEM((GW, past), BF16), pltpu.VMEM((past, GW), BF16)],
        compiler_params=_params(2),
        name="df_latent",
    )(df, df, df, kc, vc, lamp)


def _rg_kernel(x_ref, g_ref, cw_ref, cb_ref, wbd_ref, bias_ref, lam_ref, h0_ref,
               y_ref, fin_ref, a_s, b_s, *, n):
    x = x_ref[...]
    row = lax.broadcasted_iota(jnp.int32, x.shape, 0)
    xc = cb_ref[...] + jnp.where(row >= 2, pltpu.roll(x, 2, axis=0), 0.0) * cw_ref[0:1, :]
    xc = xc + jnp.where(row >= 1, pltpu.roll(x, 1, axis=0), 0.0) * cw_ref[1:2, :]
    xc = xc + x * cw_ref[2:3, :]
    xc = xc + jnp.where(row < n - 1, pltpu.roll(x, n - 1, axis=0), 0.0) * cw_ref[3:4, :]
    z = _dot(_bf(xc), wbd_ref[...]) + bias_ref[...]
    sub = row & 7
    for d in range(2):
        rgate = jax.nn.sigmoid(z[:, 512 * d:512 * d + GW])
        igate = jax.nn.sigmoid(z[:, 512 * d + GW:512 * d + 2 * GW])
        la = -RG_C * rgate * _softplus(-lam_ref[d:d + 1, :])
        a = jnp.exp(la)
        t = jnp.tanh(la)
        b = jnp.sqrt(-2.0 * t / (1.0 - t)) * igate * xc
        for dd in (1, 2, 4):
            if d == 0:
                keep = sub >= dd
                a_sh = jnp.where(keep, pltpu.roll(a, dd, axis=0), 1.0)
                b_sh = jnp.where(keep, pltpu.roll(b, dd, axis=0), 0.0)
            else:
                keep = sub < 8 - dd
                a_sh = jnp.where(keep, pltpu.roll(a, n - dd, axis=0), 1.0)
                b_sh = jnp.where(keep, pltpu.roll(b, n - dd, axis=0), 0.0)
            b = b + a * b_sh
            a = a * a_sh
        a_s[d] = a
        b_s[d] = b

    nt = n // 8

    def body(t, carry):
        hf, hb = carry
        sf = pl.multiple_of(t * 8, 8)
        sb = pl.multiple_of((nt - 1 - t) * 8, 8)
        tf = a_s[0, pl.ds(sf, 8), :] * hf + b_s[0, pl.ds(sf, 8), :]
        tb = a_s[1, pl.ds(sb, 8), :] * hb + b_s[1, pl.ds(sb, 8), :]
        b_s[0, pl.ds(sf, 8), :] = tf
        b_s[1, pl.ds(sb, 8), :] = tb
        return tf[7:8, :], tb[0:1, :]

    hf, hb = lax.fori_loop(0, nt, body, (h0_ref[0, 0:1, :], h0_ref[0, 1:2, :]))
    fin_ref[0, 0:1, :] = hf
    fin_ref[0, 1:2, :] = hb
    gg = g_ref[...]
    cdf = 0.5 * (1.0 + jnp.tanh(math.sqrt(2.0 / math.pi) * (gg + 0.044715 * (gg * gg * gg))))
    y_ref[...] = _bf((b_s[0] + b_s[1]) * (gg * cdf))


def _rglru(rg, cw, cb, wbd, bias, lam, h0, *, nreq, n):
    tp = rg.shape[0]
    return pl.pallas_call(
        functools.partial(_rg_kernel, n=n),
        grid=(nreq,),
        in_specs=[pl.BlockSpec((n, GW), lambda b: (b, 0)), pl.BlockSpec((n, GW), lambda b: (b, 1)),
                  _full((4, GW)), _full((1, GW)), _full((GW, 4 * GW)), _full((1, 4 * GW)), _full((2, GW)),
                  pl.BlockSpec((1, 2, GW), lambda b: (b, 0, 0))],
        out_specs=[pl.BlockSpec((n, GW), lambda b: (b, 0)), pl.BlockSpec((1, 2, GW), lambda b: (b, 0, 0))],
        out_shape=[_sds((tp, GW), BF16), _sds((nreq, 2, GW))],
        scratch_shapes=[pltpu.VMEM((2, n, GW), F32), pltpu.VMEM((2, n, GW), F32)],
        compiler_params=_params(1),
        name="rglru",
    )(rg, rg, cw, cb, wbd, bias, lam, h0)


def _merge_kernel(hf_ref, hb_ref, mlo_ref, yna_ref, yrg_ref, odf_ref, x_ref, mod_ref, mlg_ref, sub_ref,
                  wout_ref, n2_ref, rw_ref, x1_ref, hn2_ref, lg_ref, *, row_base, row_div, lam_init):
    r = _mod_row(pl.program_id(0), row_base, row_div)
    y_ml = _seg_rms(hf_ref[...] + hb_ref[...], NH, mlg_ref[...]) * jax.nn.sigmoid(mlo_ref[...])
    y_df = _seg_rms(odf_ref[...], NH, sub_ref[...]) * (1.0 - lam_init)
    y = jnp.concatenate([_bf(y_ml), _bf(yna_ref[...]), _bf(yrg_ref[...]), _bf(y_df)], axis=-1)
    o = _dot(y, wout_ref[...])
    x1 = x_ref[...] + mod_ref[pl.ds(r, 1), 2 * D:3 * D] * o
    x1_ref[...] = x1
    ms = jnp.mean(x1 * x1, axis=-1, keepdims=True)
    hn = x1 * lax.rsqrt(ms + EPS) * n2_ref[...]
    hn = _bf(hn * (1.0 + mod_ref[pl.ds(r, 1), 4 * D:5 * D]) + mod_ref[pl.ds(r, 1), 3 * D:4 * D])
    hn2_ref[...] = hn
    lg_ref[...] = _dot(hn, rw_ref[...])


def _merge(hf, hb, mlz, yna, yrg, odf, x, mod, mlg, sub, wout, n2, rw, lam_init, *, row_base, row_div, tm=512):
    tp = x.shape[0]
    g = lambda c: pl.BlockSpec((tm, GW), lambda i: (i, c))
    return pl.pallas_call(
        functools.partial(_merge_kernel, row_base=row_base, row_div=row_div, lam_init=lam_init),
        grid=(tp // tm,),
        in_specs=[g(0), g(0), g(3), g(0), g(0), g(0),
                  pl.BlockSpec((tm, D), lambda i: (i, 0)), _full((8, 6 * D)), _full((1, GW)), _full((1, GW)),
                  _full((D, D)), _full((1, D)), _full((D, 128))],
        out_specs=[pl.BlockSpec((tm, D), lambda i: (i, 0)), pl.BlockSpec((tm, D), lambda i: (i, 0)),
                   pl.BlockSpec((tm, 128), lambda i: (i, 0))],
        out_shape=[_sds((tp, D)), _sds((tp, D), BF16), _sds((tp, 128))],
        compiler_params=_params(1),
        name="merge",
    )(hf, hb, mlz, yna, yrg, odf, x, mod, mlg, sub, wout, n2, rw)


def _excl_cumsum_lanes(mask):
    blk = 256
    r = lax.broadcasted_iota(jnp.int32, (blk, blk), 0)
    c = lax.broadcasted_iota(jnp.int32, (blk, blk), 1)
    tri = (r < c).astype(F32).astype(BF16)
    off = jnp.zeros((mask.shape[0], 1), F32)
    outs = []
    for i in range(mask.shape[1] // blk):
        mb = mask[:, i * blk:(i + 1) * blk]
        outs.append(_dot(_bf(mb), tri) + off)
        off = off + jnp.sum(mb, axis=-1, keepdims=True)
    return jnp.concatenate(outs, axis=-1)


def _route_kernel(lg_ref, pos_ref, aff_ref, *post_ref, cap, n, rb):
    lg = lg_ref[...].T[0:NE, :]
    ex = jnp.exp(lg - jnp.max(lg, axis=0, keepdims=True))
    aff = ex / jnp.sum(ex, axis=0, keepdims=True)
    aff_ref[...] = aff
    aff = jnp.concatenate([aff[:, i * n:(i + 1) * n] for i in range(rb)], axis=0)
    thr = jnp.zeros((rb * NE, 1), jnp.int32)
    for bit in range(30, -1, -1):
        cand = thr | (1 << bit)
        cnt = jnp.sum((aff >= pltpu.bitcast(cand, F32)).astype(jnp.int32), axis=-1, keepdims=True)
        thr = jnp.where(cnt >= cap, cand, thr)
    thr_f = pltpu.bitcast(thr, F32)
    gt = aff > thr_f
    eq = aff == thr_f
    need = (cap - jnp.sum(gt.astype(jnp.int32), axis=-1, keepdims=True)).astype(F32)
    eq_rank = _excl_cumsum_lanes(eq.astype(F32))
    sel = gt | (eq & (eq_rank < need))
    slot = _excl_cumsum_lanes(sel.astype(F32))
    pos = jnp.where(sel, slot.astype(jnp.int32), -1)
    pos = jnp.concatenate([pos[i * NE:(i + 1) * NE, :] for i in range(rb)], axis=1)
    pos_ref[...] = pos
    if post_ref:
        post_ref[0][...] = jnp.concatenate([pos, jnp.full((128 - NE, rb * n), -1, jnp.int32)], axis=0).T


def _route(lg, *, nreq, n, cap, rb, token_major_copy):
    tp = lg.shape[0]
    espec = pl.BlockSpec((NE, rb * n), lambda b: (0, b))
    out_specs = [espec, espec]
    out_shape = [_sds((NE, tp), jnp.int32), _sds((NE, tp))]
    if token_major_copy:
        out_specs.append(pl.BlockSpec((rb * n, 128), lambda b: (b, 0)))
        out_shape.append(_sds((tp, 128), jnp.int32))
    return pl.pallas_call(
        functools.partial(_route_kernel, cap=cap, n=n, rb=rb),
        grid=(nreq // rb,),
        in_specs=[pl.BlockSpec((rb * n, 128), lambda b: (b, 0))],
        out_specs=out_specs,
        out_shape=out_shape,
        compiler_params=_params(1),
        name="route",
    )(lg)


def _gather_kernel(pos_ref, aff_ref, h_ref, xs_ref, w_ref, *, eb, cap, n, rb):
    eg = pl.program_id(1)
    io = lax.broadcasted_iota(jnp.int32, (cap, n), 0)
    for i in range(rb):
        toks = slice(i * n, (i + 1) * n)
        slots = slice(i * cap, (i + 1) * cap)
        sels = []
        for k in range(eb):
            e = eg * eb + k
            sel = pos_ref[pl.ds(e, 1), toks] == io
            sels.append(sel.astype(F32).astype(BF16))
            w = jnp.sum(jnp.where(sel, aff_ref[pl.ds(e, 1), toks], 0.0), axis=-1, keepdims=True)
            w_ref[k, slots, :] = jnp.broadcast_to(w, (cap, 128))
        xs = _dot(jnp.concatenate(sels, axis=0), h_ref[toks, :]).astype(BF16)
        for k in range(eb):
            xs_ref[k, slots, :] = xs[k * cap:(k + 1) * cap, :]


def _gather(pos, aff, hn2, *, nreq, n, cap, eb, rb=1):
    return pl.pallas_call(
        functools.partial(_gather_kernel, eb=eb, cap=cap, n=n, rb=rb),
        grid=(nreq // rb, NE // eb),
        in_specs=[pl.BlockSpec((NE, rb * n), lambda b, g: (0, b)), pl.BlockSpec((NE, rb * n), lambda b, g: (0, b)),
                  pl.BlockSpec((rb * n, D), lambda b, g: (b, 0))],
        out_specs=[pl.BlockSpec((eb, rb * cap, D), lambda b, g: (g, b, 0)),
                   pl.BlockSpec((eb, rb * cap, 128), lambda b, g: (g, b, 0))],
        out_shape=[_sds((NE, nreq * cap, D), BF16), _sds((NE, nreq * cap, 128))],
        compiler_params=_params(2),
        name="moe_gather",
    )(pos, aff, hn2)


def _expert_kernel(xc_ref, xl_ref, wc_ref, wl_ref, wg_ref, wu_ref, wd_ref, yc_ref, yl_ref, *, rows):
    wg = _bf(wg_ref[0, 0])
    wu = _bf(wu_ref[0, 0])
    wd = _bf(wd_ref[0, 0])
    tm = 512
    for x_ref, w_ref, y_ref in ((xc_ref, wc_ref, yc_ref), (xl_ref, wl_ref, yl_ref)):
        for ch in range(rows // tm):
            sl = slice(ch * tm, (ch + 1) * tm)
            x = x_ref[0, sl, :]
            g = _dot(x, wg)
            u = _dot(x, wu)
            a = _bf(g * jax.nn.sigmoid(g) * u)
            y_ref[0, sl, :] = _bf(_dot(a, wd) * w_ref[0, sl, 0:1])


def _experts(xs_c, xs_l, w_c, w_l, wg, wu, wd, layer):
    rows = xs_c.shape[1]
    dff = wg.shape[-1]
    xspec = pl.BlockSpec((1, rows, D), lambda e: (e, 0, 0))
    wspec = pl.BlockSpec((1, rows, 128), lambda e: (e, 0, 0))
    return pl.pallas_call(
        functools.partial(_expert_kernel, rows=rows),
        grid=(NE,),
        in_specs=[xspec, xspec, wspec, wspec,
                  pl.BlockSpec((1, 1, D, dff), lambda e: (layer, e, 0, 0)),
                  pl.BlockSpec((1, 1, D, dff), lambda e: (layer, e, 0, 0)),
                  pl.BlockSpec((1, 1, dff, D), lambda e: (layer, e, 0, 0))],
        out_specs=[xspec, xspec],
        out_shape=[_sds((NE, rows, D), BF16), _sds((NE, rows, D), BF16)],
        compiler_params=_params(1),
        name="moe_experts",
    )(xs_c, xs_l, w_c, w_l, wg, wu, wd)


def _scatter_kernel(idx_ref, y_ref, x1_ref, mod_ref, o_ref, *, eb, cap, n, rb, row_base, row_mul, token_major):
    eg = pl.program_id(1)
    last = pl.num_programs(1) - 1
    r = row_base + row_mul * pl.program_id(0)
    g2 = mod_ref[pl.ds(r, 1), 5 * D:6 * D]
    tn = min(n, 512)
    for i in range(rb):
        ys = jnp.concatenate([y_ref[k, i * cap:(i + 1) * cap, :] for k in range(eb)], axis=0)
        for t in range(n // tn):
            rows = slice(i * n + t * tn, i * n + (t + 1) * tn)
            if token_major:
                io = lax.broadcasted_iota(jnp.int32, (tn, cap), 1)
                shift = (128 - eg * eb) & 127
                pt = pltpu.roll(idx_ref[rows, :], shift, axis=1)
                sels = [(pt[:, k:k + 1] == io).astype(F32).astype(BF16) for k in range(eb)]
                part = _dot(jnp.concatenate(sels, axis=1), ys)
            else:
                io = lax.broadcasted_iota(jnp.int32, (cap, tn), 0)
                sels = [(idx_ref[pl.ds(eg * eb + k, 1), rows] == io).astype(F32).astype(BF16) for k in range(eb)]
                part = _dot_tn(jnp.concatenate(sels, axis=0), ys)
            o_ref[rows, :] = jnp.where(eg == 0, part, o_ref[rows, :] + part)

    @pl.when(eg == last)
    def _():
        o_ref[...] = x1_ref[...] + g2 * o_ref[...]


def _scatter(idx, y, x1, mod, *, nreq, n, cap, eb, row_base, row_mul, rb=1):
    tp = x1.shape[0]
    assert rb == 1 or row_mul == 0
    token_major = idx.shape[0] == tp
    idx_spec = (pl.BlockSpec((rb * n, 128), lambda b, g: (b, 0)) if token_major
                else pl.BlockSpec((NE, rb * n), lambda b, g: (0, b)))
    return pl.pallas_call(
        functools.partial(_scatter_kernel, eb=eb, cap=cap, n=n, rb=rb, row_base=row_base, row_mul=row_mul,
                          token_major=token_major),
        grid=(nreq // rb, NE // eb),
        in_specs=[idx_spec,
                  pl.BlockSpec((eb, rb * cap, D), lambda b, g: (g, b, 0)),
                  pl.BlockSpec((rb * n, D), lambda b, g: (b, 0)), _full((8, 6 * D))],
        out_specs=pl.BlockSpec((rb * n, D), lambda b, g: (b, 0)),
        out_shape=_sds((tp, D)),
        compiler_params=_params(2),
        name="moe_scatter",
    )(idx, y, x1, mod)


def _rope_tables(n):
    nf = 8
    t = np.arange(n)
    rowp = (t // GRID_W).astype(np.float32)
    colp = (t % GRID_W).astype(np.float32)
    inv = (np.float32(ROPE_BASE) ** (-np.arange(nf, dtype=np.float32) / np.float32(nf))).astype(np.float32)
    lane = np.arange(GW)
    c32 = lane % 32
    pos = np.where((c32 < 16)[None, :], rowp[:, None], colp[:, None]).astype(np.float32)
    ang = (pos * inv[(c32 % 8)][None, :]).astype(np.float32).astype(np.float64)
    sign = np.where((lane % 16) < 8, -1.0, 1.0)[None, :]
    return jnp.asarray(np.cos(ang), F32), jnp.asarray(np.sin(ang) * sign, F32)


def _block_diag(w):
    nb, bi, bo = w.shape
    return (jnp.eye(nb, dtype=w.dtype)[:, None, :, None] * w[:, :, None, :]).reshape(nb * bi, nb * bo)


def _layer_params(l, w_in, ml_gate_b, na_qn_g, na_kn_g, df_qn_g, df_kn_g, rg_wa, rg_wx, rg_ba, rg_bx,
                  df_lq1, df_lk1, df_lq2, df_lk2, df_subln_g, w_out, router_w):
    wi = w_in[l]
    w_r = jnp.concatenate([wi[:, 0:1024], wi[:, 1040:3088], wi[:, 1024:1040],
                           jnp.zeros((D, PROJ_PAD - 3088), F32)], axis=1).astype(BF16)
    gate_b = jnp.pad(ml_gate_b[l], (0, 128 - 16)).reshape(1, 128)
    qkg = jnp.stack([jnp.tile(na_qn_g[l], NH), jnp.tile(na_kn_g[l], NH),
                     jnp.tile(df_qn_g[l], 2 * NH), jnp.tile(df_kn_g[l], 2 * NH)])
    wbd = jnp.concatenate([_block_diag(rg_wa[l, 0]), _block_diag(rg_wx[l, 0]),
                           _block_diag(rg_wa[l, 1]), _block_diag(rg_wx[l, 1])], axis=1).astype(BF16)
    rg_bias = jnp.concatenate([rg_ba[l, 0], rg_bx[l, 0], rg_ba[l, 1], rg_bx[l, 1]]).reshape(1, 4 * GW)
    lamp = jnp.stack([df_lq1[l], df_lk1[l], df_lq2[l], df_lk2[l]])
    sub = jnp.tile(df_subln_g[l], NH).reshape(1, GW)
    rw = jnp.pad(router_w[l], ((0, 0), (0, 128 - NE))).astype(BF16)
    return dict(w_r=w_r, gate_b=gate_b, qkg=qkg, wbd=wbd, rg_bias=rg_bias, lamp=lamp, sub=sub,
                wout=w_out[l].astype(BF16), rw=rw)


def kernel(x_prompt, x_sample, cache_na_k, cache_na_v, cache_df_k, cache_df_v, state_ml_c, state_ml_n, state_ml_m, state_rg_h, c, c_ctx, norm1_g, norm2_g, w_mod, b_mod, w_in, ml_gate_b, ml_norm_g, na_qn_g, na_kn_g, na_rpb, rg_conv_w, rg_conv_b, rg_wa, rg_ba, rg_wx, rg_bx, rg_lam, df_qn_g, df_kn_g, df_lq1, df_lk1, df_lq2, df_lk2, df_subln_g, w_out, router_w, moe_wg, moe_wu, moe_wd):
    nb, seq, _ = x_prompt.shape
    db, dseq, _ = x_sample.shape
    depth = w_in.shape[0]
    past = cache_na_k.shape[2]
    tm = 512
    cap_c = 2 * seq // NE
    cap_l = 2 * dseq // NE

    cv = jnp.concatenate([c_ctx[None, :], c, jnp.zeros((8 - 1 - db, D), F32)], axis=0)
    mod_all = _modulation(cv, w_mod, b_mod)
    rope = _rope_tables(dseq)

    xc = x_prompt.reshape(nb * seq, D)
    xl = x_sample.reshape(db * dseq, D)
    ctx_out = []
    caches = None
    for l in range(depth):
        lam_init = 0.8 - 0.6 * math.exp(-0.3 * l)
        p = _layer_params(l, w_in, ml_gate_b, na_qn_g, na_kn_g, df_qn_g, df_kn_g, rg_wa, rg_wx, rg_ba, rg_bx,
                          df_lq1, df_lk1, df_lq2, df_lk2, df_subln_g, w_out, router_w)
        mod = mod_all[l]
        g1 = norm1_g[l].reshape(1, D)
        g2 = norm2_g[l].reshape(1, D)
        mlg = ml_norm_g[l].reshape(1, GW)
        cb = rg_conv_b[l].reshape(1, GW)
        tb = _rpb_table(na_rpb[l], rows=dseq // GRID_W, rps=NA_RPS)

        nt_c = nb * seq // tm
        ml, gates, naq, rg, dfq, *caches = _proj_in(xc, mod, g1, p["w_r"], p["gate_b"], p["qkg"], caches=caches,
                                                    cache_shape=(nb, depth, seq, GW), layer=l,
                                                    row_base=0, row_div=nt_c, tm=tm)
        hf, hb, cn_c, m_c = _mlstm(ml, gates, nreq=nb, nc=seq // CHUNK)
        y_na, o_df = _ctx_attn(naq, dfq, caches, l, p["lamp"], lam_init, nreq=nb, n=seq)
        y_rg, rg_fin = _rglru(rg, rg_conv_w[l], cb, p["wbd"], p["rg_bias"], rg_lam[l],
                              jnp.zeros((nb, 2, GW), F32), nreq=nb, n=seq)
        x1_c, hn2_c, lg_c = _merge(hf, hb, ml, y_na, y_rg, o_df, xc, mod, mlg, p["sub"], p["wout"], g2, p["rw"],
                                   lam_init, row_base=0, row_div=nt_c, tm=tm)
        pos_c, aff_c = _route(lg_c, nreq=nb, n=seq, cap=cap_c, rb=8, token_major_copy=False)
        xs_c, w_c = _gather(pos_c, aff_c, hn2_c, nreq=nb, n=seq, cap=cap_c, eb=NE, rb=4)
        ctx_out.append((cn_c, m_c, rg_fin))

        tiles_req = dseq // tm
        ml, gates, na, rg, df = _proj_in(xl, mod, g1, p["w_r"], p["gate_b"], p["qkg"], rope=rope,
                                         row_base=1, row_div=tiles_req, tm=tm)
        cn0 = _mlstm_pack_state(state_ml_c[:, l], state_ml_n[:, l])
        m0 = jnp.broadcast_to(state_ml_m[:, l].reshape(db, 8, 1), (db, 8, 128))
        hf, hb, _, _ = _mlstm(ml, gates, (cn0, m0), nreq=db, nc=dseq // CHUNK)
        y_na = _na_lat(na, cache_na_k[:, l].reshape(db, past, GW), cache_na_v[:, l].reshape(db, past, GW), tb,
                       nreq=db, n=dseq, rps=NA_RPS)
        o_df = _df_lat(df, cache_df_k[:, l].reshape(db, past, GW), cache_df_v[:, l].reshape(db, past, GW),
                       p["lamp"], lam_init, nreq=db, n=dseq)
        y_rg, _ = _rglru(rg, rg_conv_w[l], cb, p["wbd"], p["rg_bias"], rg_lam[l], state_rg_h[:, l],
                         nreq=db, n=dseq)
        x1_l, hn2_l, lg_l = _merge(hf, hb, ml, y_na, y_rg, o_df, xl, mod, mlg, p["sub"], p["wout"], g2, p["rw"],
                                   lam_init, row_base=1, row_div=tiles_req, tm=tm)
        pos_l, aff_l, post_l = _route(lg_l, nreq=db, n=dseq, cap=cap_l, rb=1, token_major_copy=True)
        xs_l, w_l = _gather(pos_l, aff_l, hn2_l, nreq=db, n=dseq, cap=cap_l, eb=2)

        y_c, y_l = _experts(xs_c, xs_l, w_c, w_l, moe_wg, moe_wu, moe_wd, l)
        xc = _scatter(pos_c, y_c, x1_c, mod, nreq=nb, n=seq, cap=cap_c, eb=NE, row_base=0, row_mul=0, rb=4)
        xl = _scatter(post_l, y_l, x1_l, mod, nreq=db, n=dseq, cap=cap_l, eb=4, row_base=1, row_mul=1)

    y_prompt = xc.reshape(nb, seq, D)
    y_sample = xl.reshape(db, dseq, D)
    st = lambda f: jnp.stack([f(o) for o in ctx_out], axis=1)
    na_k = caches[0].reshape(nb, depth, seq, NH, HD)
    na_v = caches[1].reshape(nb, depth, seq, NH, HD)
    df_k = caches[2].reshape(nb, depth, seq, NH, 2, HD // 2)
    df_v = caches[3].reshape(nb, depth, seq, NH, HD)
    ml_c = st(lambda o: _mlstm_unpack_state(o[0])[0])
    ml_n = st(lambda o: _mlstm_unpack_state(o[0])[1])
    ml_m = st(lambda o: o[1][:, :, 0].reshape(nb, 2, NH))
    rg_h = st(lambda o: o[2])
    return (y_prompt, y_sample, na_k, na_v, df_k, df_v, ml_c, ml_n, ml_m, rg_h)
```

```python
import functools
import math

import numpy as np
import jax
import jax.numpy as jnp
from jax import lax
from jax.experimental import pallas as pl
from jax.experimental.pallas import tpu as pltpu

F32 = jnp.float32
BF16 = jnp.bfloat16

D = 1024
GW = 256
NH = 4
HD = 64
NE = 16
EPS = 1e-6
CHUNK = 256
GRID_W = 64
NA_ROWS = 8
NA_COLS = 16
DF_HEAD_GROUP = 1
SUB_ROWS = 256
NA_RPS = 4
RG_C = 8.0
ROPE_BASE = 10000.0
PROJ_PAD = 3200
VMEM_LIMIT_BYTES = 56 * 1024 * 1024
NEG_BIG = -1e30
LOG2E = 1.4426950408889634


def _bf(x):
    return x.astype(BF16)


def _dot(a, b):
    return jnp.dot(a, b, preferred_element_type=F32)


def _dot_nt(a, b):
    return lax.dot_general(a, b, (((1,), (1,)), ((), ())), preferred_element_type=F32)


def _dot_tn(a, b):
    return lax.dot_general(a, b, (((0,), (0,)), ((), ())), preferred_element_type=F32)


def _split3(x):
    p0 = _bf(x)
    r1 = x - p0.astype(F32)
    p1 = _bf(r1)
    return p0, p1, _bf(r1 - p1.astype(F32))


def _params(n_axes):
    return pltpu.CompilerParams(dimension_semantics=("arbitrary",) * n_axes,
                                vmem_limit_bytes=VMEM_LIMIT_BYTES)


def _full(shape):
    return pl.BlockSpec(shape, lambda *_: (0,) * len(shape))


def _sds(shape, dtype=F32):
    return jax.ShapeDtypeStruct(shape, dtype)


def _softplus(x):
    return jnp.maximum(x, 0.0) + jnp.log1p(jnp.exp(-jnp.abs(x)))


def _log_sigmoid(x):
    return jnp.minimum(x, 0.0) - jnp.log1p(jnp.exp(-jnp.abs(x)))


def _seg_rms(x, nseg, g_row):
    seg = x.shape[-1] // nseg
    lane = lax.broadcasted_iota(jnp.int32, x.shape, 1)
    x2 = x * x
    tot = jnp.zeros_like(x)
    for s in range(nseg):
        m = (lane >= s * seg) & (lane < (s + 1) * seg)
        t = jnp.sum(jnp.where(m, x2, 0.0), axis=-1, keepdims=True)
        tot = jnp.where(m, t, tot)
    return x * lax.rsqrt(tot * (1.0 / seg) + EPS) * g_row


def _mod_row(pid, row_base, row_div):
    return row_base + pid // row_div


def _mod_kernel(cv_ref, w_ref, b_ref, o_ref):
    cv = cv_ref[...]
    s = cv * jax.nn.sigmoid(cv)
    o_ref[0] = _dot(_bf(s), _bf(w_ref[0])) + b_ref[0]


def _modulation(cv, w_mod, b_mod):
    nl = w_mod.shape[0]
    tn = 1536
    return pl.pallas_call(
        _mod_kernel,
        grid=(nl, 6 * D // tn),
        in_specs=[_full((8, D)),
                  pl.BlockSpec((1, D, tn), lambda l, j: (l, 0, j)),
                  pl.BlockSpec((1, 1, tn), lambda l, j: (l, 0, j))],
        out_specs=pl.BlockSpec((1, 8, tn), lambda l, j: (l, 0, j)),
        out_shape=_sds((nl, 8, 6 * D)),
        compiler_params=_params(2),
        name="modulation",
    )(cv, w_mod, b_mod.reshape(nl, 1, 6 * D))


def _rope(x, cos_t, sin_t):
    lane = lax.broadcasted_iota(jnp.int32, x.shape, 1)
    first = (lane & 15) < 8
    sw = jnp.where(first, pltpu.roll(x, GW - 8, axis=1), pltpu.roll(x, 8, axis=1))
    return x * cos_t + sw * sin_t


def _proj_kernel(*refs, row_base, row_div, layer):
    ctx = layer is not None
    if ctx:
        (x_ref, mod_ref, g1_ref, w_ref, gb_ref, qkg_ref) = refs[:6]
        (ml_ref, gate_ref, naq_ref, rg_ref, dfq_ref, nk_ref, nv_ref, dk_ref, dv_ref) = refs[-9:]
        cache_layer = layer if nk_ref.shape[1] > 1 else 0
    else:
        (x_ref, mod_ref, g1_ref, w_ref, gb_ref, qkg_ref, cos_ref, sin_ref,
         ml_ref, gate_ref, na_ref, rg_ref, df_ref) = refs
    r = _mod_row(pl.program_id(0), row_base, row_div)
    sh = mod_ref[pl.ds(r, 1), 0:D]
    sc = mod_ref[pl.ds(r, 1), D:2 * D]
    for sub in range(x_ref.shape[0] // SUB_ROWS):
        rows = slice(sub * SUB_ROWS, (sub + 1) * SUB_ROWS)
        x = x_ref[rows, :]
        ms = jnp.mean(x * x, axis=-1, keepdims=True)
        y = x * lax.rsqrt(ms + EPS) * g1_ref[...]
        hn = _bf(y * (1.0 + sc) + sh)

        ml = _dot(hn, w_ref[:, 0:1024])
        ml_ref[rows, 0:256] = ml[:, 0:256]
        ml_ref[rows, 256:512] = ml[:, 256:512] * (HD ** -0.5)
        ml_ref[rows, 512:1024] = ml[:, 512:1024]

        gz = _dot(hn, w_ref[:, 3072:3200]) + gb_ref[...]
        lane = lax.broadcasted_iota(jnp.int32, gz.shape, 1)
        gate_ref[rows, :] = jnp.where(((lane >> 2) & 1) == 1, _log_sigmoid(gz), gz)

        rg_ref[rows, :] = _dot(hn, w_ref[:, 1792:2304])

        nz = _dot(hn, w_ref[:, 1024:1792])
        nq = _seg_rms(nz[:, 0:256], NH, qkg_ref[0:1, :])
        nk = _seg_rms(nz[:, 256:512], NH, qkg_ref[1:2, :])
        dz = _dot(hn, w_ref[:, 2304:3072])
        dq = _seg_rms(dz[:, 0:256], 2 * NH, qkg_ref[2:3, :])
        dk = _seg_rms(dz[:, 256:512], 2 * NH, qkg_ref[3:4, :])
        if ctx:
            naq_ref[rows, :] = nq
            dfq_ref[rows, :] = dq
            for c_ref, val in ((nk_ref, nk), (nv_ref, nz[:, 512:768]), (dk_ref, dk), (dv_ref, dz[:, 512:768])):
                c_ref[sub, cache_layer] = val
                for other in range(c_ref.shape[1]):
                    if other != cache_layer:
                        c_ref[sub, other] = jnp.zeros_like(val)
        else:
            cos_t = cos_ref[rows, :]
            sin_t = sin_ref[rows, :]
            na_ref[rows, 0:256] = nq
            na_ref[rows, 256:512] = nk
            na_ref[rows, 512:768] = nz[:, 512:768]
            df_ref[rows, 0:256] = _rope(dq, cos_t, sin_t)
            df_ref[rows, 256:512] = _rope(dk, cos_t, sin_t)
            df_ref[rows, 512:768] = dz[:, 512:768]


def _proj_in(x, mod, g1, w_r, gate_b, qkg, *, rope=None, caches=None, cache_shape=None, layer=None,
             row_base, row_div, tm=512):
    tp = x.shape[0]
    in_specs = [pl.BlockSpec((tm, D), lambda i: (i, 0)), _full((8, 6 * D)), _full((1, D)),
                _full((D, PROJ_PAD)), _full((1, 128)), _full((4, GW))]
    args = [x, mod, g1, w_r, gate_b, qkg]
    tok = lambda n: pl.BlockSpec((tm, n), lambda i: (i, 0))
    aliases = {}
    if layer is None:
        tiles = rope[0].shape[0] // tm
        in_specs += [pl.BlockSpec((tm, GW), lambda i: (i % tiles, 0))] * 2
        args += list(rope)
        out_specs = [tok(1024), tok(128), tok(768), tok(512), tok(768)]
        out_shape = [_sds((tp, n)) for n in (1024, 128, 768, 512, 768)]
    else:
        if caches is None:
            depth, seq = cache_shape[1], cache_shape[2]
            cspec = pl.BlockSpec((tm // seq, depth, seq, GW), lambda i: (i, 0, 0, 0))
        else:
            cache_shape, seq = caches[0].shape, caches[0].shape[2]
            cspec = pl.BlockSpec((tm // seq, 1, seq, GW), lambda i: (i, layer, 0, 0))
            in_specs += [pl.BlockSpec(memory_space=pl.ANY)] * 4
            args += list(caches)
            aliases = {6 + i: 5 + i for i in range(4)}
        assert seq == SUB_ROWS
        out_specs = [tok(1024), tok(128), tok(GW), tok(512), tok(GW)] + [cspec] * 4
        out_shape = [_sds((tp, n)) for n in (1024, 128, GW, 512, GW)] + [_sds(cache_shape)] * 4
    return pl.pallas_call(
        functools.partial(_proj_kernel, row_base=row_base, row_div=row_div, layer=layer),
        grid=(tp // tm,),
        in_specs=in_specs,
        out_specs=out_specs,
        out_shape=out_shape,
        input_output_aliases=aliases,
        compiler_params=_params(1),
        name="proj_in",
    )(*args)


def _mlstm_chunk(dir_refs, st_s, m_s):
    lc = CHUNK
    row = lax.broadcasted_iota(jnp.int32, (lc, lc), 0)
    col = lax.broadcasted_iota(jnp.int32, (lc, lc), 1)
    ones = jnp.ones((lc, 128), BF16)
    hs = [slice(h * HD, (h + 1) * HD) for h in range(NH)]
    chains = [(d, h) for d in range(2) for h in range(NH)]
    tris, bc_cols, r_rows, k_ts, k_tbs, q_alls, v_alls = [], [], [], [], [], [], []
    for d, (q_ref, k_ref, v_ref, g_ref, _) in enumerate(dir_refs):
        g = g_ref[...]
        gt = g.T
        tri = (row >= col) if d == 0 else (row <= col)
        tri_t = (col >= row) if d == 0 else (col <= row)
        g_parts = _split3(g)
        gt_parts = _split3(gt)
        tri_b = tri.astype(F32).astype(BF16)
        tri_tb = tri_t.astype(F32).astype(BF16)
        bc_cols.append(_dot(tri_b, g_parts[0]) + _dot(tri_b, g_parts[1]) + _dot(tri_b, g_parts[2]))
        bc_row = _dot(gt_parts[0], tri_tb) + _dot(gt_parts[1], tri_tb) + _dot(gt_parts[2], tri_tb)
        r_rows.append(gt[d * 8:d * 8 + NH, :] - bc_row[d * 8 + NH:d * 8 + 2 * NH, :])
        tris.append(tri)
        k_t = k_ref[...].T
        k_ts.append(k_t)
        k_tbs.append(_bf(k_t))
        q_alls.append(_bf(q_ref[...]))
        v_alls.append(_bf(v_ref[...]))
    ms = [m_s[d * NH + h:d * NH + h + 1, 0:1] for d, h in chains]
    rms = [jnp.where(tris[d], r_rows[d][h:h + 1, :], -jnp.inf) for d, h in chains]
    big_rs = [jnp.maximum(jnp.max(rm, axis=-1, keepdims=True), m) for rm, m in zip(rms, ms)]
    ss = [_bf(_dot(q_alls[d][:, hs[h]], k_tbs[d][hs[h], :]) * jnp.exp(rms[i] - big_rs[i]))
          for i, (d, h) in enumerate(chains)]
    v2s = [v_alls[d][:, (h // 2) * 128:(h // 2 + 1) * 128] for d, h in chains]
    es = [jnp.exp(m - big_r) for m, big_r in zip(ms, big_rs)]
    c_cols = [bc_cols[d][:, d * 8 + NH + h:d * 8 + NH + h + 1] for d, h in chains]
    tots = [_dot(ss[i], v2s[i]) + es[i] * _dot(q_alls[d][:, hs[h]], _bf(st_s[d, h, 0]))
            for i, (d, h) in enumerate(chains)]
    dens = [_dot(ss[i], ones) + es[i] * _dot(q_alls[d][:, hs[h]], _bf(st_s[d, h, 1]))
            for i, (d, h) in enumerate(chains)]
    outs = [tots[i] / jnp.maximum(jnp.abs(dens[i]), jnp.exp(-(c_cols[i] + big_rs[i]))) for i in range(len(chains))]
    for i, (d, h) in enumerate(chains):
        last = lc - 1 if d == 0 else 0
        r_last = big_rs[i][last:last + 1, :]
        kw = _bf(k_ts[d][hs[h], :] * jnp.exp(r_rows[d][h:h + 1, :] - r_last))
        gdec = jnp.exp(ms[i] - r_last)
        st_s[d, h, 0] = gdec * st_s[d, h, 0] + _dot(kw, v2s[i])
        st_s[d, h, 1] = gdec * st_s[d, h, 1] + _dot(kw, ones)
        m_s[d * NH + h:d * NH + h + 1, :] = jnp.broadcast_to(c_cols[i][last:last + 1, :] + r_last, (1, 128))
    lane = lax.broadcasted_iota(jnp.int32, (lc, 128), 1)
    for d, refs in enumerate(dir_refs):
        o = outs[d * NH:(d + 1) * NH]
        refs[4][...] = jnp.concatenate([jnp.where(lane < HD, o[2 * p], o[2 * p + 1]) for p in range(NH // 2)],
                                       axis=-1)


def _mlstm_kernel(qf, kf, vf, gf, qb, kb, vb, gb, *rest):
    hf_ref, hb_ref, c_out, n_out, m_out, st_s, m_s = rest[-7:]
    j = pl.program_id(1)

    @pl.when(j == 0)
    def _():
        if len(rest) == 9:
            st_s[...] = rest[0][0]
            m_s[...] = rest[1][0]
        else:
            st_s[...] = jnp.zeros_like(st_s)
            m_s[...] = jnp.zeros_like(m_s)

    _mlstm_chunk(((qf, kf, vf, gf, hf_ref), (qb, kb, vb, gb, hb_ref)), st_s, m_s)

    @pl.when(j == pl.num_programs(1) - 1)
    def _():
        diag = (lax.broadcasted_iota(jnp.int32, (HD, 128), 0) == lax.broadcasted_iota(jnp.int32, (HD, 128), 1))
        for d in range(2):
            for h in range(NH):
                lo = HD * (h % 2)
                c_out[0, d, h] = st_s[d, h, 0][:, lo:lo + HD]
                n_row = jnp.sum(jnp.where(diag, st_s[d, h, 1], 0.0), axis=0, keepdims=True)
                n_out[0, d, h:h + 1, :] = n_row[:, 0:HD]
        m_out[0] = m_s[...]


def _mlstm_pack_state(c, n):
    z = jnp.zeros_like(c)
    odd = (jnp.arange(NH) % 2 == 1)[None, None, :, None, None]
    c_pair = jnp.where(odd, jnp.concatenate([z, c], axis=-1), jnp.concatenate([c, z], axis=-1))
    n_rep = jnp.broadcast_to(n[..., None], n.shape + (128,))
    return jnp.stack([c_pair, n_rep], axis=3)


def _mlstm(mlz, gates, state=None, *, nreq, nc):
    tp = mlz.shape[0]
    st_spec = pl.BlockSpec((1, 2, NH, 2, HD, 128), lambda r, j: (r, 0, 0, 0, 0, 0))

    def fwd(col):
        return lambda r, j: (r * nc + j, col)

    def bwd(col):
        return lambda r, j: (r * nc + nc - 1 - j, col)

    in_specs = []
    for mk in (fwd, bwd):
        in_specs += [pl.BlockSpec((CHUNK, GW), mk(0)), pl.BlockSpec((CHUNK, GW), mk(1)),
                     pl.BlockSpec((CHUNK, GW), mk(2)), pl.BlockSpec((CHUNK, 128), mk(0))]
    args = [mlz, mlz, mlz, gates, mlz, mlz, mlz, gates]
    if state is not None:
        in_specs += [st_spec, pl.BlockSpec((1, 8, 128), lambda r, j: (r, 0, 0))]
        args += list(state)
    return pl.pallas_call(
        _mlstm_kernel,
        grid=(nreq, nc),
        in_specs=in_specs,
        out_specs=[pl.BlockSpec((CHUNK, GW), fwd(0)), pl.BlockSpec((CHUNK, GW), bwd(0)),
                   pl.BlockSpec((1, 2, NH, HD, HD), lambda r, j: (r, 0, 0, 0, 0)),
                   pl.BlockSpec((1, 2, NH, HD), lambda r, j: (r, 0, 0, 0)),
                   pl.BlockSpec((1, 8, 128), lambda r, j: (r, 0, 0))],
        out_shape=[_sds((tp, GW)), _sds((tp, GW)), _sds((nreq, 2, NH, HD, HD)), _sds((nreq, 2, NH, HD)),
                   _sds((nreq, 8, 128))],
        scratch_shapes=[pltpu.VMEM((2, NH, 2, HD, 128), F32), pltpu.VMEM((8, 128), F32)],
        compiler_params=_params(2),
        name="mlstm",
    )(*args)


def _diff_lambda(lam_ref, lam_init):
    lp = lam_ref[...]
    a = jnp.exp(jnp.sum(lp[0:1, :] * lp[1:2, :], axis=-1, keepdims=True))
    b = jnp.exp(jnp.sum(lp[2:3, :] * lp[3:4, :], axis=-1, keepdims=True))
    return a - b + lam_init


def _softmax_pv(jobs):
    ss = [_dot(q, k_t) for q, k_t, _ in jobs]
    es = [jnp.exp2(s - jnp.max(s, axis=-1, keepdims=True)) for s in ss]
    return [_dot(_bf(e), v) / jnp.sum(e, axis=-1, keepdims=True) for e, (_, _, v) in zip(es, jobs)]


def _ctx_attn_kernel(nq, nk, nv, dq, dk, dv, lam_ref, ona_ref, odf_ref, *, lam_init):
    lam = _diff_lambda(lam_ref, lam_init)
    n = nk.shape[2]
    for i in range(nk.shape[0]):
        rows = slice(i * n, (i + 1) * n)
        nk_t = _bf(nk[i, 0].T)
        dk_t = _bf(dk[i, 0].T)
        jobs = []
        for h in range(NH):
            hs = slice(h * HD, (h + 1) * HD)
            jobs.append((_bf(nq[rows, hs] * (HD ** -0.5 * LOG2E)), nk_t[hs, :], _bf(nv[i, 0, :, hs])))
        for h in range(NH):
            vh = _bf(dv[i, 0, :, h * HD:(h + 1) * HD])
            for c in range(2):
                cs = slice(h * HD + c * 32, h * HD + (c + 1) * 32)
                jobs.append((_bf(dq[rows, cs] * (32 ** -0.5 * LOG2E)), dk_t[cs, :], vh))
        outs = _softmax_pv(jobs)
        ona_ref[rows, :] = _bf(jnp.concatenate(outs[0:NH], axis=-1))
        odf_ref[rows, :] = jnp.concatenate([outs[NH + 2 * h] - lam * outs[NH + 2 * h + 1] for h in range(NH)],
                                           axis=-1)


def _ctx_attn(naq, dfq, caches, layer, lamp, lam_init, *, nreq, n, rb=1):
    tp = naq.shape[0]
    qs = pl.BlockSpec((rb * n, GW), lambda b: (b, 0))
    cs = pl.BlockSpec((rb, 1, n, GW), lambda b: (b, layer, 0, 0))
    return pl.pallas_call(
        functools.partial(_ctx_attn_kernel, lam_init=lam_init),
        grid=(nreq // rb,),
        in_specs=[qs, cs, cs, qs, cs, cs, _full((4, 32))],
        out_specs=[qs, qs],
        out_shape=[_sds((tp, GW), BF16), _sds((tp, GW))],
        compiler_params=_params(1),
        name="ctx_attn",
    )(naq, caches[0], caches[1], dfq, caches[2], caches[3], lamp)


def _rpb_kernel(rpb_ref, o_ref, *, rows, rps):
    h = pl.program_id(0)
    shape = (GRID_W, NA_ROWS * GRID_W)
    lane = lax.broadcasted_iota(jnp.int32, shape, 1)
    cq = lax.broadcasted_iota(jnp.int32, shape, 0)
    ck = lane & (GRID_W - 1)
    c0 = jnp.clip(cq - NA_COLS // 2, 0, GRID_W - NA_COLS)
    ok = (ck >= c0) & (ck < c0 + NA_COLS)
    ncol = 2 * NA_COLS - 1
    ndr = 2 * NA_ROWS - 1
    x = lax.broadcasted_iota(jnp.int32, (1, 2 * GRID_W), 1)
    didx = jnp.clip(x - GRID_W, -(NA_COLS - 1), NA_COLS - 1) + (NA_COLS - 1)
    tiles = []
    for dr in range(ndr):
        frow = jnp.zeros((1, 2 * GRID_W), F32)
        for b in range(ncol):
            frow = jnp.where(didx == b, rpb_ref[(h * ndr + dr) * ncol + b], frow)
        rolled = pltpu.roll(jnp.broadcast_to(frow, (GRID_W, 2 * GRID_W)), GRID_W, axis=1, stride=1, stride_axis=0)
        tiles.append(rolled[:, 0:GRID_W])
    tbs = [jnp.where(ok, jnp.concatenate(tiles[s:s + NA_ROWS], axis=-1) * LOG2E, NEG_BIG) for s in range(NA_ROWS)]
    wrows = NA_ROWS + rps
    steps = rows // rps
    for t, j in enumerate((0, 1, steps - 1)):
        w0 = min(max(j * rps - NA_ROWS // 2, 0), rows - wrows)
        for a in range(rps):
            r = j * rps + a
            r0 = min(max(r - NA_ROWS // 2, 0), rows - NA_ROWS)
            off = (r0 - w0) * GRID_W
            rest = rps * GRID_W - off
            pieces = [tbs[r0 - r + NA_ROWS - 1]]
            if off:
                pieces = [jnp.full((GRID_W, off), NEG_BIG, F32)] + pieces
            if rest:
                pieces = pieces + [jnp.full((GRID_W, rest), NEG_BIG, F32)]
            o_ref[0, t, a * GRID_W:(a + 1) * GRID_W, :] = jnp.concatenate(pieces, axis=-1)


def _na_step_offsets(j, rows, rps):
    w0 = min(max(j * rps - NA_ROWS // 2, 0), rows - NA_ROWS - rps)
    return [(min(max(r - NA_ROWS // 2, 0), rows - NA_ROWS) - r,
             min(max(r - NA_ROWS // 2, 0), rows - NA_ROWS) - w0) for r in range(j * rps, (j + 1) * rps)]


def _rpb_table(rpb, *, rows, rps):
    steps = rows // rps
    assert all(_na_step_offsets(j, rows, rps) == _na_step_offsets(1, rows, rps) for j in range(1, steps - 1))
    shape = (rps * GRID_W, (NA_ROWS + rps) * GRID_W)
    return pl.pallas_call(
        functools.partial(_rpb_kernel, rows=rows, rps=rps),
        grid=(NH,),
        in_specs=[pl.BlockSpec(memory_space=pltpu.SMEM)],
        out_specs=pl.BlockSpec((1, 3) + shape, lambda h: (h, 0, 0, 0)),
        out_shape=_sds((NH, 3) + shape),
        compiler_params=_params(1),
        name="rpb_table",
    )(rpb.reshape(-1))


def _na_lat_kernel(q_ref, k_ref, v_ref, kc_ref, vc_ref, tb_ref, o_ref, k_s, v_s, kct_s, vc_s, *, rows, rps):
    j = pl.program_id(1)

    @pl.when(j == 0)
    def _():
        k_s[...] = _bf(k_ref[...])
        v_s[...] = _bf(v_ref[...])
        kct_s[...] = _bf(kc_ref[0].T)
        vc_s[...] = _bf(vc_ref[0])

    wrows = NA_ROWS + rps
    win = wrows * GRID_W
    w0 = jnp.clip(j * rps - NA_ROWS // 2, 0, rows - wrows)
    start = pl.multiple_of(w0 * GRID_W, GRID_W)
    hss = [slice(h * HD, (h + 1) * HD) for h in range(NH)]
    qs = [_bf(q_ref[:, hs] * (HD ** -0.5 * LOG2E)) for hs in hss]
    s_cs = [_dot(q, kct_s[hs, :]) for q, hs in zip(qs, hss)]
    s_ls = [_dot_nt(q, k_s[pl.ds(start, win), hs]) + tb_ref[h, 0] for h, (q, hs) in enumerate(zip(qs, hss))]
    ms = [jnp.maximum(jnp.max(s_c, axis=-1, keepdims=True), jnp.max(s_l, axis=-1, keepdims=True))
          for s_c, s_l in zip(s_cs, s_ls)]
    e_cs = [jnp.exp2(s_c - m) for s_c, m in zip(s_cs, ms)]
    e_ls = [jnp.exp2(s_l - m) for s_l, m in zip(s_ls, ms)]
    outs = []
    for e_c, e_l, hs in zip(e_cs, e_ls, hss):
        den = jnp.sum(e_c, axis=-1, keepdims=True) + jnp.sum(e_l, axis=-1, keepdims=True)
        o = _dot(_bf(e_c), vc_s[:, hs]) + _dot(_bf(e_l), v_s[pl.ds(start, win), hs])
        outs.append(o / den)
    o_ref[...] = _bf(jnp.concatenate(outs, axis=-1))


def _na_lat(na, kc, vc, tb, *, nreq, n, rps):
    tp = na.shape[0]
    rows = n // GRID_W
    past = kc.shape[1]
    steps = rows // rps
    tq = rps * GRID_W
    return pl.pallas_call(
        functools.partial(_na_lat_kernel, rows=rows, rps=rps),
        grid=(nreq, steps),
        in_specs=[pl.BlockSpec((tq, GW), lambda b, r: (b * steps + r, 0)),
                  pl.BlockSpec((n, GW), lambda b, r: (b, 1)),
                  pl.BlockSpec((n, GW), lambda b, r: (b, 2)),
                  pl.BlockSpec((1, past, GW), lambda b, r: (b, 0, 0)),
                  pl.BlockSpec((1, past, GW), lambda b, r: (b, 0, 0)),
                  pl.BlockSpec((NH, 1) + tb.shape[2:],
                               lambda b, r: (0, jnp.minimum(r, 1) + r // (steps - 1), 0, 0))],
        out_specs=pl.BlockSpec((tq, GW), lambda b, r: (b * steps + r, 0)),
        out_shape=_sds((tp, GW), BF16),
        scratch_shapes=[pltpu.VMEM((n, GW), BF16), pltpu.VMEM((n, GW), BF16),
                        pltpu.VMEM((GW, past), BF16), pltpu.VMEM((past, GW), BF16)],
        compiler_params=_params(2),
        name="na_latent",
    )(na, na, na, kc, vc, tb)


def _df_lat_kernel(q_ref, k_ref, v_ref, kc_ref, vc_ref, lam_ref, o_ref, kt_s, v_s, kct_s, vc_s, *, lam_init):
    @pl.when(pl.program_id(1) == 0)
    def _():
        kt_s[...] = _bf(k_ref[...].T)
        v_s[...] = _bf(v_ref[...])
        kct_s[...] = _bf(kc_ref[0].T)
        vc_s[...] = _bf(vc_ref[0])

    lam = _diff_lambda(lam_ref, lam_init)
    outs = []
    for h0 in range(0, NH, DF_HEAD_GROUP):
        heads = range(h0, h0 + DF_HEAD_GROUP)
        css = [slice(h * HD + c * 32, h * HD + (c + 1) * 32) for h in heads for c in range(2)]
        qs = [_bf(q_ref[:, cs] * (32 ** -0.5 * LOG2E)) for cs in css]
        s_cs = [_dot(q, kct_s[cs, :]) for q, cs in zip(qs, css)]
        s_ls = [_dot(q, kt_s[cs, :]) for q, cs in zip(qs, css)]
        ms = [jnp.maximum(jnp.max(s_c, axis=-1, keepdims=True), jnp.max(s_l, axis=-1, keepdims=True))
              for s_c, s_l in zip(s_cs, s_ls)]
        e_cs = [jnp.exp2(s_c - m) for s_c, m in zip(s_cs, ms)]
        e_ls = [jnp.exp2(s_l - m) for s_l, m in zip(s_ls, ms)]
        os = []
        for i, (e_c, e_l) in enumerate(zip(e_cs, e_ls)):
            hs = slice((h0 + i // 2) * HD, (h0 + i // 2 + 1) * HD)
            den = jnp.sum(e_c, axis=-1, keepdims=True) + jnp.sum(e_l, axis=-1, keepdims=True)
            os.append((_dot(_bf(e_c), vc_s[:, hs]) + _dot(_bf(e_l), v_s[:, hs])) / den)
        outs += [os[2 * i] - lam * os[2 * i + 1] for i in range(DF_HEAD_GROUP)]
    o_ref[...] = jnp.concatenate(outs, axis=-1)


def _df_lat(df, kc, vc, lamp, lam_init, *, nreq, n, tq=512):
    tp = df.shape[0]
    nq = n // tq
    past = kc.shape[1]
    return pl.pallas_call(
        functools.partial(_df_lat_kernel, lam_init=lam_init),
        grid=(nreq, nq),
        in_specs=[pl.BlockSpec((tq, GW), lambda b, j: (b * nq + j, 0)),
                  pl.BlockSpec((n, GW), lambda b, j: (b, 1)),
                  pl.BlockSpec((n, GW), lambda b, j: (b, 2)),
                  pl.BlockSpec((1, past, GW), lambda b, j: (b, 0, 0)),
                  pl.BlockSpec((1, past, GW), lambda b, j: (b, 0, 0)),
                  _full((4, 32))],
        out_specs=pl.BlockSpec((tq, GW), lambda b, j: (b * nq + j, 0)),
        out_shape=_sds((tp, GW)),
        scratch_shapes=[pltpu.VMEM((GW, n), BF16), pltpu.VMEM((n, GW), BF16),
                        pltpu.VMEM((GW, past), BF16), pltpu.VMEM((past, GW), BF16)],
        compiler_params=_params(2),
        name="df_latent",
    )(df, df, df, kc, vc, lamp)


def _rg_kernel(x_ref, g_ref, cw_ref, cb_ref, wbd_ref, bias_ref, lam_ref, h0_ref,
               y_ref, fin_ref, a_s, b_s, *, n):
    x = x_ref[...]
    row = lax.broadcasted_iota(jnp.int32, x.shape, 0)
    xc = cb_ref[...] + jnp.where(row >= 2, pltpu.roll(x, 2, axis=0), 0.0) * cw_ref[0:1, :]
    xc = xc + jnp.where(row >= 1, pltpu.roll(x, 1, axis=0), 0.0) * cw_ref[1:2, :]
    xc = xc + x * cw_ref[2:3, :]
    xc = xc + jnp.where(row < n - 1, pltpu.roll(x, n - 1, axis=0), 0.0) * cw_ref[3:4, :]
    z = _dot(_bf(xc), wbd_ref[...]) + bias_ref[...]
    sub = row & 7
    for d in range(2):
        rgate = jax.nn.sigmoid(z[:, 512 * d:512 * d + GW])
        igate = jax.nn.sigmoid(z[:, 512 * d + GW:512 * d + 2 * GW])
        la = -RG_C * rgate * _softplus(-lam_ref[d:d + 1, :])
        a = jnp.exp(la)
        t = jnp.tanh(la)
        b = jnp.sqrt(-2.0 * t / (1.0 - t)) * igate * xc
        for dd in (1, 2, 4):
            if d == 0:
                keep = sub >= dd
                a_sh = jnp.where(keep, pltpu.roll(a, dd, axis=0), 1.0)
                b_sh = jnp.where(keep, pltpu.roll(b, dd, axis=0), 0.0)
            else:
                keep = sub < 8 - dd
                a_sh = jnp.where(keep, pltpu.roll(a, n - dd, axis=0), 1.0)
                b_sh = jnp.where(keep, pltpu.roll(b, n - dd, axis=0), 0.0)
            b = b + a * b_sh
            a = a * a_sh
        a_s[d] = a
        b_s[d] = b

    nt = n // 8

    def body(t, carry):
        hf, hb = carry
        sf = pl.multiple_of(t * 8, 8)
        sb = pl.multiple_of((nt - 1 - t) * 8, 8)
        tf = a_s[0, pl.ds(sf, 8), :] * hf + b_s[0, pl.ds(sf, 8), :]
        tb = a_s[1, pl.ds(sb, 8), :] * hb + b_s[1, pl.ds(sb, 8), :]
        b_s[0, pl.ds(sf, 8), :] = tf
        b_s[1, pl.ds(sb, 8), :] = tb
        return tf[7:8, :], tb[0:1, :]

    hf, hb = lax.fori_loop(0, nt, body, (h0_ref[0, 0:1, :], h0_ref[0, 1:2, :]))
    fin_ref[0, 0:1, :] = hf
    fin_ref[0, 1:2, :] = hb
    gg = g_ref[...]
    cdf = 0.5 * (1.0 + jnp.tanh(math.sqrt(2.0 / math.pi) * (gg + 0.044715 * (gg * gg * gg))))
    y_ref[...] = _bf((b_s[0] + b_s[1]) * (gg * cdf))


def _rglru(rg, cw, cb, wbd, bias, lam, h0, *, nreq, n):
    tp = rg.shape[0]
    return pl.pallas_call(
        functools.partial(_rg_kernel, n=n),
        grid=(nreq,),
        in_specs=[pl.BlockSpec((n, GW), lambda b: (b, 0)), pl.BlockSpec((n, GW), lambda b: (b, 1)),
                  _full((4, GW)), _full((1, GW)), _full((GW, 4 * GW)), _full((1, 4 * GW)), _full((2, GW)),
                  pl.BlockSpec((1, 2, GW), lambda b: (b, 0, 0))],
        out_specs=[pl.BlockSpec((n, GW), lambda b: (b, 0)), pl.BlockSpec((1, 2, GW), lambda b: (b, 0, 0))],
        out_shape=[_sds((tp, GW), BF16), _sds((nreq, 2, GW))],
        scratch_shapes=[pltpu.VMEM((2, n, GW), F32), pltpu.VMEM((2, n, GW), F32)],
        compiler_params=_params(1),
        name="rglru",
    )(rg, rg, cw, cb, wbd, bias, lam, h0)


def _merge_kernel(hf_ref, hb_ref, mlo_ref, yna_ref, yrg_ref, odf_ref, x_ref, mod_ref, mlg_ref, sub_ref,
                  wout_ref, n2_ref, rw_ref, x1_ref, hn2_ref, lg_ref, *, row_base, row_div, lam_init):
    r = _mod_row(pl.program_id(0), row_base, row_div)
    y_ml = _seg_rms(hf_ref[...] + hb_ref[...], NH, mlg_ref[...]) * jax.nn.sigmoid(mlo_ref[...])
    y_df = _seg_rms(odf_ref[...], NH, sub_ref[...]) * (1.0 - lam_init)
    y = jnp.concatenate([_bf(y_ml), _bf(yna_ref[...]), _bf(yrg_ref[...]), _bf(y_df)], axis=-1)
    o = _dot(y, wout_ref[...])
    x1 = x_ref[...] + mod_ref[pl.ds(r, 1), 2 * D:3 * D] * o
    x1_ref[...] = x1
    ms = jnp.mean(x1 * x1, axis=-1, keepdims=True)
    hn = x1 * lax.rsqrt(ms + EPS) * n2_ref[...]
    hn = _bf(hn * (1.0 + mod_ref[pl.ds(r, 1), 4 * D:5 * D]) + mod_ref[pl.ds(r, 1), 3 * D:4 * D])
    hn2_ref[...] = hn
    lg_ref[...] = _dot(hn, rw_ref[...])


def _merge(hf, hb, mlz, yna, yrg, odf, x, mod, mlg, sub, wout, n2, rw, lam_init, *, row_base, row_div, tm=512):
    tp = x.shape[0]
    g = lambda c: pl.BlockSpec((tm, GW), lambda i: (i, c))
    return pl.pallas_call(
        functools.partial(_merge_kernel, row_base=row_base, row_div=row_div, lam_init=lam_init),
        grid=(tp // tm,),
        in_specs=[g(0), g(0), g(3), g(0), g(0), g(0),
                  pl.BlockSpec((tm, D), lambda i: (i, 0)), _full((8, 6 * D)), _full((1, GW)), _full((1, GW)),
                  _full((D, D)), _full((1, D)), _full((D, 128))],
        out_specs=[pl.BlockSpec((tm, D), lambda i: (i, 0)), pl.BlockSpec((tm, D), lambda i: (i, 0)),
                   pl.BlockSpec((tm, 128), lambda i: (i, 0))],
        out_shape=[_sds((tp, D)), _sds((tp, D), BF16), _sds((tp, 128))],
        compiler_params=_params(1),
        name="merge",
    )(hf, hb, mlz, yna, yrg, odf, x, mod, mlg, sub, wout, n2, rw)


def _excl_cumsum_lanes(mask):
    blk = 256
    r = lax.broadcasted_iota(jnp.int32, (blk, blk), 0)
    c = lax.broadcasted_iota(jnp.int32, (blk, blk), 1)
    tri = (r < c).astype(F32).astype(BF16)
    off = jnp.zeros((mask.shape[0], 1), F32)
    outs = []
    for i in range(mask.shape[1] // blk):
        mb = mask[:, i * blk:(i + 1) * blk]
        outs.append(_dot(_bf(mb), tri) + off)
        off = off + jnp.sum(mb, axis=-1, keepdims=True)
    return jnp.concatenate(outs, axis=-1)


def _route_kernel(lg_ref, pos_ref, aff_ref, *post_ref, cap, n, rb):
    lg = lg_ref[...].T[0:NE, :]
    ex = jnp.exp(lg - jnp.max(lg, axis=0, keepdims=True))
    aff = ex / jnp.sum(ex, axis=0, keepdims=True)
    aff_ref[...] = aff
    aff = jnp.concatenate([aff[:, i * n:(i + 1) * n] for i in range(rb)], axis=0)
    thr = jnp.zeros((rb * NE, 1), jnp.int32)
    for bit in range(30, -1, -1):
        cand = thr | (1 << bit)
        cnt = jnp.sum((aff >= pltpu.bitcast(cand, F32)).astype(jnp.int32), axis=-1, keepdims=True)
        thr = jnp.where(cnt >= cap, cand, thr)
    thr_f = pltpu.bitcast(thr, F32)
    gt = aff > thr_f
    eq = aff == thr_f
    need = (cap - jnp.sum(gt.astype(jnp.int32), axis=-1, keepdims=True)).astype(F32)
    eq_rank = _excl_cumsum_lanes(eq.astype(F32))
    sel = gt | (eq & (eq_rank < need))
    slot = _excl_cumsum_lanes(sel.astype(F32))
    pos = jnp.where(sel, slot.astype(jnp.int32), -1)
    pos = jnp.concatenate([pos[i * NE:(i + 1) * NE, :] for i in range(rb)], axis=1)
    pos_ref[...] = pos
    if post_ref:
        post_ref[0][...] = jnp.concatenate([pos, jnp.full((128 - NE, rb * n), -1, jnp.int32)], axis=0).T


def _route(lg, *, nreq, n, cap, rb, token_major_copy):
    tp = lg.shape[0]
    espec = pl.BlockSpec((NE, rb * n), lambda b: (0, b))
    out_specs = [espec, espec]
    out_shape = [_sds((NE, tp), jnp.int32), _sds((NE, tp))]
    if token_major_copy:
        out_specs.append(pl.BlockSpec((rb * n, 128), lambda b: (b, 0)))
        out_shape.append(_sds((tp, 128), jnp.int32))
    return pl.pallas_call(
        functools.partial(_route_kernel, cap=cap, n=n, rb=rb),
        grid=(nreq // rb,),
        in_specs=[pl.BlockSpec((rb * n, 128), lambda b: (b, 0))],
        out_specs=out_specs,
        out_shape=out_shape,
        compiler_params=_params(1),
        name="route",
    )(lg)


def _gather_kernel(pos_ref, aff_ref, h_ref, xs_ref, w_ref, *, eb, cap, n, rb):
    eg = pl.program_id(1)
    io = lax.broadcasted_iota(jnp.int32, (cap, n), 0)
    for i in range(rb):
        toks = slice(i * n, (i + 1) * n)
        slots = slice(i * cap, (i + 1) * cap)
        sels = []
        for k in range(eb):
            e = eg * eb + k
            sel = pos_ref[pl.ds(e, 1), toks] == io
            sels.append(sel.astype(F32).astype(BF16))
            w = jnp.sum(jnp.where(sel, aff_ref[pl.ds(e, 1), toks], 0.0), axis=-1, keepdims=True)
            w_ref[k, slots, :] = jnp.broadcast_to(w, (cap, 128))
        xs = _dot(jnp.concatenate(sels, axis=0), h_ref[toks, :]).astype(BF16)
        for k in range(eb):
            xs_ref[k, slots, :] = xs[k * cap:(k + 1) * cap, :]


def _gather(pos, aff, hn2, *, nreq, n, cap, eb, rb=1):
    return pl.pallas_call(
        functools.partial(_gather_kernel, eb=eb, cap=cap, n=n, rb=rb),
        grid=(nreq // rb, NE // eb),
        in_specs=[pl.BlockSpec((NE, rb * n), lambda b, g: (0, b)), pl.BlockSpec((NE, rb * n), lambda b, g: (0, b)),
                  pl.BlockSpec((rb * n, D), lambda b, g: (b, 0))],
        out_specs=[pl.BlockSpec((eb, rb * cap, D), lambda b, g: (g, b, 0)),
                   pl.BlockSpec((eb, rb * cap, 128), lambda b, g: (g, b, 0))],
        out_shape=[_sds((NE, nreq * cap, D), BF16), _sds((NE, nreq * cap, 128))],
        compiler_params=_params(2),
        name="moe_gather",
    )(pos, aff, hn2)


def _expert_kernel(xc_ref, xl_ref, wc_ref, wl_ref, wg_ref, wu_ref, wd_ref, yc_ref, yl_ref, *, rows):
    wg = _bf(wg_ref[0, 0])
    wu = _bf(wu_ref[0, 0])
    wd = _bf(wd_ref[0, 0])
    tm = 512
    for x_ref, w_ref, y_ref in ((xc_ref, wc_ref, yc_ref), (xl_ref, wl_ref, yl_ref)):
        for ch in range(rows // tm):
            sl = slice(ch * tm, (ch + 1) * tm)
            x = x_ref[0, sl, :]
            g = _dot(x, wg)
            u = _dot(x, wu)
            a = _bf(g * jax.nn.sigmoid(g) * u)
            y_ref[0, sl, :] = _bf(_dot(a, wd) * w_ref[0, sl, 0:1])


def _experts(xs_c, xs_l, w_c, w_l, wg, wu, wd, layer):
    rows = xs_c.shape[1]
    dff = wg.shape[-1]
    xspec = pl.BlockSpec((1, rows, D), lambda e: (e, 0, 0))
    wspec = pl.BlockSpec((1, rows, 128), lambda e: (e, 0, 0))
    return pl.pallas_call(
        functools.partial(_expert_kernel, rows=rows),
        grid=(NE,),
        in_specs=[xspec, xspec, wspec, wspec,
                  pl.BlockSpec((1, 1, D, dff), lambda e: (layer, e, 0, 0)),
                  pl.BlockSpec((1, 1, D, dff), lambda e: (layer, e, 0, 0)),
                  pl.BlockSpec((1, 1, dff, D), lambda e: (layer, e, 0, 0))],
        out_specs=[xspec, xspec],
        out_shape=[_sds((NE, rows, D), BF16), _sds((NE, rows, D), BF16)],
        compiler_params=_params(1),
        name="moe_experts",
    )(xs_c, xs_l, w_c, w_l, wg, wu, wd)


def _scatter_kernel(idx_ref, y_ref, x1_ref, mod_ref, o_ref, *, eb, cap, n, rb, row_base, row_mul, token_major):
    eg = pl.program_id(1)
    last = pl.num_programs(1) - 1
    r = row_base + row_mul * pl.program_id(0)
    g2 = mod_ref[pl.ds(r, 1), 5 * D:6 * D]
    tn = min(n, 512)
    for i in range(rb):
        ys = jnp.concatenate([y_ref[k, i * cap:(i + 1) * cap, :] for k in range(eb)], axis=0)
        for t in range(n // tn):
            rows = slice(i * n + t * tn, i * n + (t + 1) * tn)
            if token_major:
                io = lax.broadcasted_iota(jnp.int32, (tn, cap), 1)
                shift = (128 - eg * eb) & 127
                pt = pltpu.roll(idx_ref[rows, :], shift, axis=1)
                sels = [(pt[:, k:k + 1] == io).astype(F32).astype(BF16) for k in range(eb)]
                part = _dot(jnp.concatenate(sels, axis=1), ys)
            else:
                io = lax.broadcasted_iota(jnp.int32, (cap, tn), 0)
                sels = [(idx_ref[pl.ds(eg * eb + k, 1), rows] == io).astype(F32).astype(BF16) for k in range(eb)]
                part = _dot_tn(jnp.concatenate(sels, axis=0), ys)
            o_ref[rows, :] = jnp.where(eg == 0, part, o_ref[rows, :] + part)

    @pl.when(eg == last)
    def _():
        o_ref[...] = x1_ref[...] + g2 * o_ref[...]


def _scatter(idx, y, x1, mod, *, nreq, n, cap, eb, row_base, row_mul, rb=1):
    tp = x1.shape[0]
    assert rb == 1 or row_mul == 0
    token_major = idx.shape[0] == tp
    idx_spec = (pl.BlockSpec((rb * n, 128), lambda b, g: (b, 0)) if token_major
                else pl.BlockSpec((NE, rb * n), lambda b, g: (0, b)))
    return pl.pallas_call(
        functools.partial(_scatter_kernel, eb=eb, cap=cap, n=n, rb=rb, row_base=row_base, row_mul=row_mul,
                          token_major=token_major),
        grid=(nreq // rb, NE // eb),
        in_specs=[idx_spec,
                  pl.BlockSpec((eb, rb * cap, D), lambda b, g: (g, b, 0)),
                  pl.BlockSpec((rb * n, D), lambda b, g: (b, 0)), _full((8, 6 * D))],
        out_specs=pl.BlockSpec((rb * n, D), lambda b, g: (b, 0)),
        out_shape=_sds((tp, D)),
        compiler_params=_params(2),
        name="moe_scatter",
    )(idx, y, x1, mod)


def _rope_tables(n):
    nf = 8
    t = np.arange(n)
    rowp = (t // GRID_W).astype(np.float32)
    colp = (t % GRID_W).astype(np.float32)
    inv = (np.float32(ROPE_BASE) ** (-np.arange(nf, dtype=np.float32) / np.float32(nf))).astype(np.float32)
    lane = np.arange(GW)
    c32 = lane % 32
    pos = np.where((c32 < 16)[None, :], rowp[:, None], colp[:, None]).astype(np.float32)
    ang = (pos * inv[(c32 % 8)][None, :]).astype(np.float32).astype(np.float64)
    sign = np.where((lane % 16) < 8, -1.0, 1.0)[None, :]
    return jnp.asarray(np.cos(ang), F32), jnp.asarray(np.sin(ang) * sign, F32)


def _block_diag(w):
    nb, bi, bo = w.shape
    return (jnp.eye(nb, dtype=w.dtype)[:, None, :, None] * w[:, :, None, :]).reshape(nb * bi, nb * bo)


def _layer_params(l, w_in, ml_gate_b, na_qn_g, na_kn_g, df_qn_g, df_kn_g, rg_wa, rg_wx, rg_ba, rg_bx,
                  df_lq1, df_lk1, df_lq2, df_lk2, df_subln_g, w_out, router_w):
    wi = w_in[l]
    w_r = jnp.concatenate([wi[:, 0:1024], wi[:, 1040:3088], wi[:, 1024:1040],
                           jnp.zeros((D, PROJ_PAD - 3088), F32)], axis=1).astype(BF16)
    gate_b = jnp.pad(ml_gate_b[l], (0, 128 - 16)).reshape(1, 128)
    qkg = jnp.stack([jnp.tile(na_qn_g[l], NH), jnp.tile(na_kn_g[l], NH),
                     jnp.tile(df_qn_g[l], 2 * NH), jnp.tile(df_kn_g[l], 2 * NH)])
    wbd = jnp.concatenate([_block_diag(rg_wa[l, 0]), _block_diag(rg_wx[l, 0]),
                           _block_diag(rg_wa[l, 1]), _block_diag(rg_wx[l, 1])], axis=1).astype(BF16)
    rg_bias = jnp.concatenate([rg_ba[l, 0], rg_bx[l, 0], rg_ba[l, 1], rg_bx[l, 1]]).reshape(1, 4 * GW)
    lamp = jnp.stack([df_lq1[l], df_lk1[l], df_lq2[l], df_lk2[l]])
    sub = jnp.tile(df_subln_g[l], NH).reshape(1, GW)
    rw = jnp.pad(router_w[l], ((0, 0), (0, 128 - NE))).astype(BF16)
    return dict(w_r=w_r, gate_b=gate_b, qkg=qkg, wbd=wbd, rg_bias=rg_bias, lamp=lamp, sub=sub,
                wout=w_out[l].astype(BF16), rw=rw)


def kernel(x_prompt, x_sample, cache_na_k, cache_na_v, cache_df_k, cache_df_v, state_ml_c, state_ml_n, state_ml_m, state_rg_h, c, c_ctx, norm1_g, norm2_g, w_mod, b_mod, w_in, ml_gate_b, ml_norm_g, na_qn_g, na_kn_g, na_rpb, rg_conv_w, rg_conv_b, rg_wa, rg_ba, rg_wx, rg_bx, rg_lam, df_qn_g, df_kn_g, df_lq1, df_lk1, df_lq2, df_lk2, df_subln_g, w_out, router_w, moe_wg, moe_wu, moe_wd):
    nb, seq, _ = x_prompt.shape
    db, dseq, _ = x_sample.shape
    depth = w_in.shape[0]
    past = cache_na_k.shape[2]
    tm = 512
    cap_c = 2 * seq // NE
    cap_l = 2 * dseq // NE

    cv = jnp.concatenate([c_ctx[None, :], c, jnp.zeros((8 - 1 - db, D), F32)], axis=0)
    mod_all = _modulation(cv, w_mod, b_mod)
    rope = _rope_tables(dseq)

    xc = x_prompt.reshape(nb * seq, D)
    xl = x_sample.reshape(db * dseq, D)
    ctx_out = []
    caches = None
    for l in range(depth):
        lam_init = 0.8 - 0.6 * math.exp(-0.3 * l)
        p = _layer_params(l, w_in, ml_gate_b, na_qn_g, na_kn_g, df_qn_g, df_kn_g, rg_wa, rg_wx, rg_ba, rg_bx,
                          df_lq1, df_lk1, df_lq2, df_lk2, df_subln_g, w_out, router_w)
        mod = mod_all[l]
        g1 = norm1_g[l].reshape(1, D)
        g2 = norm2_g[l].reshape(1, D)
        mlg = ml_norm_g[l].reshape(1, GW)
        cb = rg_conv_b[l].reshape(1, GW)
        tb = _rpb_table(na_rpb[l], rows=dseq // GRID_W, rps=NA_RPS)

        nt_c = nb * seq // tm
        ml, gates, naq, rg, dfq, *caches = _proj_in(xc, mod, g1, p["w_r"], p["gate_b"], p["qkg"], caches=caches,
                                                    cache_shape=(nb, depth, seq, GW), layer=l,
                                                    row_base=0, row_div=nt_c, tm=tm)
        hf, hb, c_c, n_c, m_c = _mlstm(ml, gates, nreq=nb, nc=seq // CHUNK)
        y_na, o_df = _ctx_attn(naq, dfq, caches, l, p["lamp"], lam_init, nreq=nb, n=seq)
        y_rg, rg_fin = _rglru(rg, rg_conv_w[l], cb, p["wbd"], p["rg_bias"], rg_lam[l],
                              jnp.zeros((nb, 2, GW), F32), nreq=nb, n=seq)
        x1_c, hn2_c, lg_c = _merge(hf, hb, ml, y_na, y_rg, o_df, xc, mod, mlg, p["sub"], p["wout"], g2, p["rw"],
                                   lam_init, row_base=0, row_div=nt_c, tm=tm)
        pos_c, aff_c = _route(lg_c, nreq=nb, n=seq, cap=cap_c, rb=8, token_major_copy=False)
        xs_c, w_c = _gather(pos_c, aff_c, hn2_c, nreq=nb, n=seq, cap=cap_c, eb=NE, rb=4)
        ctx_out.append((c_c, n_c, m_c, rg_fin))

        tiles_req = dseq // tm
        ml, gates, na, rg, df = _proj_in(xl, mod, g1, p["w_r"], p["gate_b"], p["qkg"], rope=rope,
                                         row_base=1, row_div=tiles_req, tm=tm)
        cn0 = _mlstm_pack_state(state_ml_c[:, l], state_ml_n[:, l])
        m0 = jnp.broadcast_to(state_ml_m[:, l].reshape(db, 8, 1), (db, 8, 128))
        hf, hb, _, _, _ = _mlstm(ml, gates, (cn0, m0), nreq=db, nc=dseq // CHUNK)
        y_na = _na_lat(na, cache_na_k[:, l].reshape(db, past, GW), cache_na_v[:, l].reshape(db, past, GW), tb,
                       nreq=db, n=dseq, rps=NA_RPS)
        o_df = _df_lat(df, cache_df_k[:, l].reshape(db, past, GW), cache_df_v[:, l].reshape(db, past, GW),
                       p["lamp"], lam_init, nreq=db, n=dseq)
        y_rg, _ = _rglru(rg, rg_conv_w[l], cb, p["wbd"], p["rg_bias"], rg_lam[l], state_rg_h[:, l],
                         nreq=db, n=dseq)
        x1_l, hn2_l, lg_l = _merge(hf, hb, ml, y_na, y_rg, o_df, xl, mod, mlg, p["sub"], p["wout"], g2, p["rw"],
                                   lam_init, row_base=1, row_div=tiles_req, tm=tm)
        pos_l, aff_l, post_l = _route(lg_l, nreq=db, n=dseq, cap=cap_l, rb=1, token_major_copy=True)
        xs_l, w_l = _gather(pos_l, aff_l, hn2_l, nreq=db, n=dseq, cap=cap_l, eb=2)

        y_c, y_l = _experts(xs_c, xs_l, w_c, w_l, moe_wg, moe_wu, moe_wd, l)
        xc = _scatter(pos_c, y_c, x1_c, mod, nreq=nb, n=seq, cap=cap_c, eb=NE, row_base=0, row_mul=0, rb=4)
        xl = _scatter(post_l, y_l, x1_l, mod, nreq=db, n=dseq, cap=cap_l, eb=4, row_base=1, row_mul=1)

    y_prompt = xc.reshape(nb, seq, D)
    y_sample = xl.reshape(db, dseq, D)
    st = lambda f: jnp.stack([f(o) for o in ctx_out], axis=1)
    na_k = caches[0].reshape(nb, depth, seq, NH, HD)
    na_v = caches[1].reshape(nb, depth, seq, NH, HD)
    df_k = caches[2].reshape(nb, depth, seq, NH, 2, HD // 2)
    df_v = caches[3].reshape(nb, depth, seq, NH, HD)
    ml_c = st(lambda o: o[0])
    ml_n = st(lambda o: o[1])
    ml_m = st(lambda o: o[2][:, :, 0].reshape(nb, 2, NH))
    rg_h = st(lambda o: o[3])
    return (y_prompt, y_sample, na_k, na_v, df_k, df_v, ml_c, ml_n, ml_m, rg_h)
```

```python
import functools
import math

import numpy as np
import jax
import jax.numpy as jnp
from jax import lax
from jax.experimental import pallas as pl
from jax.experimental.pallas import tpu as pltpu

F32 = jnp.float32
BF16 = jnp.bfloat16

D = 1024
GW = 256
NH = 4
HD = 64
NE = 16
EPS = 1e-6
CHUNK = 256
GRID_W = 64
NA_ROWS = 8
NA_COLS = 16
DF_HEAD_GROUP = 1
SUB_ROWS = 256
NA_RPS = 4
RG_C = 8.0
ROPE_BASE = 10000.0
PROJ_PAD = 3200
VMEM_LIMIT_BYTES = 56 * 1024 * 1024
NEG_BIG = -1e30
LOG2E = 1.4426950408889634


def _bf(x):
    return x.astype(BF16)


def _dot(a, b):
    return jnp.dot(a, b, preferred_element_type=F32)


def _dot_nt(a, b):
    return lax.dot_general(a, b, (((1,), (1,)), ((), ())), preferred_element_type=F32)


def _dot_tn(a, b):
    return lax.dot_general(a, b, (((0,), (0,)), ((), ())), preferred_element_type=F32)


def _split3(x):
    p0 = _bf(x)
    r1 = x - p0.astype(F32)
    p1 = _bf(r1)
    return p0, p1, _bf(r1 - p1.astype(F32))


def _params(n_axes):
    return pltpu.CompilerParams(dimension_semantics=("arbitrary",) * n_axes,
                                vmem_limit_bytes=VMEM_LIMIT_BYTES)


def _full(shape):
    return pl.BlockSpec(shape, lambda *_: (0,) * len(shape))


def _sds(shape, dtype=F32):
    return jax.ShapeDtypeStruct(shape, dtype)


def _softplus(x):
    return jnp.maximum(x, 0.0) + jnp.log1p(jnp.exp(-jnp.abs(x)))


def _log_sigmoid(x):
    return jnp.minimum(x, 0.0) - jnp.log1p(jnp.exp(-jnp.abs(x)))


def _seg_rms(x, nseg, g_row):
    seg = x.shape[-1] // nseg
    lane = lax.broadcasted_iota(jnp.int32, x.shape, 1)
    x2 = x * x
    tot = jnp.zeros_like(x)
    for s in range(nseg):
        m = (lane >= s * seg) & (lane < (s + 1) * seg)
        t = jnp.sum(jnp.where(m, x2, 0.0), axis=-1, keepdims=True)
        tot = jnp.where(m, t, tot)
    return x * lax.rsqrt(tot * (1.0 / seg) + EPS) * g_row


def _mod_row(pid, row_base, row_div):
    return row_base + pid // row_div


def _mod_kernel(cv_ref, w_ref, b_ref, o_ref):
    cv = cv_ref[...]
    s = cv * jax.nn.sigmoid(cv)
    o_ref[0] = _dot(_bf(s), _bf(w_ref[0])) + b_ref[0]


def _modulation(cv, w_mod, b_mod):
    nl = w_mod.shape[0]
    tn = 1536
    return pl.pallas_call(
        _mod_kernel,
        grid=(nl, 6 * D // tn),
        in_specs=[_full((8, D)),
                  pl.BlockSpec((1, D, tn), lambda l, j: (l, 0, j)),
                  pl.BlockSpec((1, 1, tn), lambda l, j: (l, 0, j))],
        out_specs=pl.BlockSpec((1, 8, tn), lambda l, j: (l, 0, j)),
        out_shape=_sds((nl, 8, 6 * D)),
        compiler_params=_params(2),
        name="modulation",
    )(cv, w_mod, b_mod.reshape(nl, 1, 6 * D))


def _rope(x, cos_t, sin_t):
    lane = lax.broadcasted_iota(jnp.int32, x.shape, 1)
    first = (lane & 15) < 8
    sw = jnp.where(first, pltpu.roll(x, GW - 8, axis=1), pltpu.roll(x, 8, axis=1))
    return x * cos_t + sw * sin_t


def _proj_kernel(*refs, row_base, row_div, layer):
    ctx = layer is not None
    if ctx:
        (x_ref, mod_ref, g1_ref, w_ref, gb_ref, qkg_ref) = refs[:6]
        (ml_ref, gate_ref, naq_ref, rg_ref, dfq_ref, nk_ref, nv_ref, dk_ref, dv_ref) = refs[-9:]
        cache_layer = layer if nk_ref.shape[1] > 1 else 0
    else:
        (x_ref, mod_ref, g1_ref, w_ref, gb_ref, qkg_ref, cos_ref, sin_ref,
         ml_ref, gate_ref, na_ref, rg_ref, df_ref) = refs
    r = _mod_row(pl.program_id(0), row_base, row_div)
    sh = mod_ref[pl.ds(r, 1), 0:D]
    sc = mod_ref[pl.ds(r, 1), D:2 * D]
    for sub in range(x_ref.shape[0] // SUB_ROWS):
        rows = slice(sub * SUB_ROWS, (sub + 1) * SUB_ROWS)
        x = x_ref[rows, :]
        ms = jnp.mean(x * x, axis=-1, keepdims=True)
        y = x * lax.rsqrt(ms + EPS) * g1_ref[...]
        hn = _bf(y * (1.0 + sc) + sh)

        ml = _dot(hn, w_ref[:, 0:1024])
        ml_ref[rows, 0:256] = ml[:, 0:256]
        ml_ref[rows, 256:512] = ml[:, 256:512] * (HD ** -0.5)
        ml_ref[rows, 512:1024] = ml[:, 512:1024]

        gz = _dot(hn, w_ref[:, 3072:3200]) + gb_ref[...]
        lane = lax.broadcasted_iota(jnp.int32, gz.shape, 1)
        gate_ref[rows, :] = jnp.where(((lane >> 2) & 1) == 1, _log_sigmoid(gz), gz)

        rg_ref[rows, :] = _dot(hn, w_ref[:, 1792:2304])

        nz = _dot(hn, w_ref[:, 1024:1792])
        nq = _seg_rms(nz[:, 0:256], NH, qkg_ref[0:1, :])
        nk = _seg_rms(nz[:, 256:512], NH, qkg_ref[1:2, :])
        dz = _dot(hn, w_ref[:, 2304:3072])
        dq = _seg_rms(dz[:, 0:256], 2 * NH, qkg_ref[2:3, :])
        dk = _seg_rms(dz[:, 256:512], 2 * NH, qkg_ref[3:4, :])
        if ctx:
            naq_ref[rows, :] = nq
            dfq_ref[rows, :] = dq
            for c_ref, val in ((nk_ref, nk), (nv_ref, nz[:, 512:768]), (dk_ref, dk), (dv_ref, dz[:, 512:768])):
                c_ref[sub, cache_layer] = val
                for other in range(c_ref.shape[1]):
                    if other != cache_layer:
                        c_ref[sub, other] = jnp.zeros_like(val)
        else:
            cos_t = cos_ref[rows, :]
            sin_t = sin_ref[rows, :]
            na_ref[rows, 0:256] = nq
            na_ref[rows, 256:512] = nk
            na_ref[rows, 512:768] = nz[:, 512:768]
            df_ref[rows, 0:256] = _rope(dq, cos_t, sin_t)
            df_ref[rows, 256:512] = _rope(dk, cos_t, sin_t)
            df_ref[rows, 512:768] = dz[:, 512:768]


def _proj_in(x, mod, g1, w_r, gate_b, qkg, *, rope=None, caches=None, cache_shape=None, layer=None,
             row_base, row_div, tm=512):
    tp = x.shape[0]
    in_specs = [pl.BlockSpec((tm, D), lambda i: (i, 0)), _full((8, 6 * D)), _full((1, D)),
                _full((D, PROJ_PAD)), _full((1, 128)), _full((4, GW))]
    args = [x, mod, g1, w_r, gate_b, qkg]
    tok = lambda n: pl.BlockSpec((tm, n), lambda i: (i, 0))
    aliases = {}
    if layer is None:
        tiles = rope[0].shape[0] // tm
        in_specs += [pl.BlockSpec((tm, GW), lambda i: (i % tiles, 0))] * 2
        args += list(rope)
        out_specs = [tok(1024), tok(128), tok(768), tok(512), tok(768)]
        out_shape = [_sds((tp, n)) for n in (1024, 128, 768, 512, 768)]
    else:
        if caches is None:
            depth, seq = cache_shape[1], cache_shape[2]
            cspec = pl.BlockSpec((tm // seq, depth, seq, GW), lambda i: (i, 0, 0, 0))
        else:
            cache_shape, seq = caches[0].shape, caches[0].shape[2]
            cspec = pl.BlockSpec((tm // seq, 1, seq, GW), lambda i: (i, layer, 0, 0))
            in_specs += [pl.BlockSpec(memory_space=pl.ANY)] * 4
            args += list(caches)
            aliases = {6 + i: 5 + i for i in range(4)}
        assert seq == SUB_ROWS
        out_specs = [tok(1024), tok(128), tok(GW), tok(512), tok(GW)] + [cspec] * 4
        out_shape = [_sds((tp, n)) for n in (1024, 128, GW, 512, GW)] + [_sds(cache_shape)] * 4
    return pl.pallas_call(
        functools.partial(_proj_kernel, row_base=row_base, row_div=row_div, layer=layer),
        grid=(tp // tm,),
        in_specs=in_specs,
        out_specs=out_specs,
        out_shape=out_shape,
        input_output_aliases=aliases,
        compiler_params=_params(1),
        name="proj_in",
    )(*args)


def _mlstm_chunk(dir_refs, st_s, m_s):
    lc = CHUNK
    row = lax.broadcasted_iota(jnp.int32, (lc, lc), 0)
    col = lax.broadcasted_iota(jnp.int32, (lc, lc), 1)
    ones = jnp.ones((lc, 128), BF16)
    hs = [slice(h * HD, (h + 1) * HD) for h in range(NH)]
    chains = [(d, h) for d in range(2) for h in range(NH)]
    tris, bc_cols, r_rows, k_ts, k_tbs, q_alls, v_alls = [], [], [], [], [], [], []
    for d, (q_ref, k_ref, v_ref, g_ref, _) in enumerate(dir_refs):
        g = g_ref[...]
        gt = g.T
        tri = (row >= col) if d == 0 else (row <= col)
        tri_t = (col >= row) if d == 0 else (col <= row)
        g_parts = _split3(g)
        gt_parts = _split3(gt)
        tri_b = tri.astype(F32).astype(BF16)
        tri_tb = tri_t.astype(F32).astype(BF16)
        bc_cols.append(_dot(tri_b, g_parts[0]) + _dot(tri_b, g_parts[1]) + _dot(tri_b, g_parts[2]))
        bc_row = _dot(gt_parts[0], tri_tb) + _dot(gt_parts[1], tri_tb) + _dot(gt_parts[2], tri_tb)
        r_rows.append(gt[d * 8:d * 8 + NH, :] - bc_row[d * 8 + NH:d * 8 + 2 * NH, :])
        tris.append(tri)
        k_t = k_ref[...].T
        k_ts.append(k_t)
        k_tbs.append(_bf(k_t))
        q_alls.append(_bf(q_ref[...]))
        v_alls.append(_bf(v_ref[...]))
    ms = [m_s[d * NH + h:d * NH + h + 1, 0:1] for d, h in chains]
    rms = [jnp.where(tris[d], r_rows[d][h:h + 1, :], -jnp.inf) for d, h in chains]
    big_rs = [jnp.maximum(jnp.max(rm, axis=-1, keepdims=True), m) for rm, m in zip(rms, ms)]
    ss = [_bf(_dot(q_alls[d][:, hs[h]], k_tbs[d][hs[h], :]) * jnp.exp(rms[i] - big_rs[i]))
          for i, (d, h) in enumerate(chains)]
    v2s = [v_alls[d][:, (h // 2) * 128:(h // 2 + 1) * 128] for d, h in chains]
    es = [jnp.exp(m - big_r) for m, big_r in zip(ms, big_rs)]
    c_cols = [bc_cols[d][:, d * 8 + NH + h:d * 8 + NH + h + 1] for d, h in chains]
    tots = [_dot(ss[i], v2s[i]) + es[i] * _dot(q_alls[d][:, hs[h]], _bf(st_s[d, h, 0]))
            for i, (d, h) in enumerate(chains)]
    dens = [_dot(ss[i], ones) + es[i] * _dot(q_alls[d][:, hs[h]], _bf(st_s[d, h, 1]))
            for i, (d, h) in enumerate(chains)]
    outs = [tots[i] / jnp.maximum(jnp.abs(dens[i]), jnp.exp(-(c_cols[i] + big_rs[i]))) for i in range(len(chains))]
    for i, (d, h) in enumerate(chains):
        last = lc - 1 if d == 0 else 0
        r_last = big_rs[i][last:last + 1, :]
        kw = _bf(k_ts[d][hs[h], :] * jnp.exp(r_rows[d][h:h + 1, :] - r_last))
        gdec = jnp.exp(ms[i] - r_last)
        st_s[d, h, 0] = gdec * st_s[d, h, 0] + _dot(kw, v2s[i])
        st_s[d, h, 1] = gdec * st_s[d, h, 1] + _dot(kw, ones)
        m_s[d * NH + h:d * NH + h + 1, :] = jnp.broadcast_to(c_cols[i][last:last + 1, :] + r_last, (1, 128))
    lane = lax.broadcasted_iota(jnp.int32, (lc, 128), 1)
    for d, refs in enumerate(dir_refs):
        o = outs[d * NH:(d + 1) * NH]
        refs[4][...] = jnp.concatenate([jnp.where(lane < HD, o[2 * p], o[2 * p + 1]) for p in range(NH // 2)],
                                       axis=-1)


def _mlstm_kernel(qf, kf, vf, gf, qb, kb, vb, gb, *rest):
    hf_ref, hb_ref, c_out, n_out, m_out, st_s, m_s = rest[-7:]
    j = pl.program_id(1)

    @pl.when(j == 0)
    def _():
        if len(rest) == 9:
            st_s[...] = rest[0][0]
            m_s[...] = rest[1][0]
        else:
            st_s[...] = jnp.zeros_like(st_s)
            m_s[...] = jnp.zeros_like(m_s)

    _mlstm_chunk(((qf, kf, vf, gf, hf_ref), (qb, kb, vb, gb, hb_ref)), st_s, m_s)

    @pl.when(j == pl.num_programs(1) - 1)
    def _():
        diag = (lax.broadcasted_iota(jnp.int32, (HD, 128), 0) == lax.broadcasted_iota(jnp.int32, (HD, 128), 1))
        for d in range(2):
            for h in range(NH):
                lo = HD * (h % 2)
                c_out[0, d, h] = st_s[d, h, 0][:, lo:lo + HD]
                n_row = jnp.sum(jnp.where(diag, st_s[d, h, 1], 0.0), axis=0, keepdims=True)
                n_out[0, d, h:h + 1, :] = n_row[:, 0:HD]
        m_out[0] = m_s[...]


def _mlstm_pack_state(c, n):
    z = jnp.zeros_like(c)
    odd = (jnp.arange(NH) % 2 == 1)[None, None, :, None, None]
    c_pair = jnp.where(odd, jnp.concatenate([z, c], axis=-1), jnp.concatenate([c, z], axis=-1))
    n_rep = jnp.broadcast_to(n[..., None], n.shape + (128,))
    return jnp.stack([c_pair, n_rep], axis=3)


def _mlstm(mlz, gates, state=None, *, nreq, nc):
    tp = mlz.shape[0]
    st_spec = pl.BlockSpec((1, 2, NH, 2, HD, 128), lambda r, j: (r, 0, 0, 0, 0, 0))

    def fwd(col):
        return lambda r, j: (r * nc + j, col)

    def bwd(col):
        return lambda r, j: (r * nc + nc - 1 - j, col)

    in_specs = []
    for mk in (fwd, bwd):
        in_specs += [pl.BlockSpec((CHUNK, GW), mk(0)), pl.BlockSpec((CHUNK, GW), mk(1)),
                     pl.BlockSpec((CHUNK, GW), mk(2)), pl.BlockSpec((CHUNK, 128), mk(0))]
    args = [mlz, mlz, mlz, gates, mlz, mlz, mlz, gates]
    if state is not None:
        in_specs += [st_spec, pl.BlockSpec((1, 8, 128), lambda r, j: (r, 0, 0))]
        args += list(state)
    return pl.pallas_call(
        _mlstm_kernel,
        grid=(nreq, nc),
        in_specs=in_specs,
        out_specs=[pl.BlockSpec((CHUNK, GW), fwd(0)), pl.BlockSpec((CHUNK, GW), bwd(0)),
                   pl.BlockSpec((1, 2, NH, HD, HD), lambda r, j: (r, 0, 0, 0, 0)),
                   pl.BlockSpec((1, 2, NH, HD), lambda r, j: (r, 0, 0, 0)),
                   pl.BlockSpec((1, 8, 128), lambda r, j: (r, 0, 0))],
        out_shape=[_sds((tp, GW)), _sds((tp, GW)), _sds((nreq, 2, NH, HD, HD)), _sds((nreq, 2, NH, HD)),
                   _sds((nreq, 8, 128))],
        scratch_shapes=[pltpu.VMEM((2, NH, 2, HD, 128), F32), pltpu.VMEM((8, 128), F32)],
        compiler_params=_params(2),
        name="mlstm",
    )(*args)


def _diff_lambda(lam_ref, lam_init):
    lp = lam_ref[...]
    a = jnp.exp(jnp.sum(lp[0:1, :] * lp[1:2, :], axis=-1, keepdims=True))
    b = jnp.exp(jnp.sum(lp[2:3, :] * lp[3:4, :], axis=-1, keepdims=True))
    return a - b + lam_init


def _softmax_pv(jobs):
    ss = [_dot(q, k_t) for q, k_t, _ in jobs]
    es = [jnp.exp2(s - jnp.max(s, axis=-1, keepdims=True)) for s in ss]
    return [_dot(_bf(e), v) / jnp.sum(e, axis=-1, keepdims=True) for e, (_, _, v) in zip(es, jobs)]


def _ctx_attn_kernel(nq, nk, nv, dq, dk, dv, lam_ref, ona_ref, odf_ref, *, lam_init):
    lam = _diff_lambda(lam_ref, lam_init)
    n = nk.shape[2]
    for i in range(nk.shape[0]):
        rows = slice(i * n, (i + 1) * n)
        nk_t = _bf(nk[i, 0].T)
        dk_t = _bf(dk[i, 0].T)
        jobs = []
        for h in range(NH):
            hs = slice(h * HD, (h + 1) * HD)
            jobs.append((_bf(nq[rows, hs] * (HD ** -0.5 * LOG2E)), nk_t[hs, :], _bf(nv[i, 0, :, hs])))
        for h in range(NH):
            vh = _bf(dv[i, 0, :, h * HD:(h + 1) * HD])
            for c in range(2):
                cs = slice(h * HD + c * 32, h * HD + (c + 1) * 32)
                jobs.append((_bf(dq[rows, cs] * (32 ** -0.5 * LOG2E)), dk_t[cs, :], vh))
        outs = _softmax_pv(jobs)
        ona_ref[rows, :] = _bf(jnp.concatenate(outs[0:NH], axis=-1))
        odf_ref[rows, :] = jnp.concatenate([outs[NH + 2 * h] - lam * outs[NH + 2 * h + 1] for h in range(NH)],
                                           axis=-1)


def _ctx_attn(naq, dfq, caches, layer, lamp, lam_init, *, nreq, n, rb=1):
    tp = naq.shape[0]
    qs = pl.BlockSpec((rb * n, GW), lambda b: (b, 0))
    cs = pl.BlockSpec((rb, 1, n, GW), lambda b: (b, layer, 0, 0))
    return pl.pallas_call(
        functools.partial(_ctx_attn_kernel, lam_init=lam_init),
        grid=(nreq // rb,),
        in_specs=[qs, cs, cs, qs, cs, cs, _full((4, 32))],
        out_specs=[qs, qs],
        out_shape=[_sds((tp, GW), BF16), _sds((tp, GW))],
        compiler_params=_params(1),
        name="ctx_attn",
    )(naq, caches[0], caches[1], dfq, caches[2], caches[3], lamp)


def _rpb_kernel(rpb_ref, o_ref, *, rows, rps):
    h = pl.program_id(0)
    shape = (GRID_W, NA_ROWS * GRID_W)
    lane = lax.broadcasted_iota(jnp.int32, shape, 1)
    cq = lax.broadcasted_iota(jnp.int32, shape, 0)
    ck = lane & (GRID_W - 1)
    c0 = jnp.clip(cq - NA_COLS // 2, 0, GRID_W - NA_COLS)
    ok = (ck >= c0) & (ck < c0 + NA_COLS)
    ncol = 2 * NA_COLS - 1
    ndr = 2 * NA_ROWS - 1
    x = lax.broadcasted_iota(jnp.int32, (1, 2 * GRID_W), 1)
    didx = jnp.clip(x - GRID_W, -(NA_COLS - 1), NA_COLS - 1) + (NA_COLS - 1)
    tiles = []
    for dr in range(ndr):
        frow = jnp.zeros((1, 2 * GRID_W), F32)
        for b in range(ncol):
            frow = jnp.where(didx == b, rpb_ref[(h * ndr + dr) * ncol + b], frow)
        rolled = pltpu.roll(jnp.broadcast_to(frow, (GRID_W, 2 * GRID_W)), GRID_W, axis=1, stride=1, stride_axis=0)
        tiles.append(rolled[:, 0:GRID_W])
    tbs = [jnp.where(ok, jnp.concatenate(tiles[s:s + NA_ROWS], axis=-1) * LOG2E, NEG_BIG) for s in range(NA_ROWS)]
    wrows = NA_ROWS + rps
    steps = rows // rps
    for t, j in enumerate((0, 1, steps - 1)):
        w0 = min(max(j * rps - NA_ROWS // 2, 0), rows - wrows)
        for a in range(rps):
            r = j * rps + a
            r0 = min(max(r - NA_ROWS // 2, 0), rows - NA_ROWS)
            off = (r0 - w0) * GRID_W
            rest = rps * GRID_W - off
            pieces = [tbs[r0 - r + NA_ROWS - 1]]
            if off:
                pieces = [jnp.full((GRID_W, off), NEG_BIG, F32)] + pieces
            if rest:
                pieces = pieces + [jnp.full((GRID_W, rest), NEG_BIG, F32)]
            o_ref[0, t, a * GRID_W:(a + 1) * GRID_W, :] = jnp.concatenate(pieces, axis=-1)


def _na_step_offsets(j, rows, rps):
    w0 = min(max(j * rps - NA_ROWS // 2, 0), rows - NA_ROWS - rps)
    return [(min(max(r - NA_ROWS // 2, 0), rows - NA_ROWS) - r,
             min(max(r - NA_ROWS // 2, 0), rows - NA_ROWS) - w0) for r in range(j * rps, (j + 1) * rps)]


def _rpb_table(rpb, *, rows, rps):
    steps = rows // rps
    assert all(_na_step_offsets(j, rows, rps) == _na_step_offsets(1, rows, rps) for j in range(1, steps - 1))
    shape = (rps * GRID_W, (NA_ROWS + rps) * GRID_W)
    return pl.pallas_call(
        functools.partial(_rpb_kernel, rows=rows, rps=rps),
        grid=(NH,),
        in_specs=[pl.BlockSpec(memory_space=pltpu.SMEM)],
        out_specs=pl.BlockSpec((1, 3) + shape, lambda h: (h, 0, 0, 0)),
        out_shape=_sds((NH, 3) + shape),
        compiler_params=_params(1),
        name="rpb_table",
    )(rpb.reshape(-1))


def _na_lat_kernel(q_ref, k_ref, v_ref, kc_ref, vc_ref, tb_ref, o_ref, k_s, v_s, kct_s, vc_s, *, rows, rps):
    j = pl.program_id(1)

    @pl.when(j == 0)
    def _():
        k_s[...] = _bf(k_ref[...])
        v_s[...] = _bf(v_ref[...])
        kct_s[...] = _bf(kc_ref[0].T)
        vc_s[...] = _bf(vc_ref[0])

    wrows = NA_ROWS + rps
    win = wrows * GRID_W
    w0 = jnp.clip(j * rps - NA_ROWS // 2, 0, rows - wrows)
    start = pl.multiple_of(w0 * GRID_W, GRID_W)
    hss = [slice(h * HD, (h + 1) * HD) for h in range(NH)]
    qs = [_bf(q_ref[:, hs] * (HD ** -0.5 * LOG2E)) for hs in hss]
    s_cs = [_dot(q, kct_s[hs, :]) for q, hs in zip(qs, hss)]
    s_ls = [_dot_nt(q, k_s[pl.ds(start, win), hs]) + tb_ref[h, 0] for h, (q, hs) in enumerate(zip(qs, hss))]
    ms = [jnp.maximum(jnp.max(s_c, axis=-1, keepdims=True), jnp.max(s_l, axis=-1, keepdims=True))
          for s_c, s_l in zip(s_cs, s_ls)]
    e_cs = [jnp.exp2(s_c - m) for s_c, m in zip(s_cs, ms)]
    e_ls = [jnp.exp2(s_l - m) for s_l, m in zip(s_ls, ms)]
    outs = []
    for e_c, e_l, hs in zip(e_cs, e_ls, hss):
        den = jnp.sum(e_c, axis=-1, keepdims=True) + jnp.sum(e_l, axis=-1, keepdims=True)
        o = _dot(_bf(e_c), vc_s[:, hs]) + _dot(_bf(e_l), v_s[pl.ds(start, win), hs])
        outs.append(o / den)
    o_ref[...] = _bf(jnp.concatenate(outs, axis=-1))


def _na_lat(na, kc, vc, tb, *, nreq, n, rps):
    tp = na.shape[0]
    rows = n // GRID_W
    past = kc.shape[1]
    steps = rows // rps
    tq = rps * GRID_W
    return pl.pallas_call(
        functools.partial(_na_lat_kernel, rows=rows, rps=rps),
        grid=(nreq, steps),
        in_specs=[pl.BlockSpec((tq, GW), lambda b, r: (b * steps + r, 0)),
                  pl.BlockSpec((n, GW), lambda b, r: (b, 1)),
                  pl.BlockSpec((n, GW), lambda b, r: (b, 2)),
                  pl.BlockSpec((1, past, GW), lambda b, r: (b, 0, 0)),
                  pl.BlockSpec((1, past, GW), lambda b, r: (b, 0, 0)),
                  pl.BlockSpec((NH, 1) + tb.shape[2:],
                               lambda b, r: (0, jnp.minimum(r, 1) + r // (steps - 1), 0, 0))],
        out_specs=pl.BlockSpec((tq, GW), lambda b, r: (b * steps + r, 0)),
        out_shape=_sds((tp, GW), BF16),
        scratch_shapes=[pltpu.VMEM((n, GW), BF16), pltpu.VMEM((n, GW), BF16),
                        pltpu.VMEM((GW, past), BF16), pltpu.VMEM((past, GW), BF16)],
        compiler_params=_params(2),
        name="na_latent",
    )(na, na, na, kc, vc, tb)


def _df_lat_kernel(q_ref, k_ref, v_ref, kc_ref, vc_ref, lam_ref, o_ref, kt_s, v_s, kct_s, vc_s, *, lam_init):
    @pl.when(pl.program_id(1) == 0)
    def _():
        kt_s[...] = _bf(k_ref[...].T)
        v_s[...] = _bf(v_ref[...])
        kct_s[...] = _bf(kc_ref[0].T)
        vc_s[...] = _bf(vc_ref[0])

    lam = _diff_lambda(lam_ref, lam_init)
    outs = []
    for h0 in range(0, NH, DF_HEAD_GROUP):
        heads = range(h0, h0 + DF_HEAD_GROUP)
        css = [slice(h * HD + c * 32, h * HD + (c + 1) * 32) for h in heads for c in range(2)]
        qs = [_bf(q_ref[:, cs] * (32 ** -0.5 * LOG2E)) for cs in css]
        s_cs = [_dot(q, kct_s[cs, :]) for q, cs in zip(qs, css)]
        s_ls = [_dot(q, kt_s[cs, :]) for q, cs in zip(qs, css)]
        ms = [jnp.maximum(jnp.max(s_c, axis=-1, keepdims=True), jnp.max(s_l, axis=-1, keepdims=True))
              for s_c, s_l in zip(s_cs, s_ls)]
        e_cs = [jnp.exp2(s_c - m) for s_c, m in zip(s_cs, ms)]
        e_ls = [jnp.exp2(s_l - m) for s_l, m in zip(s_ls, ms)]
        os = []
        for i, (e_c, e_l) in enumerate(zip(e_cs, e_ls)):
            hs = slice((h0 + i // 2) * HD, (h0 + i // 2 + 1) * HD)
            den = jnp.sum(e_c, axis=-1, keepdims=True) + jnp.sum(e_l, axis=-1, keepdims=True)
            os.append((_dot(_bf(e_c), vc_s[:, hs]) + _dot(_bf(e_l), v_s[:, hs])) / den)
        outs += [os[2 * i] - lam * os[2 * i + 1] for i in range(DF_HEAD_GROUP)]
    o_ref[...] = jnp.concatenate(outs, axis=-1)


def _df_lat(df, kc, vc, lamp, lam_init, *, nreq, n, tq=512):
    tp = df.shape[0]
    nq = n // tq
    past = kc.shape[1]
    return pl.pallas_call(
        functools.partial(_df_lat_kernel, lam_init=lam_init),
        grid=(nreq, nq),
        in_specs=[pl.BlockSpec((tq, GW), lambda b, j: (b * nq + j, 0)),
                  pl.BlockSpec((n, GW), lambda b, j: (b, 1)),
                  pl.BlockSpec((n, GW), lambda b, j: (b, 2)),
                  pl.BlockSpec((1, past, GW), lambda b, j: (b, 0, 0)),
                  pl.BlockSpec((1, past, GW), lambda b, j: (b, 0, 0)),
                  _full((4, 32))],
        out_specs=pl.BlockSpec((tq, GW), lambda b, j: (b * nq + j, 0)),
        out_shape=_sds((tp, GW)),
        scratch_shapes=[pltpu.VMEM((GW, n), BF16), pltpu.VMEM((n, GW), BF16),
                        pltpu.VMEM((GW, past), BF16), pltpu.VMEM((past, GW), BF16)],
        compiler_params=_params(2),
        name="df_latent",
    )(df, df, df, kc, vc, lamp)


def _rg_kernel(x_ref, g_ref, cw_ref, cb_ref, wbd_ref, bias_ref, lam_ref, h0_ref,
               y_ref, fin_ref, a_s, b_s, *, n):
    x = x_ref[...]
    row = lax.broadcasted_iota(jnp.int32, x.shape, 0)
    xc = cb_ref[...] + jnp.where(row >= 2, pltpu.roll(x, 2, axis=0), 0.0) * cw_ref[0:1, :]
    xc = xc + jnp.where(row >= 1, pltpu.roll(x, 1, axis=0), 0.0) * cw_ref[1:2, :]
    xc = xc + x * cw_ref[2:3, :]
    xc = xc + jnp.where(row < n - 1, pltpu.roll(x, n - 1, axis=0), 0.0) * cw_ref[3:4, :]
    z = _dot(_bf(xc), wbd_ref[...]) + bias_ref[...]
    sub = row & 7
    for d in range(2):
        rgate = jax.nn.sigmoid(z[:, 512 * d:512 * d + GW])
        igate = jax.nn.sigmoid(z[:, 512 * d + GW:512 * d + 2 * GW])
        la = -RG_C * rgate * _softplus(-lam_ref[d:d + 1, :])
        a = jnp.exp(la)
        t = jnp.tanh(la)
        b = jnp.sqrt(-2.0 * t / (1.0 - t)) * igate * xc
        for dd in (1, 2, 4):
            if d == 0:
                keep = sub >= dd
                a_sh = jnp.where(keep, pltpu.roll(a, dd, axis=0), 1.0)
                b_sh = jnp.where(keep, pltpu.roll(b, dd, axis=0), 0.0)
            else:
                keep = sub < 8 - dd
                a_sh = jnp.where(keep, pltpu.roll(a, n - dd, axis=0), 1.0)
                b_sh = jnp.where(keep, pltpu.roll(b, n - dd, axis=0), 0.0)
            b = b + a * b_sh
            a = a * a_sh
        a_s[d] = a
        b_s[d] = b

    nt = n // 8

    def body(t, carry):
        hf, hb = carry
        sf = pl.multiple_of(t * 8, 8)
        sb = pl.multiple_of((nt - 1 - t) * 8, 8)
        tf = a_s[0, pl.ds(sf, 8), :] * hf + b_s[0, pl.ds(sf, 8), :]
        tb = a_s[1, pl.ds(sb, 8), :] * hb + b_s[1, pl.ds(sb, 8), :]
        b_s[0, pl.ds(sf, 8), :] = tf
        b_s[1, pl.ds(sb, 8), :] = tb
        return tf[7:8, :], tb[0:1, :]

    hf, hb = lax.fori_loop(0, nt, body, (h0_ref[0, 0:1, :], h0_ref[0, 1:2, :]))
    fin_ref[0, 0:1, :] = hf
    fin_ref[0, 1:2, :] = hb
    gg = g_ref[...]
    cdf = 0.5 * (1.0 + jnp.tanh(math.sqrt(2.0 / math.pi) * (gg + 0.044715 * (gg * gg * gg))))
    y_ref[...] = _bf((b_s[0] + b_s[1]) * (gg * cdf))


def _rglru(rg, cw, cb, wbd, bias, lam, h0, *, nreq, n):
    tp = rg.shape[0]
    return pl.pallas_call(
        functools.partial(_rg_kernel, n=n),
        grid=(nreq,),
        in_specs=[pl.BlockSpec((n, GW), lambda b: (b, 0)), pl.BlockSpec((n, GW), lambda b: (b, 1)),
                  _full((4, GW)), _full((1, GW)), _full((GW, 4 * GW)), _full((1, 4 * GW)), _full((2, GW)),
                  pl.BlockSpec((1, 2, GW), lambda b: (b, 0, 0))],
        out_specs=[pl.BlockSpec((n, GW), lambda b: (b, 0)), pl.BlockSpec((1, 2, GW), lambda b: (b, 0, 0))],
        out_shape=[_sds((tp, GW), BF16), _sds((nreq, 2, GW))],
        scratch_shapes=[pltpu.VMEM((2, n, GW), F32), pltpu.VMEM((2, n, GW), F32)],
        compiler_params=_params(1),
        name="rglru",
    )(rg, rg, cw, cb, wbd, bias, lam, h0)


def _merge_kernel(hf_ref, hb_ref, mlo_ref, yna_ref, yrg_ref, odf_ref, x_ref, mod_ref, mlg_ref, sub_ref,
                  wout_ref, n2_ref, rw_ref, x1_ref, hn2_ref, lg_ref, *, row_base, row_div, lam_init):
    r = _mod_row(pl.program_id(0), row_base, row_div)
    y_ml = _seg_rms(hf_ref[...] + hb_ref[...], NH, mlg_ref[...]) * jax.nn.sigmoid(mlo_ref[...])
    y_df = _seg_rms(odf_ref[...], NH, sub_ref[...]) * (1.0 - lam_init)
    y = jnp.concatenate([_bf(y_ml), _bf(yna_ref[...]), _bf(yrg_ref[...]), _bf(y_df)], axis=-1)
    o = _dot(y, wout_ref[...])
    x1 = x_ref[...] + mod_ref[pl.ds(r, 1), 2 * D:3 * D] * o
    x1_ref[...] = x1
    ms = jnp.mean(x1 * x1, axis=-1, keepdims=True)
    hn = x1 * lax.rsqrt(ms + EPS) * n2_ref[...]
    hn = _bf(hn * (1.0 + mod_ref[pl.ds(r, 1), 4 * D:5 * D]) + mod_ref[pl.ds(r, 1), 3 * D:4 * D])
    hn2_ref[...] = hn
    lg_ref[...] = _dot(hn, rw_ref[...])


def _merge(hf, hb, mlz, yna, yrg, odf, x, mod, mlg, sub, wout, n2, rw, lam_init, *, row_base, row_div, tm=512):
    tp = x.shape[0]
    g = lambda c: pl.BlockSpec((tm, GW), lambda i: (i, c))
    return pl.pallas_call(
        functools.partial(_merge_kernel, row_base=row_base, row_div=row_div, lam_init=lam_init),
        grid=(tp // tm,),
        in_specs=[g(0), g(0), g(3), g(0), g(0), g(0),
                  pl.BlockSpec((tm, D), lambda i: (i, 0)), _full((8, 6 * D)), _full((1, GW)), _full((1, GW)),
                  _full((D, D)), _full((1, D)), _full((D, 128))],
        out_specs=[pl.BlockSpec((tm, D), lambda i: (i, 0)), pl.BlockSpec((tm, D), lambda i: (i, 0)),
                   pl.BlockSpec((tm, 128), lambda i: (i, 0))],
        out_shape=[_sds((tp, D)), _sds((tp, D), BF16), _sds((tp, 128))],
        compiler_params=_params(1),
        name="merge",
    )(hf, hb, mlz, yna, yrg, odf, x, mod, mlg, sub, wout, n2, rw)


def _excl_cumsum_lanes(mask):
    blk = 256
    r = lax.broadcasted_iota(jnp.int32, (blk, blk), 0)
    c = lax.broadcasted_iota(jnp.int32, (blk, blk), 1)
    tri = (r < c).astype(F32).astype(BF16)
    off = jnp.zeros((mask.shape[0], 1), F32)
    outs = []
    for i in range(mask.shape[1] // blk):
        mb = mask[:, i * blk:(i + 1) * blk]
        outs.append(_dot(_bf(mb), tri) + off)
        off = off + jnp.sum(mb, axis=-1, keepdims=True)
    return jnp.concatenate(outs, axis=-1)


def _route_kernel(lg_ref, pos_ref, aff_ref, *post_ref, cap, n, rb):
    lg = lg_ref[...].T[0:NE, :]
    ex = jnp.exp(lg - jnp.max(lg, axis=0, keepdims=True))
    aff = ex / jnp.sum(ex, axis=0, keepdims=True)
    aff_ref[...] = aff
    aff = jnp.concatenate([aff[:, i * n:(i + 1) * n] for i in range(rb)], axis=0)
    thr = jnp.zeros((rb * NE, 1), jnp.int32)
    for bit in range(30, -1, -1):
        cand = thr | (1 << bit)
        cnt = jnp.sum((aff >= pltpu.bitcast(cand, F32)).astype(jnp.int32), axis=-1, keepdims=True)
        thr = jnp.where(cnt >= cap, cand, thr)
    thr_f = pltpu.bitcast(thr, F32)
    gt = aff > thr_f
    eq = aff == thr_f
    need = (cap - jnp.sum(gt.astype(jnp.int32), axis=-1, keepdims=True)).astype(F32)
    eq_rank = _excl_cumsum_lanes(eq.astype(F32))
    sel = gt | (eq & (eq_rank < need))
    slot = _excl_cumsum_lanes(sel.astype(F32))
    pos = jnp.where(sel, slot.astype(jnp.int32), -1)
    pos = jnp.concatenate([pos[i * NE:(i + 1) * NE, :] for i in range(rb)], axis=1)
    pos_ref[...] = pos
    if post_ref:
        post_ref[0][...] = jnp.concatenate([pos, jnp.full((128 - NE, rb * n), -1, jnp.int32)], axis=0).T


def _route(lg, *, nreq, n, cap, rb, token_major_copy):
    tp = lg.shape[0]
    espec = pl.BlockSpec((NE, rb * n), lambda b: (0, b))
    out_specs = [espec, espec]
    out_shape = [_sds((NE, tp), jnp.int32), _sds((NE, tp))]
    if token_major_copy:
        out_specs.append(pl.BlockSpec((rb * n, 128), lambda b: (b, 0)))
        out_shape.append(_sds((tp, 128), jnp.int32))
    return pl.pallas_call(
        functools.partial(_route_kernel, cap=cap, n=n, rb=rb),
        grid=(nreq // rb,),
        in_specs=[pl.BlockSpec((rb * n, 128), lambda b: (b, 0))],
        out_specs=out_specs,
        out_shape=out_shape,
        compiler_params=_params(1),
        name="route",
    )(lg)


def _gather_kernel(pos_ref, aff_ref, h_ref, xs_ref, w_ref, *, eb, cap, n, rb):
    eg = pl.program_id(1)
    io = lax.broadcasted_iota(jnp.int32, (cap, n), 0)
    for i in range(rb):
        toks = slice(i * n, (i + 1) * n)
        slots = slice(i * cap, (i + 1) * cap)
        sels = []
        for k in range(eb):
            e = eg * eb + k
            sel = pos_ref[pl.ds(e, 1), toks] == io
            sels.append(sel.astype(F32).astype(BF16))
            w = jnp.sum(jnp.where(sel, aff_ref[pl.ds(e, 1), toks], 0.0), axis=-1, keepdims=True)
            w_ref[k, slots, :] = jnp.broadcast_to(w, (cap, 128))
        xs = _dot(jnp.concatenate(sels, axis=0), h_ref[toks, :]).astype(BF16)
        for k in range(eb):
            xs_ref[k, slots, :] = xs[k * cap:(k + 1) * cap, :]


def _gather(pos, aff, hn2, *, nreq, n, cap, eb, rb=1):
    return pl.pallas_call(
        functools.partial(_gather_kernel, eb=eb, cap=cap, n=n, rb=rb),
        grid=(nreq // rb, NE // eb),
        in_specs=[pl.BlockSpec((NE, rb * n), lambda b, g: (0, b)), pl.BlockSpec((NE, rb * n), lambda b, g: (0, b)),
                  pl.BlockSpec((rb * n, D), lambda b, g: (b, 0))],
        out_specs=[pl.BlockSpec((eb, rb * cap, D), lambda b, g: (g, b, 0)),
                   pl.BlockSpec((eb, rb * cap, 128), lambda b, g: (g, b, 0))],
        out_shape=[_sds((NE, nreq * cap, D), BF16), _sds((NE, nreq * cap, 128))],
        compiler_params=_params(2),
        name="moe_gather",
    )(pos, aff, hn2)


def _expert_kernel(xc_ref, xl_ref, wc_ref, wl_ref, wg_ref, wu_ref, wd_ref, yc_ref, yl_ref, *, rows):
    wg = _bf(wg_ref[0, 0])
    wu = _bf(wu_ref[0, 0])
    wd = _bf(wd_ref[0, 0])
    tm = 512
    for x_ref, w_ref, y_ref in ((xc_ref, wc_ref, yc_ref), (xl_ref, wl_ref, yl_ref)):
        for ch in range(rows // tm):
            sl = slice(ch * tm, (ch + 1) * tm)
            x = x_ref[0, sl, :]
            g = _dot(x, wg)
            u = _dot(x, wu)
            a = _bf(g * jax.nn.sigmoid(g) * u)
            y_ref[0, sl, :] = _bf(_dot(a, wd) * w_ref[0, sl, 0:1])


def _experts(xs_c, xs_l, w_c, w_l, wg, wu, wd, layer):
    rows = xs_c.shape[1]
    dff = wg.shape[-1]
    xspec = pl.BlockSpec((1, rows, D), lambda e: (e, 0, 0))
    wspec = pl.BlockSpec((1, rows, 128), lambda e: (e, 0, 0))
    return pl.pallas_call(
        functools.partial(_expert_kernel, rows=rows),
        grid=(NE,),
        in_specs=[xspec, xspec, wspec, wspec,
                  pl.BlockSpec((1, 1, D, dff), lambda e: (layer, e, 0, 0)),
                  pl.BlockSpec((1, 1, D, dff), lambda e: (layer, e, 0, 0)),
                  pl.BlockSpec((1, 1, dff, D), lambda e: (layer, e, 0, 0))],
        out_specs=[xspec, xspec],
        out_shape=[_sds((NE, rows, D), BF16), _sds((NE, rows, D), BF16)],
        compiler_params=_params(1),
        name="moe_experts",
    )(xs_c, xs_l, w_c, w_l, wg, wu, wd)


def _scatter_kernel(idx_ref, y_ref, x1_ref, mod_ref, o_ref, *, eb, cap, n, rb, row_base, row_mul, token_major):
    eg = pl.program_id(1)
    last = pl.num_programs(1) - 1
    r = row_base + row_mul * pl.program_id(0)
    g2 = mod_ref[pl.ds(r, 1), 5 * D:6 * D]
    tn = min(n, 512)
    for i in range(rb):
        ys = jnp.concatenate([y_ref[k, i * cap:(i + 1) * cap, :] for k in range(eb)], axis=0)
        for t in range(n // tn):
            rows = slice(i * n + t * tn, i * n + (t + 1) * tn)
            if token_major:
                io = lax.broadcasted_iota(jnp.int32, (tn, cap), 1)
                shift = (128 - eg * eb) & 127
                pt = pltpu.roll(idx_ref[rows, :], shift, axis=1)
                sels = [(pt[:, k:k + 1] == io).astype(F32).astype(BF16) for k in range(eb)]
                part = _dot(jnp.concatenate(sels, axis=1), ys)
            else:
                io = lax.broadcasted_iota(jnp.int32, (cap, tn), 0)
                sels = [(idx_ref[pl.ds(eg * eb + k, 1), rows] == io).astype(F32).astype(BF16) for k in range(eb)]
                part = _dot_tn(jnp.concatenate(sels, axis=0), ys)
            o_ref[rows, :] = jnp.where(eg == 0, part, o_ref[rows, :] + part)

    @pl.when(eg == last)
    def _():
        o_ref[...] = x1_ref[...] + g2 * o_ref[...]


def _scatter(idx, y, x1, mod, *, nreq, n, cap, eb, row_base, row_mul, rb=1):
    tp = x1.shape[0]
    assert rb == 1 or row_mul == 0
    token_major = idx.shape[0] == tp
    idx_spec = (pl.BlockSpec((rb * n, 128), lambda b, g: (b, 0)) if token_major
                else pl.BlockSpec((NE, rb * n), lambda b, g: (0, b)))
    return pl.pallas_call(
        functools.partial(_scatter_kernel, eb=eb, cap=cap, n=n, rb=rb, row_base=row_base, row_mul=row_mul,
                          token_major=token_major),
        grid=(nreq // rb, NE // eb),
        in_specs=[idx_spec,
                  pl.BlockSpec((eb, rb * cap, D), lambda b, g: (g, b, 0)),
                  pl.BlockSpec((rb * n, D), lambda b, g: (b, 0)), _full((8, 6 * D))],
        out_specs=pl.BlockSpec((rb * n, D), lambda b, g: (b, 0)),
        out_shape=_sds((tp, D)),
        compiler_params=_params(2),
        name="moe_scatter",
    )(idx, y, x1, mod)


def _rope_tables(n):
    nf = 8
    t = np.arange(n)
    rowp = (t // GRID_W).astype(np.float32)
    colp = (t % GRID_W).astype(np.float32)
    inv = (np.float32(ROPE_BASE) ** (-np.arange(nf, dtype=np.float32) / np.float32(nf))).astype(np.float32)
    lane = np.arange(GW)
    c32 = lane % 32
    pos = np.where((c32 < 16)[None, :], rowp[:, None], colp[:, None]).astype(np.float32)
    ang = (pos * inv[(c32 % 8)][None, :]).astype(np.float32).astype(np.float64)
    sign = np.where((lane % 16) < 8, -1.0, 1.0)[None, :]
    return jnp.asarray(np.cos(ang), F32), jnp.asarray(np.sin(ang) * sign, F32)


def _block_diag(w):
    nb, bi, bo = w.shape
    return (jnp.eye(nb, dtype=w.dtype)[:, None, :, None] * w[:, :, None, :]).reshape(nb * bi, nb * bo)


def _w_in_kernel(w_ref, o_ref):
    gate_lo = 4 * GW
    rest_lo = gate_lo + 16
    rest = o_ref.shape[1] - 128 - gate_lo
    o_ref[:, 0:gate_lo] = _bf(w_ref[0, :, 0:gate_lo])
    o_ref[:, gate_lo:gate_lo + rest] = _bf(w_ref[0, :, rest_lo:rest_lo + rest])
    lane = lax.broadcasted_iota(jnp.int32, (o_ref.shape[0], 128), 1)
    o_ref[:, gate_lo + rest:] = _bf(jnp.where(lane < 16, w_ref[0, :, gate_lo:gate_lo + 128], 0.0))


def _w_in_layout(w_in, l, tr=256):
    d_in, width = w_in.shape[1:]
    return pl.pallas_call(
        _w_in_kernel,
        grid=(d_in // tr,),
        in_specs=[pl.BlockSpec((1, tr, width), lambda i: (l, i, 0))],
        out_specs=pl.BlockSpec((tr, PROJ_PAD), lambda i: (i, 0)),
        out_shape=_sds((d_in, PROJ_PAD), BF16),
        compiler_params=_params(1),
        name="w_in_layout",
    )(w_in)


def _layer_params(l, w_in, ml_gate_b, na_qn_g, na_kn_g, df_qn_g, df_kn_g, rg_wa, rg_wx, rg_ba, rg_bx,
                  df_lq1, df_lk1, df_lq2, df_lk2, df_subln_g, w_out, router_w):
    w_r = _w_in_layout(w_in, l)
    gate_b =jnp.pad(ml_gate_b[l], (0, 128 - 16)).reshape(1, 128)
    qkg = jnp.stack([jnp.tile(na_qn_g[l], NH), jnp.tile(na_kn_g[l], NH),
                     jnp.tile(df_qn_g[l], 2 * NH), jnp.tile(df_kn_g[l], 2 * NH)])
    wbd = jnp.concatenate([_block_diag(rg_wa[l, 0]), _block_diag(rg_wx[l, 0]),
                           _block_diag(rg_wa[l, 1]), _block_diag(rg_wx[l, 1])], axis=1).astype(BF16)
    rg_bias = jnp.concatenate([rg_ba[l, 0], rg_bx[l, 0], rg_ba[l, 1], rg_bx[l, 1]]).reshape(1, 4 * GW)
    lamp = jnp.stack([df_lq1[l], df_lk1[l], df_lq2[l], df_lk2[l]])
    sub = jnp.tile(df_subln_g[l], NH).reshape(1, GW)
    rw = jnp.pad(router_w[l], ((0, 0), (0, 128 - NE))).astype(BF16)
    return dict(w_r=w_r, gate_b=gate_b, qkg=qkg, wbd=wbd, rg_bias=rg_bias, lamp=lamp, sub=sub,
                wout=w_out[l].astype(BF16), rw=rw)


def kernel(x_prompt, x_sample, cache_na_k, cache_na_v, cache_df_k, cache_df_v, state_ml_c, state_ml_n, state_ml_m, state_rg_h, c, c_ctx, norm1_g, norm2_g, w_mod, b_mod, w_in, ml_gate_b, ml_norm_g, na_qn_g, na_kn_g, na_rpb, rg_conv_w, rg_conv_b, rg_wa, rg_ba, rg_wx, rg_bx, rg_lam, df_qn_g, df_kn_g, df_lq1, df_lk1, df_lq2, df_lk2, df_subln_g, w_out, router_w, moe_wg, moe_wu, moe_wd):
    nb, seq, _ = x_prompt.shape
    db, dseq, _ = x_sample.shape
    depth = w_in.shape[0]
    past = cache_na_k.shape[2]
    tm = 512
    cap_c = 2 * seq // NE
    cap_l = 2 * dseq // NE

    cv = jnp.concatenate([c_ctx[None, :], c, jnp.zeros((8 - 1 - db, D), F32)], axis=0)
    mod_all = _modulation(cv, w_mod, b_mod)
    rope = _rope_tables(dseq)

    xc = x_prompt.reshape(nb * seq, D)
    xl = x_sample.reshape(db * dseq, D)
    ctx_out = []
    caches = None
    for l in range(depth):
        lam_init = 0.8 - 0.6 * math.exp(-0.3 * l)
        p = _layer_params(l, w_in, ml_gate_b, na_qn_g, na_kn_g, df_qn_g, df_kn_g, rg_wa, rg_wx, rg_ba, rg_bx,
                          df_lq1, df_lk1, df_lq2, df_lk2, df_subln_g, w_out, router_w)
        mod = mod_all[l]
        g1 = norm1_g[l].reshape(1, D)
        g2 = norm2_g[l].reshape(1, D)
        mlg = ml_norm_g[l].reshape(1, GW)
        cb = rg_conv_b[l].reshape(1, GW)
        tb = _rpb_table(na_rpb[l], rows=dseq // GRID_W, rps=NA_RPS)

        nt_c = nb * seq // tm
        ml, gates, naq, rg, dfq, *caches = _proj_in(xc, mod, g1, p["w_r"], p["gate_b"], p["qkg"], caches=caches,
                                                    cache_shape=(nb, depth, seq, GW), layer=l,
                                                    row_base=0, row_div=nt_c, tm=tm)
        hf, hb, c_c, n_c, m_c = _mlstm(ml, gates, nreq=nb, nc=seq // CHUNK)
        y_na, o_df = _ctx_attn(naq, dfq, caches, l, p["lamp"], lam_init, nreq=nb, n=seq)
        y_rg, rg_fin = _rglru(rg, rg_conv_w[l], cb, p["wbd"], p["rg_bias"], rg_lam[l],
                              jnp.zeros((nb, 2, GW), F32), nreq=nb, n=seq)
        x1_c, hn2_c, lg_c = _merge(hf, hb, ml, y_na, y_rg, o_df, xc, mod, mlg, p["sub"], p["wout"], g2, p["rw"],
                                   lam_init, row_base=0, row_div=nt_c, tm=tm)
        pos_c, aff_c = _route(lg_c, nreq=nb, n=seq, cap=cap_c, rb=8, token_major_copy=False)
        xs_c, w_c = _gather(pos_c, aff_c, hn2_c, nreq=nb, n=seq, cap=cap_c, eb=NE, rb=4)
        ctx_out.append((c_c, n_c, m_c, rg_fin))

        tiles_req = dseq // tm
        ml, gates, na, rg, df = _proj_in(xl, mod, g1, p["w_r"], p["gate_b"], p["qkg"], rope=rope,
                                         row_base=1, row_div=tiles_req, tm=tm)
        cn0 = _mlstm_pack_state(state_ml_c[:, l], state_ml_n[:, l])
        m0 = jnp.broadcast_to(state_ml_m[:, l].reshape(db, 8, 1), (db, 8, 128))
        hf, hb, _, _, _ = _mlstm(ml, gates, (cn0, m0), nreq=db, nc=dseq // CHUNK)
        y_na = _na_lat(na, cache_na_k[:, l].reshape(db, past, GW), cache_na_v[:, l].reshape(db, past, GW), tb,
                       nreq=db, n=dseq, rps=NA_RPS)
        o_df = _df_lat(df, cache_df_k[:, l].reshape(db, past, GW), cache_df_v[:, l].reshape(db, past, GW),
                       p["lamp"], lam_init, nreq=db, n=dseq)
        y_rg, _ = _rglru(rg, rg_conv_w[l], cb, p["wbd"], p["rg_bias"], rg_lam[l], state_rg_h[:, l],
                         nreq=db, n=dseq)
        x1_l, hn2_l, lg_l = _merge(hf, hb, ml, y_na, y_rg, o_df, xl, mod, mlg, p["sub"], p["wout"], g2, p["rw"],
                                   lam_init, row_base=1, row_div=tiles_req, tm=tm)
        pos_l, aff_l, post_l = _route(lg_l, nreq=db, n=dseq, cap=cap_l, rb=1, token_major_copy=True)
        xs_l, w_l = _gather(pos_l, aff_l, hn2_l, nreq=db, n=dseq, cap=cap_l, eb=2)

        y_c, y_l = _experts(xs_c, xs_l, w_c, w_l, moe_wg, moe_wu, moe_wd, l)
        xc = _scatter(pos_c, y_c, x1_c, mod, nreq=nb, n=seq, cap=cap_c, eb=NE, row_base=0, row_mul=0, rb=4)
        xl = _scatter(post_l, y_l, x1_l, mod, nreq=db, n=dseq, cap=cap_l, eb=4, row_base=1, row_mul=1)

    y_prompt = xc.reshape(nb, seq, D)
    y_sample = xl.reshape(db, dseq, D)
    st = lambda f: jnp.stack([f(o) for o in ctx_out], axis=1)
    na_k = caches[0].reshape(nb, depth, seq, NH, HD)
    na_v = caches[1].reshape(nb, depth, seq, NH, HD)
    df_k = caches[2].reshape(nb, depth, seq, NH, 2, HD // 2)
    df_v = caches[3].reshape(nb, depth, seq, NH, HD)
    ml_c = st(lambda o: o[0])
    ml_n = st(lambda o: o[1])
    ml_m = st(lambda o: o[2][:, :, 0].reshape(nb, 2, NH))
    rg_h = st(lambda o: o[3])
    return (y_prompt, y_sample, na_k, na_v, df_k, df_v, ml_c, ml_n, ml_m, rg_h)
```
